```python
import jax, jax.numpy as jnp
from jax import lax
import numpy as np

D_MODEL = 2048
BATCH = 32
SEQ = 256
DEPTH = 1
DEC_BATCH = 4
DEC_SEQ = 4096
PAST_LEN = 512

GRID_W = 64
N_HEADS = 8
QK_NOPE = 128
QK_ROPE = 64
V_DIM = 128
Q_LORA = 512
KV_LORA = 512
ATTN_WIDTH = N_HEADS * V_DIM
GMLP_GROUPS = 8
GMLP_GROUP_DIM = 128
GMLP_WIDTH = GMLP_GROUPS * GMLP_GROUP_DIM
CHUNK = 128
MIX_WIDTH = ATTN_WIDTH + GMLP_WIDTH
SPLITS = (Q_LORA, Q_LORA + KV_LORA, Q_LORA + KV_LORA + QK_ROPE,
          Q_LORA + KV_LORA + QK_ROPE + GMLP_WIDTH)
IN_WIDTH = Q_LORA + KV_LORA + QK_ROPE + 2 * GMLP_WIDTH
N_EXPERT_GROUPS = 4
EXPERTS_PER_GROUP = 8
N_EXPERTS = N_EXPERT_GROUPS * EXPERTS_PER_GROUP
TOP_K = 2
EXPERT_HIDDEN = 512
MOE_BLOCK = 128
ROPE_THETA = 10000.0
NORM_EPS = 1e-6
ALPHA = (2.0 * DEPTH) ** 0.25
BETA = (8.0 * DEPTH) ** -0.25

kernel_name = "hybrid_gmlp_mla_hmoe_diffusion_step"


def layer_norm(x, g, b):
    xf = x.astype(jnp.float32)
    mu = jnp.mean(xf, -1, keepdims=True)
    var = jnp.mean(jnp.square(xf - mu), -1, keepdims=True)
    y = (xf - mu) * lax.rsqrt(var + NORM_EPS) * g.astype(jnp.float32) + b.astype(jnp.float32)
    return y.astype(x.dtype)


def rms_norm(x, g):
    xf = x.astype(jnp.float32)
    y = xf * lax.rsqrt(jnp.mean(xf * xf, -1, keepdims=True) + NORM_EPS)
    return (y * g.astype(jnp.float32)).astype(x.dtype)


def adaln(cond, w_ada, b_ada):
    m = jax.nn.silu(cond) @ w_ada + b_ada
    return [t[:, None, :] for t in jnp.split(m, 6, axis=-1)]


def axial_rope_tables(n):
    rows = n // GRID_W
    row = jnp.repeat(jnp.arange(rows, dtype=jnp.float32), GRID_W)
    col = jnp.tile(jnp.arange(GRID_W, dtype=jnp.float32), rows)
    half = QK_ROPE // 2
    inv = ROPE_THETA ** (-jnp.arange(0, half, 2, dtype=jnp.float32) / half)
    ang = jnp.concatenate([row[:, None] * inv, col[:, None] * inv], -1)
    return jnp.cos(ang), jnp.sin(ang)


def apply_axial_rope(x, cos, sin):
    half, quarter = QK_ROPE // 2, QK_ROPE // 4
    xf = x.astype(jnp.float32)

    def rot(z, c, s):
        z1, z2 = z[..., :quarter], z[..., quarter:]
        return jnp.concatenate([z1 * c - z2 * s, z2 * c + z1 * s], -1)

    out = jnp.concatenate([
        rot(xf[..., :half], cos[..., :quarter], sin[..., :quarter]),
        rot(xf[..., half:], cos[..., quarter:], sin[..., quarter:])], -1)
    return out.astype(x.dtype)


def mla_attention(q_nope, q_rope, k_nope, k_rope, v):
    b, nq, h, _ = q_nope.shape
    nb = nq // CHUNK
    scale = (QK_NOPE + QK_ROPE) ** -0.5
    qn_b = jnp.moveaxis(q_nope.reshape(b, nb, CHUNK, h, QK_NOPE), 1, 0)
    qr_b = jnp.moveaxis(q_rope.reshape(b, nb, CHUNK, h, QK_ROPE), 1, 0)

    def block(qs):
        qn, qr = qs
        s = (jnp.einsum("bqhd,bkhd->bhqk", qn, k_nope)
             + jnp.einsum("bqhr,bkr->bhqk", qr, k_rope))
        p = jax.nn.softmax(s.astype(jnp.float32) * scale, axis=-1).astype(v.dtype)
        return jnp.einsum("bhqk,bkhd->bqhd", p, v)

    out = lax.map(block, (qn_b, qr_b))
    return jnp.moveaxis(out, 0, 1).reshape(b, nq, h * V_DIM)


def chunk_gmlp(u, v, ln_g, ln_b, w_s, b_s):
    b, n, _ = u.shape
    vn = layer_norm(v, ln_g, ln_b).reshape(b, n // CHUNK, CHUNK, GMLP_GROUPS, GMLP_GROUP_DIM)
    s = jnp.einsum("gqp,bcpgd->bcqgd", w_s, vn) + b_s.T[None, None, :, :, None]
    return u * s.reshape(b, n, GMLP_WIDTH)


def hierarchical_moe(h, w_group, b_group, w_router, b_router, w1, w3, w2):
    t, d = h.shape
    glog = (h @ w_group + b_group).astype(jnp.float32)
    gprob = jax.nn.softmax(glog, axis=-1)
    g_top = jnp.argmax(glog, axis=-1).astype(jnp.int32)
    p_top = jnp.take_along_axis(gprob, g_top[:, None], axis=-1)
    elog = (h @ w_router + b_router).astype(jnp.float32).reshape(t, N_EXPERT_GROUPS, EXPERTS_PER_GROUP)
    elog_g = jnp.take_along_axis(elog, g_top[:, None, None], axis=1)[:, 0]
    top_p, top_i = lax.top_k(jax.nn.softmax(elog_g, axis=-1), TOP_K)
    top_p = top_p / jnp.sum(top_p, -1, keepdims=True)
    weights = (p_top * top_p).reshape(-1)
    expert = (g_top[:, None] * EXPERTS_PER_GROUP + top_i).reshape(-1).astype(jnp.int32)
    token = jnp.repeat(jnp.arange(t, dtype=jnp.int32), TOP_K)
    a = expert.shape[0]
    order = jnp.argsort(expert)
    e_sorted = expert[order]
    counts = jnp.bincount(expert, length=N_EXPERTS)
    starts = jnp.cumsum(counts) - counts
    padded = ((counts + MOE_BLOCK - 1) // MOE_BLOCK) * MOE_BLOCK
    pends = jnp.cumsum(padded)
    pstarts = pends - padded
    dest = pstarts[e_sorted] + (jnp.arange(a, dtype=jnp.int32) - starts[e_sorted])
    n_blocks = -(-a // MOE_BLOCK) + N_EXPERTS
    length = n_blocks * MOE_BLOCK
    buf_tok = jnp.full((length,), t, jnp.int32).at[dest].set(token[order])
    buf_w = jnp.zeros((length,), h.dtype).at[dest].set(weights[order].astype(h.dtype))
    blk_start = jnp.arange(n_blocks, dtype=jnp.int32) * MOE_BLOCK
    blk_expert = jnp.clip(jnp.searchsorted(pends, blk_start, side="right"), 0, N_EXPERTS - 1)
    h_pad = jnp.concatenate([h, jnp.zeros((1, d), h.dtype)], axis=0)

    def run(args):
        idx, e = args
        xb = h_pad[idx]
        hid = jax.nn.silu(xb @ w1[e]) * (xb @ w3[e])
        return hid @ w2[e]

    y = lax.map(run, (buf_tok.reshape(n_blocks, MOE_BLOCK), blk_expert))
    y = y.reshape(length, d) * buf_w[:, None]
    return jnp.zeros((t + 1, d), h.dtype).at[buf_tok].add(y)[:t]


def trunk_layer(x, mods, lp, rope, ctx):
    shift1, scale1, gate1, shift2, scale2, gate2 = mods
    b, n, d = x.shape
    h = x * (1 + scale1) + shift1
    cq, ckv, krope, u, v = jnp.split(h @ lp["w_in"], SPLITS, axis=-1)
    q = (rms_norm(cq, lp["q_norm_g"]) @ lp["w_uq"]).reshape(b, n, N_HEADS, QK_NOPE + QK_ROPE)
    q_nope, q_rope = q[..., :QK_NOPE], q[..., QK_NOPE:]
    ckv_n = rms_norm(ckv, lp["kv_norm_g"])
    if rope is None:
        k_rope = krope
    else:
        cos, sin = rope
        q_rope = apply_axial_rope(q_rope, cos[:, None, :], sin[:, None, :])
        k_rope = apply_axial_rope(krope, cos, sin)
    if ctx is None:
        kv_src, krope_all = ckv_n, k_rope
    else:
        kv_src = jnp.concatenate([ckv_n, ctx[0]], axis=1)
        krope_all = jnp.concatenate([k_rope, ctx[1]], axis=1)
    kv = (kv_src @ lp["w_ukv"]).reshape(b, kv_src.shape[1], N_HEADS, QK_NOPE + V_DIM)
    attn = mla_attention(q_nope, q_rope, kv[..., :QK_NOPE], krope_all, kv[..., QK_NOPE:])
    gm = chunk_gmlp(jax.nn.gelu(u, approximate=False), jax.nn.gelu(v, approximate=False),
                    lp["gmlp_ln_g"], lp["gmlp_ln_b"], lp["w_spatial"], lp["b_spatial"])
    mix = jnp.concatenate([attn, gm], axis=-1) @ lp["w_out"]
    x = layer_norm(ALPHA * x + gate1 * mix, lp["ln1_g"], lp["ln1_b"])
    h2 = x * (1 + scale2) + shift2
    moe = hierarchical_moe(h2.reshape(b * n, d), lp["w_group"], lp["b_group"], lp["w_router"],
                           lp["b_router"], lp["w1"], lp["w3"], lp["w2"]).reshape(b, n, d)
    x = layer_norm(ALPHA * x + gate2 * moe, lp["ln2_g"], lp["ln2_b"])
    return x, ckv_n, krope


def setup_inputs(seed: int = 0) -> dict:
    key = jax.random.key(seed)
    ks = jax.random.split(key, 32)
    f = jnp.float32
    nrm = lambda k, s: jax.random.normal(k, s, f)
    D = D_MODEL
    return {
        "x_prompt": nrm(ks[0], (BATCH, SEQ, D)),
        "x_sample": nrm(ks[1], (DEC_BATCH, DEC_SEQ, D)),
        "cache_ckv": nrm(ks[2], (DEC_BATCH, DEPTH, PAST_LEN, KV_LORA)),
        "cache_krope": nrm(ks[3], (DEC_BATCH, DEPTH, PAST_LEN, QK_ROPE)),
        "c": nrm(ks[4], (DEC_BATCH, D)),
        "c_ctx": nrm(ks[5], (D,)),
        "w_ada": nrm(ks[6], (DEPTH, D, 6 * D)) * D ** -0.5,
        "b_ada": nrm(ks[7], (DEPTH, 6 * D)) * 0.02,
        "w_in": nrm(ks[8], (DEPTH, D, IN_WIDTH)) * D ** -0.5,
        "q_norm_g": 1.0 + 0.02 * nrm(ks[9], (DEPTH, Q_LORA)),
        "w_uq": nrm(ks[10], (DEPTH, Q_LORA, N_HEADS * (QK_NOPE + QK_ROPE))) * Q_LORA ** -0.5,
        "kv_norm_g": 1.0 + 0.02 * nrm(ks[11], (DEPTH, KV_LORA)),
        "w_ukv": nrm(ks[12], (DEPTH, KV_LORA, N_HEADS * (QK_NOPE + V_DIM))) * KV_LORA ** -0.5,
        "gmlp_ln_g": 1.0 + 0.02 * nrm(ks[13], (DEPTH, GMLP_WIDTH)),
        "gmlp_ln_b": 0.02 * nrm(ks[14], (DEPTH, GMLP_WIDTH)),
        "w_spatial": nrm(ks[15], (DEPTH, GMLP_GROUPS, CHUNK, CHUNK)) * CHUNK ** -0.5,
        "b_spatial": 1.0 + 0.02 * nrm(ks[16], (DEPTH, GMLP_GROUPS, CHUNK)),
        "w_out": nrm(ks[17], (DEPTH, MIX_WIDTH, D)) * MIX_WIDTH ** -0.5 * BETA,
        "ln1_g": 1.0 + 0.02 * nrm(ks[18], (DEPTH, D)),
        "ln1_b": 0.02 * nrm(ks[19], (DEPTH, D)),
        "w_group": nrm(ks[20], (DEPTH, D, N_EXPERT_GROUPS)) * D ** -0.5,
        "b_group": 0.01 * nrm(ks[21], (DEPTH, N_EXPERT_GROUPS)),
        "w_router": nrm(ks[22], (DEPTH, D, N_EXPERTS)) * D ** -0.5,
        "b_router": 0.01 * nrm(ks[23], (DEPTH, N_EXPERTS)),
        "w1": nrm(ks[24], (DEPTH, N_EXPERTS, D, EXPERT_HIDDEN)) * D ** -0.5,
        "w3": nrm(ks[25], (DEPTH, N_EXPERTS, D, EXPERT_HIDDEN)) * D ** -0.5,
        "w2": nrm(ks[26], (DEPTH, N_EXPERTS, EXPERT_HIDDEN, D)) * EXPERT_HIDDEN ** -0.5 * BETA,
        "ln2_g": 1.0 + 0.02 * nrm(ks[27], (DEPTH, D)),
        "ln2_b": 0.02 * nrm(ks[28], (DEPTH, D)),
    }


def reference(x_prompt, x_sample, cache_ckv, cache_krope, c, c_ctx, w_ada, b_ada, w_in,
              q_norm_g, w_uq, kv_norm_g, w_ukv, gmlp_ln_g, gmlp_ln_b, w_spatial, b_spatial,
              w_out, ln1_g, ln1_b, w_group, b_group, w_router, b_router, w1, w3, w2,
              ln2_g, ln2_b):
    rope = axial_rope_tables(x_sample.shape[1])
    y_prompt, y_sample = x_prompt, x_sample
    ckv_states, krope_states = [], []
    for l in range(DEPTH):
        lp = {
            "w_in": w_in[l], "q_norm_g": q_norm_g[l], "w_uq": w_uq[l], "kv_norm_g": kv_norm_g[l],
            "w_ukv": w_ukv[l], "gmlp_ln_g": gmlp_ln_g[l], "gmlp_ln_b": gmlp_ln_b[l],
            "w_spatial": w_spatial[l], "b_spatial": b_spatial[l], "w_out": w_out[l],
            "ln1_g": ln1_g[l], "ln1_b": ln1_b[l], "w_group": w_group[l], "b_group": b_group[l],
            "w_router": w_router[l], "b_router": b_router[l], "w1": w1[l], "w3": w3[l],
            "w2": w2[l], "ln2_g": ln2_g[l], "ln2_b": ln2_b[l],
        }
        mods_ctx = adaln(c_ctx[None, :], w_ada[l], b_ada[l])
        y_prompt, ckv_l, krope_l = trunk_layer(y_prompt, mods_ctx, lp, None, None)
        ckv_states.append(ckv_l)
        krope_states.append(krope_l)
        mods_lat = adaln(c, w_ada[l], b_ada[l])
        y_sample, _, _ = trunk_layer(y_sample, mods_lat, lp, rope,
                                     (cache_ckv[:, l], cache_krope[:, l]))
    state_ckv = jnp.stack(ckv_states, axis=1)
    state_krope = jnp.stack(krope_states, axis=1)
    return (y_prompt, y_sample, state_ckv, state_krope)
```

```python
import functools

import jax
import jax.numpy as jnp
import numpy as np
from jax import lax
from jax.experimental import pallas as pl
from jax.experimental.pallas import tpu as pltpu

F32 = jnp.float32
BF16 = jnp.bfloat16
U32 = jnp.uint32
I32 = jnp.int32

D = 2048
N_CTX_B, N_CTX = 32, 256
N_LAT_B, N_LAT = 4, 4096
PAST = 512
GRID_W = 64
H = 8
QK_NOPE, QK_ROPE, V_DIM = 128, 64, 128
QL, KVL = 512, 512
GW = 1024
G_GROUPS, G_DIM, CHUNK = 8, 128, 128
N_GRP, E_PER_GRP, N_EXP, TOP_K = 4, 8, 32, 2
E_HID = 512
ROPE_THETA = 10000.0
EPS = 1e-6
ALPHA = 2.0 ** 0.25
SM_SCALE = (QK_NOPE + QK_ROPE) ** -0.5

T_CTX = N_CTX_B * N_CTX
T_LAT = N_LAT_B * N_LAT
T = T_CTX + T_LAT
N_ASSIGN = T * TOP_K

LANES = 128
HEAD_PAD = 256
TM = 256
TQ = 512
BM = 256
N_BLOCKS = N_ASSIGN // BM + N_EXP
VMEM_LIMIT = 56 * 1024 * 1024

NP_TILES = T_CTX // TM
LAT_TILES_PER_B = N_LAT // TM
IN_W = QL + KVL + 2 * GW + LANES


def _cp(sem):
    return pltpu.CompilerParams(dimension_semantics=sem, vmem_limit_bytes=VMEM_LIMIT)


def _const_spec(shape):
    nd = len(shape)
    return pl.BlockSpec(shape, lambda *a: (0,) * nd, pipeline_mode=pl.Buffered(1))


def _dot(a, b):
    return jnp.dot(a, b, preferred_element_type=F32)


def _dot_nt(a, b):
    return lax.dot_general(a, b, (((1,), (1,)), ((), ())), preferred_element_type=F32)


def _layer_norm(x, g, b):
    mu = jnp.mean(x, -1, keepdims=True)
    xc = x - mu
    var = jnp.mean(xc * xc, -1, keepdims=True)
    return xc * lax.rsqrt(var + EPS) * g + b


def _rms_norm(x, g):
    return x * lax.rsqrt(jnp.mean(x * x, -1, keepdims=True) + EPS) * g


def _gelu(x):
    return 0.5 * x * (1.0 + lax.erf(x * np.float32(np.sqrt(0.5))))


def _pack_pair(lo, hi):
    lo_b = lax.bitcast_convert_type(lo.astype(BF16).astype(F32), U32)
    hi_b = lax.bitcast_convert_type(hi.astype(BF16).astype(F32), U32)
    return hi_b | (lo_b >> 16)


def _unpack_pair(w):
    lo = lax.bitcast_convert_type(w << 16, F32)
    hi = lax.bitcast_convert_type(w & jnp.uint32(0xFFFF0000), F32)
    return lo, hi


def _adaln_kernel(cond_ref, w_ref, b_ref, o_ref):
    s = jax.nn.silu(cond_ref[...])
    s_hi = s.astype(BF16)
    s_lo = (s - s_hi.astype(F32)).astype(BF16)
    w = w_ref[...]
    w_hi = w.astype(BF16)
    w_lo = (w - w_hi.astype(F32)).astype(BF16)
    o_ref[...] = _dot(s_hi, w_hi) + _dot(s_lo, w_hi) + _dot(s_hi, w_lo) + b_ref[...]


def _adaln(cond16, w_ada, b_ada):
    tn = 1024
    n = w_ada.shape[1]
    return pl.pallas_call(
        _adaln_kernel,
        grid=(n // tn,),
        in_specs=[pl.BlockSpec((16, D), lambda j: (0, 0)),
                  pl.BlockSpec((D, tn), lambda j: (0, j)),
                  pl.BlockSpec((1, tn), lambda j: (0, j))],
        out_specs=pl.BlockSpec((16, tn), lambda j: (0, j)),
        out_shape=jax.ShapeDtypeStruct((16, n), F32),
        compiler_params=_cp(("arbitrary",)),
        name="adaln",
    )(cond16, w_ada, b_ada)


def _expand_kv(ckv_n, kr128, wuk_ref, wuv_ref, k_ref, v_ref):
    cb = ckv_n.astype(BF16)
    kn = _dot(cb, wuk_ref[...])
    v_ref[...] = _dot(cb, wuv_ref[...]).astype(BF16)
    krb = kr128.astype(BF16)
    for h in range(H):
        k_ref[:, h * HEAD_PAD:h * HEAD_PAD + QK_NOPE] = kn[:, h * QK_NOPE:(h + 1) * QK_NOPE].astype(BF16)
        k_ref[:, h * HEAD_PAD + QK_NOPE:(h + 1) * HEAD_PAD] = krb


def _rope128(x, cos, s_plus, s_minus):
    return (x * cos + pltpu.roll(x, 16, 1) * s_plus + pltpu.roll(x, LANES - 16, 1) * s_minus)


def _proj_kernel(xc_ref, xl_ref, mods_ref, win_ref, qg_ref, wuq_ref, kvg_ref, wuk_ref, wuv_ref,
                 lng_ref, lnb_ref, ws_ref, bsb_ref, cos_ref, sp_ref, sm_ref,
                 q_ref, k_ref, v_ref, gm_ref, ckv_ref, kr_ref):
    i = pl.program_id(0)
    x = jnp.where(i < NP_TILES, xc_ref[...], xl_ref[...])
    m = mods_ref[0]
    shift1, scale1 = m[0:1], m[1:2]
    hmod = (x * (1.0 + scale1) + shift1).astype(BF16)
    proj = _dot(hmod, win_ref[...])
    cq = proj[:, 0:QL]
    ckv = proj[:, QL:QL + KVL]
    u = proj[:, QL + KVL:QL + KVL + GW]
    v = proj[:, QL + KVL + GW:QL + KVL + 2 * GW]
    kr = proj[:, QL + KVL + 2 * GW:IN_W]

    cos, s_plus, s_minus = cos_ref[...], sp_ref[...], sm_ref[...]

    cq_n = _rms_norm(cq, qg_ref[...]).astype(BF16)
    q = _dot(cq_n, wuq_ref[...]) * SM_SCALE
    for h in range(H):
        q_ref[:, h * HEAD_PAD:h * HEAD_PAD + QK_NOPE] = q[:, h * HEAD_PAD:h * HEAD_PAD + QK_NOPE].astype(BF16)
        q_ref[:, h * HEAD_PAD + QK_NOPE:(h + 1) * HEAD_PAD] = _rope128(
            q[:, h * HEAD_PAD + QK_NOPE:(h + 1) * HEAD_PAD], cos, s_plus, s_minus).astype(BF16)

    ckv_n = _rms_norm(ckv, kvg_ref[...])
    ckv_ref[...] = ckv_n
    kr_ref[...] = kr
    _expand_kv(ckv_n, _rope128(kr, cos, s_plus, s_minus), wuk_ref, wuv_ref, k_ref, v_ref)

    gu = _gelu(u)
    vn = _layer_norm(_gelu(v), lng_ref[...], lnb_ref[...]).astype(BF16)
    for c in range(TM // CHUNK):
        rows = slice(c * CHUNK, (c + 1) * CHUNK)
        for g in range(G_GROUPS):
            cols = slice(g * G_DIM, (g + 1) * G_DIM)
            s = _dot(ws_ref[g], vn[rows, cols]) + bsb_ref[g]
            gm_ref[rows, cols] = (gu[rows, cols] * s).astype(BF16)


def _mods_index(i):
    return jnp.where(i < NP_TILES, N_LAT_B, (i - NP_TILES) // LAT_TILES_PER_B)


def _proj(xc, xl, mods, win, qg, wuq, kvg, wuk, wuv, lng, lnb, ws, bsb, cos_t, sp_t, sm_t):
    def tok(width):
        return pl.BlockSpec((TM, width), lambda i: (i, 0))

    rope_spec = pl.BlockSpec(
        (TM, LANES),
        lambda i: (jnp.where(i < NP_TILES, LAT_TILES_PER_B, (i - NP_TILES) % LAT_TILES_PER_B), 0))
    return pl.pallas_call(
        _proj_kernel,
        grid=(T // TM,),
        in_specs=[
            pl.BlockSpec((TM, D), lambda i: (jnp.minimum(i, NP_TILES - 1), 0)),
            pl.BlockSpec((TM, D), lambda i: (jnp.maximum(i - NP_TILES, 0), 0)),
            pl.BlockSpec((1, 6, D), lambda i: (_mods_index(i), 0, 0)),
            _const_spec((D, IN_W)), _const_spec((1, QL)), _const_spec((QL, H * HEAD_PAD)),
            _const_spec((1, KVL)), _const_spec((KVL, H * QK_NOPE)), _const_spec((KVL, H * V_DIM)),
            _const_spec((1, GW)), _const_spec((1, GW)),
            _const_spec((G_GROUPS, CHUNK, CHUNK)), _const_spec((G_GROUPS, CHUNK, G_DIM)),
            rope_spec, rope_spec, rope_spec,
        ],
        out_specs=[tok(H * HEAD_PAD), tok(H * HEAD_PAD), tok(H * V_DIM), tok(GW), tok(KVL), tok(LANES)],
        out_shape=[
            jax.ShapeDtypeStruct((T, H * HEAD_PAD), BF16),
            jax.ShapeDtypeStruct((T, H * HEAD_PAD), BF16),
            jax.ShapeDtypeStruct((T, H * V_DIM), BF16),
            jax.ShapeDtypeStruct((T, GW), BF16),
            jax.ShapeDtypeStruct((T, KVL), F32),
            jax.ShapeDtypeStruct((T, LANES), F32),
        ],
        compiler_params=_cp(("arbitrary",)),
        name="proj",
    )(xc, xl, mods, win, qg, wuq, kvg, wuk, wuv, lng, lnb, ws, bsb, cos_t, sp_t, sm_t)


def _kvexp_kernel(ckv_ref, kr_ref, wuk_ref, wuv_ref, k_ref, v_ref):
    kr = kr_ref[...]
    kr128 = jnp.concatenate([kr, jnp.zeros_like(kr)], axis=1)
    _expand_kv(ckv_ref[...], kr128, wuk_ref, wuv_ref, k_ref, v_ref)


def _kvexp(ckv, kr, wuk, wuv):
    rows = ckv.shape[0]
    return pl.pallas_call(
        _kvexp_kernel,
        grid=(rows // TM,),
        in_specs=[pl.BlockSpec((TM, KVL), lambda i: (i, 0)), pl.BlockSpec((TM, QK_ROPE), lambda i: (i, 0)),
                  _const_spec((KVL, H * QK_NOPE)), _const_spec((KVL, H * V_DIM))],
        out_specs=[pl.BlockSpec((TM, H * HEAD_PAD), lambda i: (i, 0)),
                   pl.BlockSpec((TM, H * V_DIM), lambda i: (i, 0))],
        out_shape=[jax.ShapeDtypeStruct((rows, H * HEAD_PAD), BF16),
                   jax.ShapeDtypeStruct((rows, H * V_DIM), BF16)],
        compiler_params=_cp(("arbitrary",)),
        name="kvexp",
    )(ckv, kr, wuk, wuv)


def _attn_ctx_kernel(q_ref, k_ref, v_ref, o_ref):
    for h in range(H):
        qh = q_ref[:, h * HEAD_PAD:(h + 1) * HEAD_PAD]
        kh = k_ref[:, h * HEAD_PAD:(h + 1) * HEAD_PAD]
        s = _dot_nt(qh, kh)
        p = jnp.exp(s - jnp.max(s, -1, keepdims=True))
        o = _dot(p.astype(BF16), v_ref[:, h * V_DIM:(h + 1) * V_DIM])
        o_ref[:, h * V_DIM:(h + 1) * V_DIM] = (o / jnp.sum(p, -1, keepdims=True)).astype(BF16)


def _attn_ctx(q, k, v):
    return pl.pallas_call(
        _attn_ctx_kernel,
        grid=(N_CTX_B,),
        in_specs=[pl.BlockSpec((N_CTX, H * HEAD_PAD), lambda b: (b, 0)),
                  pl.BlockSpec((N_CTX, H * HEAD_PAD), lambda b: (b, 0)),
                  pl.BlockSpec((N_CTX, H * V_DIM), lambda b: (b, 0))],
        out_specs=pl.BlockSpec((N_CTX, H * V_DIM), lambda b: (b, 0)),
        out_shape=jax.ShapeDtypeStruct((T_CTX, H * V_DIM), BF16),
        compiler_params=_cp(("arbitrary",)),
        name="attn_ctx",
    )(q, k, v)


def _attn_lat_kernel(q_ref, k_ref, v_ref, kc_ref, vc_ref, o_ref):
    q = q_ref[...]
    s1 = _dot_nt(q, k_ref[...])
    s2 = _dot_nt(q, kc_ref[...])
    mx = jnp.maximum(jnp.max(s1, -1, keepdims=True), jnp.max(s2, -1, keepdims=True))
    p1 = jnp.exp(s1 - mx)
    p2 = jnp.exp(s2 - mx)
    den = jnp.sum(p1, -1, keepdims=True) + jnp.sum(p2, -1, keepdims=True)
    o = _dot(p1.astype(BF16), v_ref[...]) + _dot(p2.astype(BF16), vc_ref[...])
    o_ref[...] = (o / den).astype(BF16)


def _attn_lat(q, k, v, kc, vc):
    qb0 = T_CTX // TQ
    qpb = N_LAT // TQ
    kb0 = T_CTX // N_LAT
    return pl.pallas_call(
        _attn_lat_kernel,
        grid=(N_LAT_B, H, qpb),
        in_specs=[pl.BlockSpec((TQ, HEAD_PAD), lambda b, h, qi: (qb0 + b * qpb + qi, h)),
                  pl.BlockSpec((N_LAT, HEAD_PAD), lambda b, h, qi: (kb0 + b, h)),
                  pl.BlockSpec((N_LAT, V_DIM), lambda b, h, qi: (kb0 + b, h)),
                  pl.BlockSpec((PAST, HEAD_PAD), lambda b, h, qi: (b, h)),
                  pl.BlockSpec((PAST, V_DIM), lambda b, h, qi: (b, h))],
        out_specs=pl.BlockSpec((TQ, V_DIM), lambda b, h, qi: (b * qpb + qi, h)),
        out_shape=jax.ShapeDtypeStruct((T_LAT, H * V_DIM), BF16),
        compiler_params=_cp(("arbitrary", "arbitrary", "arbitrary")),
        name="attn_lat",
    )(q, k, v, kc, vc)


def _post_kernel(ac_ref, al_ref, gm_ref, xc_ref, xl_ref, mods_ref, wout_ref, g1_ref, b1_ref, wrt_ref,
                 brt_ref, x1_ref, h2p_ref, ridx_ref, rwt_ref):
    i = pl.program_id(0)
    is_ctx = i < NP_TILES
    x = jnp.where(is_ctx, xc_ref[...], xl_ref[...])
    attn = jnp.where(is_ctx, ac_ref[...], al_ref[...])
    m = mods_ref[0]
    gate1, shift2, scale2 = m[2:3], m[3:4], m[4:5]
    mix = _dot(jnp.concatenate([attn, gm_ref[...]], axis=1), wout_ref[...])
    x1 = _layer_norm(ALPHA * x + gate1 * mix, g1_ref[...], b1_ref[...])
    x1_ref[...] = x1
    h2 = x1 * (1.0 + scale2) + shift2
    h2p_ref[...] = _pack_pair(h2[:, :D // 2], h2[:, D // 2:])

    logits = _dot(h2.astype(BF16), wrt_ref[...]) + brt_ref[...]
    lane_i = lax.broadcasted_iota(I32, logits.shape, 1)
    lane = lane_i.astype(F32)
    neg = jnp.float32(-jnp.inf)
    far = jnp.float32(LANES)
    gl = jnp.where(lane_i < N_GRP, logits, neg)
    gmax = jnp.max(gl, -1, keepdims=True)
    gidx = jnp.min(jnp.where(gl == gmax, lane, far), -1, keepdims=True)
    p_top = 1.0 / jnp.sum(jnp.exp(gl - gmax), -1, keepdims=True)
    grp_of_lane = ((lane_i - N_GRP) >> 3).astype(F32)
    in_grp = (lane_i >= N_GRP) & (lane_i < N_GRP + N_EXP) & (grp_of_lane == gidx)
    el = jnp.where(in_grp, logits, neg)
    m1 = jnp.max(el, -1, keepdims=True)
    i1 = jnp.min(jnp.where(el == m1, lane, far), -1, keepdims=True)
    el2 = jnp.where(lane == i1, neg, el)
    m2 = jnp.max(el2, -1, keepdims=True)
    i2 = jnp.min(jnp.where(el2 == m2, lane, far), -1, keepdims=True)
    e2 = jnp.exp(m2 - m1)
    w1 = p_top / (1.0 + e2)
    w2 = p_top * e2 / (1.0 + e2)
    two = lax.broadcasted_iota(I32, (logits.shape[0], TOP_K), 1)
    ridx_ref[...] = (jnp.where(two == 0, i1, i2) - N_GRP).astype(I32)
    rwt_ref[...] = jnp.where(two == 0, w1, w2)


def _post(attn_c, attn_l, gm, xc, xl, mods, wout, g1, b1, wrt, brt):
    def tok(width):
        return pl.BlockSpec((TM, width), lambda i: (i, 0))

    def ctx(width):
        return pl.BlockSpec((TM, width), lambda i: (jnp.minimum(i, NP_TILES - 1), 0))

    def lat(width):
        return pl.BlockSpec((TM, width), lambda i: (jnp.maximum(i - NP_TILES, 0), 0))

    return pl.pallas_call(
        _post_kernel,
        grid=(T // TM,),
        in_specs=[ctx(H * V_DIM), lat(H * V_DIM), tok(GW), ctx(D), lat(D),
                  pl.BlockSpec((1, 6, D), lambda i: (_mods_index(i), 0, 0)),
                  _const_spec((D, D)), _const_spec((1, D)), _const_spec((1, D)),
                  _const_spec((D, LANES)), _const_spec((1, LANES))],
        out_specs=[tok(D), tok(D // 2), tok(TOP_K), tok(TOP_K)],
        out_shape=[jax.ShapeDtypeStruct((T, D), F32),
                   jax.ShapeDtypeStruct((T, D // 2), U32),
                   jax.ShapeDtypeStruct((T, TOP_K), I32),
                   jax.ShapeDtypeStruct((T, TOP_K), F32)],
        compiler_params=_cp(("arbitrary",)),
        name="post",
    )(attn_c, attn_l, gm, xc, xl, mods, wout, g1, b1, wrt, brt)


def _row_copy(src, src_row, dst, dst_row, sem):
    return pltpu.make_async_copy(src.at[pl.ds(src_row, 1)], dst.at[pl.ds(dst_row, 1)], sem)


def _dispatch_kernel(dest_ref, h2p_ref, zeros_ref, xs_ref, sems):
    del zeros_ref
    i = pl.program_id(0)
    n = pl.num_programs(0)
    slot = i % 2

    def issue(r, carry):
        for k in range(TOP_K):
            _row_copy(h2p_ref, i * TM + r, xs_ref, dest_ref[0, 0, TOP_K * r + k], sems.at[slot]).start()
        return carry

    lax.fori_loop(0, TM, issue, 0, unroll=8)

    def wait_step(s):
        pltpu.make_async_copy(h2p_ref.at[pl.ds(0, TOP_K * TM)], xs_ref.at[pl.ds(0, TOP_K * TM)],
                              sems.at[s]).wait()

    @pl.when(i > 0)
    def _():
        wait_step(1 - slot)

    @pl.when(i == n - 1)
    def _():
        wait_step(slot)


def _dispatch(dest3, h2p):
    return pl.pallas_call(
        _dispatch_kernel,
        grid=(T // TM,),
        in_specs=[pl.BlockSpec((1, 1, TOP_K * TM), lambda i: (i, 0, 0), memory_space=pltpu.SMEM),
                  pl.BlockSpec(memory_space=pl.ANY), pl.BlockSpec(memory_space=pl.ANY)],
        out_specs=pl.BlockSpec(memory_space=pl.ANY),
        out_shape=jax.ShapeDtypeStruct((N_BLOCKS * BM, D // 2), U32),
        input_output_aliases={2: 0},
        scratch_shapes=[pltpu.SemaphoreType.DMA((2,))],
        compiler_params=_cp(("arbitrary",)),
        name="dispatch",
    )(dest3, h2p, jnp.zeros((N_BLOCKS * BM, D // 2), U32))


def _experts_kernel(be_ref, bv_ref, x_ref, w1_ref, w3_ref, w2_ref, y_ref, w1b, w3b, w2b):
    i = pl.program_id(0)
    e = be_ref[i]
    nv = bv_ref[i]
    prev = be_ref[jnp.maximum(i - 1, 0)]

    @pl.when((i == 0) | (e != prev))
    def _():
        w1b[...] = w1_ref[0].astype(BF16)
        w3b[...] = w3_ref[0].astype(BF16)
        w2b[...] = w2_ref[0].astype(BF16)

    @pl.when(nv > 0)
    def _():
        lo, hi = _unpack_pair(x_ref[...])
        xb = jnp.concatenate([lo.astype(BF16), hi.astype(BF16)], axis=1)
        a = _dot(xb, w1b[...])
        b = _dot(xb, w3b[...])
        hid = (jax.nn.silu(a) * b).astype(BF16)
        y = _dot(hid, w2b[...])
        y_ref[...] = _pack_pair(y[:, :D // 2], y[:, D // 2:])

    @pl.when(nv == 0)
    def _():
        y_ref[...] = jnp.zeros_like(y_ref)


def _experts(blk_expert, blk_valid, xs, w1, w3, w2):
    grid_spec = pltpu.PrefetchScalarGridSpec(
        num_scalar_prefetch=2,
        grid=(N_BLOCKS,),
        in_specs=[pl.BlockSpec((BM, D // 2), lambda i, be, bv: (i, 0)),
                  pl.BlockSpec((1, D, E_HID), lambda i, be, bv: (be[i], 0, 0)),
                  pl.BlockSpec((1, D, E_HID), lambda i, be, bv: (be[i], 0, 0)),
                  pl.BlockSpec((1, E_HID, D), lambda i, be, bv: (be[i], 0, 0))],
        out_specs=pl.BlockSpec((BM, D // 2), lambda i, be, bv: (i, 0)),
        scratch_shapes=[pltpu.VMEM((D, E_HID), BF16), pltpu.VMEM((D, E_HID), BF16),
                        pltpu.VMEM((E_HID, D), BF16)],
    )
    return pl.pallas_call(
        _experts_kernel,
        grid_spec=grid_spec,
        out_shape=jax.ShapeDtypeStruct((N_BLOCKS * BM, D // 2), U32),
        compiler_params=_cp(("arbitrary",)),
        name="experts",
    )(blk_expert, blk_valid, xs, w1, w3, w2)


def _final_kernel(dcur_ref, dnext_ref, ys_ref, x1_ref, rwt_ref, mods_ref, g2_ref, b2_ref, o_ref, buf, sems):
    i = pl.program_id(0)
    n = pl.num_programs(0)
    slot = i % 2

    def issue(d_ref, s):
        def body(r, carry):
            for k in range(TOP_K):
                _row_copy(ys_ref, d_ref[0, 0, TOP_K * r + k], buf.at[s, k], r, sems.at[s]).start()
            return carry

        lax.fori_loop(0, TM, body, 0, unroll=8)

    @pl.when(i == 0)
    def _():
        issue(dcur_ref, 0)

    @pl.when(i + 1 < n)
    def _():
        issue(dnext_ref, 1 - slot)

    for k in range(TOP_K):
        pltpu.make_async_copy(ys_ref.at[pl.ds(0, TM)], buf.at[slot, k], sems.at[slot]).wait()

    w = rwt_ref[...]
    lo0, hi0 = _unpack_pair(buf[slot, 0])
    lo1, hi1 = _unpack_pair(buf[slot, 1])
    w0, w1 = w[:, 0:1], w[:, 1:2]
    moe = jnp.concatenate([lo0 * w0 + lo1 * w1, hi0 * w0 + hi1 * w1], axis=1)
    gate2 = mods_ref[0][5:6]
    o_ref[...] = _layer_norm(ALPHA * x1_ref[...] + gate2 * moe, g2_ref[...], b2_ref[...])


def _final(dest3, ys, x1, rwt, mods, g2, b2):
    n = T // TM
    return pl.pallas_call(
        _final_kernel,
        grid=(n,),
        in_specs=[pl.BlockSpec((1, 1, TOP_K * TM), lambda i: (i, 0, 0), memory_space=pltpu.SMEM),
                  pl.BlockSpec((1, 1, TOP_K * TM), lambda i: (jnp.minimum(i + 1, n - 1), 0, 0),
                               memory_space=pltpu.SMEM),
                  pl.BlockSpec(memory_space=pl.ANY),
                  pl.BlockSpec((TM, D), lambda i: (i, 0)),
                  pl.BlockSpec((TM, TOP_K), lambda i: (i, 0)),
                  pl.BlockSpec((1, 6, D), lambda i: (_mods_index(i), 0, 0)),
                  _const_spec((1, D)), _const_spec((1, D))],
        out_specs=pl.BlockSpec((TM, D), lambda i: (i, 0)),
        out_shape=jax.ShapeDtypeStruct((T, D), F32),
        scratch_shapes=[pltpu.VMEM((2, TOP_K, TM, D // 2), U32), pltpu.SemaphoreType.DMA((2,))],
        compiler_params=_cp(("arbitrary",)),
        name="final",
    )(dest3, dest3, ys, x1, rwt, mods, g2, b2)


def _rope_tables():
    n = N_LAT
    rows = n // GRID_W
    row = jnp.repeat(jnp.arange(rows, dtype=F32), GRID_W)
    col = jnp.tile(jnp.arange(GRID_W, dtype=F32), rows)
    half = QK_ROPE // 2
    inv = ROPE_THETA ** (-jnp.arange(0, half, 2, dtype=F32) / half)
    cr, sr = jnp.cos(row[:, None] * inv), jnp.sin(row[:, None] * inv)
    cc, sc = jnp.cos(col[:, None] * inv), jnp.sin(col[:, None] * inv)
    z16 = jnp.zeros_like(cr)
    pad = lambda v: jnp.zeros((n, LANES - QK_ROPE), F32) + v
    cos = jnp.concatenate([cr, cr, cc, cc, pad(1.0)], axis=1)
    s_plus = jnp.concatenate([z16, sr, z16, sc, pad(0.0)], axis=1)
    s_minus = jnp.concatenate([-sr, z16, -sc, z16, pad(0.0)], axis=1)
    ident = lambda v: jnp.full((TM, LANES), v, F32)
    return (jnp.concatenate([cos, ident(1.0)], 0), jnp.concatenate([s_plus, ident(0.0)], 0),
            jnp.concatenate([s_minus, ident(0.0)], 0))


def _block_plan(ridx):
    e_flat = ridx.reshape(-1)
    onehot = (e_flat[:, None] == jnp.arange(N_EXP, dtype=I32)[None, :]).astype(I32)
    csum = jnp.cumsum(onehot, axis=0)
    rank = jnp.sum(onehot * csum, axis=1) - 1
    counts = csum[-1]
    nblk = (counts + BM - 1) // BM
    bends = jnp.cumsum(nblk)
    bstarts = bends - nblk
    dest = bstarts[e_flat] * BM + rank
    blk = jnp.arange(N_BLOCKS, dtype=I32)
    blk_expert = jnp.clip(jnp.searchsorted(bends, blk, side="right"), 0, N_EXP - 1).astype(I32)
    left = counts[blk_expert] - (blk - bstarts[blk_expert]) * BM
    blk_valid = jnp.where(blk < bends[-1], jnp.clip(left, 0, BM), 0).astype(I32)
    return dest.astype(I32).reshape(T // TM, 1, TOP_K * TM), blk_expert, blk_valid


def kernel(x_prompt, x_sample, cache_ckv, cache_krope, c, c_ctx, w_ada, b_ada, w_in, q_norm_g, w_uq,
           kv_norm_g, w_ukv, gmlp_ln_g, gmlp_ln_b, w_spatial, b_spatial, w_out, ln1_g, ln1_b, w_group,
           b_group, w_router, b_router, w1, w3, w2, ln2_g, ln2_b):
    l = 0
    xc = x_prompt.reshape(T_CTX, D)
    xl = x_sample.reshape(T_LAT, D)

    cond16 = jnp.concatenate([c, c_ctx[None, :], jnp.zeros((16 - N_LAT_B - 1, D), F32)], axis=0)
    mods = _adaln(cond16, w_ada[l], b_ada[l][None, :]).reshape(16, 6, D)

    wi = w_in[l]
    o_kr = QL + KVL
    o_u = o_kr + QK_ROPE
    win = jnp.concatenate([wi[:, :o_kr], wi[:, o_u:], wi[:, o_kr:o_u],
                           jnp.zeros((D, LANES - QK_ROPE), F32)], axis=1).astype(BF16)
    wuq = jnp.pad(w_uq[l].reshape(QL, H, QK_NOPE + QK_ROPE),
                  ((0, 0), (0, 0), (0, HEAD_PAD - QK_NOPE - QK_ROPE))).reshape(QL, H * HEAD_PAD).astype(BF16)
    wkv = w_ukv[l].reshape(KVL, H, QK_NOPE + V_DIM)
    wuk = wkv[:, :, :QK_NOPE].reshape(KVL, H * QK_NOPE).astype(BF16)
    wuv = wkv[:, :, QK_NOPE:].reshape(KVL, H * V_DIM).astype(BF16)
    ws = w_spatial[l].astype(BF16)
    bsb = jnp.broadcast_to(b_spatial[l][:, :, None], (G_GROUPS, CHUNK, G_DIM))
    wrt = jnp.concatenate([w_group[l], w_router[l], jnp.zeros((D, LANES - N_GRP - N_EXP), F32)],
                          axis=1).astype(BF16)
    brt = jnp.concatenate([b_group[l], b_router[l], jnp.zeros((LANES - N_GRP - N_EXP,), F32)])[None, :]
    cos_t, sp_t, sm_t = _rope_tables()
    row = lambda v: v[l][None, :]

    q, k, v, gm, ckv_n, kr = _proj(xc, xl, mods, win, row(q_norm_g), wuq, row(kv_norm_g), wuk, wuv,
                                   row(gmlp_ln_g), row(gmlp_ln_b), ws, bsb, cos_t, sp_t, sm_t)
    kc, vc = _kvexp(cache_ckv[:, l].reshape(N_LAT_B * PAST, KVL),
                    cache_krope[:, l].reshape(N_LAT_B * PAST, QK_ROPE), wuk, wuv)
    attn_c = _attn_ctx(q, k, v)
    attn_l = _attn_lat(q, k, v, kc, vc)

    x1, h2p, ridx, rwt = _post(attn_c, attn_l, gm, xc, xl, mods, w_out[l].astype(BF16), row(ln1_g), row(ln1_b),
                               wrt, brt)
    dest3, blk_expert, blk_valid = _block_plan(ridx)
    xs = _dispatch(dest3, h2p)
    ys = _experts(blk_expert, blk_valid, xs, w1[l], w3[l], w2[l])
    y = _final(dest3, ys, x1, rwt, mods, row(ln2_g), row(ln2_b))

    y_prompt = y[:T_CTX].reshape(N_CTX_B, N_CTX, D)
    y_sample = y[T_CTX:].reshape(N_LAT_B, N_LAT, D)
    state_ckv = ckv_n[:T_CTX].reshape(N_CTX_B, 1, N_CTX, KVL)
    state_krope = kr[:T_CTX, :QK_ROPE].reshape(N_CTX_B, 1, N_CTX, QK_ROPE)
    return (y_prompt, y_sample, state_ckv, state_krope)
```

```python
import functools

import jax
import jax.numpy as jnp
import numpy as np
from jax import lax
from jax.experimental import pallas as pl
from jax.experimental.pallas import tpu as pltpu

F32 = jnp.float32
BF16 = jnp.bfloat16
U32 = jnp.uint32
I32 = jnp.int32

D = 2048
N_CTX_B, N_CTX = 32, 256
N_LAT_B, N_LAT = 4, 4096
PAST = 512
GRID_W = 64
H = 8
QK_NOPE, QK_ROPE, V_DIM = 128, 64, 128
QL, KVL = 512, 512
GW = 1024
G_GROUPS, G_DIM, CHUNK = 8, 128, 128
N_GRP, E_PER_GRP, N_EXP, TOP_K = 4, 8, 32, 2
E_HID = 512
ROPE_THETA = 10000.0
EPS = 1e-6
ALPHA = 2.0 ** 0.25
SM_SCALE = (QK_NOPE + QK_ROPE) ** -0.5

T_CTX = N_CTX_B * N_CTX
T_LAT = N_LAT_B * N_LAT
T = T_CTX + T_LAT
N_ASSIGN = T * TOP_K

LANES = 128
HEAD_PAD = 256
TM = 256
TD = 1024
TQ = 512
BM = 256
N_BLOCKS = N_ASSIGN // BM + N_EXP
VMEM_LIMIT = 56 * 1024 * 1024

NP_TILES = T_CTX // TM
LAT_TILES_PER_B = N_LAT // TM
IN_W = QL + KVL + 2 * GW + LANES


def _cp(sem):
    return pltpu.CompilerParams(dimension_semantics=sem, vmem_limit_bytes=VMEM_LIMIT)


def _const_spec(shape):
    nd = len(shape)
    return pl.BlockSpec(shape, lambda *a: (0,) * nd, pipeline_mode=pl.Buffered(1))


def _dot(a, b):
    return jnp.dot(a, b, preferred_element_type=F32)


def _dot_nt(a, b):
    return lax.dot_general(a, b, (((1,), (1,)), ((), ())), preferred_element_type=F32)


def _layer_norm(x, g, b):
    mu = jnp.mean(x, -1, keepdims=True)
    xc = x - mu
    var = jnp.mean(xc * xc, -1, keepdims=True)
    return xc * lax.rsqrt(var + EPS) * g + b


def _rms_norm(x, g):
    return x * lax.rsqrt(jnp.mean(x * x, -1, keepdims=True) + EPS) * g


def _gelu(x):
    return 0.5 * x * (1.0 + lax.erf(x * np.float32(np.sqrt(0.5))))


def _pack_pair(lo, hi):
    lo_b = lax.bitcast_convert_type(lo.astype(BF16).astype(F32), U32)
    hi_b = lax.bitcast_convert_type(hi.astype(BF16).astype(F32), U32)
    return hi_b | (lo_b >> 16)


def _unpack_pair(w):
    lo = lax.bitcast_convert_type(w << 16, F32)
    hi = lax.bitcast_convert_type(w & jnp.uint32(0xFFFF0000), F32)
    return lo, hi


def _adaln_kernel(cond_ref, w_ref, b_ref, o_ref):
    s = jax.nn.silu(cond_ref[...])
    s_hi = s.astype(BF16)
    s_lo = (s - s_hi.astype(F32)).astype(BF16)
    w = w_ref[...]
    w_hi = w.astype(BF16)
    w_lo = (w - w_hi.astype(F32)).astype(BF16)
    o_ref[...] = _dot(s_hi, w_hi) + _dot(s_lo, w_hi) + _dot(s_hi, w_lo) + b_ref[...]


def _adaln(cond16, w_ada, b_ada):
    tn = 1024
    n = w_ada.shape[1]
    return pl.pallas_call(
        _adaln_kernel,
        grid=(n // tn,),
        in_specs=[pl.BlockSpec((16, D), lambda j: (0, 0)),
                  pl.BlockSpec((D, tn), lambda j: (0, j)),
                  pl.BlockSpec((1, tn), lambda j: (0, j))],
        out_specs=pl.BlockSpec((16, tn), lambda j: (0, j)),
        out_shape=jax.ShapeDtypeStruct((16, n), F32),
        compiler_params=_cp(("arbitrary",)),
        name="adaln",
    )(cond16, w_ada, b_ada)


def _expand_kv(ckv_n, kr128, wuk_ref, wuv_ref, k_ref, v_ref):
    cb = ckv_n.astype(BF16)
    kn = _dot(cb, wuk_ref[...])
    v_ref[...] = _dot(cb, wuv_ref[...]).astype(BF16)
    krb = kr128.astype(BF16)
    for h in range(H):
        k_ref[:, h * HEAD_PAD:h * HEAD_PAD + QK_NOPE] = kn[:, h * QK_NOPE:(h + 1) * QK_NOPE].astype(BF16)
        k_ref[:, h * HEAD_PAD + QK_NOPE:(h + 1) * HEAD_PAD] = krb


def _rope128(x, cos, s_plus, s_minus):
    return (x * cos + pltpu.roll(x, 16, 1) * s_plus + pltpu.roll(x, LANES - 16, 1) * s_minus)


def _proj_kernel(xc_ref, xl_ref, mods_ref, win_ref, qg_ref, wuq_ref, kvg_ref, wuk_ref, wuv_ref,
                 lng_ref, lnb_ref, ws_ref, bsb_ref, cos_ref, sp_ref, sm_ref,
                 q_ref, k_ref, v_ref, gm_ref, ckv_ref, kr_ref):
    i = pl.program_id(0)
    x = jnp.where(i < NP_TILES, xc_ref[...], xl_ref[...])
    m = mods_ref[0]
    shift1, scale1 = m[0:1], m[1:2]
    hmod = (x * (1.0 + scale1) + shift1).astype(BF16)
    proj = _dot(hmod, win_ref[...])
    cq = proj[:, 0:QL]
    ckv = proj[:, QL:QL + KVL]
    u = proj[:, QL + KVL:QL + KVL + GW]
    v = proj[:, QL + KVL + GW:QL + KVL + 2 * GW]
    kr = proj[:, QL + KVL + 2 * GW:IN_W]

    cos, s_plus, s_minus = cos_ref[...], sp_ref[...], sm_ref[...]

    cq_n = _rms_norm(cq, qg_ref[...]).astype(BF16)
    q = _dot(cq_n, wuq_ref[...]) * SM_SCALE
    for h in range(H):
        q_ref[:, h * HEAD_PAD:h * HEAD_PAD + QK_NOPE] = q[:, h * HEAD_PAD:h * HEAD_PAD + QK_NOPE].astype(BF16)
        q_ref[:, h * HEAD_PAD + QK_NOPE:(h + 1) * HEAD_PAD] = _rope128(
            q[:, h * HEAD_PAD + QK_NOPE:(h + 1) * HEAD_PAD], cos, s_plus, s_minus).astype(BF16)

    ckv_n = _rms_norm(ckv, kvg_ref[...])

    @pl.when(i < NP_TILES)
    def _():
        ckv_ref[...] = ckv_n
        kr_ref[...] = kr[:, :QK_ROPE]

    _expand_kv(ckv_n, _rope128(kr, cos, s_plus, s_minus), wuk_ref, wuv_ref, k_ref, v_ref)

    gu = _gelu(u)
    vn = _layer_norm(_gelu(v), lng_ref[...], lnb_ref[...]).astype(BF16)
    for c in range(TM // CHUNK):
        rows = slice(c * CHUNK, (c + 1) * CHUNK)
        for g in range(G_GROUPS):
            cols = slice(g * G_DIM, (g + 1) * G_DIM)
            s = _dot(ws_ref[g], vn[rows, cols]) + bsb_ref[g]
            gm_ref[rows, cols] = (gu[rows, cols] * s).astype(BF16)


def _mods_index(i):
    return jnp.where(i < NP_TILES, N_LAT_B, (i - NP_TILES) // LAT_TILES_PER_B)


def _proj(xc, xl, mods, win, qg, wuq, kvg, wuk, wuv, lng, lnb, ws, bsb, cos_t, sp_t, sm_t):
    def tok(width):
        return pl.BlockSpec((TM, width), lambda i: (i, 0))

    rope_spec = pl.BlockSpec(
        (TM, LANES),
        lambda i: (jnp.where(i < NP_TILES, LAT_TILES_PER_B, (i - NP_TILES) % LAT_TILES_PER_B), 0))
    return pl.pallas_call(
        _proj_kernel,
        grid=(T // TM,),
        in_specs=[
            pl.BlockSpec((TM, D), lambda i: (jnp.minimum(i, NP_TILES - 1), 0)),
            pl.BlockSpec((TM, D), lambda i: (jnp.maximum(i - NP_TILES, 0), 0)),
            pl.BlockSpec((1, 6, D), lambda i: (_mods_index(i), 0, 0)),
            _const_spec((D, IN_W)), _const_spec((1, QL)), _const_spec((QL, H * HEAD_PAD)),
            _const_spec((1, KVL)), _const_spec((KVL, H * QK_NOPE)), _const_spec((KVL, H * V_DIM)),
            _const_spec((1, GW)), _const_spec((1, GW)),
            _const_spec((G_GROUPS, CHUNK, CHUNK)), _const_spec((G_GROUPS, CHUNK, G_DIM)),
            rope_spec, rope_spec, rope_spec,
        ],
        out_specs=[tok(H * HEAD_PAD), tok(H * HEAD_PAD), tok(H * V_DIM), tok(GW),
                   pl.BlockSpec((TM, KVL), lambda i: (jnp.minimum(i, NP_TILES - 1), 0)),
                   pl.BlockSpec((TM, QK_ROPE), lambda i: (jnp.minimum(i, NP_TILES - 1), 0))],
        out_shape=[
            jax.ShapeDtypeStruct((T, H * HEAD_PAD), BF16),
            jax.ShapeDtypeStruct((T, H * HEAD_PAD), BF16),
            jax.ShapeDtypeStruct((T, H * V_DIM), BF16),
            jax.ShapeDtypeStruct((T, GW), BF16),
            jax.ShapeDtypeStruct((T_CTX, KVL), F32),
            jax.ShapeDtypeStruct((T_CTX, QK_ROPE), F32),
        ],
        compiler_params=_cp(("arbitrary",)),
        name="proj",
    )(xc, xl, mods, win, qg, wuq, kvg, wuk, wuv, lng, lnb, ws, bsb, cos_t, sp_t, sm_t)


def _kvexp_kernel(ckv_ref, kr_ref, wuk_ref, wuv_ref, k_ref, v_ref):
    kr = kr_ref[...]
    kr128 = jnp.concatenate([kr, jnp.zeros_like(kr)], axis=1)
    _expand_kv(ckv_ref[...], kr128, wuk_ref, wuv_ref, k_ref, v_ref)


def _kvexp(ckv, kr, wuk, wuv):
    rows = ckv.shape[0]
    return pl.pallas_call(
        _kvexp_kernel,
        grid=(rows // TM,),
        in_specs=[pl.BlockSpec((TM, KVL), lambda i: (i, 0)), pl.BlockSpec((TM, QK_ROPE), lambda i: (i, 0)),
                  _const_spec((KVL, H * QK_NOPE)), _const_spec((KVL, H * V_DIM))],
        out_specs=[pl.BlockSpec((TM, H * HEAD_PAD), lambda i: (i, 0)),
                   pl.BlockSpec((TM, H * V_DIM), lambda i: (i, 0))],
        out_shape=[jax.ShapeDtypeStruct((rows, H * HEAD_PAD), BF16),
                   jax.ShapeDtypeStruct((rows, H * V_DIM), BF16)],
        compiler_params=_cp(("arbitrary",)),
        name="kvexp",
    )(ckv, kr, wuk, wuv)


def _attn_ctx_kernel(q_ref, k_ref, v_ref, o_ref):
    for h in range(H):
        qh = q_ref[:, h * HEAD_PAD:(h + 1) * HEAD_PAD]
        kh = k_ref[:, h * HEAD_PAD:(h + 1) * HEAD_PAD]
        s = _dot_nt(qh, kh)
        p = jnp.exp(s - jnp.max(s, -1, keepdims=True))
        o = _dot(p.astype(BF16), v_ref[:, h * V_DIM:(h + 1) * V_DIM])
        o_ref[:, h * V_DIM:(h + 1) * V_DIM] = (o / jnp.sum(p, -1, keepdims=True)).astype(BF16)


def _attn_ctx(q, k, v):
    return pl.pallas_call(
        _attn_ctx_kernel,
        grid=(N_CTX_B,),
        in_specs=[pl.BlockSpec((N_CTX, H * HEAD_PAD), lambda b: (b, 0)),
                  pl.BlockSpec((N_CTX, H * HEAD_PAD), lambda b: (b, 0)),
                  pl.BlockSpec((N_CTX, H * V_DIM), lambda b: (b, 0))],
        out_specs=pl.BlockSpec((N_CTX, H * V_DIM), lambda b: (b, 0)),
        out_shape=jax.ShapeDtypeStruct((T_CTX, H * V_DIM), BF16),
        compiler_params=_cp(("arbitrary",)),
        name="attn_ctx",
    )(q, k, v)


def _attn_lat_kernel(q_ref, k_ref, v_ref, kc_ref, vc_ref, o_ref):
    q = q_ref[...]
    s1 = _dot_nt(q, k_ref[...])
    s2 = _dot_nt(q, kc_ref[...])
    mx = jnp.maximum(jnp.max(s1, -1, keepdims=True), jnp.max(s2, -1, keepdims=True))
    p1 = jnp.exp(s1 - mx)
    p2 = jnp.exp(s2 - mx)
    den = jnp.sum(p1, -1, keepdims=True) + jnp.sum(p2, -1, keepdims=True)
    o = _dot(p1.astype(BF16), v_ref[...]) + _dot(p2.astype(BF16), vc_ref[...])
    o_ref[...] = (o / den).astype(BF16)


def _attn_lat(q, k, v, kc, vc):
    qb0 = T_CTX // TQ
    qpb = N_LAT // TQ
    kb0 = T_CTX // N_LAT
    return pl.pallas_call(
        _attn_lat_kernel,
        grid=(N_LAT_B, H, qpb),
        in_specs=[pl.BlockSpec((TQ, HEAD_PAD), lambda b, h, qi: (qb0 + b * qpb + qi, h)),
                  pl.BlockSpec((N_LAT, HEAD_PAD), lambda b, h, qi: (kb0 + b, h)),
                  pl.BlockSpec((N_LAT, V_DIM), lambda b, h, qi: (kb0 + b, h)),
                  pl.BlockSpec((PAST, HEAD_PAD), lambda b, h, qi: (b, h)),
                  pl.BlockSpec((PAST, V_DIM), lambda b, h, qi: (b, h))],
        out_specs=pl.BlockSpec((TQ, V_DIM), lambda b, h, qi: (b * qpb + qi, h)),
        out_shape=jax.ShapeDtypeStruct((T_LAT, H * V_DIM), BF16),
        compiler_params=_cp(("arbitrary", "arbitrary", "arbitrary")),
        name="attn_lat",
    )(q, k, v, kc, vc)


def _post_kernel(ac_ref, al_ref, gm_ref, xc_ref, xl_ref, mods_ref, wout_ref, g1_ref, b1_ref, wrt_ref,
                 brt_ref, x1_ref, h2p_ref, ridx_ref, rwt_ref):
    i = pl.program_id(0)
    is_ctx = i < NP_TILES
    x = jnp.where(is_ctx, xc_ref[...], xl_ref[...])
    attn = jnp.where(is_ctx, ac_ref[...], al_ref[...])
    m = mods_ref[0]
    gate1, shift2, scale2 = m[2:3], m[3:4], m[4:5]
    mix = _dot(jnp.concatenate([attn, gm_ref[...]], axis=1), wout_ref[...])
    x1 = _layer_norm(ALPHA * x + gate1 * mix, g1_ref[...], b1_ref[...])
    x1_ref[...] = x1
    h2 = x1 * (1.0 + scale2) + shift2
    h2p_ref[...] = _pack_pair(h2[:, :D // 2], h2[:, D // 2:])

    logits = _dot(h2.astype(BF16), wrt_ref[...]) + brt_ref[...]
    lane_i = lax.broadcasted_iota(I32, logits.shape, 1)
    lane = lane_i.astype(F32)
    neg = jnp.float32(-jnp.inf)
    far = jnp.float32(LANES)
    gl = jnp.where(lane_i < N_GRP, logits, neg)
    gmax = jnp.max(gl, -1, keepdims=True)
    gidx = jnp.min(jnp.where(gl == gmax, lane, far), -1, keepdims=True)
    p_top = 1.0 / jnp.sum(jnp.exp(gl - gmax), -1, keepdims=True)
    grp_of_lane = ((lane_i - N_GRP) >> 3).astype(F32)
    in_grp = (lane_i >= N_GRP) & (lane_i < N_GRP + N_EXP) & (grp_of_lane == gidx)
    el = jnp.where(in_grp, logits, neg)
    m1 = jnp.max(el, -1, keepdims=True)
    i1 = jnp.min(jnp.where(el == m1, lane, far), -1, keepdims=True)
    el2 = jnp.where(lane == i1, neg, el)
    m2 = jnp.max(el2, -1, keepdims=True)
    i2 = jnp.min(jnp.where(el2 == m2, lane, far), -1, keepdims=True)
    e2 = jnp.exp(m2 - m1)
    w1 = p_top / (1.0 + e2)
    w2 = p_top * e2 / (1.0 + e2)
    two = lax.broadcasted_iota(I32, (logits.shape[0], TOP_K), 1)
    ridx_ref[...] = (jnp.where(two == 0, i1, i2) - N_GRP).astype(I32)
    rwt_ref[...] = jnp.where(two == 0, w1, w2)


def _post(attn_c, attn_l, gm, xc, xl, mods, wout, g1, b1, wrt, brt):
    def tok(width):
        return pl.BlockSpec((TM, width), lambda i: (i, 0))

    def ctx(width):
        return pl.BlockSpec((TM, width), lambda i: (jnp.minimum(i, NP_TILES - 1), 0))

    def lat(width):
        return pl.BlockSpec((TM, width), lambda i: (jnp.maximum(i - NP_TILES, 0), 0))

    return pl.pallas_call(
        _post_kernel,
        grid=(T // TM,),
        in_specs=[ctx(H * V_DIM), lat(H * V_DIM), tok(GW), ctx(D), lat(D),
                  pl.BlockSpec((1, 6, D), lambda i: (_mods_index(i), 0, 0)),
                  _const_spec((D, D)), _const_spec((1, D)), _const_spec((1, D)),
                  _const_spec((D, LANES)), _const_spec((1, LANES))],
        out_specs=[tok(D), tok(D // 2), tok(TOP_K), tok(TOP_K)],
        out_shape=[jax.ShapeDtypeStruct((T, D), F32),
                   jax.ShapeDtypeStruct((T, D // 2), U32),
                   jax.ShapeDtypeStruct((T, TOP_K), I32),
                   jax.ShapeDtypeStruct((T, TOP_K), F32)],
        compiler_params=_cp(("arbitrary",)),
        name="post",
    )(attn_c, attn_l, gm, xc, xl, mods, wout, g1, b1, wrt, brt)


def _row_copy(src, src_row, dst, dst_row, sem):
    return pltpu.make_async_copy(src.at[pl.ds(src_row, 1)], dst.at[pl.ds(dst_row, 1)], sem)


def _dispatch_kernel(dest_ref, h2p_ref, zeros_ref, xs_ref, sem):
    del zeros_ref

    def issue(r, carry):
        for k in range(TOP_K):
            _row_copy(h2p_ref, r, xs_ref, dest_ref[0, 0, TOP_K * r + k], sem).start()
        return carry

    lax.fori_loop(0, TD, issue, 0, unroll=8)
    for _ in range(TOP_K):
        pltpu.make_async_copy(h2p_ref, xs_ref.at[pl.ds(0, TD)], sem).wait()


def _dispatch(dest, h2p):
    return pl.pallas_call(
        _dispatch_kernel,
        grid=(T // TD,),
        in_specs=[pl.BlockSpec((1, 1, TOP_K * TD), lambda i: (i, 0, 0), memory_space=pltpu.SMEM),
                  pl.BlockSpec((TD, D // 2), lambda i: (i, 0)), pl.BlockSpec(memory_space=pl.ANY)],
        out_specs=pl.BlockSpec(memory_space=pl.ANY),
        out_shape=jax.ShapeDtypeStruct((N_BLOCKS * BM, D // 2), U32),
        input_output_aliases={2: 0},
        scratch_shapes=[pltpu.SemaphoreType.DMA(())],
        compiler_params=_cp(("arbitrary",)),
        name="dispatch",
    )(dest.reshape(T // TD, 1, TOP_K * TD), h2p, jnp.zeros((N_BLOCKS * BM, D // 2), U32))


def _experts_kernel(be_ref, bv_ref, x_ref, w1_ref, w3_ref, w2_ref, y_ref, w1b, w3b, w2b):
    i = pl.program_id(0)
    e = be_ref[i]
    nv = bv_ref[i]
    prev = be_ref[jnp.maximum(i - 1, 0)]

    @pl.when((i == 0) | (e != prev))
    def _():
        w1b[...] = w1_ref[0].astype(BF16)
        w3b[...] = w3_ref[0].astype(BF16)
        w2b[...] = w2_ref[0].astype(BF16)

    @pl.when(nv > 0)
    def _():
        lo, hi = _unpack_pair(x_ref[...])
        xb = jnp.concatenate([lo.astype(BF16), hi.astype(BF16)], axis=1)
        a = _dot(xb, w1b[...])
        b = _dot(xb, w3b[...])
        hid = (jax.nn.silu(a) * b).astype(BF16)
        y = _dot(hid, w2b[...])
        y_ref[...] = _pack_pair(y[:, :D // 2], y[:, D // 2:])

    @pl.when(nv == 0)
    def _():
        y_ref[...] = jnp.zeros_like(y_ref)


def _experts(blk_expert, blk_valid, xs, w1, w3, w2):
    grid_spec = pltpu.PrefetchScalarGridSpec(
        num_scalar_prefetch=2,
        grid=(N_BLOCKS,),
        in_specs=[pl.BlockSpec((BM, D // 2), lambda i, be, bv: (i, 0)),
                  pl.BlockSpec((1, D, E_HID), lambda i, be, bv: (be[i], 0, 0)),
                  pl.BlockSpec((1, D, E_HID), lambda i, be, bv: (be[i], 0, 0)),
                  pl.BlockSpec((1, E_HID, D), lambda i, be, bv: (be[i], 0, 0))],
        out_specs=pl.BlockSpec((BM, D // 2), lambda i, be, bv: (i, 0)),
        scratch_shapes=[pltpu.VMEM((D, E_HID), BF16), pltpu.VMEM((D, E_HID), BF16),
                        pltpu.VMEM((E_HID, D), BF16)],
    )
    return pl.pallas_call(
        _experts_kernel,
        grid_spec=grid_spec,
        out_shape=jax.ShapeDtypeStruct((N_BLOCKS * BM, D // 2), U32),
        compiler_params=_cp(("arbitrary",)),
        name="experts",
    )(blk_expert, blk_valid, xs, w1, w3, w2)


def _final_kernel(dcur_ref, dnext_ref, ys_ref, x1_ref, rwt_ref, mods_ref, g2_ref, b2_ref, oc_ref, ol_ref,
                  buf, sems):
    i = pl.program_id(0)
    n = pl.num_programs(0)
    slot = i % 2

    def issue(d_ref, s):
        def body(r, carry):
            for k in range(TOP_K):
                _row_copy(ys_ref, d_ref[0, 0, TOP_K * r + k], buf.at[s, k], r, sems.at[s]).start()
            return carry

        lax.fori_loop(0, TM, body, 0, unroll=8)

    @pl.when(i == 0)
    def _():
        issue(dcur_ref, 0)

    @pl.when(i + 1 < n)
    def _():
        issue(dnext_ref, 1 - slot)

    for k in range(TOP_K):
        pltpu.make_async_copy(ys_ref.at[pl.ds(0, TM)], buf.at[slot, k], sems.at[slot]).wait()

    w = rwt_ref[...]
    lo0, hi0 = _unpack_pair(buf[slot, 0])
    lo1, hi1 = _unpack_pair(buf[slot, 1])
    w0, w1 = w[:, 0:1], w[:, 1:2]
    moe = jnp.concatenate([lo0 * w0 + lo1 * w1, hi0 * w0 + hi1 * w1], axis=1)
    gate2 = mods_ref[0][5:6]
    out = _layer_norm(ALPHA * x1_ref[...] + gate2 * moe, g2_ref[...], b2_ref[...])

    @pl.when(i < NP_TILES)
    def _():
        oc_ref[...] = out

    @pl.when(i >= NP_TILES)
    def _():
        ol_ref[...] = out


def _final(dest3, ys, x1, rwt, mods, g2, b2):
    n = T // TM
    return pl.pallas_call(
        _final_kernel,
        grid=(n,),
        in_specs=[pl.BlockSpec((1, 1, TOP_K * TM), lambda i: (i, 0, 0), memory_space=pltpu.SMEM),
                  pl.BlockSpec((1, 1, TOP_K * TM), lambda i: (jnp.minimum(i + 1, n - 1), 0, 0),
                               memory_space=pltpu.SMEM),
                  pl.BlockSpec(memory_space=pl.ANY),
                  pl.BlockSpec((TM, D), lambda i: (i, 0)),
                  pl.BlockSpec((TM, TOP_K), lambda i: (i, 0)),
                  pl.BlockSpec((1, 6, D), lambda i: (_mods_index(i), 0, 0)),
                  _const_spec((1, D)), _const_spec((1, D))],
        out_specs=[pl.BlockSpec((TM, D), lambda i: (jnp.minimum(i, NP_TILES - 1), 0)),
                   pl.BlockSpec((TM, D), lambda i: (jnp.maximum(i - NP_TILES, 0), 0))],
        out_shape=[jax.ShapeDtypeStruct((T_CTX, D), F32), jax.ShapeDtypeStruct((T_LAT, D), F32)],
        scratch_shapes=[pltpu.VMEM((2, TOP_K, TM, D // 2), U32), pltpu.SemaphoreType.DMA((2,))],
        compiler_params=_cp(("arbitrary",)),
        name="final",
    )(dest3, dest3, ys, x1, rwt, mods, g2, b2)


def _rope_tables():
    n = N_LAT
    rows = n // GRID_W
    row = jnp.repeat(jnp.arange(rows, dtype=F32), GRID_W)
    col = jnp.tile(jnp.arange(GRID_W, dtype=F32), rows)
    half = QK_ROPE // 2
    inv = ROPE_THETA ** (-jnp.arange(0, half, 2, dtype=F32) / half)
    cr, sr = jnp.cos(row[:, None] * inv), jnp.sin(row[:, None] * inv)
    cc, sc = jnp.cos(col[:, None] * inv), jnp.sin(col[:, None] * inv)
    z16 = jnp.zeros_like(cr)
    pad = lambda v: jnp.zeros((n, LANES - QK_ROPE), F32) + v
    cos = jnp.concatenate([cr, cr, cc, cc, pad(1.0)], axis=1)
    s_plus = jnp.concatenate([z16, sr, z16, sc, pad(0.0)], axis=1)
    s_minus = jnp.concatenate([-sr, z16, -sc, z16, pad(0.0)], axis=1)
    ident = lambda v: jnp.full((TM, LANES), v, F32)
    return (jnp.concatenate([cos, ident(1.0)], 0), jnp.concatenate([s_plus, ident(0.0)], 0),
            jnp.concatenate([s_minus, ident(0.0)], 0))


def _block_plan(ridx):
    e_flat = ridx.reshape(-1)
    onehot = (e_flat[:, None] == jnp.arange(N_EXP, dtype=I32)[None, :]).astype(I32)
    csum = jnp.cumsum(onehot, axis=0)
    rank = jnp.sum(onehot * csum, axis=1) - 1
    counts = csum[-1]
    nblk = (counts + BM - 1) // BM
    bends = jnp.cumsum(nblk)
    bstarts = bends - nblk
    dest = bstarts[e_flat] * BM + rank
    blk = jnp.arange(N_BLOCKS, dtype=I32)
    blk_expert = jnp.minimum(jnp.sum((blk[:, None] >= bends[None, :]).astype(I32), axis=1), N_EXP - 1)
    left = counts[blk_expert] - (blk - bstarts[blk_expert]) * BM
    blk_valid = jnp.where(blk < bends[-1], jnp.clip(left, 0, BM), 0).astype(I32)
    return dest.astype(I32), blk_expert.astype(I32), blk_valid


def kernel(x_prompt, x_sample, cache_ckv, cache_krope, c, c_ctx, w_ada, b_ada, w_in, q_norm_g, w_uq,
           kv_norm_g, w_ukv, gmlp_ln_g, gmlp_ln_b, w_spatial, b_spatial, w_out, ln1_g, ln1_b, w_group,
           b_group, w_router, b_router, w1, w3, w2, ln2_g, ln2_b):
    l = 0
    xc = x_prompt.reshape(T_CTX, D)
    xl = x_sample.reshape(T_LAT, D)

    cond16 = jnp.concatenate([c, c_ctx[None, :], jnp.zeros((16 - N_LAT_B - 1, D), F32)], axis=0)
    mods = _adaln(cond16, w_ada[l], b_ada[l][None, :]).reshape(16, 6, D)

    wi = w_in[l]
    o_kr = QL + KVL
    o_u = o_kr + QK_ROPE
    win = jnp.concatenate([wi[:, :o_kr], wi[:, o_u:], wi[:, o_kr:o_u],
                           jnp.zeros((D, LANES - QK_ROPE), F32)], axis=1).astype(BF16)
    wuq = jnp.pad(w_uq[l].reshape(QL, H, QK_NOPE + QK_ROPE),
                  ((0, 0), (0, 0), (0, HEAD_PAD - QK_NOPE - QK_ROPE))).reshape(QL, H * HEAD_PAD).astype(BF16)
    wkv = w_ukv[l].reshape(KVL, H, QK_NOPE + V_DIM)
    wuk = wkv[:, :, :QK_NOPE].reshape(KVL, H * QK_NOPE).astype(BF16)
    wuv = wkv[:, :, QK_NOPE:].reshape(KVL, H * V_DIM).astype(BF16)
    ws = w_spatial[l].astype(BF16)
    bsb = jnp.broadcast_to(b_spatial[l][:, :, None], (G_GROUPS, CHUNK, G_DIM))
    wrt = jnp.concatenate([w_group[l], w_router[l], jnp.zeros((D, LANES - N_GRP - N_EXP), F32)],
                          axis=1).astype(BF16)
    brt = jnp.concatenate([b_group[l], b_router[l], jnp.zeros((LANES - N_GRP - N_EXP,), F32)])[None, :]
    cos_t, sp_t, sm_t = _rope_tables()
    row = lambda v: v[l][None, :]

    q, k, v, gm, ckv_n, kr = _proj(xc, xl, mods, win, row(q_norm_g), wuq, row(kv_norm_g), wuk, wuv,
                                   row(gmlp_ln_g), row(gmlp_ln_b), ws, bsb, cos_t, sp_t, sm_t)
    kc, vc = _kvexp(cache_ckv[:, l].reshape(N_LAT_B * PAST, KVL),
                    cache_krope[:, l].reshape(N_LAT_B * PAST, QK_ROPE), wuk, wuv)
    attn_c = _attn_ctx(q, k, v)
    attn_l = _attn_lat(q, k, v, kc, vc)

    x1, h2p, ridx, rwt = _post(attn_c, attn_l, gm, xc, xl, mods, w_out[l].astype(BF16), row(ln1_g), row(ln1_b),
                               wrt, brt)
    dest, blk_expert, blk_valid = _block_plan(ridx)
    xs = _dispatch(dest, h2p)
    ys = _experts(blk_expert, blk_valid, xs, w1[l], w3[l], w2[l])
    y_c, y_l = _final(dest.reshape(T // TM, 1, TOP_K * TM), ys, x1, rwt, mods, row(ln2_g), row(ln2_b))

    return (y_c.reshape(N_CTX_B, N_CTX, D), y_l.reshape(N_LAT_B, N_LAT, D),
            ckv_n.reshape(N_CTX_B, 1, N_CTX, KVL), kr.reshape(N_CTX_B, 1, N_CTX, QK_ROPE))
```

```python
import functools

import jax
import jax.numpy as jnp
import numpy as np
from jax import lax
from jax.experimental import pallas as pl
from jax.experimental.pallas import tpu as pltpu

F32 = jnp.float32
BF16 = jnp.bfloat16
U32 = jnp.uint32
I32 = jnp.int32

D = 2048
N_CTX_B, N_CTX = 32, 256
N_LAT_B, N_LAT = 4, 4096
PAST = 512
GRID_W = 64
H = 8
QK_NOPE, QK_ROPE, V_DIM = 128, 64, 128
QL, KVL = 512, 512
GW = 1024
G_GROUPS, G_DIM, CHUNK = 8, 128, 128
N_GRP, E_PER_GRP, N_EXP, TOP_K = 4, 8, 32, 2
E_HID = 512
ROPE_THETA = 10000.0
EPS = 1e-6
ALPHA = 2.0 ** 0.25
SM_SCALE = (QK_NOPE + QK_ROPE) ** -0.5
Q_SCALE = SM_SCALE * float(np.log2(np.e))

T_CTX = N_CTX_B * N_CTX
T_LAT = N_LAT_B * N_LAT
T = T_CTX + T_LAT
N_ASSIGN = T * TOP_K

LANES = 128
HEAD_PAD = 256
V_AUG = V_DIM + 16
TM = 256
TD = 1024
TQ = 1024
TK = 512
BM = 256
N_BLOCKS = N_ASSIGN // BM + N_EXP
VMEM_LIMIT = 56 * 1024 * 1024

NP_TILES = T_CTX // TM
LAT_TILES_PER_B = N_LAT // TM
IN_W = QL + KVL + 2 * GW + LANES


def _cp(sem):
    return pltpu.CompilerParams(dimension_semantics=sem, vmem_limit_bytes=VMEM_LIMIT)


def _const_spec(shape):
    nd = len(shape)
    return pl.BlockSpec(shape, lambda *a: (0,) * nd, pipeline_mode=pl.Buffered(1))


def _dot(a, b):
    return jnp.dot(a, b, preferred_element_type=F32)


def _dot_nt(a, b):
    return lax.dot_general(a, b, (((1,), (1,)), ((), ())), preferred_element_type=F32)


def _layer_norm(x, g, b):
    mu = jnp.mean(x, -1, keepdims=True)
    xc = x - mu
    var = jnp.mean(xc * xc, -1, keepdims=True)
    return xc * lax.rsqrt(var + EPS) * g + b


def _rms_norm(x, g):
    return x * lax.rsqrt(jnp.mean(x * x, -1, keepdims=True) + EPS) * g


def _gelu(x):
    return 0.5 * x * (1.0 + lax.erf(x * np.float32(np.sqrt(0.5))))


def _pack_pair(lo, hi):
    lo_b = lax.bitcast_convert_type(lo.astype(BF16).astype(F32), U32)
    hi_b = lax.bitcast_convert_type(hi.astype(BF16).astype(F32), U32)
    return hi_b | (lo_b >> 16)


def _unpack_pair(w):
    lo = lax.bitcast_convert_type(w << 16, F32)
    hi = lax.bitcast_convert_type(w & jnp.uint32(0xFFFF0000), F32)
    return lo, hi


def _adaln_kernel(cond_ref, w_ref, b_ref, o_ref):
    s = jax.nn.silu(cond_ref[...])
    s_hi = s.astype(BF16)
    s_lo = (s - s_hi.astype(F32)).astype(BF16)
    w = w_ref[...]
    w_hi = w.astype(BF16)
    w_lo = (w - w_hi.astype(F32)).astype(BF16)
    o_ref[...] = _dot(s_hi, w_hi) + _dot(s_lo, w_hi) + _dot(s_hi, w_lo) + b_ref[...]


def _adaln(cond16, w_ada, b_ada):
    tn = 1024
    n = w_ada.shape[1]
    return pl.pallas_call(
        _adaln_kernel,
        grid=(n // tn,),
        in_specs=[pl.BlockSpec((16, D), lambda j: (0, 0)),
                  pl.BlockSpec((D, tn), lambda j: (0, j)),
                  pl.BlockSpec((1, tn), lambda j: (0, j))],
        out_specs=pl.BlockSpec((16, tn), lambda j: (0, j)),
        out_shape=jax.ShapeDtypeStruct((16, n), F32),
        compiler_params=_cp(("arbitrary",)),
        name="adaln",
    )(cond16, w_ada, b_ada)


def _expand_kv(ckv_n, kr128, wuk_ref, wuvt_ref, k_ref, vt_ref):
    cb = ckv_n.astype(BF16)
    kn = _dot(cb, wuk_ref[...])
    vt = _dot_nt(wuvt_ref[...], cb).astype(BF16)
    ones = jnp.ones((V_AUG - V_DIM, vt.shape[1]), BF16)
    for h in range(H):
        vt_ref[h * V_AUG:h * V_AUG + V_DIM, :] = vt[h * V_DIM:(h + 1) * V_DIM]
        vt_ref[h * V_AUG + V_DIM:(h + 1) * V_AUG, :] = ones
    krb = kr128.astype(BF16)
    for h in range(H):
        k_ref[:, h * HEAD_PAD:h * HEAD_PAD + QK_NOPE] = kn[:, h * QK_NOPE:(h + 1) * QK_NOPE].astype(BF16)
        k_ref[:, h * HEAD_PAD + QK_NOPE:(h + 1) * HEAD_PAD] = krb


def _rope128(x, cos, s_plus, s_minus):
    return (x * cos + pltpu.roll(x, 16, 1) * s_plus + pltpu.roll(x, LANES - 16, 1) * s_minus)


def _proj_kernel(xc_ref, xl_ref, mods_ref, win_ref, qg_ref, wuq_ref, kvg_ref, wuk_ref, wuv_ref,
                 lng_ref, lnb_ref, ws_ref, bsb_ref, cos_ref, sp_ref, sm_ref,
                 q_ref, k_ref, v_ref, gm_ref, ckv_ref, kr_ref):
    i = pl.program_id(0)
    x = jnp.where(i < NP_TILES, xc_ref[...], xl_ref[...])
    m = mods_ref[0]
    shift1, scale1 = m[0:1], m[1:2]
    hmod = (x * (1.0 + scale1) + shift1).astype(BF16)
    proj = _dot(hmod, win_ref[...])
    cq = proj[:, 0:QL]
    ckv = proj[:, QL:QL + KVL]
    u = proj[:, QL + KVL:QL + KVL + GW]
    v = proj[:, QL + KVL + GW:QL + KVL + 2 * GW]
    kr = proj[:, QL + KVL + 2 * GW:IN_W]

    cos, s_plus, s_minus = cos_ref[...], sp_ref[...], sm_ref[...]

    cq_n = _rms_norm(cq, qg_ref[...]).astype(BF16)
    q = _dot(cq_n, wuq_ref[...]) * Q_SCALE
    for h in range(H):
        q_ref[:, h * HEAD_PAD:h * HEAD_PAD + QK_NOPE] = q[:, h * HEAD_PAD:h * HEAD_PAD + QK_NOPE].astype(BF16)
        q_ref[:, h * HEAD_PAD + QK_NOPE:(h + 1) * HEAD_PAD] = _rope128(
            q[:, h * HEAD_PAD + QK_NOPE:(h + 1) * HEAD_PAD], cos, s_plus, s_minus).astype(BF16)

    ckv_n = _rms_norm(ckv, kvg_ref[...])

    @pl.when(i < NP_TILES)
    def _():
        ckv_ref[...] = ckv_n
        kr_ref[...] = kr[:, :QK_ROPE]

    _expand_kv(ckv_n, _rope128(kr, cos, s_plus, s_minus), wuk_ref, wuv_ref, k_ref, v_ref)

    gu = _gelu(u)
    vn = _layer_norm(_gelu(v), lng_ref[...], lnb_ref[...]).astype(BF16)
    for c in range(TM // CHUNK):
        rows = slice(c * CHUNK, (c + 1) * CHUNK)
        for g in range(G_GROUPS):
            cols = slice(g * G_DIM, (g + 1) * G_DIM)
            s = _dot(ws_ref[g], vn[rows, cols]) + bsb_ref[g]
            gm_ref[rows, cols] = (gu[rows, cols] * s).astype(BF16)


def _mods_index(i):
    return jnp.where(i < NP_TILES, N_LAT_B, (i - NP_TILES) // LAT_TILES_PER_B)


def _proj(xc, xl, mods, win, qg, wuq, kvg, wuk, wuv, lng, lnb, ws, bsb, cos_t, sp_t, sm_t):
    def tok(width):
        return pl.BlockSpec((TM, width), lambda i: (i, 0))

    rope_spec = pl.BlockSpec(
        (TM, LANES),
        lambda i: (jnp.where(i < NP_TILES, LAT_TILES_PER_B, (i - NP_TILES) % LAT_TILES_PER_B), 0))
    return pl.pallas_call(
        _proj_kernel,
        grid=(T // TM,),
        in_specs=[
            pl.BlockSpec((TM, D), lambda i: (jnp.minimum(i, NP_TILES - 1), 0)),
            pl.BlockSpec((TM, D), lambda i: (jnp.maximum(i - NP_TILES, 0), 0)),
            pl.BlockSpec((1, 6, D), lambda i: (_mods_index(i), 0, 0)),
            _const_spec((D, IN_W)), _const_spec((1, QL)), _const_spec((QL, H * HEAD_PAD)),
            _const_spec((1, KVL)), _const_spec((KVL, H * QK_NOPE)), _const_spec((H * V_DIM, KVL)),
            _const_spec((1, GW)), _const_spec((1, GW)),
            _const_spec((G_GROUPS, CHUNK, CHUNK)), _const_spec((G_GROUPS, CHUNK, G_DIM)),
            rope_spec, rope_spec, rope_spec,
        ],
        out_specs=[tok(H * HEAD_PAD), tok(H * HEAD_PAD),
                   pl.BlockSpec((H * V_AUG, TM), lambda i: (0, i)), tok(GW),
                   pl.BlockSpec((TM, KVL), lambda i: (jnp.minimum(i, NP_TILES - 1), 0)),
                   pl.BlockSpec((TM, QK_ROPE), lambda i: (jnp.minimum(i, NP_TILES - 1), 0))],
        out_shape=[
            jax.ShapeDtypeStruct((T, H * HEAD_PAD), BF16),
            jax.ShapeDtypeStruct((T, H * HEAD_PAD), BF16),
            jax.ShapeDtypeStruct((H * V_AUG, T), BF16),
            jax.ShapeDtypeStruct((T, GW), BF16),
            jax.ShapeDtypeStruct((T_CTX, KVL), F32),
            jax.ShapeDtypeStruct((T_CTX, QK_ROPE), F32),
        ],
        compiler_params=_cp(("arbitrary",)),
        name="proj",
    )(xc, xl, mods, win, qg, wuq, kvg, wuk, wuv, lng, lnb, ws, bsb, cos_t, sp_t, sm_t)


def _kvexp_kernel(ckv_ref, kr_ref, wuk_ref, wuv_ref, k_ref, v_ref):
    kr = kr_ref[...]
    kr128 = jnp.concatenate([kr, jnp.zeros_like(kr)], axis=1)
    _expand_kv(ckv_ref[...], kr128, wuk_ref, wuv_ref, k_ref, v_ref)


def _kvexp(ckv, kr, wuk, wuv):
    rows = ckv.shape[0]
    return pl.pallas_call(
        _kvexp_kernel,
        grid=(rows // TM,),
        in_specs=[pl.BlockSpec((TM, KVL), lambda i: (i, 0)), pl.BlockSpec((TM, QK_ROPE), lambda i: (i, 0)),
                  _const_spec((KVL, H * QK_NOPE)), _const_spec((H * V_DIM, KVL))],
        out_specs=[pl.BlockSpec((TM, H * HEAD_PAD), lambda i: (i, 0)),
                   pl.BlockSpec((H * V_AUG, TM), lambda i: (0, i))],
        out_shape=[jax.ShapeDtypeStruct((rows, H * HEAD_PAD), BF16),
                   jax.ShapeDtypeStruct((H * V_AUG, rows), BF16)],
        compiler_params=_cp(("arbitrary",)),
        name="kvexp",
    )(ckv, kr, wuk, wuv)


def _probs(s, m):
    m_new = jnp.maximum(m, jnp.max(s, 0, keepdims=True))
    return m_new, jnp.exp2(m - m_new), jnp.exp2(s - m_new).astype(BF16)


def _attn_finish(acc):
    return (acc[:V_DIM] / acc[V_DIM:V_DIM + 1]).T.astype(BF16)


def _attn_ctx_kernel(q_ref, k_ref, vt_ref, o_ref):
    m0 = jnp.full((1, N_CTX), -jnp.inf, F32)
    for h in range(H):
        qh = q_ref[:, h * HEAD_PAD:(h + 1) * HEAD_PAD]
        kh = k_ref[:, h * HEAD_PAD:(h + 1) * HEAD_PAD]
        _, _, p = _probs(_dot_nt(kh, qh), m0)
        acc = _dot(vt_ref[h * V_AUG:(h + 1) * V_AUG, :], p)
        o_ref[:, h * V_DIM:(h + 1) * V_DIM] = _attn_finish(acc)


def _attn_ctx(q, k, vt):
    return pl.pallas_call(
        _attn_ctx_kernel,
        grid=(N_CTX_B,),
        in_specs=[pl.BlockSpec((N_CTX, H * HEAD_PAD), lambda b: (b, 0)),
                  pl.BlockSpec((N_CTX, H * HEAD_PAD), lambda b: (b, 0)),
                  pl.BlockSpec((H * V_AUG, N_CTX), lambda b: (0, b))],
        out_specs=pl.BlockSpec((N_CTX, H * V_DIM), lambda b: (b, 0)),
        out_shape=jax.ShapeDtypeStruct((T_CTX, H * V_DIM), BF16),
        compiler_params=_cp(("arbitrary",)),
        name="attn_ctx",
    )(q, k, vt)


def _attn_lat_kernel(q_ref, k_ref, vt_ref, kc_ref, vct_ref, o_ref):
    q = q_ref[...]
    n_lat = N_LAT // TK

    def keys(c):
        return k_ref[c * TK:(c + 1) * TK, :] if c < n_lat else kc_ref[...]

    def values_t(c):
        return vt_ref[:, c * TK:(c + 1) * TK] if c < n_lat else vct_ref[...]

    m = jnp.full((1, TQ), -jnp.inf, F32)
    acc = jnp.zeros((V_AUG, TQ), F32)
    s_next = _dot_nt(keys(0), q)
    p_prev = alpha_prev = None
    for c in range(n_lat + 1):
        s = s_next
        if c < n_lat:
            s_next = _dot_nt(keys(c + 1), q)
        if c > 0:
            acc = alpha_prev * acc + _dot(values_t(c - 1), p_prev)
        m, alpha_prev, p_prev = _probs(s, m)
    acc = alpha_prev * acc + _dot(values_t(n_lat), p_prev)
    o_ref[...] = _attn_finish(acc)


def _attn_lat(q, k, vt, kc, vct):
    qb0 = T_CTX // TQ
    qpb = N_LAT // TQ
    kb0 = T_CTX // N_LAT
    return pl.pallas_call(
        _attn_lat_kernel,
        grid=(N_LAT_B, H, qpb),
        in_specs=[pl.BlockSpec((TQ, HEAD_PAD), lambda b, h, qi: (qb0 + b * qpb + qi, h)),
                  pl.BlockSpec((N_LAT, HEAD_PAD), lambda b, h, qi: (kb0 + b, h)),
                  pl.BlockSpec((V_AUG, N_LAT), lambda b, h, qi: (h, kb0 + b)),
                  pl.BlockSpec((PAST, HEAD_PAD), lambda b, h, qi: (b, h)),
                  pl.BlockSpec((V_AUG, PAST), lambda b, h, qi: (h, b))],
        out_specs=pl.BlockSpec((TQ, V_DIM), lambda b, h, qi: (b * qpb + qi, h)),
        out_shape=jax.ShapeDtypeStruct((T_LAT, H * V_DIM), BF16),
        compiler_params=_cp(("arbitrary", "arbitrary", "arbitrary")),
        name="attn_lat",
    )(q, k, vt, kc, vct)


def _post_kernel(ac_ref, al_ref, gm_ref, xc_ref, xl_ref, mods_ref, wout_ref, g1_ref, b1_ref, wrt_ref,
                 brt_ref, x1_ref, h2p_ref, ridx_ref, rwt_ref):
    i = pl.program_id(0)
    is_ctx = i < NP_TILES
    x = jnp.where(is_ctx, xc_ref[...], xl_ref[...])
    attn = jnp.where(is_ctx, ac_ref[...], al_ref[...])
    m = mods_ref[0]
    gate1, shift2, scale2 = m[2:3], m[3:4], m[4:5]
    mix = _dot(jnp.concatenate([attn, gm_ref[...]], axis=1), wout_ref[...])
    x1 = _layer_norm(ALPHA * x + gate1 * mix, g1_ref[...], b1_ref[...])
    x1_ref[...] = x1
    h2 = x1 * (1.0 + scale2) + shift2
    h2p_ref[...] = _pack_pair(h2[:, :D // 2], h2[:, D // 2:])

    logits = _dot(h2.astype(BF16), wrt_ref[...]) + brt_ref[...]
    lane_i = lax.broadcasted_iota(I32, logits.shape, 1)
    lane = lane_i.astype(F32)
    neg = jnp.float32(-jnp.inf)
    far = jnp.float32(LANES)
    gl = jnp.where(lane_i < N_GRP, logits, neg)
    gmax = jnp.max(gl, -1, keepdims=True)
    gidx = jnp.min(jnp.where(gl == gmax, lane, far), -1, keepdims=True)
    p_top = 1.0 / jnp.sum(jnp.exp(gl - gmax), -1, keepdims=True)
    grp_of_lane = ((lane_i - N_GRP) >> 3).astype(F32)
    in_grp = (lane_i >= N_GRP) & (lane_i < N_GRP + N_EXP) & (grp_of_lane == gidx)
    el = jnp.where(in_grp, logits, neg)
    m1 = jnp.max(el, -1, keepdims=True)
    i1 = jnp.min(jnp.where(el == m1, lane, far), -1, keepdims=True)
    el2 = jnp.where(lane == i1, neg, el)
    m2 = jnp.max(el2, -1, keepdims=True)
    i2 = jnp.min(jnp.where(el2 == m2, lane, far), -1, keepdims=True)
    e2 = jnp.exp(m2 - m1)
    w1 = p_top / (1.0 + e2)
    w2 = p_top * e2 / (1.0 + e2)
    two = lax.broadcasted_iota(I32, (logits.shape[0], TOP_K), 1)
    ridx_ref[...] = (jnp.where(two == 0, i1, i2) - N_GRP).astype(I32)
    rwt_ref[...] = jnp.where(two == 0, w1, w2)


def _post(attn_c, attn_l, gm, xc, xl, mods, wout, g1, b1, wrt, brt):
    def tok(width):
        return pl.BlockSpec((TM, width), lambda i: (i, 0))

    def ctx(width):
        return pl.BlockSpec((TM, width), lambda i: (jnp.minimum(i, NP_TILES - 1), 0))

    def lat(width):
        return pl.BlockSpec((TM, width), lambda i: (jnp.maximum(i - NP_TILES, 0), 0))

    return pl.pallas_call(
        _post_kernel,
        grid=(T // TM,),
        in_specs=[ctx(H * V_DIM), lat(H * V_DIM), tok(GW), ctx(D), lat(D),
                  pl.BlockSpec((1, 6, D), lambda i: (_mods_index(i), 0, 0)),
                  _const_spec((D, D)), _const_spec((1, D)), _const_spec((1, D)),
                  _const_spec((D, LANES)), _const_spec((1, LANES))],
        out_specs=[tok(D), tok(D // 2), tok(TOP_K), tok(TOP_K)],
        out_shape=[jax.ShapeDtypeStruct((T, D), F32),
                   jax.ShapeDtypeStruct((T, D // 2), U32),
                   jax.ShapeDtypeStruct((T, TOP_K), I32),
                   jax.ShapeDtypeStruct((T, TOP_K), F32)],
        compiler_params=_cp(("arbitrary",)),
        name="post",
    )(attn_c, attn_l, gm, xc, xl, mods, wout, g1, b1, wrt, brt)


def _row_copy(src, src_row, dst, dst_row, sem):
    return pltpu.make_async_copy(src.at[pl.ds(src_row, 1)], dst.at[pl.ds(dst_row, 1)], sem)


def _dispatch_kernel(dest_ref, h2p_ref, zeros_ref, xs_ref, sem):
    del zeros_ref

    def issue(r, carry):
        for k in range(TOP_K):
            _row_copy(h2p_ref, r, xs_ref, dest_ref[0, 0, TOP_K * r + k], sem).start()
        return carry

    lax.fori_loop(0, TD, issue, 0, unroll=8)
    for _ in range(TOP_K):
        pltpu.make_async_copy(h2p_ref, xs_ref.at[pl.ds(0, TD)], sem).wait()


def _dispatch(dest, h2p):
    return pl.pallas_call(
        _dispatch_kernel,
        grid=(T // TD,),
        in_specs=[pl.BlockSpec((1, 1, TOP_K * TD), lambda i: (i, 0, 0), memory_space=pltpu.SMEM),
                  pl.BlockSpec((TD, D // 2), lambda i: (i, 0)), pl.BlockSpec(memory_space=pl.ANY)],
        out_specs=pl.BlockSpec(memory_space=pl.ANY),
        out_shape=jax.ShapeDtypeStruct((N_BLOCKS * BM, D // 2), U32),
        input_output_aliases={2: 0},
        scratch_shapes=[pltpu.SemaphoreType.DMA(())],
        compiler_params=_cp(("arbitrary",)),
        name="dispatch",
    )(dest.reshape(T // TD, 1, TOP_K * TD), h2p, jnp.zeros((N_BLOCKS * BM, D // 2), U32))


def _experts_kernel(be_ref, bv_ref, x_ref, w1_ref, w3_ref, w2_ref, y_ref, w1b, w3b, w2b):
    i = pl.program_id(0)
    e = be_ref[i]
    nv = bv_ref[i]
    prev = be_ref[jnp.maximum(i - 1, 0)]

    @pl.when((i == 0) | (e != prev))
    def _():
        w1b[...] = w1_ref[0].astype(BF16)
        w3b[...] = w3_ref[0].astype(BF16)
        w2b[...] = w2_ref[0].astype(BF16)

    @pl.when(nv > 0)
    def _():
        lo, hi = _unpack_pair(x_ref[...])
        xb = jnp.concatenate([lo.astype(BF16), hi.astype(BF16)], axis=1)
        a = _dot(xb, w1b[...])
        b = _dot(xb, w3b[...])
        hid = (jax.nn.silu(a) * b).astype(BF16)
        y = _dot(hid, w2b[...])
        y_ref[...] = _pack_pair(y[:, :D // 2], y[:, D // 2:])

    @pl.when(nv == 0)
    def _():
        y_ref[...] = jnp.zeros_like(y_ref)


def _experts(blk_expert, blk_valid, xs, w1, w3, w2):
    grid_spec = pltpu.PrefetchScalarGridSpec(
        num_scalar_prefetch=2,
        grid=(N_BLOCKS,),
        in_specs=[pl.BlockSpec((BM, D // 2), lambda i, be, bv: (i, 0)),
                  pl.BlockSpec((1, D, E_HID), lambda i, be, bv: (be[i], 0, 0)),
                  pl.BlockSpec((1, D, E_HID), lambda i, be, bv: (be[i], 0, 0)),
                  pl.BlockSpec((1, E_HID, D), lambda i, be, bv: (be[i], 0, 0))],
        out_specs=pl.BlockSpec((BM, D // 2), lambda i, be, bv: (i, 0)),
        scratch_shapes=[pltpu.VMEM((D, E_HID), BF16), pltpu.VMEM((D, E_HID), BF16),
                        pltpu.VMEM((E_HID, D), BF16)],
    )
    return pl.pallas_call(
        _experts_kernel,
        grid_spec=grid_spec,
        out_shape=jax.ShapeDtypeStruct((N_BLOCKS * BM, D // 2), U32),
        compiler_params=_cp(("arbitrary",)),
        name="experts",
    )(blk_expert, blk_valid, xs, w1, w3, w2)


def _final_kernel(dcur_ref, dnext_ref, ys_ref, x1_ref, rwt_ref, mods_ref, g2_ref, b2_ref, oc_ref, ol_ref,
                  buf, sems):
    i = pl.program_id(0)
    n = pl.num_programs(0)
    slot = i % 2

    def issue(d_ref, s):
        def body(r, carry):
            for k in range(TOP_K):
                _row_copy(ys_ref, d_ref[0, 0, TOP_K * r + k], buf.at[s, k], r, sems.at[s]).start()
            return carry

        lax.fori_loop(0, TM, body, 0, unroll=8)

    @pl.when(i == 0)
    def _():
        issue(dcur_ref, 0)

    @pl.when(i + 1 < n)
    def _():
        issue(dnext_ref, 1 - slot)

    for k in range(TOP_K):
        pltpu.make_async_copy(ys_ref.at[pl.ds(0, TM)], buf.at[slot, k], sems.at[slot]).wait()

    w = rwt_ref[...]
    lo0, hi0 = _unpack_pair(buf[slot, 0])
    lo1, hi1 = _unpack_pair(buf[slot, 1])
    w0, w1 = w[:, 0:1], w[:, 1:2]
    moe = jnp.concatenate([lo0 * w0 + lo1 * w1, hi0 * w0 + hi1 * w1], axis=1)
    gate2 = mods_ref[0][5:6]
    out = _layer_norm(ALPHA * x1_ref[...] + gate2 * moe, g2_ref[...], b2_ref[...])

    @pl.when(i < NP_TILES)
    def _():
        oc_ref[...] = out

    @pl.when(i >= NP_TILES)
    def _():
        ol_ref[...] = out


def _final(dest3, ys, x1, rwt, mods, g2, b2):
    n = T // TM
    return pl.pallas_call(
        _final_kernel,
        grid=(n,),
        in_specs=[pl.BlockSpec((1, 1, TOP_K * TM), lambda i: (i, 0, 0), memory_space=pltpu.SMEM),
                  pl.BlockSpec((1, 1, TOP_K * TM), lambda i: (jnp.minimum(i + 1, n - 1), 0, 0),
                               memory_space=pltpu.SMEM),
                  pl.BlockSpec(memory_space=pl.ANY),
                  pl.BlockSpec((TM, D), lambda i: (i, 0)),
                  pl.BlockSpec((TM, TOP_K), lambda i: (i, 0)),
                  pl.BlockSpec((1, 6, D), lambda i: (_mods_index(i), 0, 0)),
                  _const_spec((1, D)), _const_spec((1, D))],
        out_specs=[pl.BlockSpec((TM, D), lambda i: (jnp.minimum(i, NP_TILES - 1), 0)),
                   pl.BlockSpec((TM, D), lambda i: (jnp.maximum(i - NP_TILES, 0), 0))],
        out_shape=[jax.ShapeDtypeStruct((T_CTX, D), F32), jax.ShapeDtypeStruct((T_LAT, D), F32)],
        scratch_shapes=[pltpu.VMEM((2, TOP_K, TM, D // 2), U32), pltpu.SemaphoreType.DMA((2,))],
        compiler_params=_cp(("arbitrary",)),
        name="final",
    )(dest3, dest3, ys, x1, rwt, mods, g2, b2)


def _rope_tables():
    n = N_LAT
    rows = n // GRID_W
    row = jnp.repeat(jnp.arange(rows, dtype=F32), GRID_W)
    col = jnp.tile(jnp.arange(GRID_W, dtype=F32), rows)
    half = QK_ROPE // 2
    inv = ROPE_THETA ** (-jnp.arange(0, half, 2, dtype=F32) / half)
    cr, sr = jnp.cos(row[:, None] * inv), jnp.sin(row[:, None] * inv)
    cc, sc = jnp.cos(col[:, None] * inv), jnp.sin(col[:, None] * inv)
    z16 = jnp.zeros_like(cr)
    pad = lambda v: jnp.zeros((n, LANES - QK_ROPE), F32) + v
    cos = jnp.concatenate([cr, cr, cc, cc, pad(1.0)], axis=1)
    s_plus = jnp.concatenate([z16, sr, z16, sc, pad(0.0)], axis=1)
    s_minus = jnp.concatenate([-sr, z16, -sc, z16, pad(0.0)], axis=1)
    ident = lambda v: jnp.full((TM, LANES), v, F32)
    return (jnp.concatenate([cos, ident(1.0)], 0), jnp.concatenate([s_plus, ident(0.0)], 0),
            jnp.concatenate([s_minus, ident(0.0)], 0))


def _block_plan(ridx):
    e_flat = ridx.reshape(-1)
    onehot = (e_flat[:, None] == jnp.arange(N_EXP, dtype=I32)[None, :]).astype(I32)
    csum = jnp.cumsum(onehot, axis=0)
    rank = jnp.sum(onehot * csum, axis=1) - 1
    counts = csum[-1]
    nblk = (counts + BM - 1) // BM
    bends = jnp.cumsum(nblk)
    bstarts = bends - nblk
    dest = bstarts[e_flat] * BM + rank
    blk = jnp.arange(N_BLOCKS, dtype=I32)
    blk_expert = jnp.minimum(jnp.sum((blk[:, None] >= bends[None, :]).astype(I32), axis=1), N_EXP - 1)
    left = counts[blk_expert] - (blk - bstarts[blk_expert]) * BM
    blk_valid = jnp.where(blk < bends[-1], jnp.clip(left, 0, BM), 0).astype(I32)
    return dest.astype(I32), blk_expert.astype(I32), blk_valid


def kernel(x_prompt, x_sample, cache_ckv, cache_krope, c, c_ctx, w_ada, b_ada, w_in, q_norm_g, w_uq,
           kv_norm_g, w_ukv, gmlp_ln_g, gmlp_ln_b, w_spatial, b_spatial, w_out, ln1_g, ln1_b, w_group,
           b_group, w_router, b_router, w1, w3, w2, ln2_g, ln2_b):
    l = 0
    xc = x_prompt.reshape(T_CTX, D)
    xl = x_sample.reshape(T_LAT, D)

    cond16 = jnp.concatenate([c, c_ctx[None, :], jnp.zeros((16 - N_LAT_B - 1, D), F32)], axis=0)
    mods = _adaln(cond16, w_ada[l], b_ada[l][None, :]).reshape(16, 6, D)

    wi = w_in[l]
    o_kr = QL + KVL
    o_u = o_kr + QK_ROPE
    win = jnp.concatenate([wi[:, :o_kr], wi[:, o_u:], wi[:, o_kr:o_u],
                           jnp.zeros((D, LANES - QK_ROPE), F32)], axis=1).astype(BF16)
    wuq = jnp.pad(w_uq[l].reshape(QL, H, QK_NOPE + QK_ROPE),
                  ((0, 0), (0, 0), (0, HEAD_PAD - QK_NOPE - QK_ROPE))).reshape(QL, H * HEAD_PAD).astype(BF16)
    wkv = w_ukv[l].reshape(KVL, H, QK_NOPE + V_DIM)
    wuk = wkv[:, :, :QK_NOPE].reshape(KVL, H * QK_NOPE).astype(BF16)
    wuvt = wkv[:, :, QK_NOPE:].reshape(KVL, H * V_DIM).T.astype(BF16)
    ws = w_spatial[l].astype(BF16)
    bsb = jnp.broadcast_to(b_spatial[l][:, :, None], (G_GROUPS, CHUNK, G_DIM))
    wrt = jnp.concatenate([w_group[l], w_router[l], jnp.zeros((D, LANES - N_GRP - N_EXP), F32)],
                          axis=1).astype(BF16)
    brt = jnp.concatenate([b_group[l], b_router[l], jnp.zeros((LANES - N_GRP - N_EXP,), F32)])[None, :]
    cos_t, sp_t, sm_t = _rope_tables()
    row = lambda v: v[l][None, :]

    q, k, vt, gm, ckv_n, kr = _proj(xc, xl, mods, win, row(q_norm_g), wuq, row(kv_norm_g), wuk, wuvt,
                                   row(gmlp_ln_g), row(gmlp_ln_b), ws, bsb, cos_t, sp_t, sm_t)
    kc, vct = _kvexp(cache_ckv[:, l].reshape(N_LAT_B * PAST, KVL),
                     cache_krope[:, l].reshape(N_LAT_B * PAST, QK_ROPE), wuk, wuvt)
    attn_c = _attn_ctx(q, k, vt)
    attn_l = _attn_lat(q, k, vt, kc, vct)

    x1, h2p, ridx, rwt = _post(attn_c, attn_l, gm, xc, xl, mods, w_out[l].astype(BF16), row(ln1_g), row(ln1_b),
                               wrt, brt)
    dest, blk_expert, blk_valid = _block_plan(ridx)
    xs = _dispatch(dest, h2p)
    ys = _experts(blk_expert, blk_valid, xs, w1[l], w3[l], w2[l])
    y_c, y_l = _final(dest.reshape(T // TM, 1, TOP_K * TM), ys, x1, rwt, mods, row(ln2_g), row(ln2_b))

    return (y_c.reshape(N_CTX_B, N_CTX, D), y_l.reshape(N_LAT_B, N_LAT, D),
            ckv_n.reshape(N_CTX_B, 1, N_CTX, KVL), kr.reshape(N_CTX_B, 1, N_CTX, QK_ROPE))
```

```python
import functools

import jax
import jax.numpy as jnp
import numpy as np
from jax import lax
from jax.experimental import pallas as pl
from jax.experimental.pallas import tpu as pltpu

F32 = jnp.float32
BF16 = jnp.bfloat16
U32 = jnp.uint32
I32 = jnp.int32

D = 2048
N_CTX_B, N_CTX = 32, 256
N_LAT_B, N_LAT = 4, 4096
PAST = 512
GRID_W = 64
H = 8
QK_NOPE, QK_ROPE, V_DIM = 128, 64, 128
QL, KVL = 512, 512
GW = 1024
G_GROUPS, G_DIM, CHUNK = 8, 128, 128
N_GRP, E_PER_GRP, N_EXP, TOP_K = 4, 8, 32, 2
E_HID = 512
ROPE_THETA = 10000.0
EPS = 1e-6
ALPHA = 2.0 ** 0.25
SM_SCALE = (QK_NOPE + QK_ROPE) ** -0.5
Q_SCALE = SM_SCALE * float(np.log2(np.e))

T_CTX = N_CTX_B * N_CTX
T_LAT = N_LAT_B * N_LAT
T = T_CTX + T_LAT
N_ASSIGN = T * TOP_K

LANES = 128
HEAD_PAD = 256
V_AUG = V_DIM + 16
TM = 256
TQ = 1024
TK = 512
BM = 256
N_BLOCKS = N_ASSIGN // BM + N_EXP
VMEM_LIMIT = 56 * 1024 * 1024

NP_TILES = T_CTX // TM
LAT_TILES_PER_B = N_LAT // TM
IN_W = QL + KVL + 2 * GW + LANES


def _cp(sem):
    return pltpu.CompilerParams(dimension_semantics=sem, vmem_limit_bytes=VMEM_LIMIT)


def _const_spec(shape):
    nd = len(shape)
    return pl.BlockSpec(shape, lambda *a: (0,) * nd, pipeline_mode=pl.Buffered(1))


def _dot(a, b):
    return jnp.dot(a, b, preferred_element_type=F32)


def _dot_nt(a, b):
    return lax.dot_general(a, b, (((1,), (1,)), ((), ())), preferred_element_type=F32)


def _layer_norm(x, g, b):
    mu = jnp.mean(x, -1, keepdims=True)
    xc = x - mu
    var = jnp.mean(xc * xc, -1, keepdims=True)
    return xc * lax.rsqrt(var + EPS) * g + b


def _rms_norm(x, g):
    return x * lax.rsqrt(jnp.mean(x * x, -1, keepdims=True) + EPS) * g


def _gelu(x):
    return 0.5 * x * (1.0 + lax.erf(x * np.float32(np.sqrt(0.5))))


def _pack_pair(lo, hi):
    lo_b = lax.bitcast_convert_type(lo.astype(BF16).astype(F32), U32)
    hi_b = lax.bitcast_convert_type(hi.astype(BF16).astype(F32), U32)
    return hi_b | (lo_b >> 16)


def _unpack_pair(w):
    lo = lax.bitcast_convert_type(w << 16, F32)
    hi = lax.bitcast_convert_type(w & jnp.uint32(0xFFFF0000), F32)
    return lo, hi


def _adaln_kernel(cond_ref, w_ref, b_ref, o_ref):
    s = jax.nn.silu(cond_ref[...])
    s_hi = s.astype(BF16)
    s_lo = (s - s_hi.astype(F32)).astype(BF16)
    w = w_ref[...]
    w_hi = w.astype(BF16)
    w_lo = (w - w_hi.astype(F32)).astype(BF16)
    o_ref[...] = _dot(s_hi, w_hi) + _dot(s_lo, w_hi) + _dot(s_hi, w_lo) + b_ref[...]


def _adaln(cond16, w_ada, b_ada):
    tn = 1024
    n = w_ada.shape[1]
    return pl.pallas_call(
        _adaln_kernel,
        grid=(n // tn,),
        in_specs=[pl.BlockSpec((16, D), lambda j: (0, 0)),
                  pl.BlockSpec((D, tn), lambda j: (0, j)),
                  pl.BlockSpec((1, tn), lambda j: (0, j))],
        out_specs=pl.BlockSpec((16, tn), lambda j: (0, j)),
        out_shape=jax.ShapeDtypeStruct((16, n), F32),
        compiler_params=_cp(("arbitrary",)),
        name="adaln",
    )(cond16, w_ada, b_ada)


def _expand_kv(ckv_n, kr128, wuk_ref, wuvt_ref, k_ref, vt_ref):
    cb = ckv_n.astype(BF16)
    kn = _dot(cb, wuk_ref[...])
    vt = _dot_nt(wuvt_ref[...], cb).astype(BF16)
    ones = jnp.ones((V_AUG - V_DIM, vt.shape[1]), BF16)
    for h in range(H):
        vt_ref[h * V_AUG:h * V_AUG + V_DIM, :] = vt[h * V_DIM:(h + 1) * V_DIM]
        vt_ref[h * V_AUG + V_DIM:(h + 1) * V_AUG, :] = ones
    krb = kr128.astype(BF16)
    for h in range(H):
        k_ref[:, h * HEAD_PAD:h * HEAD_PAD + QK_NOPE] = kn[:, h * QK_NOPE:(h + 1) * QK_NOPE].astype(BF16)
        k_ref[:, h * HEAD_PAD + QK_NOPE:(h + 1) * HEAD_PAD] = krb


def _rope128(x, cos, s_plus, s_minus):
    return (x * cos + pltpu.roll(x, 16, 1) * s_plus + pltpu.roll(x, LANES - 16, 1) * s_minus)


def _proj_kernel(xc_ref, xl_ref, mods_ref, win_ref, qg_ref, wuq_ref, kvg_ref, wuk_ref, wuv_ref,
                 lng_ref, lnb_ref, ws_ref, bsb_ref, cos_ref, sp_ref, sm_ref,
                 q_ref, k_ref, v_ref, gm_ref, ckv_ref, kr_ref):
    i = pl.program_id(0)
    x = jnp.where(i < NP_TILES, xc_ref[...], xl_ref[...])
    m = mods_ref[0]
    shift1, scale1 = m[0:1], m[1:2]
    hmod = (x * (1.0 + scale1) + shift1).astype(BF16)
    proj = _dot(hmod, win_ref[...])
    cq = proj[:, 0:QL]
    ckv = proj[:, QL:QL + KVL]
    u = proj[:, QL + KVL:QL + KVL + GW]
    v = proj[:, QL + KVL + GW:QL + KVL + 2 * GW]
    kr = proj[:, QL + KVL + 2 * GW:IN_W]

    cos, s_plus, s_minus = cos_ref[...], sp_ref[...], sm_ref[...]

    cq_n = _rms_norm(cq, qg_ref[...]).astype(BF16)
    q = _dot(cq_n, wuq_ref[...]) * Q_SCALE
    for h in range(H):
        q_ref[:, h * HEAD_PAD:h * HEAD_PAD + QK_NOPE] = q[:, h * HEAD_PAD:h * HEAD_PAD + QK_NOPE].astype(BF16)
        q_ref[:, h * HEAD_PAD + QK_NOPE:(h + 1) * HEAD_PAD] = _rope128(
            q[:, h * HEAD_PAD + QK_NOPE:(h + 1) * HEAD_PAD], cos, s_plus, s_minus).astype(BF16)

    ckv_n = _rms_norm(ckv, kvg_ref[...])

    @pl.when(i < NP_TILES)
    def _():
        ckv_ref[...] = ckv_n
        kr_ref[...] = kr[:, :QK_ROPE]

    _expand_kv(ckv_n, _rope128(kr, cos, s_plus, s_minus), wuk_ref, wuv_ref, k_ref, v_ref)

    gu = _gelu(u)
    vn = _layer_norm(_gelu(v), lng_ref[...], lnb_ref[...]).astype(BF16)
    for c in range(TM // CHUNK):
        rows = slice(c * CHUNK, (c + 1) * CHUNK)
        for g in range(G_GROUPS):
            cols = slice(g * G_DIM, (g + 1) * G_DIM)
            s = _dot(ws_ref[g], vn[rows, cols]) + bsb_ref[g]
            gm_ref[rows, cols] = (gu[rows, cols] * s).astype(BF16)


def _mods_index(i):
    return jnp.where(i < NP_TILES, N_LAT_B, (i - NP_TILES) // LAT_TILES_PER_B)


def _proj(xc, xl, mods, win, qg, wuq, kvg, wuk, wuv, lng, lnb, ws, bsb, cos_t, sp_t, sm_t):
    def tok(width):
        return pl.BlockSpec((TM, width), lambda i: (i, 0))

    rope_spec = pl.BlockSpec(
        (TM, LANES),
        lambda i: (jnp.where(i < NP_TILES, LAT_TILES_PER_B, (i - NP_TILES) % LAT_TILES_PER_B), 0))
    return pl.pallas_call(
        _proj_kernel,
        grid=(T // TM,),
        in_specs=[
            pl.BlockSpec((TM, D), lambda i: (jnp.minimum(i, NP_TILES - 1), 0)),
            pl.BlockSpec((TM, D), lambda i: (jnp.maximum(i - NP_TILES, 0), 0)),
            pl.BlockSpec((1, 6, D), lambda i: (_mods_index(i), 0, 0)),
            _const_spec((D, IN_W)), _const_spec((1, QL)), _const_spec((QL, H * HEAD_PAD)),
            _const_spec((1, KVL)), _const_spec((KVL, H * QK_NOPE)), _const_spec((H * V_DIM, KVL)),
            _const_spec((1, GW)), _const_spec((1, GW)),
            _const_spec((G_GROUPS, CHUNK, CHUNK)), _const_spec((G_GROUPS, CHUNK, G_DIM)),
            rope_spec, rope_spec, rope_spec,
        ],
        out_specs=[tok(H * HEAD_PAD), tok(H * HEAD_PAD),
                   pl.BlockSpec((H * V_AUG, TM), lambda i: (0, i)), tok(GW),
                   pl.BlockSpec((TM, KVL), lambda i: (jnp.minimum(i, NP_TILES - 1), 0)),
                   pl.BlockSpec((TM, QK_ROPE), lambda i: (jnp.minimum(i, NP_TILES - 1), 0))],
        out_shape=[
            jax.ShapeDtypeStruct((T, H * HEAD_PAD), BF16),
            jax.ShapeDtypeStruct((T, H * HEAD_PAD), BF16),
            jax.ShapeDtypeStruct((H * V_AUG, T), BF16),
            jax.ShapeDtypeStruct((T, GW), BF16),
            jax.ShapeDtypeStruct((T_CTX, KVL), F32),
            jax.ShapeDtypeStruct((T_CTX, QK_ROPE), F32),
        ],
        compiler_params=_cp(("arbitrary",)),
        name="proj",
    )(xc, xl, mods, win, qg, wuq, kvg, wuk, wuv, lng, lnb, ws, bsb, cos_t, sp_t, sm_t)


def _kvexp_kernel(ckv_ref, kr_ref, wuk_ref, wuv_ref, k_ref, v_ref):
    kr = kr_ref[...]
    kr128 = jnp.concatenate([kr, jnp.zeros_like(kr)], axis=1)
    _expand_kv(ckv_ref[...], kr128, wuk_ref, wuv_ref, k_ref, v_ref)


def _kvexp(ckv, kr, wuk, wuv):
    rows = ckv.shape[0]
    return pl.pallas_call(
        _kvexp_kernel,
        grid=(rows // TM,),
        in_specs=[pl.BlockSpec((TM, KVL), lambda i: (i, 0)), pl.BlockSpec((TM, QK_ROPE), lambda i: (i, 0)),
                  _const_spec((KVL, H * QK_NOPE)), _const_spec((H * V_DIM, KVL))],
        out_specs=[pl.BlockSpec((TM, H * HEAD_PAD), lambda i: (i, 0)),
                   pl.BlockSpec((H * V_AUG, TM), lambda i: (0, i))],
        out_shape=[jax.ShapeDtypeStruct((rows, H * HEAD_PAD), BF16),
                   jax.ShapeDtypeStruct((H * V_AUG, rows), BF16)],
        compiler_params=_cp(("arbitrary",)),
        name="kvexp",
    )(ckv, kr, wuk, wuv)


def _probs(s, m):
    m_new = jnp.maximum(m, jnp.max(s, 0, keepdims=True))
    return m_new, jnp.exp2(m - m_new), jnp.exp2(s - m_new).astype(BF16)


def _attn_finish(acc):
    return (acc[:V_DIM] / acc[V_DIM:V_DIM + 1]).T.astype(BF16)


def _attn_ctx_kernel(q_ref, k_ref, vt_ref, o_ref):
    m0 = jnp.full((1, N_CTX), -jnp.inf, F32)
    for h in range(H):
        qh = q_ref[:, h * HEAD_PAD:(h + 1) * HEAD_PAD]
        kh = k_ref[:, h * HEAD_PAD:(h + 1) * HEAD_PAD]
        _, _, p = _probs(_dot_nt(kh, qh), m0)
        acc = _dot(vt_ref[h * V_AUG:(h + 1) * V_AUG, :], p)
        o_ref[:, h * V_DIM:(h + 1) * V_DIM] = _attn_finish(acc)


def _attn_ctx(q, k, vt):
    return pl.pallas_call(
        _attn_ctx_kernel,
        grid=(N_CTX_B,),
        in_specs=[pl.BlockSpec((N_CTX, H * HEAD_PAD), lambda b: (b, 0)),
                  pl.BlockSpec((N_CTX, H * HEAD_PAD), lambda b: (b, 0)),
                  pl.BlockSpec((H * V_AUG, N_CTX), lambda b: (0, b))],
        out_specs=pl.BlockSpec((N_CTX, H * V_DIM), lambda b: (b, 0)),
        out_shape=jax.ShapeDtypeStruct((T_CTX, H * V_DIM), BF16),
        compiler_params=_cp(("arbitrary",)),
        name="attn_ctx",
    )(q, k, vt)


def _attn_lat_kernel(q_ref, k_ref, vt_ref, kc_ref, vct_ref, o_ref):
    q = q_ref[...]
    n_lat = N_LAT // TK

    def keys(c):
        return k_ref[c * TK:(c + 1) * TK, :] if c < n_lat else kc_ref[...]

    def values_t(c):
        return vt_ref[:, c * TK:(c + 1) * TK] if c < n_lat else vct_ref[...]

    m = jnp.full((1, TQ), -jnp.inf, F32)
    acc = jnp.zeros((V_AUG, TQ), F32)
    s_next = _dot_nt(keys(0), q)
    p_prev = alpha_prev = None
    for c in range(n_lat + 1):
        s = s_next
        if c < n_lat:
            s_next = _dot_nt(keys(c + 1), q)
        if c > 0:
            acc = alpha_prev * acc + _dot(values_t(c - 1), p_prev)
        m, alpha_prev, p_prev = _probs(s, m)
    acc = alpha_prev * acc + _dot(values_t(n_lat), p_prev)
    o_ref[...] = _attn_finish(acc)


def _attn_lat(q, k, vt, kc, vct):
    qb0 = T_CTX // TQ
    qpb = N_LAT // TQ
    kb0 = T_CTX // N_LAT
    return pl.pallas_call(
        _attn_lat_kernel,
        grid=(N_LAT_B, H, qpb),
        in_specs=[pl.BlockSpec((TQ, HEAD_PAD), lambda b, h, qi: (qb0 + b * qpb + qi, h)),
                  pl.BlockSpec((N_LAT, HEAD_PAD), lambda b, h, qi: (kb0 + b, h)),
                  pl.BlockSpec((V_AUG, N_LAT), lambda b, h, qi: (h, kb0 + b)),
                  pl.BlockSpec((PAST, HEAD_PAD), lambda b, h, qi: (b, h)),
                  pl.BlockSpec((V_AUG, PAST), lambda b, h, qi: (h, b))],
        out_specs=pl.BlockSpec((TQ, V_DIM), lambda b, h, qi: (b * qpb + qi, h)),
        out_shape=jax.ShapeDtypeStruct((T_LAT, H * V_DIM), BF16),
        compiler_params=_cp(("arbitrary", "arbitrary", "arbitrary")),
        name="attn_lat",
    )(q, k, vt, kc, vct)


def _post_kernel(ac_ref, al_ref, gm_ref, xc_ref, xl_ref, mods_ref, wout_ref, g1_ref, b1_ref, wrt_ref,
                 brt_ref, x1_ref, h2p_ref, ridx_ref, rwt_ref):
    i = pl.program_id(0)
    is_ctx = i < NP_TILES
    x = jnp.where(is_ctx, xc_ref[...], xl_ref[...])
    attn = jnp.where(is_ctx, ac_ref[...], al_ref[...])
    m = mods_ref[0]
    gate1, shift2, scale2 = m[2:3], m[3:4], m[4:5]
    mix = _dot(jnp.concatenate([attn, gm_ref[...]], axis=1), wout_ref[...])
    x1 = _layer_norm(ALPHA * x + gate1 * mix, g1_ref[...], b1_ref[...])
    x1_ref[...] = x1
    h2 = x1 * (1.0 + scale2) + shift2
    h2p_ref[...] = _pack_pair(h2[:, :D // 2], h2[:, D // 2:])

    logits = _dot(h2.astype(BF16), wrt_ref[...]) + brt_ref[...]
    lane_i = lax.broadcasted_iota(I32, logits.shape, 1)
    lane = lane_i.astype(F32)
    neg = jnp.float32(-jnp.inf)
    far = jnp.float32(LANES)
    gl = jnp.where(lane_i < N_GRP, logits, neg)
    gmax = jnp.max(gl, -1, keepdims=True)
    gidx = jnp.min(jnp.where(gl == gmax, lane, far), -1, keepdims=True)
    p_top = 1.0 / jnp.sum(jnp.exp(gl - gmax), -1, keepdims=True)
    grp_of_lane = ((lane_i - N_GRP) >> 3).astype(F32)
    in_grp = (lane_i >= N_GRP) & (lane_i < N_GRP + N_EXP) & (grp_of_lane == gidx)
    el = jnp.where(in_grp, logits, neg)
    m1 = jnp.max(el, -1, keepdims=True)
    i1 = jnp.min(jnp.where(el == m1, lane, far), -1, keepdims=True)
    el2 = jnp.where(lane == i1, neg, el)
    m2 = jnp.max(el2, -1, keepdims=True)
    i2 = jnp.min(jnp.where(el2 == m2, lane, far), -1, keepdims=True)
    e2 = jnp.exp(m2 - m1)
    w1 = p_top / (1.0 + e2)
    w2 = p_top * e2 / (1.0 + e2)
    two = lax.broadcasted_iota(I32, (logits.shape[0], TOP_K), 1)
    ridx_ref[...] = (jnp.where(two == 0, i1, i2) - N_GRP).astype(I32)
    rwt_ref[...] = jnp.where(two == 0, w1, w2)


def _post(attn_c, attn_l, gm, xc, xl, mods, wout, g1, b1, wrt, brt):
    def tok(width):
        return pl.BlockSpec((TM, width), lambda i: (i, 0))

    def ctx(width):
        return pl.BlockSpec((TM, width), lambda i: (jnp.minimum(i, NP_TILES - 1), 0))

    def lat(width):
        return pl.BlockSpec((TM, width), lambda i: (jnp.maximum(i - NP_TILES, 0), 0))

    return pl.pallas_call(
        _post_kernel,
        grid=(T // TM,),
        in_specs=[ctx(H * V_DIM), lat(H * V_DIM), tok(GW), ctx(D), lat(D),
                  pl.BlockSpec((1, 6, D), lambda i: (_mods_index(i), 0, 0)),
                  _const_spec((D, D)), _const_spec((1, D)), _const_spec((1, D)),
                  _const_spec((D, LANES)), _const_spec((1, LANES))],
        out_specs=[tok(D), tok(D // 2), tok(TOP_K), tok(TOP_K)],
        out_shape=[jax.ShapeDtypeStruct((T, D), F32),
                   jax.ShapeDtypeStruct((T, D // 2), U32),
                   jax.ShapeDtypeStruct((T, TOP_K), I32),
                   jax.ShapeDtypeStruct((T, TOP_K), F32)],
        compiler_params=_cp(("arbitrary",)),
        name="post",
    )(attn_c, attn_l, gm, xc, xl, mods, wout, g1, b1, wrt, brt)


def _row_copy(src, src_row, dst, dst_row, sem):
    return pltpu.make_async_copy(src.at[pl.ds(src_row, 1)], dst.at[pl.ds(dst_row, 1)], sem)


def _experts_kernel(be_ref, nu_ref, s0_ref, sn_ref, h2p_ref, w1_ref, w3_ref, w2_ref, y_ref,
                    w1b, w3b, w2b, xbuf, sems):
    i = pl.program_id(0)
    n_used = nu_ref[0]
    e = be_ref[i]
    prev = be_ref[jnp.maximum(i - 1, 0)]
    slot = i % 2

    def wait_block(s):
        pltpu.make_async_copy(h2p_ref.at[pl.ds(0, BM)], xbuf.at[s], sems.at[s]).wait()

    @pl.when(i == 0)
    def _():
        def issue(r, carry):
            _row_copy(h2p_ref, s0_ref[0, 0, r], xbuf.at[0], r, sems.at[0]).start()
            return carry

        lax.fori_loop(0, BM, issue, 0, unroll=8)

    @pl.when((i == 0) | (e != prev))
    def _():
        w1b[...] = w1_ref[0].astype(BF16)
        w3b[...] = w3_ref[0].astype(BF16)
        w2b[...] = w2_ref[0].astype(BF16)

    @pl.when(i < n_used)
    def _():
        wait_block(slot)
        for r in range(BM):
            _row_copy(h2p_ref, sn_ref[0, 0, r], xbuf.at[1 - slot], r, sems.at[1 - slot]).start()
        lo, hi = _unpack_pair(xbuf[slot])
        xb = jnp.concatenate([lo.astype(BF16), hi.astype(BF16)], axis=1)
        a = _dot(xb, w1b[...])
        b = _dot(xb, w3b[...])
        hid = (jax.nn.silu(a) * b).astype(BF16)
        y = _dot(hid, w2b[...])
        y_ref[...] = _pack_pair(y[:, :D // 2], y[:, D // 2:])

    @pl.when(i >= n_used)
    def _():
        y_ref[...] = jnp.zeros_like(y_ref)

    @pl.when(i == n_used - 1)
    def _():
        wait_block(1 - slot)


def _experts(blk_expert, n_used, src, h2p, w1, w3, w2):
    def nxt(i, be, nu):
        return (jnp.minimum(i + 1, nu[0] - 1), 0, 0)

    grid_spec = pltpu.PrefetchScalarGridSpec(
        num_scalar_prefetch=2,
        grid=(N_BLOCKS,),
        in_specs=[pl.BlockSpec((1, 1, BM), lambda i, be, nu: (0, 0, 0), memory_space=pltpu.SMEM),
                  pl.BlockSpec((1, 1, BM), nxt, memory_space=pltpu.SMEM),
                  pl.BlockSpec(memory_space=pl.ANY),
                  pl.BlockSpec((1, D, E_HID), lambda i, be, nu: (be[i], 0, 0)),
                  pl.BlockSpec((1, D, E_HID), lambda i, be, nu: (be[i], 0, 0)),
                  pl.BlockSpec((1, E_HID, D), lambda i, be, nu: (be[i], 0, 0))],
        out_specs=pl.BlockSpec((BM, D // 2), lambda i, be, nu: (i, 0)),
        scratch_shapes=[pltpu.VMEM((D, E_HID), BF16), pltpu.VMEM((D, E_HID), BF16),
                        pltpu.VMEM((E_HID, D), BF16), pltpu.VMEM((2, BM, D // 2), U32),
                        pltpu.SemaphoreType.DMA((2,))],
    )
    src3 = src.reshape(N_BLOCKS, 1, BM)
    return pl.pallas_call(
        _experts_kernel,
        grid_spec=grid_spec,
        out_shape=jax.ShapeDtypeStruct((N_BLOCKS * BM, D // 2), U32),
        compiler_params=_cp(("arbitrary",)),
        name="experts",
    )(blk_expert, n_used, src3, src3, h2p, w1, w3, w2)


def _final_kernel(dcur_ref, dnext_ref, ys_ref, x1_ref, rwt_ref, mods_ref, g2_ref, b2_ref, oc_ref, ol_ref,
                  buf, sems):
    i = pl.program_id(0)
    n = pl.num_programs(0)
    slot = i % 2

    def issue(d_ref, s):
        def body(r, carry):
            for k in range(TOP_K):
                _row_copy(ys_ref, d_ref[0, 0, TOP_K * r + k], buf.at[s, k], r, sems.at[s]).start()
            return carry

        lax.fori_loop(0, TM, body, 0, unroll=8)

    @pl.when(i == 0)
    def _():
        issue(dcur_ref, 0)

    @pl.when(i + 1 < n)
    def _():
        issue(dnext_ref, 1 - slot)

    for k in range(TOP_K):
        pltpu.make_async_copy(ys_ref.at[pl.ds(0, TM)], buf.at[slot, k], sems.at[slot]).wait()

    w = rwt_ref[...]
    lo0, hi0 = _unpack_pair(buf[slot, 0])
    lo1, hi1 = _unpack_pair(buf[slot, 1])
    w0, w1 = w[:, 0:1], w[:, 1:2]
    moe = jnp.concatenate([lo0 * w0 + lo1 * w1, hi0 * w0 + hi1 * w1], axis=1)
    gate2 = mods_ref[0][5:6]
    out = _layer_norm(ALPHA * x1_ref[...] + gate2 * moe, g2_ref[...], b2_ref[...])

    @pl.when(i < NP_TILES)
    def _():
        oc_ref[...] = out

    @pl.when(i >= NP_TILES)
    def _():
        ol_ref[...] = out


def _final(dest3, ys, x1, rwt, mods, g2, b2):
    n = T // TM
    return pl.pallas_call(
        _final_kernel,
        grid=(n,),
        in_specs=[pl.BlockSpec((1, 1, TOP_K * TM), lambda i: (i, 0, 0), memory_space=pltpu.SMEM),
                  pl.BlockSpec((1, 1, TOP_K * TM), lambda i: (jnp.minimum(i + 1, n - 1), 0, 0),
                               memory_space=pltpu.SMEM),
                  pl.BlockSpec(memory_space=pl.ANY),
                  pl.BlockSpec((TM, D), lambda i: (i, 0)),
                  pl.BlockSpec((TM, TOP_K), lambda i: (i, 0)),
                  pl.BlockSpec((1, 6, D), lambda i: (_mods_index(i), 0, 0)),
                  _const_spec((1, D)), _const_spec((1, D))],
        out_specs=[pl.BlockSpec((TM, D), lambda i: (jnp.minimum(i, NP_TILES - 1), 0)),
                   pl.BlockSpec((TM, D), lambda i: (jnp.maximum(i - NP_TILES, 0), 0))],
        out_shape=[jax.ShapeDtypeStruct((T_CTX, D), F32), jax.ShapeDtypeStruct((T_LAT, D), F32)],
        scratch_shapes=[pltpu.VMEM((2, TOP_K, TM, D // 2), U32), pltpu.SemaphoreType.DMA((2,))],
        compiler_params=_cp(("arbitrary",)),
        name="final",
    )(dest3, dest3, ys, x1, rwt, mods, g2, b2)


def _rope_tables():
    n = N_LAT
    rows = n // GRID_W
    row = jnp.repeat(jnp.arange(rows, dtype=F32), GRID_W)
    col = jnp.tile(jnp.arange(GRID_W, dtype=F32), rows)
    half = QK_ROPE // 2
    inv = ROPE_THETA ** (-jnp.arange(0, half, 2, dtype=F32) / half)
    cr, sr = jnp.cos(row[:, None] * inv), jnp.sin(row[:, None] * inv)
    cc, sc = jnp.cos(col[:, None] * inv), jnp.sin(col[:, None] * inv)
    z16 = jnp.zeros_like(cr)
    pad = lambda v: jnp.zeros((n, LANES - QK_ROPE), F32) + v
    cos = jnp.concatenate([cr, cr, cc, cc, pad(1.0)], axis=1)
    s_plus = jnp.concatenate([z16, sr, z16, sc, pad(0.0)], axis=1)
    s_minus = jnp.concatenate([-sr, z16, -sc, z16, pad(0.0)], axis=1)
    ident = lambda v: jnp.full((TM, LANES), v, F32)
    return (jnp.concatenate([cos, ident(1.0)], 0), jnp.concatenate([s_plus, ident(0.0)], 0),
            jnp.concatenate([s_minus, ident(0.0)], 0))


def _block_plan(ridx):
    e_flat = ridx.reshape(-1)
    onehot = (e_flat[:, None] == jnp.arange(N_EXP, dtype=I32)[None, :]).astype(I32)
    csum = jnp.cumsum(onehot, axis=0)
    rank = jnp.sum(onehot * csum, axis=1) - 1
    counts = csum[-1]
    nblk = (counts + BM - 1) // BM
    bends = jnp.cumsum(nblk)
    bstarts = bends - nblk
    dest = bstarts[e_flat] * BM + rank
    blk = jnp.arange(N_BLOCKS, dtype=I32)
    blk_expert = jnp.minimum(jnp.sum((blk[:, None] >= bends[None, :]).astype(I32), axis=1), N_EXP - 1)
    token = jnp.arange(N_ASSIGN, dtype=I32) // TOP_K
    src = jnp.zeros((N_BLOCKS * BM,), I32).at[dest].set(token, unique_indices=True)
    return dest.astype(I32), src, blk_expert.astype(I32), bends[-1:].astype(I32)


def kernel(x_prompt, x_sample, cache_ckv, cache_krope, c, c_ctx, w_ada, b_ada, w_in, q_norm_g, w_uq,
           kv_norm_g, w_ukv, gmlp_ln_g, gmlp_ln_b, w_spatial, b_spatial, w_out, ln1_g, ln1_b, w_group,
           b_group, w_router, b_router, w1, w3, w2, ln2_g, ln2_b):
    l = 0
    xc = x_prompt.reshape(T_CTX, D)
    xl = x_sample.reshape(T_LAT, D)

    cond16 = jnp.concatenate([c, c_ctx[None, :], jnp.zeros((16 - N_LAT_B - 1, D), F32)], axis=0)
    mods = _adaln(cond16, w_ada[l], b_ada[l][None, :]).reshape(16, 6, D)

    wi = w_in[l]
    o_kr = QL + KVL
    o_u = o_kr + QK_ROPE
    win = jnp.concatenate([wi[:, :o_kr], wi[:, o_u:], wi[:, o_kr:o_u],
                           jnp.zeros((D, LANES - QK_ROPE), F32)], axis=1).astype(BF16)
    wuq = jnp.pad(w_uq[l].reshape(QL, H, QK_NOPE + QK_ROPE),
                  ((0, 0), (0, 0), (0, HEAD_PAD - QK_NOPE - QK_ROPE))).reshape(QL, H * HEAD_PAD).astype(BF16)
    wkv = w_ukv[l].reshape(KVL, H, QK_NOPE + V_DIM)
    wuk = wkv[:, :, :QK_NOPE].reshape(KVL, H * QK_NOPE).astype(BF16)
    wuvt = wkv[:, :, QK_NOPE:].reshape(KVL, H * V_DIM).T.astype(BF16)
    ws = w_spatial[l].astype(BF16)
    bsb = jnp.broadcast_to(b_spatial[l][:, :, None], (G_GROUPS, CHUNK, G_DIM))
    wrt = jnp.concatenate([w_group[l], w_router[l], jnp.zeros((D, LANES - N_GRP - N_EXP), F32)],
                          axis=1).astype(BF16)
    brt = jnp.concatenate([b_group[l], b_router[l], jnp.zeros((LANES - N_GRP - N_EXP,), F32)])[None, :]
    cos_t, sp_t, sm_t = _rope_tables()
    row = lambda v: v[l][None, :]

    q, k, vt, gm, ckv_n, kr = _proj(xc, xl, mods, win, row(q_norm_g), wuq, row(kv_norm_g), wuk, wuvt,
                                   row(gmlp_ln_g), row(gmlp_ln_b), ws, bsb, cos_t, sp_t, sm_t)
    kc, vct = _kvexp(cache_ckv[:, l].reshape(N_LAT_B * PAST, KVL),
                     cache_krope[:, l].reshape(N_LAT_B * PAST, QK_ROPE), wuk, wuvt)
    attn_c = _attn_ctx(q, k, vt)
    attn_l = _attn_lat(q, k, vt, kc, vct)

    x1, h2p, ridx, rwt = _post(attn_c, attn_l, gm, xc, xl, mods, w_out[l].astype(BF16), row(ln1_g), row(ln1_b),
                               wrt, brt)
    dest, src, blk_expert, n_used = _block_plan(ridx)
    ys = _experts(blk_expert, n_used, src, h2p, w1[l], w3[l], w2[l])
    y_c, y_l = _final(dest.reshape(T // TM, 1, TOP_K * TM), ys, x1, rwt, mods, row(ln2_g), row(ln2_b))

    return (y_c.reshape(N_CTX_B, N_CTX, D), y_l.reshape(N_LAT_B, N_LAT, D),
            ckv_n.reshape(N_CTX_B, 1, N_CTX, KVL), kr.reshape(N_CTX_B, 1, N_CTX, QK_ROPE))
```

```python
import functools

import jax
import jax.numpy as jnp
import numpy as np
from jax import lax
from jax.experimental import pallas as pl
from jax.experimental.pallas import tpu as pltpu

F32 = jnp.float32
BF16 = jnp.bfloat16
U32 = jnp.uint32
I32 = jnp.int32

D = 2048
N_CTX_B, N_CTX = 32, 256
N_LAT_B, N_LAT = 4, 4096
PAST = 512
GRID_W = 64
H = 8
QK_NOPE, QK_ROPE, V_DIM = 128, 64, 128
QL, KVL = 512, 512
GW = 1024
G_GROUPS, G_DIM, CHUNK = 8, 128, 128
N_GRP, E_PER_GRP, N_EXP, TOP_K = 4, 8, 32, 2
E_HID = 512
ROPE_THETA = 10000.0
EPS = 1e-6
ALPHA = 2.0 ** 0.25
SM_SCALE = (QK_NOPE + QK_ROPE) ** -0.5
Q_SCALE = SM_SCALE * float(np.log2(np.e))

T_CTX = N_CTX_B * N_CTX
T_LAT = N_LAT_B * N_LAT
T = T_CTX + T_LAT
N_ASSIGN = T * TOP_K

LANES = 128
HEAD_PAD = 256
V_AUG = V_DIM + 16
TM = 256
TP = 512
SUB = 256
TD = 1024
TQ = 1024
TK = 512
BM = 256
N_BLOCKS = N_ASSIGN // BM + N_EXP
N_ZERO = 2 * N_EXP
VMEM_LIMIT = 56 * 1024 * 1024

NP_TILES = T_CTX // TM
LAT_TILES_PER_B = N_LAT // TM
IN_W = QL + KVL + 2 * GW + LANES


def _cp(sem):
    return pltpu.CompilerParams(dimension_semantics=sem, vmem_limit_bytes=VMEM_LIMIT)


def _const_spec(shape):
    nd = len(shape)
    return pl.BlockSpec(shape, lambda *a: (0,) * nd, pipeline_mode=pl.Buffered(1))


def _dot(a, b):
    return jnp.dot(a, b, preferred_element_type=F32)


def _dot_nt(a, b):
    return lax.dot_general(a, b, (((1,), (1,)), ((), ())), preferred_element_type=F32)


def _layer_norm(x, g, b):
    mu = jnp.mean(x, -1, keepdims=True)
    xc = x - mu
    var = jnp.mean(xc * xc, -1, keepdims=True)
    return xc * lax.rsqrt(var + EPS) * g + b


def _rms_norm(x, g):
    return x * lax.rsqrt(jnp.mean(x * x, -1, keepdims=True) + EPS) * g


def _gelu(x):
    return 0.5 * x * (1.0 + lax.erf(x * np.float32(np.sqrt(0.5))))


def _pack_pair(lo, hi):
    lo_b = lax.bitcast_convert_type(lo.astype(BF16).astype(F32), U32)
    hi_b = lax.bitcast_convert_type(hi.astype(BF16).astype(F32), U32)
    return hi_b | (lo_b >> 16)


def _unpack_pair(w):
    lo = lax.bitcast_convert_type(w << 16, F32)
    hi = lax.bitcast_convert_type(w & jnp.uint32(0xFFFF0000), F32)
    return lo, hi


def _adaln_kernel(cond_ref, w_ref, b_ref, o_ref):
    s = jax.nn.silu(cond_ref[...])
    s_hi = s.astype(BF16)
    s_lo = (s - s_hi.astype(F32)).astype(BF16)
    w = w_ref[...]
    w_hi = w.astype(BF16)
    w_lo = (w - w_hi.astype(F32)).astype(BF16)
    o_ref[...] = _dot(s_hi, w_hi) + _dot(s_lo, w_hi) + _dot(s_hi, w_lo) + b_ref[...]


def _adaln(cond16, w_ada, b_ada):
    tn = 1024
    n = w_ada.shape[1]
    return pl.pallas_call(
        _adaln_kernel,
        grid=(n // tn,),
        in_specs=[pl.BlockSpec((16, D), lambda j: (0, 0)),
                  pl.BlockSpec((D, tn), lambda j: (0, j)),
                  pl.BlockSpec((1, tn), lambda j: (0, j))],
        out_specs=pl.BlockSpec((16, tn), lambda j: (0, j)),
        out_shape=jax.ShapeDtypeStruct((16, n), F32),
        compiler_params=_cp(("arbitrary",)),
        name="adaln",
    )(cond16, w_ada, b_ada)


def _expand_kv(ckv_n, kr128, wuk_ref, wuvt_ref, k_ref, vt_ref):
    cb = ckv_n.astype(BF16)
    kn = _dot(cb, wuk_ref[...])
    vt = _dot_nt(wuvt_ref[...], cb).astype(BF16)
    ones = jnp.ones((V_AUG - V_DIM, vt.shape[1]), BF16)
    for h in range(H):
        vt_ref[h * V_AUG:h * V_AUG + V_DIM, :] = vt[h * V_DIM:(h + 1) * V_DIM]
        vt_ref[h * V_AUG + V_DIM:(h + 1) * V_AUG, :] = ones
    krb = kr128.astype(BF16)
    for h in range(H):
        k_ref[:, h * HEAD_PAD:h * HEAD_PAD + QK_NOPE] = kn[:, h * QK_NOPE:(h + 1) * QK_NOPE].astype(BF16)
        k_ref[:, h * HEAD_PAD + QK_NOPE:(h + 1) * HEAD_PAD] = krb


def _rope128(x, cos, s_plus, s_minus):
    return (x * cos + pltpu.roll(x, 16, 1) * s_plus + pltpu.roll(x, LANES - 16, 1) * s_minus)


def _proj_kernel(xc_ref, xl_ref, mods_ref, win_ref, qg_ref, wuq_ref, kvg_ref, wuk_ref, wuv_ref,
                 lng_ref, lnb_ref, ws_ref, bsb_ref, cos_ref, sp_ref, sm_ref,
                 q_ref, k_ref, v_ref, gm_ref, ckv_ref, kr_ref):
    i = pl.program_id(0)
    x = jnp.where(i < NP_TILES, xc_ref[...], xl_ref[...])
    m = mods_ref[0]
    shift1, scale1 = m[0:1], m[1:2]
    hmod = (x * (1.0 + scale1) + shift1).astype(BF16)
    proj = _dot(hmod, win_ref[...])
    cq = proj[:, 0:QL]
    ckv = proj[:, QL:QL + KVL]
    u = proj[:, QL + KVL:QL + KVL + GW]
    v = proj[:, QL + KVL + GW:QL + KVL + 2 * GW]
    kr = proj[:, QL + KVL + 2 * GW:IN_W]

    cos, s_plus, s_minus = cos_ref[...], sp_ref[...], sm_ref[...]

    cq_n = _rms_norm(cq, qg_ref[...]).astype(BF16)
    q = _dot(cq_n, wuq_ref[...]) * Q_SCALE
    for h in range(H):
        q_ref[:, h * HEAD_PAD:h * HEAD_PAD + QK_NOPE] = q[:, h * HEAD_PAD:h * HEAD_PAD + QK_NOPE].astype(BF16)
        q_ref[:, h * HEAD_PAD + QK_NOPE:(h + 1) * HEAD_PAD] = _rope128(
            q[:, h * HEAD_PAD + QK_NOPE:(h + 1) * HEAD_PAD], cos, s_plus, s_minus).astype(BF16)

    ckv_n = _rms_norm(ckv, kvg_ref[...])

    @pl.when(i < NP_TILES)
    def _():
        ckv_ref[...] = ckv_n
        kr_ref[...] = kr[:, :QK_ROPE]

    _expand_kv(ckv_n, _rope128(kr, cos, s_plus, s_minus), wuk_ref, wuv_ref, k_ref, v_ref)

    gu = _gelu(u)
    vn = _layer_norm(_gelu(v), lng_ref[...], lnb_ref[...]).astype(BF16)
    for c in range(TM // CHUNK):
        rows = slice(c * CHUNK, (c + 1) * CHUNK)
        for g in range(G_GROUPS):
            cols = slice(g * G_DIM, (g + 1) * G_DIM)
            s = _dot(ws_ref[g], vn[rows, cols]) + bsb_ref[g]
            gm_ref[rows, cols] = (gu[rows, cols] * s).astype(BF16)


def _mods_index(i, tile=TM):
    return jnp.where(i < T_CTX // tile, N_LAT_B, (i - T_CTX // tile) // (N_LAT // tile))


def _proj(xc, xl, mods, win, qg, wuq, kvg, wuk, wuv, lng, lnb, ws, bsb, cos_t, sp_t, sm_t):
    def tok(width):
        return pl.BlockSpec((TM, width), lambda i: (i, 0))

    rope_spec = pl.BlockSpec(
        (TM, LANES),
        lambda i: (jnp.where(i < NP_TILES, LAT_TILES_PER_B, (i - NP_TILES) % LAT_TILES_PER_B), 0))
    return pl.pallas_call(
        _proj_kernel,
        grid=(T // TM,),
        in_specs=[
            pl.BlockSpec((TM, D), lambda i: (jnp.minimum(i, NP_TILES - 1), 0)),
            pl.BlockSpec((TM, D), lambda i: (jnp.maximum(i - NP_TILES, 0), 0)),
            pl.BlockSpec((1, 6, D), lambda i: (_mods_index(i), 0, 0)),
            _const_spec((D, IN_W)), _const_spec((1, QL)), _const_spec((QL, H * HEAD_PAD)),
            _const_spec((1, KVL)), _const_spec((KVL, H * QK_NOPE)), _const_spec((H * V_DIM, KVL)),
            _const_spec((1, GW)), _const_spec((1, GW)),
            _const_spec((G_GROUPS, CHUNK, CHUNK)), _const_spec((G_GROUPS, CHUNK, G_DIM)),
            rope_spec, rope_spec, rope_spec,
        ],
        out_specs=[tok(H * HEAD_PAD), tok(H * HEAD_PAD),
                   pl.BlockSpec((H * V_AUG, TM), lambda i: (0, i)), tok(GW),
                   pl.BlockSpec((TM, KVL), lambda i: (jnp.minimum(i, NP_TILES - 1), 0)),
                   pl.BlockSpec((TM, QK_ROPE), lambda i: (jnp.minimum(i, NP_TILES - 1), 0))],
        out_shape=[
            jax.ShapeDtypeStruct((T, H * HEAD_PAD), BF16),
            jax.ShapeDtypeStruct((T, H * HEAD_PAD), BF16),
            jax.ShapeDtypeStruct((H * V_AUG, T), BF16),
            jax.ShapeDtypeStruct((T, GW), BF16),
            jax.ShapeDtypeStruct((T_CTX, KVL), F32),
            jax.ShapeDtypeStruct((T_CTX, QK_ROPE), F32),
        ],
        compiler_params=_cp(("arbitrary",)),
        name="proj",
    )(xc, xl, mods, win, qg, wuq, kvg, wuk, wuv, lng, lnb, ws, bsb, cos_t, sp_t, sm_t)


def _kvexp_kernel(ckv_ref, kr_ref, wuk_ref, wuv_ref, k_ref, v_ref):
    kr = kr_ref[...]
    kr128 = jnp.concatenate([kr, jnp.zeros_like(kr)], axis=1)
    _expand_kv(ckv_ref[...], kr128, wuk_ref, wuv_ref, k_ref, v_ref)


def _kvexp(ckv, kr, wuk, wuv):
    rows = ckv.shape[0]
    return pl.pallas_call(
        _kvexp_kernel,
        grid=(rows // TM,),
        in_specs=[pl.BlockSpec((TM, KVL), lambda i: (i, 0)), pl.BlockSpec((TM, QK_ROPE), lambda i: (i, 0)),
                  _const_spec((KVL, H * QK_NOPE)), _const_spec((H * V_DIM, KVL))],
        out_specs=[pl.BlockSpec((TM, H * HEAD_PAD), lambda i: (i, 0)),
                   pl.BlockSpec((H * V_AUG, TM), lambda i: (0, i))],
        out_shape=[jax.ShapeDtypeStruct((rows, H * HEAD_PAD), BF16),
                   jax.ShapeDtypeStruct((H * V_AUG, rows), BF16)],
        compiler_params=_cp(("arbitrary",)),
        name="kvexp",
    )(ckv, kr, wuk, wuv)


def _probs(s, m):
    m_new = jnp.maximum(m, jnp.max(s, 0, keepdims=True))
    return m_new, jnp.exp2(m - m_new), jnp.exp2(s - m_new).astype(BF16)


def _attn_finish(acc):
    return (acc[:V_DIM] / acc[V_DIM:V_DIM + 1]).T.astype(BF16)


def _attn_ctx_kernel(q_ref, k_ref, vt_ref, o_ref):
    m0 = jnp.full((1, N_CTX), -jnp.inf, F32)
    for h in range(H):
        qh = q_ref[:, h * HEAD_PAD:(h + 1) * HEAD_PAD]
        kh = k_ref[:, h * HEAD_PAD:(h + 1) * HEAD_PAD]
        _, _, p = _probs(_dot_nt(kh, qh), m0)
        acc = _dot(vt_ref[h * V_AUG:(h + 1) * V_AUG, :], p)
        o_ref[:, h * V_DIM:(h + 1) * V_DIM] = _attn_finish(acc)


def _attn_ctx(q, k, vt):
    return pl.pallas_call(
        _attn_ctx_kernel,
        grid=(N_CTX_B,),
        in_specs=[pl.BlockSpec((N_CTX, H * HEAD_PAD), lambda b: (b, 0)),
                  pl.BlockSpec((N_CTX, H * HEAD_PAD), lambda b: (b, 0)),
                  pl.BlockSpec((H * V_AUG, N_CTX), lambda b: (0, b))],
        out_specs=pl.BlockSpec((N_CTX, H * V_DIM), lambda b: (b, 0)),
        out_shape=jax.ShapeDtypeStruct((T_CTX, H * V_DIM), BF16),
        compiler_params=_cp(("arbitrary",)),
        name="attn_ctx",
    )(q, k, vt)


def _attn_lat_kernel(q_ref, k_ref, vt_ref, kc_ref, vct_ref, o_ref):
    q = q_ref[...]
    n_lat = N_LAT // TK

    def keys(c):
        return k_ref[c * TK:(c + 1) * TK, :] if c < n_lat else kc_ref[...]

    def values_t(c):
        return vt_ref[:, c * TK:(c + 1) * TK] if c < n_lat else vct_ref[...]

    m = jnp.full((1, TQ), -jnp.inf, F32)
    acc = jnp.zeros((V_AUG, TQ), F32)
    s_next = _dot_nt(keys(0), q)
    p_prev = alpha_prev = None
    for c in range(n_lat + 1):
        s = s_next
        if c < n_lat:
            s_next = _dot_nt(keys(c + 1), q)
        if c > 0:
            acc = alpha_prev * acc + _dot(values_t(c - 1), p_prev)
        m, alpha_prev, p_prev = _probs(s, m)
    acc = alpha_prev * acc + _dot(values_t(n_lat), p_prev)
    o_ref[...] = _attn_finish(acc)


def _attn_lat(q, k, vt, kc, vct):
    qb0 = T_CTX // TQ
    qpb = N_LAT // TQ
    kb0 = T_CTX // N_LAT
    return pl.pallas_call(
        _attn_lat_kernel,
        grid=(N_LAT_B, H, qpb),
        in_specs=[pl.BlockSpec((TQ, HEAD_PAD), lambda b, h, qi: (qb0 + b * qpb + qi, h)),
                  pl.BlockSpec((N_LAT, HEAD_PAD), lambda b, h, qi: (kb0 + b, h)),
                  pl.BlockSpec((V_AUG, N_LAT), lambda b, h, qi: (h, kb0 + b)),
                  pl.BlockSpec((PAST, HEAD_PAD), lambda b, h, qi: (b, h)),
                  pl.BlockSpec((V_AUG, PAST), lambda b, h, qi: (h, b))],
        out_specs=pl.BlockSpec((TQ, V_DIM), lambda b, h, qi: (b * qpb + qi, h)),
        out_shape=jax.ShapeDtypeStruct((T_LAT, H * V_DIM), BF16),
        compiler_params=_cp(("arbitrary", "arbitrary", "arbitrary")),
        name="attn_lat",
    )(q, k, vt, kc, vct)


def _route(logits):
    lane_i = lax.broadcasted_iota(I32, logits.shape, 1)
    lane = lane_i.astype(F32)
    neg = jnp.float32(-jnp.inf)
    far = jnp.float32(LANES)
    gl = jnp.where(lane_i < N_GRP, logits, neg)
    gmax = jnp.max(gl, -1, keepdims=True)
    gidx = jnp.min(jnp.where(gl == gmax, lane, far), -1, keepdims=True)
    p_top = 1.0 / jnp.sum(jnp.exp(gl - gmax), -1, keepdims=True)
    grp_of_lane = ((lane_i - N_GRP) >> 3).astype(F32)
    in_grp = (lane_i >= N_GRP) & (lane_i < N_GRP + N_EXP) & (grp_of_lane == gidx)
    el = jnp.where(in_grp, logits, neg)
    m1 = jnp.max(el, -1, keepdims=True)
    i1 = jnp.min(jnp.where(el == m1, lane, far), -1, keepdims=True)
    el2 = jnp.where(lane == i1, neg, el)
    m2 = jnp.max(el2, -1, keepdims=True)
    i2 = jnp.min(jnp.where(el2 == m2, lane, far), -1, keepdims=True)
    e2 = jnp.exp(m2 - m1)
    w1 = p_top / (1.0 + e2)
    w2 = p_top * e2 / (1.0 + e2)
    two = lax.broadcasted_iota(I32, (logits.shape[0], TOP_K), 1)
    return (jnp.where(two == 0, i1, i2) - N_GRP).astype(I32), jnp.where(two == 0, w1, w2)


def _post_kernel(ac_ref, al_ref, gm_ref, xc_ref, xl_ref, mods_ref, wout_ref, g1_ref, b1_ref, wrt_ref,
                 brt_ref, x1_ref, h2p_ref, ridx_ref, rwt_ref):
    i = pl.program_id(0)
    is_ctx = i < T_CTX // TP
    m = mods_ref[0]
    gate1, shift2, scale2 = m[2:3], m[3:4], m[4:5]

    def mix_of(s):
        rows = slice(s * SUB, (s + 1) * SUB)
        attn = jnp.where(is_ctx, ac_ref[rows, :], al_ref[rows, :])
        return _dot(jnp.concatenate([attn, gm_ref[rows, :]], axis=1), wout_ref[...])

    n_sub = TP // SUB
    mix_next = mix_of(0)
    for s in range(n_sub):
        rows = slice(s * SUB, (s + 1) * SUB)
        mix = mix_next
        if s + 1 < n_sub:
            mix_next = mix_of(s + 1)
        x = jnp.where(is_ctx, xc_ref[rows, :], xl_ref[rows, :])
        x1 = _layer_norm(ALPHA * x + gate1 * mix, g1_ref[...], b1_ref[...])
        x1_ref[rows, :] = x1
        h2 = x1 * (1.0 + scale2) + shift2
        h2p_ref[rows, :] = _pack_pair(h2[:, :D // 2], h2[:, D // 2:])
        ridx, rwt = _route(_dot(h2.astype(BF16), wrt_ref[...]) + brt_ref[...])
        ridx_ref[rows, :] = ridx
        rwt_ref[rows, :] = rwt


def _post(attn_c, attn_l, gm, xc, xl, mods, wout, g1, b1, wrt, brt):
    npt = T_CTX // TP

    def tok(width):
        return pl.BlockSpec((TP, width), lambda i: (i, 0))

    def ctx(width):
        return pl.BlockSpec((TP, width), lambda i: (jnp.minimum(i, npt - 1), 0))

    def lat(width):
        return pl.BlockSpec((TP, width), lambda i: (jnp.maximum(i - npt, 0), 0))

    return pl.pallas_call(
        _post_kernel,
        grid=(T // TP,),
        in_specs=[ctx(H * V_DIM), lat(H * V_DIM), tok(GW), ctx(D), lat(D),
                  pl.BlockSpec((1, 6, D), lambda i: (_mods_index(i, TP), 0, 0)),
                  _const_spec((D, D)), _const_spec((1, D)), _const_spec((1, D)),
                  _const_spec((D, LANES)), _const_spec((1, LANES))],
        out_specs=[tok(D), tok(D // 2), tok(TOP_K), tok(TOP_K)],
        out_shape=[jax.ShapeDtypeStruct((T, D), F32),
                   jax.ShapeDtypeStruct((T, D // 2), U32),
                   jax.ShapeDtypeStruct((T, TOP_K), I32),
                   jax.ShapeDtypeStruct((T, TOP_K), F32)],
        compiler_params=_cp(("arbitrary",)),
        name="post",
    )(attn_c, attn_l, gm, xc, xl, mods, wout, g1, b1, wrt, brt)


def _row_copy(src, src_row, dst, dst_row, sem):
    return pltpu.make_async_copy(src.at[pl.ds(src_row, 1)], dst.at[pl.ds(dst_row, 1)], sem)


def _dispatch_kernel(zb_ref, zv_ref, dest_ref, h2p_ref, xs_ref, zbuf, sem):
    i = pl.program_id(0)

    @pl.when(i == 0)
    def _():
        zbuf[...] = jnp.zeros_like(zbuf)

        def zero_block(j):
            return pltpu.make_async_copy(zbuf, xs_ref.at[pl.ds(zb_ref[j] * BM, BM)], sem)

        def start(j, carry):
            @pl.when(zv_ref[j] != 0)
            def _():
                zero_block(j).start()
            return carry

        def wait(j, carry):
            @pl.when(zv_ref[j] != 0)
            def _():
                zero_block(j).wait()
            return carry

        lax.fori_loop(0, N_ZERO, start, 0)
        lax.fori_loop(0, N_ZERO, wait, 0)

    def issue(r, carry):
        for k in range(TOP_K):
            _row_copy(h2p_ref, r, xs_ref, dest_ref[0, 0, TOP_K * r + k], sem).start()
        return carry

    lax.fori_loop(0, TD, issue, 0, unroll=8)
    for _ in range(TOP_K):
        pltpu.make_async_copy(h2p_ref, xs_ref.at[pl.ds(0, TD)], sem).wait()


def _dispatch(zero_blk, zero_valid, dest, h2p):
    grid_spec = pltpu.PrefetchScalarGridSpec(
        num_scalar_prefetch=2,
        grid=(T // TD,),
        in_specs=[pl.BlockSpec((1, 1, TOP_K * TD), lambda i, zb, zv: (i, 0, 0), memory_space=pltpu.SMEM),
                  pl.BlockSpec((TD, D // 2), lambda i, zb, zv: (i, 0))],
        out_specs=pl.BlockSpec(memory_space=pl.ANY),
        scratch_shapes=[pltpu.VMEM((BM, D // 2), U32), pltpu.SemaphoreType.DMA(())],
    )
    return pl.pallas_call(
        _dispatch_kernel,
        grid_spec=grid_spec,
        out_shape=jax.ShapeDtypeStruct((N_BLOCKS * BM, D // 2), U32),
        compiler_params=_cp(("arbitrary",)),
        name="dispatch",
    )(zero_blk, zero_valid, dest.reshape(T // TD, 1, TOP_K * TD), h2p)


def _experts_kernel(be_ref, bv_ref, x_ref, w1_ref, w3_ref, w2_ref, y_ref, w1b, w3b, w2b):
    i = pl.program_id(0)
    e = be_ref[i]
    nv = bv_ref[i]
    prev = be_ref[jnp.maximum(i - 1, 0)]

    @pl.when((i == 0) | (e != prev))
    def _():
        w1b[...] = w1_ref[0].astype(BF16)
        w3b[...] = w3_ref[0].astype(BF16)
        w2b[...] = w2_ref[0].astype(BF16)

    @pl.when(nv > 0)
    def _():
        lo, hi = _unpack_pair(x_ref[...])
        xb = jnp.concatenate([lo.astype(BF16), hi.astype(BF16)], axis=1)
        a = _dot(xb, w1b[...])
        b = _dot(xb, w3b[...])
        hid = (jax.nn.silu(a) * b).astype(BF16)
        y = _dot(hid, w2b[...])
        y_ref[...] = _pack_pair(y[:, :D // 2], y[:, D // 2:])

    @pl.when(nv == 0)
    def _():
        y_ref[...] = jnp.zeros_like(y_ref)


def _experts(blk_expert, blk_valid, xs, w1, w3, w2):
    grid_spec = pltpu.PrefetchScalarGridSpec(
        num_scalar_prefetch=2,
        grid=(N_BLOCKS,),
        in_specs=[pl.BlockSpec((BM, D // 2), lambda i, be, bv: (i, 0)),
                  pl.BlockSpec((1, D, E_HID), lambda i, be, bv: (be[i], 0, 0)),
                  pl.BlockSpec((1, D, E_HID), lambda i, be, bv: (be[i], 0, 0)),
                  pl.BlockSpec((1, E_HID, D), lambda i, be, bv: (be[i], 0, 0))],
        out_specs=pl.BlockSpec((BM, D // 2), lambda i, be, bv: (i, 0)),
        scratch_shapes=[pltpu.VMEM((D, E_HID), BF16), pltpu.VMEM((D, E_HID), BF16),
                        pltpu.VMEM((E_HID, D), BF16)],
    )
    return pl.pallas_call(
        _experts_kernel,
        grid_spec=grid_spec,
        out_shape=jax.ShapeDtypeStruct((N_BLOCKS * BM, D // 2), U32),
        compiler_params=_cp(("arbitrary",)),
        name="experts",
    )(blk_expert, blk_valid, xs, w1, w3, w2)


def _final_kernel(dcur_ref, dnext_ref, ys_ref, x1_ref, rwt_ref, mods_ref, g2_ref, b2_ref, oc_ref, ol_ref,
                  buf, sems):
    i = pl.program_id(0)
    n = pl.num_programs(0)
    slot = i % 2

    def issue(d_ref, s):
        def body(r, carry):
            for k in range(TOP_K):
                _row_copy(ys_ref, d_ref[0, 0, TOP_K * r + k], buf.at[s, k], r, sems.at[s]).start()
            return carry

        lax.fori_loop(0, TM, body, 0, unroll=8)

    @pl.when(i == 0)
    def _():
        issue(dcur_ref, 0)

    @pl.when(i + 1 < n)
    def _():
        issue(dnext_ref, 1 - slot)

    for k in range(TOP_K):
        pltpu.make_async_copy(ys_ref.at[pl.ds(0, TM)], buf.at[slot, k], sems.at[slot]).wait()

    w = rwt_ref[...]
    lo0, hi0 = _unpack_pair(buf[slot, 0])
    lo1, hi1 = _unpack_pair(buf[slot, 1])
    w0, w1 = w[:, 0:1], w[:, 1:2]
    moe = jnp.concatenate([lo0 * w0 + lo1 * w1, hi0 * w0 + hi1 * w1], axis=1)
    gate2 = mods_ref[0][5:6]
    out = _layer_norm(ALPHA * x1_ref[...] + gate2 * moe, g2_ref[...], b2_ref[...])

    @pl.when(i < NP_TILES)
    def _():
        oc_ref[...] = out

    @pl.when(i >= NP_TILES)
    def _():
        ol_ref[...] = out


def _final(dest3, ys, x1, rwt, mods, g2, b2):
    n = T // TM
    return pl.pallas_call(
        _final_kernel,
        grid=(n,),
        in_specs=[pl.BlockSpec((1, 1, TOP_K * TM), lambda i: (i, 0, 0), memory_space=pltpu.SMEM),
                  pl.BlockSpec((1, 1, TOP_K * TM), lambda i: (jnp.minimum(i + 1, n - 1), 0, 0),
                               memory_space=pltpu.SMEM),
                  pl.BlockSpec(memory_space=pl.ANY),
                  pl.BlockSpec((TM, D), lambda i: (i, 0)),
                  pl.BlockSpec((TM, TOP_K), lambda i: (i, 0)),
                  pl.BlockSpec((1, 6, D), lambda i: (_mods_index(i), 0, 0)),
                  _const_spec((1, D)), _const_spec((1, D))],
        out_specs=[pl.BlockSpec((TM, D), lambda i: (jnp.minimum(i, NP_TILES - 1), 0)),
                   pl.BlockSpec((TM, D), lambda i: (jnp.maximum(i - NP_TILES, 0), 0))],
        out_shape=[jax.ShapeDtypeStruct((T_CTX, D), F32), jax.ShapeDtypeStruct((T_LAT, D), F32)],
        scratch_shapes=[pltpu.VMEM((2, TOP_K, TM, D // 2), U32), pltpu.SemaphoreType.DMA((2,))],
        compiler_params=_cp(("arbitrary",)),
        name="final",
    )(dest3, dest3, ys, x1, rwt, mods, g2, b2)


def _rope_tables():
    n = N_LAT
    rows = n // GRID_W
    f32 = np.float32
    row = np.repeat(np.arange(rows, dtype=f32), GRID_W)
    col = np.tile(np.arange(GRID_W, dtype=f32), rows)
    half = QK_ROPE // 2
    inv = (f32(ROPE_THETA) ** (-np.arange(0, half, 2, dtype=f32) / f32(half))).astype(f32)
    ang_r, ang_c = (row[:, None] * inv).astype(f32), (col[:, None] * inv).astype(f32)
    cr, sr = np.cos(ang_r).astype(f32), np.sin(ang_r).astype(f32)
    cc, sc = np.cos(ang_c).astype(f32), np.sin(ang_c).astype(f32)
    z16 = np.zeros_like(cr)
    pad = lambda v: np.full((n, LANES - QK_ROPE), v, f32)
    cos = np.concatenate([cr, cr, cc, cc, pad(1.0)], axis=1)
    s_plus = np.concatenate([z16, sr, z16, sc, pad(0.0)], axis=1)
    s_minus = np.concatenate([-sr, z16, -sc, z16, pad(0.0)], axis=1)
    ident = lambda v: np.full((TM, LANES), v, f32)
    return (jnp.asarray(np.concatenate([cos, ident(1.0)], 0)),
            jnp.asarray(np.concatenate([s_plus, ident(0.0)], 0)),
            jnp.asarray(np.concatenate([s_minus, ident(0.0)], 0)))


def _block_plan(ridx):
    e_flat = ridx.reshape(-1)
    onehot = (e_flat[:, None] == jnp.arange(N_EXP, dtype=I32)[None, :]).astype(I32)
    csum = jnp.cumsum(onehot, axis=0)
    rank = jnp.sum(onehot * csum, axis=1) - 1
    counts = csum[-1]
    nblk = (counts + BM - 1) // BM
    bends = jnp.cumsum(nblk)
    bstarts = bends - nblk
    dest = bstarts[e_flat] * BM + rank
    blk = jnp.arange(N_BLOCKS, dtype=I32)
    blk_expert = jnp.minimum(jnp.sum((blk[:, None] >= bends[None, :]).astype(I32), axis=1), N_EXP - 1)
    left = counts[blk_expert] - (blk - bstarts[blk_expert]) * BM
    blk_valid = jnp.where(blk < bends[-1], jnp.clip(left, 0, BM), 0).astype(I32)
    unused = bends[-1] + jnp.arange(N_EXP, dtype=I32)
    zero_blk = jnp.concatenate([jnp.maximum(bends - 1, 0), jnp.minimum(unused, N_BLOCKS - 1)])
    zero_valid = jnp.concatenate([counts % BM != 0, unused < N_BLOCKS])
    return (dest.astype(I32), blk_expert.astype(I32), blk_valid, zero_blk.astype(I32),
            zero_valid.astype(I32))


def kernel(x_prompt, x_sample, cache_ckv, cache_krope, c, c_ctx, w_ada, b_ada, w_in, q_norm_g, w_uq,
           kv_norm_g, w_ukv, gmlp_ln_g, gmlp_ln_b, w_spatial, b_spatial, w_out, ln1_g, ln1_b, w_group,
           b_group, w_router, b_router, w1, w3, w2, ln2_g, ln2_b):
    l = 0
    xc = x_prompt.reshape(T_CTX, D)
    xl = x_sample.reshape(T_LAT, D)

    cond16 = jnp.concatenate([c, c_ctx[None, :], jnp.zeros((16 - N_LAT_B - 1, D), F32)], axis=0)
    mods = _adaln(cond16, w_ada[l], b_ada[l][None, :]).reshape(16, 6, D)

    wi = w_in[l]
    o_kr = QL + KVL
    o_u = o_kr + QK_ROPE
    win = jnp.concatenate([wi[:, :o_kr], wi[:, o_u:], wi[:, o_kr:o_u],
                           jnp.zeros((D, LANES - QK_ROPE), F32)], axis=1).astype(BF16)
    wuq = jnp.pad(w_uq[l].reshape(QL, H, QK_NOPE + QK_ROPE),
                  ((0, 0), (0, 0), (0, HEAD_PAD - QK_NOPE - QK_ROPE))).reshape(QL, H * HEAD_PAD).astype(BF16)
    wkv = w_ukv[l].reshape(KVL, H, QK_NOPE + V_DIM)
    wuk = wkv[:, :, :QK_NOPE].reshape(KVL, H * QK_NOPE).astype(BF16)
    wuvt = wkv[:, :, QK_NOPE:].reshape(KVL, H * V_DIM).T.astype(BF16)
    ws = w_spatial[l].astype(BF16)
    bsb = jnp.broadcast_to(b_spatial[l][:, :, None], (G_GROUPS, CHUNK, G_DIM))
    wrt = jnp.concatenate([w_group[l], w_router[l], jnp.zeros((D, LANES - N_GRP - N_EXP), F32)],
                          axis=1).astype(BF16)
    brt = jnp.concatenate([b_group[l], b_router[l], jnp.zeros((LANES - N_GRP - N_EXP,), F32)])[None, :]
    cos_t, sp_t, sm_t = _rope_tables()
    row = lambda v: v[l][None, :]

    q, k, vt, gm, ckv_n, kr = _proj(xc, xl, mods, win, row(q_norm_g), wuq, row(kv_norm_g), wuk, wuvt,
                                   row(gmlp_ln_g), row(gmlp_ln_b), ws, bsb, cos_t, sp_t, sm_t)
    kc, vct = _kvexp(cache_ckv[:, l].reshape(N_LAT_B * PAST, KVL),
                     cache_krope[:, l].reshape(N_LAT_B * PAST, QK_ROPE), wuk, wuvt)
    attn_c = _attn_ctx(q, k, vt)
    attn_l = _attn_lat(q, k, vt, kc, vct)

    x1, h2p, ridx, rwt = _post(attn_c, attn_l, gm, xc, xl, mods, w_out[l].astype(BF16), row(ln1_g), row(ln1_b),
                               wrt, brt)
    dest, blk_expert, blk_valid, zero_blk, zero_valid = _block_plan(ridx)
    xs = _dispatch(zero_blk, zero_valid, dest, h2p)
    ys = _experts(blk_expert, blk_valid, xs, w1[l], w3[l], w2[l])
    y_c, y_l = _final(dest.reshape(T // TM, 1, TOP_K * TM), ys, x1, rwt, mods, row(ln2_g), row(ln2_b))

    return (y_c.reshape(N_CTX_B, N_CTX, D), y_l.reshape(N_LAT_B, N_LAT, D),
            ckv_n.reshape(N_CTX_B, 1, N_CTX, KVL), kr.reshape(N_CTX_B, 1, N_CTX, QK_ROPE))
```

```python
import functools

import jax
import jax.numpy as jnp
import numpy as np
from jax import lax
from jax.experimental import pallas as pl
from jax.experimental.pallas import tpu as pltpu

F32 = jnp.float32
BF16 = jnp.bfloat16
U32 = jnp.uint32
I32 = jnp.int32

D = 2048
N_CTX_B, N_CTX = 32, 256
N_LAT_B, N_LAT = 4, 4096
PAST = 512
GRID_W = 64
H = 8
QK_NOPE, QK_ROPE, V_DIM = 128, 64, 128
QL, KVL = 512, 512
GW = 1024
G_GROUPS, G_DIM, CHUNK = 8, 128, 128
N_GRP, E_PER_GRP, N_EXP, TOP_K = 4, 8, 32, 2
E_HID = 512
ROPE_THETA = 10000.0
EPS = 1e-6
ALPHA = 2.0 ** 0.25
SM_SCALE = (QK_NOPE + QK_ROPE) ** -0.5
Q_SCALE = SM_SCALE * float(np.log2(np.e))

T_CTX = N_CTX_B * N_CTX
T_LAT = N_LAT_B * N_LAT
T = T_CTX + T_LAT
N_ASSIGN = T * TOP_K

LANES = 128
HEAD_PAD = 256
V_AUG = V_DIM + 16
TM = 256
TP = 512
SUB = 256
TD = 1024
TQ = 1024
TK = 512
BM = 256
N_BLOCKS = N_ASSIGN // BM + N_EXP
N_ZERO = 2 * N_EXP
VMEM_LIMIT = 56 * 1024 * 1024

NP_TILES = T_CTX // TM
LAT_TILES_PER_B = N_LAT // TM
IN_W = QL + KVL + 2 * GW + LANES


def _cp(sem):
    return pltpu.CompilerParams(dimension_semantics=sem, vmem_limit_bytes=VMEM_LIMIT)


def _const_spec(shape):
    nd = len(shape)
    return pl.BlockSpec(shape, lambda *a: (0,) * nd, pipeline_mode=pl.Buffered(1))


def _dot(a, b):
    return jnp.dot(a, b, preferred_element_type=F32)


def _dot_nt(a, b):
    return lax.dot_general(a, b, (((1,), (1,)), ((), ())), preferred_element_type=F32)


def _layer_norm(x, g, b):
    mu = jnp.mean(x, -1, keepdims=True)
    xc = x - mu
    var = jnp.mean(xc * xc, -1, keepdims=True)
    return xc * lax.rsqrt(var + EPS) * g + b


def _rms_norm(x, g):
    return x * lax.rsqrt(jnp.mean(x * x, -1, keepdims=True) + EPS) * g


def _gelu(x):
    return 0.5 * x * (1.0 + lax.erf(x * np.float32(np.sqrt(0.5))))


def _pack_pair(lo, hi):
    lo_b = lax.bitcast_convert_type(lo.astype(BF16).astype(F32), U32)
    hi_b = lax.bitcast_convert_type(hi.astype(BF16).astype(F32), U32)
    return hi_b | (lo_b >> 16)


def _unpack_pair(w):
    lo = lax.bitcast_convert_type(w << 16, F32)
    hi = lax.bitcast_convert_type(w & jnp.uint32(0xFFFF0000), F32)
    return lo, hi


def _adaln_kernel(cond_ref, w_ref, b_ref, o_ref):
    s = jax.nn.silu(cond_ref[...])
    s_hi = s.astype(BF16)
    s_lo = (s - s_hi.astype(F32)).astype(BF16)
    w = w_ref[...]
    w_hi = w.astype(BF16)
    w_lo = (w - w_hi.astype(F32)).astype(BF16)
    o_ref[...] = _dot(s_hi, w_hi) + _dot(s_lo, w_hi) + _dot(s_hi, w_lo) + b_ref[...]


def _adaln(cond16, w_ada, b_ada):
    tn = 1024
    n = w_ada.shape[1]
    return pl.pallas_call(
        _adaln_kernel,
        grid=(n // tn,),
        in_specs=[pl.BlockSpec((16, D), lambda j: (0, 0)),
                  pl.BlockSpec((D, tn), lambda j: (0, j)),
                  pl.BlockSpec((1, tn), lambda j: (0, j))],
        out_specs=pl.BlockSpec((16, tn), lambda j: (0, j)),
        out_shape=jax.ShapeDtypeStruct((16, n), F32),
        compiler_params=_cp(("arbitrary",)),
        name="adaln",
    )(cond16, w_ada, b_ada)


def _expand_kv(ckv_n, kr128, wuk_ref, wuvt_ref, k_ref, vt_ref):
    cb = ckv_n.astype(BF16)
    kn = _dot(cb, wuk_ref[...])
    vt = _dot_nt(wuvt_ref[...], cb).astype(BF16)
    ones = jnp.ones((V_AUG - V_DIM, vt.shape[1]), BF16)
    for h in range(H):
        vt_ref[h * V_AUG:h * V_AUG + V_DIM, :] = vt[h * V_DIM:(h + 1) * V_DIM]
        vt_ref[h * V_AUG + V_DIM:(h + 1) * V_AUG, :] = ones
    krb = kr128.astype(BF16)
    for h in range(H):
        k_ref[:, h * HEAD_PAD:h * HEAD_PAD + QK_NOPE] = kn[:, h * QK_NOPE:(h + 1) * QK_NOPE].astype(BF16)
        k_ref[:, h * HEAD_PAD + QK_NOPE:(h + 1) * HEAD_PAD] = krb


def _rope128(x, cos, s_plus, s_minus):
    return (x * cos + pltpu.roll(x, 16, 1) * s_plus + pltpu.roll(x, LANES - 16, 1) * s_minus)


def _proj_kernel(xc_ref, xl_ref, mods_ref, win_ref, qg_ref, wuq_ref, kvg_ref, wuk_ref, wuv_ref,
                 lng_ref, lnb_ref, ws_ref, bsb_ref, cos_ref, sp_ref, sm_ref,
                 q_ref, k_ref, v_ref, gm_ref, ckv_ref, kr_ref):
    i = pl.program_id(0)
    x = jnp.where(i < NP_TILES, xc_ref[...], xl_ref[...])
    m = mods_ref[0]
    shift1, scale1 = m[0:1], m[1:2]
    hmod = (x * (1.0 + scale1) + shift1).astype(BF16)
    proj = _dot(hmod, win_ref[...])
    cq = proj[:, 0:QL]
    ckv = proj[:, QL:QL + KVL]
    u = proj[:, QL + KVL:QL + KVL + GW]
    v = proj[:, QL + KVL + GW:QL + KVL + 2 * GW]
    kr = proj[:, QL + KVL + 2 * GW:IN_W]

    cos, s_plus, s_minus = cos_ref[...], sp_ref[...], sm_ref[...]

    cq_n = _rms_norm(cq, qg_ref[...]).astype(BF16)
    q = _dot(cq_n, wuq_ref[...]) * Q_SCALE
    for h in range(H):
        q_ref[:, h * HEAD_PAD:h * HEAD_PAD + QK_NOPE] = q[:, h * HEAD_PAD:h * HEAD_PAD + QK_NOPE].astype(BF16)
        q_ref[:, h * HEAD_PAD + QK_NOPE:(h + 1) * HEAD_PAD] = _rope128(
            q[:, h * HEAD_PAD + QK_NOPE:(h + 1) * HEAD_PAD], cos, s_plus, s_minus).astype(BF16)

    ckv_n = _rms_norm(ckv, kvg_ref[...])

    @pl.when(i < NP_TILES)
    def _():
        ckv_ref[...] = ckv_n
        kr_ref[...] = kr[:, :QK_ROPE]

    _expand_kv(ckv_n, _rope128(kr, cos, s_plus, s_minus), wuk_ref, wuv_ref, k_ref, v_ref)

    gu = _gelu(u)
    vn = _layer_norm(_gelu(v), lng_ref[...], lnb_ref[...]).astype(BF16)
    for c in range(TM // CHUNK):
        rows = slice(c * CHUNK, (c + 1) * CHUNK)
        for g in range(G_GROUPS):
            cols = slice(g * G_DIM, (g + 1) * G_DIM)
            s = _dot(ws_ref[g], vn[rows, cols]) + bsb_ref[g]
            gm_ref[rows, cols] = (gu[rows, cols] * s).astype(BF16)


def _mods_index(i, tile=TM):
    return jnp.where(i < T_CTX // tile, N_LAT_B, (i - T_CTX // tile) // (N_LAT // tile))


def _proj(xc, xl, mods, win, qg, wuq, kvg, wuk, wuv, lng, lnb, ws, bsb, cos_t, sp_t, sm_t):
    def tok(width):
        return pl.BlockSpec((TM, width), lambda i: (i, 0))

    def pos_block(i):
        return jnp.where(i < NP_TILES, LAT_TILES_PER_B, (i - NP_TILES) % LAT_TILES_PER_B)

    rope_spec = pl.BlockSpec((TM, LANES), lambda i: (pos_block(i), 0))
    return pl.pallas_call(
        _proj_kernel,
        grid=(T // TM,),
        in_specs=[
            pl.BlockSpec((TM, D), lambda i: (jnp.minimum(i, NP_TILES - 1), 0)),
            pl.BlockSpec((TM, D), lambda i: (jnp.maximum(i - NP_TILES, 0), 0)),
            pl.BlockSpec((1, 6, D), lambda i: (_mods_index(i), 0, 0)),
            _const_spec((D, IN_W)), _const_spec((1, QL)), _const_spec((QL, H * HEAD_PAD)),
            _const_spec((1, KVL)), _const_spec((KVL, H * QK_NOPE)), _const_spec((H * V_DIM, KVL)),
            _const_spec((1, GW)), _const_spec((1, GW)),
            _const_spec((G_GROUPS, CHUNK, CHUNK)), _const_spec((G_GROUPS, CHUNK, G_DIM)),
            rope_spec, rope_spec, rope_spec,
        ],
        out_specs=[tok(H * HEAD_PAD), tok(H * HEAD_PAD),
                   pl.BlockSpec((H * V_AUG, TM), lambda i: (0, i)), tok(GW),
                   pl.BlockSpec((TM, KVL), lambda i: (jnp.minimum(i, NP_TILES - 1), 0)),
                   pl.BlockSpec((TM, QK_ROPE), lambda i: (jnp.minimum(i, NP_TILES - 1), 0))],
        out_shape=[
            jax.ShapeDtypeStruct((T, H * HEAD_PAD), BF16),
            jax.ShapeDtypeStruct((T, H * HEAD_PAD), BF16),
            jax.ShapeDtypeStruct((H * V_AUG, T), BF16),
            jax.ShapeDtypeStruct((T, GW), BF16),
            jax.ShapeDtypeStruct((T_CTX, KVL), F32),
            jax.ShapeDtypeStruct((T_CTX, QK_ROPE), F32),
        ],
        compiler_params=_cp(("arbitrary",)),
        name="proj",
    )(xc, xl, mods, win, qg, wuq, kvg, wuk, wuv, lng, lnb, ws, bsb, cos_t, sp_t, sm_t)


def _kvexp_kernel(ckv_ref, kr_ref, wuk_ref, wuv_ref, k_ref, v_ref):
    kr = kr_ref[...]
    kr128 = jnp.concatenate([kr, jnp.zeros_like(kr)], axis=1)
    _expand_kv(ckv_ref[...], kr128, wuk_ref, wuv_ref, k_ref, v_ref)


def _kvexp(ckv, kr, wuk, wuv):
    rows = ckv.shape[0]
    return pl.pallas_call(
        _kvexp_kernel,
        grid=(rows // TM,),
        in_specs=[pl.BlockSpec((TM, KVL), lambda i: (i, 0)), pl.BlockSpec((TM, QK_ROPE), lambda i: (i, 0)),
                  _const_spec((KVL, H * QK_NOPE)), _const_spec((H * V_DIM, KVL))],
        out_specs=[pl.BlockSpec((TM, H * HEAD_PAD), lambda i: (i, 0)),
                   pl.BlockSpec((H * V_AUG, TM), lambda i: (0, i))],
        out_shape=[jax.ShapeDtypeStruct((rows, H * HEAD_PAD), BF16),
                   jax.ShapeDtypeStruct((H * V_AUG, rows), BF16)],
        compiler_params=_cp(("arbitrary",)),
        name="kvexp",
    )(ckv, kr, wuk, wuv)


def _probs(s, m):
    m_new = jnp.maximum(m, jnp.max(s, 0, keepdims=True))
    return m_new, jnp.exp2(m - m_new), jnp.exp2(s - m_new).astype(BF16)


def _attn_finish(acc):
    return (acc[:V_DIM] / acc[V_DIM:V_DIM + 1]).T.astype(BF16)


def _attn_ctx_kernel(q_ref, k_ref, vt_ref, o_ref):
    m0 = jnp.full((1, N_CTX), -jnp.inf, F32)
    for h in range(H):
        qh = q_ref[:, h * HEAD_PAD:(h + 1) * HEAD_PAD]
        kh = k_ref[:, h * HEAD_PAD:(h + 1) * HEAD_PAD]
        _, _, p = _probs(_dot_nt(kh, qh), m0)
        acc = _dot(vt_ref[h * V_AUG:(h + 1) * V_AUG, :], p)
        o_ref[:, h * V_DIM:(h + 1) * V_DIM] = _attn_finish(acc)


def _attn_ctx(q, k, vt):
    return pl.pallas_call(
        _attn_ctx_kernel,
        grid=(N_CTX_B,),
        in_specs=[pl.BlockSpec((N_CTX, H * HEAD_PAD), lambda b: (b, 0)),
                  pl.BlockSpec((N_CTX, H * HEAD_PAD), lambda b: (b, 0)),
                  pl.BlockSpec((H * V_AUG, N_CTX), lambda b: (0, b))],
        out_specs=pl.BlockSpec((N_CTX, H * V_DIM), lambda b: (b, 0)),
        out_shape=jax.ShapeDtypeStruct((T_CTX, H * V_DIM), BF16),
        compiler_params=_cp(("arbitrary",)),
        name="attn_ctx",
    )(q, k, vt)


def _attn_lat_kernel(q_ref, k_ref, vt_ref, kc_ref, vct_ref, o_ref):
    q = q_ref[...]
    n_lat = N_LAT // TK

    def keys(c):
        return k_ref[c * TK:(c + 1) * TK, :] if c < n_lat else kc_ref[...]

    def values_t(c):
        return vt_ref[:, c * TK:(c + 1) * TK] if c < n_lat else vct_ref[...]

    m = jnp.full((1, TQ), -jnp.inf, F32)
    acc = jnp.zeros((V_AUG, TQ), F32)
    s_next = _dot_nt(keys(0), q)
    p_prev = alpha_prev = None
    for c in range(n_lat + 1):
        s = s_next
        if c < n_lat:
            s_next = _dot_nt(keys(c + 1), q)
        if c > 0:
            acc = alpha_prev * acc + _dot(values_t(c - 1), p_prev)
        m, alpha_prev, p_prev = _probs(s, m)
    acc = alpha_prev * acc + _dot(values_t(n_lat), p_prev)
    o_ref[...] = _attn_finish(acc)


def _attn_lat(q, k, vt, kc, vct):
    qb0 = T_CTX // TQ
    qpb = N_LAT // TQ
    kb0 = T_CTX // N_LAT
    return pl.pallas_call(
        _attn_lat_kernel,
        grid=(N_LAT_B, H, qpb),
        in_specs=[pl.BlockSpec((TQ, HEAD_PAD), lambda b, h, qi: (qb0 + b * qpb + qi, h)),
                  pl.BlockSpec((N_LAT, HEAD_PAD), lambda b, h, qi: (kb0 + b, h)),
                  pl.BlockSpec((V_AUG, N_LAT), lambda b, h, qi: (h, kb0 + b)),
                  pl.BlockSpec((PAST, HEAD_PAD), lambda b, h, qi: (b, h)),
                  pl.BlockSpec((V_AUG, PAST), lambda b, h, qi: (h, b))],
        out_specs=pl.BlockSpec((TQ, V_DIM), lambda b, h, qi: (b * qpb + qi, h)),
        out_shape=jax.ShapeDtypeStruct((T_LAT, H * V_DIM), BF16),
        compiler_params=_cp(("arbitrary", "arbitrary", "arbitrary")),
        name="attn_lat",
    )(q, k, vt, kc, vct)


def _route(logits):
    lane_i = lax.broadcasted_iota(I32, logits.shape, 1)
    lane = lane_i.astype(F32)
    neg = jnp.float32(-jnp.inf)
    far = jnp.float32(LANES)
    gl = jnp.where(lane_i < N_GRP, logits, neg)
    gmax = jnp.max(gl, -1, keepdims=True)
    gidx = jnp.min(jnp.where(gl == gmax, lane, far), -1, keepdims=True)
    p_top = 1.0 / jnp.sum(jnp.exp(gl - gmax), -1, keepdims=True)
    grp_of_lane = ((lane_i - N_GRP) >> 3).astype(F32)
    in_grp = (lane_i >= N_GRP) & (lane_i < N_GRP + N_EXP) & (grp_of_lane == gidx)
    el = jnp.where(in_grp, logits, neg)
    m1 = jnp.max(el, -1, keepdims=True)
    i1 = jnp.min(jnp.where(el == m1, lane, far), -1, keepdims=True)
    el2 = jnp.where(lane == i1, neg, el)
    m2 = jnp.max(el2, -1, keepdims=True)
    i2 = jnp.min(jnp.where(el2 == m2, lane, far), -1, keepdims=True)
    e2 = jnp.exp(m2 - m1)
    w1 = p_top / (1.0 + e2)
    w2 = p_top * e2 / (1.0 + e2)
    two = lax.broadcasted_iota(I32, (logits.shape[0], TOP_K), 1)
    return (jnp.where(two == 0, i1, i2) - N_GRP).astype(I32), jnp.where(two == 0, w1, w2)


def _post_kernel(ac_ref, al_ref, gm_ref, xc_ref, xl_ref, mods_ref, wout_ref, g1_ref, b1_ref, wrt_ref,
                 brt_ref, x1_ref, h2p_ref, ridx_ref, rwt_ref):
    i = pl.program_id(0)
    is_ctx = i < T_CTX // TP
    m = mods_ref[0]
    gate1, shift2, scale2 = m[2:3], m[3:4], m[4:5]

    def mix_of(s):
        rows = slice(s * SUB, (s + 1) * SUB)
        attn = jnp.where(is_ctx, ac_ref[rows, :], al_ref[rows, :])
        return _dot(jnp.concatenate([attn, gm_ref[rows, :]], axis=1), wout_ref[...])

    n_sub = TP // SUB
    mix_next = mix_of(0)
    for s in range(n_sub):
        rows = slice(s * SUB, (s + 1) * SUB)
        mix = mix_next
        if s + 1 < n_sub:
            mix_next = mix_of(s + 1)
        x = jnp.where(is_ctx, xc_ref[rows, :], xl_ref[rows, :])
        x1 = _layer_norm(ALPHA * x + gate1 * mix, g1_ref[...], b1_ref[...])
        x1_ref[rows, :] = x1
        h2 = x1 * (1.0 + scale2) + shift2
        h2p_ref[rows, :] = _pack_pair(h2[:, :D // 2], h2[:, D // 2:])
        ridx, rwt = _route(_dot(h2.astype(BF16), wrt_ref[...]) + brt_ref[...])
        ridx_ref[rows, :] = ridx
        rwt_ref[rows, :] = rwt


def _post(attn_c, attn_l, gm, xc, xl, mods, wout, g1, b1, wrt, brt):
    npt = T_CTX // TP

    def tok(width):
        return pl.BlockSpec((TP, width), lambda i: (i, 0))

    def ctx(width):
        return pl.BlockSpec((TP, width), lambda i: (jnp.minimum(i, npt - 1), 0))

    def lat(width):
        return pl.BlockSpec((TP, width), lambda i: (jnp.maximum(i - npt, 0), 0))

    return pl.pallas_call(
        _post_kernel,
        grid=(T // TP,),
        in_specs=[ctx(H * V_DIM), lat(H * V_DIM), tok(GW), ctx(D), lat(D),
                  pl.BlockSpec((1, 6, D), lambda i: (_mods_index(i, TP), 0, 0)),
                  _const_spec((D, D)), _const_spec((1, D)), _const_spec((1, D)),
                  _const_spec((D, LANES)), _const_spec((1, LANES))],
        out_specs=[tok(D), tok(D // 2), tok(TOP_K), tok(TOP_K)],
        out_shape=[jax.ShapeDtypeStruct((T, D), F32),
                   jax.ShapeDtypeStruct((T, D // 2), U32),
                   jax.ShapeDtypeStruct((T, TOP_K), I32),
                   jax.ShapeDtypeStruct((T, TOP_K), F32)],
        compiler_params=_cp(("arbitrary",)),
        name="post",
    )(attn_c, attn_l, gm, xc, xl, mods, wout, g1, b1, wrt, brt)


def _row_copy(src, src_row, dst, dst_row, sem):
    return pltpu.make_async_copy(src.at[pl.ds(src_row, 1)], dst.at[pl.ds(dst_row, 1)], sem)


def _dispatch_kernel(zb_ref, zv_ref, dest_ref, h2p_ref, xs_ref, zbuf, sem):
    i = pl.program_id(0)

    @pl.when(i == 0)
    def _():
        zbuf[...] = jnp.zeros_like(zbuf)

        def zero_block(j):
            return pltpu.make_async_copy(zbuf, xs_ref.at[pl.ds(zb_ref[j] * BM, BM)], sem)

        def start(j, carry):
            @pl.when(zv_ref[j] != 0)
            def _():
                zero_block(j).start()
            return carry

        def wait(j, carry):
            @pl.when(zv_ref[j] != 0)
            def _():
                zero_block(j).wait()
            return carry

        lax.fori_loop(0, N_ZERO, start, 0)
        lax.fori_loop(0, N_ZERO, wait, 0)

    def issue(r, carry):
        for k in range(TOP_K):
            _row_copy(h2p_ref, r, xs_ref, dest_ref[0, 0, TOP_K * r + k], sem).start()
        return carry

    lax.fori_loop(0, TD, issue, 0, unroll=8)
    for _ in range(TOP_K):
        pltpu.make_async_copy(h2p_ref, xs_ref.at[pl.ds(0, TD)], sem).wait()


def _dispatch(zero_blk, zero_valid, dest, h2p):
    grid_spec = pltpu.PrefetchScalarGridSpec(
        num_scalar_prefetch=2,
        grid=(T // TD,),
        in_specs=[pl.BlockSpec((1, 1, TOP_K * TD), lambda i, zb, zv: (i, 0, 0), memory_space=pltpu.SMEM),
                  pl.BlockSpec((TD, D // 2), lambda i, zb, zv: (i, 0))],
        out_specs=pl.BlockSpec(memory_space=pl.ANY),
        scratch_shapes=[pltpu.VMEM((BM, D // 2), U32), pltpu.SemaphoreType.DMA(())],
    )
    return pl.pallas_call(
        _dispatch_kernel,
        grid_spec=grid_spec,
        out_shape=jax.ShapeDtypeStruct((N_BLOCKS * BM, D // 2), U32),
        compiler_params=_cp(("arbitrary",)),
        name="dispatch",
    )(zero_blk, zero_valid, dest.reshape(T // TD, 1, TOP_K * TD), h2p)


def _experts_kernel(be_ref, bv_ref, sw_ref, es_ref, nx_ref, x_ref, w1_hbm, w3_hbm, w2_hbm, y_ref,
                    wf1, wf3, wf2, w1b, w3b, w2b, sems):
    i = pl.program_id(0)
    e = be_ref[i]
    nv = bv_ref[i]

    def weight_copies(expert, s):
        return (pltpu.make_async_copy(w1_hbm.at[expert], wf1.at[s], sems.at[s]),
                pltpu.make_async_copy(w3_hbm.at[expert], wf3.at[s], sems.at[s]),
                pltpu.make_async_copy(w2_hbm.at[expert], wf2.at[s], sems.at[s]))

    @pl.when(sw_ref[i] != 0)
    def _():
        s = es_ref[i]

        @pl.when(i == 0)
        def _():
            for c in weight_copies(e, s):
                c.start()

        for c in weight_copies(e, s):
            c.wait()
        nxt = nx_ref[i]

        @pl.when(nxt >= 0)
        def _():
            for c in weight_copies(nxt, 1 - s):
                c.start()

        w1b[...] = wf1[s].astype(BF16)
        w3b[...] = wf3[s].astype(BF16)
        w2b[...] = wf2[s].astype(BF16)

    @pl.when(nv > 0)
    def _():
        lo, hi = _unpack_pair(x_ref[...])
        xb = jnp.concatenate([lo.astype(BF16), hi.astype(BF16)], axis=1)
        a = _dot(xb, w1b[...])
        b = _dot(xb, w3b[...])
        hid = (jax.nn.silu(a) * b).astype(BF16)
        y = _dot(hid, w2b[...])
        y_ref[...] = _pack_pair(y[:, :D // 2], y[:, D // 2:])

    @pl.when(nv == 0)
    def _():
        y_ref[...] = jnp.zeros_like(y_ref)


def _experts(blk_expert, blk_valid, switch, wslot, next_expert, xs, w1, w3, w2):
    hbm = pl.BlockSpec(memory_space=pl.ANY)
    grid_spec = pltpu.PrefetchScalarGridSpec(
        num_scalar_prefetch=5,
        grid=(N_BLOCKS,),
        in_specs=[pl.BlockSpec((BM, D // 2), lambda i, *_: (i, 0)), hbm, hbm, hbm],
        out_specs=pl.BlockSpec((BM, D // 2), lambda i, *_: (i, 0)),
        scratch_shapes=[pltpu.VMEM((2, D, E_HID), F32), pltpu.VMEM((2, D, E_HID), F32),
                        pltpu.VMEM((2, E_HID, D), F32),
                        pltpu.VMEM((D, E_HID), BF16), pltpu.VMEM((D, E_HID), BF16),
                        pltpu.VMEM((E_HID, D), BF16), pltpu.SemaphoreType.DMA((2,))],
    )
    return pl.pallas_call(
        _experts_kernel,
        grid_spec=grid_spec,
        out_shape=jax.ShapeDtypeStruct((N_BLOCKS * BM, D // 2), U32),
        compiler_params=_cp(("arbitrary",)),
        name="experts",
    )(blk_expert, blk_valid, switch, wslot, next_expert, xs, w1, w3, w2)


def _final_kernel(dcur_ref, dnext_ref, ys_ref, x1_ref, rwt_ref, mods_ref, g2_ref, b2_ref, oc_ref, ol_ref,
                  buf, sems):
    i = pl.program_id(0)
    n = pl.num_programs(0)
    slot = i % 2

    def issue(d_ref, s):
        def body(r, carry):
            for k in range(TOP_K):
                _row_copy(ys_ref, d_ref[0, 0, TOP_K * r + k], buf.at[s, k], r, sems.at[s]).start()
            return carry

        lax.fori_loop(0, TM, body, 0, unroll=8)

    @pl.when(i == 0)
    def _():
        issue(dcur_ref, 0)

    @pl.when(i + 1 < n)
    def _():
        issue(dnext_ref, 1 - slot)

    for k in range(TOP_K):
        pltpu.make_async_copy(ys_ref.at[pl.ds(0, TM)], buf.at[slot, k], sems.at[slot]).wait()

    w = rwt_ref[...]
    lo0, hi0 = _unpack_pair(buf[slot, 0])
    lo1, hi1 = _unpack_pair(buf[slot, 1])
    w0, w1 = w[:, 0:1], w[:, 1:2]
    moe = jnp.concatenate([lo0 * w0 + lo1 * w1, hi0 * w0 + hi1 * w1], axis=1)
    gate2 = mods_ref[0][5:6]
    out = _layer_norm(ALPHA * x1_ref[...] + gate2 * moe, g2_ref[...], b2_ref[...])

    @pl.when(i < NP_TILES)
    def _():
        oc_ref[...] = out

    @pl.when(i >= NP_TILES)
    def _():
        ol_ref[...] = out


def _final(dest3, ys, x1, rwt, mods, g2, b2):
    n = T // TM
    return pl.pallas_call(
        _final_kernel,
        grid=(n,),
        in_specs=[pl.BlockSpec((1, 1, TOP_K * TM), lambda i: (i, 0, 0), memory_space=pltpu.SMEM),
                  pl.BlockSpec((1, 1, TOP_K * TM), lambda i: (jnp.minimum(i + 1, n - 1), 0, 0),
                               memory_space=pltpu.SMEM),
                  pl.BlockSpec(memory_space=pl.ANY),
                  pl.BlockSpec((TM, D), lambda i: (i, 0)),
                  pl.BlockSpec((TM, TOP_K), lambda i: (i, 0)),
                  pl.BlockSpec((1, 6, D), lambda i: (_mods_index(i), 0, 0)),
                  _const_spec((1, D)), _const_spec((1, D))],
        out_specs=[pl.BlockSpec((TM, D), lambda i: (jnp.minimum(i, NP_TILES - 1), 0)),
                   pl.BlockSpec((TM, D), lambda i: (jnp.maximum(i - NP_TILES, 0), 0))],
        out_shape=[jax.ShapeDtypeStruct((T_CTX, D), F32), jax.ShapeDtypeStruct((T_LAT, D), F32)],
        scratch_shapes=[pltpu.VMEM((2, TOP_K, TM, D // 2), U32), pltpu.SemaphoreType.DMA((2,))],
        compiler_params=_cp(("arbitrary",)),
        name="final",
    )(dest3, dest3, ys, x1, rwt, mods, g2, b2)


def _rope_tables():
    n = N_LAT
    rows = n // GRID_W
    f32 = np.float32
    row = np.repeat(np.arange(rows, dtype=f32), GRID_W)
    col = np.tile(np.arange(GRID_W, dtype=f32), rows)
    half = QK_ROPE // 2
    inv = (f32(ROPE_THETA) ** (-np.arange(0, half, 2, dtype=f32) / f32(half))).astype(f32)
    ang_r, ang_c = (row[:, None] * inv).astype(f32), (col[:, None] * inv).astype(f32)
    cr, sr = np.cos(ang_r).astype(f32), np.sin(ang_r).astype(f32)
    cc, sc = np.cos(ang_c).astype(f32), np.sin(ang_c).astype(f32)
    z16 = np.zeros_like(cr)
    pad = lambda v: np.full((n, LANES - QK_ROPE), v, f32)
    cos = np.concatenate([cr, cr, cc, cc, pad(1.0)], axis=1)
    s_plus = np.concatenate([z16, sr, z16, sc, pad(0.0)], axis=1)
    s_minus = np.concatenate([-sr, z16, -sc, z16, pad(0.0)], axis=1)
    ident = lambda v: np.full((TM, LANES), v, f32)
    return (jnp.asarray(np.concatenate([cos, ident(1.0)], 0)),
            jnp.asarray(np.concatenate([s_plus, ident(0.0)], 0)),
            jnp.asarray(np.concatenate([s_minus, ident(0.0)], 0)))


def _block_plan(ridx):
    e_flat = ridx.reshape(-1)
    onehot = (e_flat[:, None] == jnp.arange(N_EXP, dtype=I32)[None, :]).astype(I32)
    csum = jnp.cumsum(onehot, axis=0)
    rank = jnp.sum(onehot * csum, axis=1) - 1
    counts = csum[-1]
    nblk = (counts + BM - 1) // BM
    bends = jnp.cumsum(nblk)
    bstarts = bends - nblk
    dest = bstarts[e_flat] * BM + rank
    blk = jnp.arange(N_BLOCKS, dtype=I32)
    blk_expert = jnp.minimum(jnp.sum((blk[:, None] >= bends[None, :]).astype(I32), axis=1), N_EXP - 1)
    left = counts[blk_expert] - (blk - bstarts[blk_expert]) * BM
    blk_valid = jnp.where(blk < bends[-1], jnp.clip(left, 0, BM), 0).astype(I32)
    unused = bends[-1] + jnp.arange(N_EXP, dtype=I32)
    zero_blk = jnp.concatenate([jnp.maximum(bends - 1, 0), jnp.minimum(unused, N_BLOCKS - 1)])
    zero_valid = jnp.concatenate([counts % BM != 0, unused < N_BLOCKS])
    prev_expert = jnp.concatenate([jnp.full((1,), -1, I32), blk_expert[:-1]])
    switch = (blk < bends[-1]) & (blk_expert != prev_expert)
    wslot = (jnp.cumsum(switch.astype(I32)) - 1) % 2
    eid = jnp.arange(N_EXP, dtype=I32)
    later = (eid[None, :] > eid[:, None]) & (nblk[None, :] > 0)
    next_of = jnp.min(jnp.where(later, eid[None, :], N_EXP), axis=1)
    next_expert = jnp.where(next_of < N_EXP, next_of, -1)[blk_expert]
    return (dest.astype(I32), blk_expert.astype(I32), blk_valid, zero_blk.astype(I32),
            zero_valid.astype(I32), switch.astype(I32), wslot.astype(I32), next_expert.astype(I32))


def kernel(x_prompt, x_sample, cache_ckv, cache_krope, c, c_ctx, w_ada, b_ada, w_in, q_norm_g, w_uq,
           kv_norm_g, w_ukv, gmlp_ln_g, gmlp_ln_b, w_spatial, b_spatial, w_out, ln1_g, ln1_b, w_group,
           b_group, w_router, b_router, w1, w3, w2, ln2_g, ln2_b):
    l = 0
    xc = x_prompt.reshape(T_CTX, D)
    xl = x_sample.reshape(T_LAT, D)

    cond16 = jnp.concatenate([c, c_ctx[None, :], jnp.zeros((16 - N_LAT_B - 1, D), F32)], axis=0)
    mods = _adaln(cond16, w_ada[l], b_ada[l][None, :]).reshape(16, 6, D)

    wi = w_in[l]
    o_kr = QL + KVL
    o_u = o_kr + QK_ROPE
    win = jnp.concatenate([wi[:, :o_kr], wi[:, o_u:], wi[:, o_kr:o_u],
                           jnp.zeros((D, LANES - QK_ROPE), F32)], axis=1).astype(BF16)
    wuq = jnp.pad(w_uq[l].reshape(QL, H, QK_NOPE + QK_ROPE),
                  ((0, 0), (0, 0), (0, HEAD_PAD - QK_NOPE - QK_ROPE))).reshape(QL, H * HEAD_PAD).astype(BF16)
    wkv = w_ukv[l].reshape(KVL, H, QK_NOPE + V_DIM)
    wuk = wkv[:, :, :QK_NOPE].reshape(KVL, H * QK_NOPE).astype(BF16)
    wuvt = wkv[:, :, QK_NOPE:].reshape(KVL, H * V_DIM).T.astype(BF16)
    ws = w_spatial[l].astype(BF16)
    bsb = jnp.broadcast_to(b_spatial[l][:, :, None], (G_GROUPS, CHUNK, G_DIM))
    wrt = jnp.concatenate([w_group[l], w_router[l], jnp.zeros((D, LANES - N_GRP - N_EXP), F32)],
                          axis=1).astype(BF16)
    brt = jnp.concatenate([b_group[l], b_router[l], jnp.zeros((LANES - N_GRP - N_EXP,), F32)])[None, :]
    cos_t, sp_t, sm_t = _rope_tables()
    row = lambda v: v[l][None, :]

    q, k, vt, gm, ckv_n, kr = _proj(xc, xl, mods, win, row(q_norm_g), wuq, row(kv_norm_g), wuk, wuvt,
                                   row(gmlp_ln_g), row(gmlp_ln_b), ws, bsb, cos_t, sp_t, sm_t)
    kc, vct = _kvexp(cache_ckv[:, l].reshape(N_LAT_B * PAST, KVL),
                     cache_krope[:, l].reshape(N_LAT_B * PAST, QK_ROPE), wuk, wuvt)
    attn_c = _attn_ctx(q, k, vt)
    attn_l = _attn_lat(q, k, vt, kc, vct)

    x1, h2p, ridx, rwt = _post(attn_c, attn_l, gm, xc, xl, mods, w_out[l].astype(BF16), row(ln1_g), row(ln1_b),
                               wrt, brt)
    dest, blk_expert, blk_valid, zero_blk, zero_valid, switch, wslot, next_expert = _block_plan(ridx)
    xs = _dispatch(zero_blk, zero_valid, dest, h2p)
    ys = _experts(blk_expert, blk_valid, switch, wslot, next_expert, xs, w1[l], w3[l], w2[l])
    y_c, y_l = _final(dest.reshape(T // TM, 1, TOP_K * TM), ys, x1, rwt, mods, row(ln2_g), row(ln2_b))

    return (y_c.reshape(N_CTX_B, N_CTX, D), y_l.reshape(N_LAT_B, N_LAT, D),
            ckv_n.reshape(N_CTX_B, 1, N_CTX, KVL), kr.reshape(N_CTX_B, 1, N_CTX, QK_ROPE))
```

```python
import functools

import jax
import jax.numpy as jnp
import numpy as np
from jax import lax
from jax.experimental import pallas as pl
from jax.experimental.pallas import tpu as pltpu

F32 = jnp.float32
BF16 = jnp.bfloat16
U32 = jnp.uint32
I32 = jnp.int32

D = 2048
N_CTX_B, N_CTX = 32, 256
N_LAT_B, N_LAT = 4, 4096
PAST = 512
GRID_W = 64
H = 8
QK_NOPE, QK_ROPE, V_DIM = 128, 64, 128
QL, KVL = 512, 512
GW = 1024
G_GROUPS, G_DIM, CHUNK = 8, 128, 128
N_GRP, E_PER_GRP, N_EXP, TOP_K = 4, 8, 32, 2
E_HID = 512
ROPE_THETA = 10000.0
EPS = 1e-6
ALPHA = 2.0 ** 0.25
SM_SCALE = (QK_NOPE + QK_ROPE) ** -0.5
Q_SCALE = SM_SCALE * float(np.log2(np.e))

T_CTX = N_CTX_B * N_CTX
T_LAT = N_LAT_B * N_LAT
T = T_CTX + T_LAT
N_ASSIGN = T * TOP_K

LANES = 128
HEAD_PAD = 256
V_AUG = V_DIM + 16
TM = 256
TP = 512
SUB = 256
TD = 1024
TQ = 1024
TK = 512
BM = 256
N_BLOCKS = N_ASSIGN // BM + N_EXP
N_ZERO = 2 * N_EXP
VMEM_LIMIT = 56 * 1024 * 1024

NP_TILES = T_CTX // TM
LAT_TILES_PER_B = N_LAT // TM


def _cp(sem):
    return pltpu.CompilerParams(dimension_semantics=sem, vmem_limit_bytes=VMEM_LIMIT)


def _const_spec(shape):
    nd = len(shape)
    return pl.BlockSpec(shape, lambda *a: (0,) * nd, pipeline_mode=pl.Buffered(1))


def _dot(a, b):
    return jnp.dot(a, b, preferred_element_type=F32)


def _dot_nt(a, b):
    return lax.dot_general(a, b, (((1,), (1,)), ((), ())), preferred_element_type=F32)


def _layer_norm(x, g, b):
    mu = jnp.mean(x, -1, keepdims=True)
    xc = x - mu
    var = jnp.mean(xc * xc, -1, keepdims=True)
    return xc * lax.rsqrt(var + EPS) * g + b


def _rms_norm(x, g):
    return x * lax.rsqrt(jnp.mean(x * x, -1, keepdims=True) + EPS) * g


def _gelu(x):
    return 0.5 * x * (1.0 + lax.erf(x * np.float32(np.sqrt(0.5))))


def _pack_pair(lo, hi):
    lo_b = lax.bitcast_convert_type(lo.astype(BF16).astype(F32), U32)
    hi_b = lax.bitcast_convert_type(hi.astype(BF16).astype(F32), U32)
    return hi_b | (lo_b >> 16)


def _unpack_pair(w):
    lo = lax.bitcast_convert_type(w << 16, F32)
    hi = lax.bitcast_convert_type(w & jnp.uint32(0xFFFF0000), F32)
    return lo, hi


def _adaln_kernel(cond_ref, w_ref, b_ref, o_ref):
    s = jax.nn.silu(cond_ref[...])
    s_hi = s.astype(BF16)
    s_lo = (s - s_hi.astype(F32)).astype(BF16)
    w = w_ref[...]
    w_hi = w.astype(BF16)
    w_lo = (w - w_hi.astype(F32)).astype(BF16)
    o_ref[...] = _dot(s_hi, w_hi) + _dot(s_lo, w_hi) + _dot(s_hi, w_lo) + b_ref[...]


def _adaln(cond16, w_ada, b_ada):
    tn = 1024
    n = w_ada.shape[1]
    return pl.pallas_call(
        _adaln_kernel,
        grid=(n // tn,),
        in_specs=[pl.BlockSpec((16, D), lambda j: (0, 0)),
                  pl.BlockSpec((D, tn), lambda j: (0, j)),
                  pl.BlockSpec((1, tn), lambda j: (0, j))],
        out_specs=pl.BlockSpec((16, tn), lambda j: (0, j)),
        out_shape=jax.ShapeDtypeStruct((16, n), F32),
        compiler_params=_cp(("arbitrary",)),
        name="adaln",
    )(cond16, w_ada, b_ada)


def _expand_kv(ckv_n, kr128, wuk_ref, wuvt_ref, k_ref, vt_ref):
    cb = ckv_n.astype(BF16)
    kn = _dot(cb, wuk_ref[...])
    vt = _dot_nt(wuvt_ref[...], cb).astype(BF16)
    ones = jnp.ones((V_AUG - V_DIM, vt.shape[1]), BF16)
    for h in range(H):
        vt_ref[h * V_AUG:h * V_AUG + V_DIM, :] = vt[h * V_DIM:(h + 1) * V_DIM]
        vt_ref[h * V_AUG + V_DIM:(h + 1) * V_AUG, :] = ones
    krb = kr128.astype(BF16)
    for h in range(H):
        k_ref[:, h * HEAD_PAD:h * HEAD_PAD + QK_NOPE] = kn[:, h * QK_NOPE:(h + 1) * QK_NOPE].astype(BF16)
        k_ref[:, h * HEAD_PAD + QK_NOPE:(h + 1) * HEAD_PAD] = krb


def _rope128(x, cos, s_plus, s_minus):
    return (x * cos + pltpu.roll(x, 16, 1) * s_plus + pltpu.roll(x, LANES - 16, 1) * s_minus)


def _proj_kernel(xc_ref, xl_ref, mods_ref, win_a_ref, win_g_ref, win_r_ref, qg_ref, wuq_ref, kvg_ref,
                 wuk_ref, wuv_ref,
                 lng_ref, lnb_ref, ws_ref, bsb_ref, cos_ref, sp_ref, sm_ref,
                 q_ref, k_ref, v_ref, gm_ref, ckv_ref, kr_ref):
    i = pl.program_id(0)
    x = jnp.where(i < NP_TILES, xc_ref[...], xl_ref[...])
    m = mods_ref[0]
    shift1, scale1 = m[0:1], m[1:2]
    hmod = (x * (1.0 + scale1) + shift1).astype(BF16)
    pa = _dot(hmod, win_a_ref[...])
    pg = _dot(hmod, win_g_ref[...])
    kr = _dot(hmod, win_r_ref[...])
    cq, ckv = pa[:, :QL], pa[:, QL:]
    u, v = pg[:, :GW], pg[:, GW:]

    cos, s_plus, s_minus = cos_ref[...], sp_ref[...], sm_ref[...]

    cq_n = _rms_norm(cq, qg_ref[...]).astype(BF16)
    q = _dot(cq_n, wuq_ref[...]) * Q_SCALE
    for h in range(H):
        q_ref[:, h * HEAD_PAD:h * HEAD_PAD + QK_NOPE] = q[:, h * HEAD_PAD:h * HEAD_PAD + QK_NOPE].astype(BF16)
        q_ref[:, h * HEAD_PAD + QK_NOPE:(h + 1) * HEAD_PAD] = _rope128(
            q[:, h * HEAD_PAD + QK_NOPE:(h + 1) * HEAD_PAD], cos, s_plus, s_minus).astype(BF16)

    ckv_n = _rms_norm(ckv, kvg_ref[...])

    @pl.when(i < NP_TILES)
    def _():
        ckv_ref[...] = ckv_n
        kr_ref[...] = kr[:, :QK_ROPE]

    _expand_kv(ckv_n, _rope128(kr, cos, s_plus, s_minus), wuk_ref, wuv_ref, k_ref, v_ref)

    gu = _gelu(u)
    vn = _layer_norm(_gelu(v), lng_ref[...], lnb_ref[...]).astype(BF16)
    for c in range(TM // CHUNK):
        rows = slice(c * CHUNK, (c + 1) * CHUNK)
        for g in range(G_GROUPS):
            cols = slice(g * G_DIM, (g + 1) * G_DIM)
            s = _dot(ws_ref[g], vn[rows, cols]) + bsb_ref[g]
            gm_ref[rows, cols] = (gu[rows, cols] * s).astype(BF16)


def _mods_index(i, tile=TM):
    return jnp.where(i < T_CTX // tile, N_LAT_B, (i - T_CTX // tile) // (N_LAT // tile))


def _proj(xc, xl, mods, win_a, win_g, win_r, qg, wuq, kvg, wuk, wuv, lng, lnb, ws, bsb, cos_t, sp_t, sm_t):
    def tok(width):
        return pl.BlockSpec((TM, width), lambda i: (i, 0))

    def pos_block(i):
        return jnp.where(i < NP_TILES, LAT_TILES_PER_B, (i - NP_TILES) % LAT_TILES_PER_B)

    rope_spec = pl.BlockSpec((TM, LANES), lambda i: (pos_block(i), 0))
    return pl.pallas_call(
        _proj_kernel,
        grid=(T // TM,),
        in_specs=[
            pl.BlockSpec((TM, D), lambda i: (jnp.minimum(i, NP_TILES - 1), 0)),
            pl.BlockSpec((TM, D), lambda i: (jnp.maximum(i - NP_TILES, 0), 0)),
            pl.BlockSpec((1, 6, D), lambda i: (_mods_index(i), 0, 0)),
            _const_spec((D, QL + KVL)), _const_spec((D, 2 * GW)), _const_spec((D, LANES)),
            _const_spec((1, QL)), _const_spec((QL, H * HEAD_PAD)),
            _const_spec((1, KVL)), _const_spec((KVL, H * QK_NOPE)), _const_spec((H * V_DIM, KVL)),
            _const_spec((1, GW)), _const_spec((1, GW)),
            _const_spec((G_GROUPS, CHUNK, CHUNK)), _const_spec((G_GROUPS, CHUNK, G_DIM)),
            rope_spec, rope_spec, rope_spec,
        ],
        out_specs=[tok(H * HEAD_PAD), tok(H * HEAD_PAD),
                   pl.BlockSpec((H * V_AUG, TM), lambda i: (0, i)), tok(GW),
                   pl.BlockSpec((TM, KVL), lambda i: (jnp.minimum(i, NP_TILES - 1), 0)),
                   pl.BlockSpec((TM, QK_ROPE), lambda i: (jnp.minimum(i, NP_TILES - 1), 0))],
        out_shape=[
            jax.ShapeDtypeStruct((T, H * HEAD_PAD), BF16),
            jax.ShapeDtypeStruct((T, H * HEAD_PAD), BF16),
            jax.ShapeDtypeStruct((H * V_AUG, T), BF16),
            jax.ShapeDtypeStruct((T, GW), BF16),
            jax.ShapeDtypeStruct((T_CTX, KVL), F32),
            jax.ShapeDtypeStruct((T_CTX, QK_ROPE), F32),
        ],
        compiler_params=_cp(("arbitrary",)),
        name="proj",
    )(xc, xl, mods, win_a, win_g, win_r, qg, wuq, kvg, wuk, wuv, lng, lnb, ws, bsb, cos_t, sp_t, sm_t)


def _kvexp_kernel(ckv_ref, kr_ref, wuk_ref, wuv_ref, k_ref, v_ref):
    kr = kr_ref[...]
    kr128 = jnp.concatenate([kr, jnp.zeros_like(kr)], axis=1)
    _expand_kv(ckv_ref[...], kr128, wuk_ref, wuv_ref, k_ref, v_ref)


def _kvexp(ckv, kr, wuk, wuv):
    rows = ckv.shape[0]
    return pl.pallas_call(
        _kvexp_kernel,
        grid=(rows // TM,),
        in_specs=[pl.BlockSpec((TM, KVL), lambda i: (i, 0)), pl.BlockSpec((TM, QK_ROPE), lambda i: (i, 0)),
                  _const_spec((KVL, H * QK_NOPE)), _const_spec((H * V_DIM, KVL))],
        out_specs=[pl.BlockSpec((TM, H * HEAD_PAD), lambda i: (i, 0)),
                   pl.BlockSpec((H * V_AUG, TM), lambda i: (0, i))],
        out_shape=[jax.ShapeDtypeStruct((rows, H * HEAD_PAD), BF16),
                   jax.ShapeDtypeStruct((H * V_AUG, rows), BF16)],
        compiler_params=_cp(("arbitrary",)),
        name="kvexp",
    )(ckv, kr, wuk, wuv)


def _probs(s, m):
    m_new = jnp.maximum(m, jnp.max(s, 0, keepdims=True))
    return m_new, jnp.exp2(m - m_new), jnp.exp2(s - m_new).astype(BF16)


def _attn_finish(acc):
    return (acc[:V_DIM] / acc[V_DIM:V_DIM + 1]).T.astype(BF16)


def _attn_ctx_kernel(q_ref, k_ref, vt_ref, o_ref):
    m0 = jnp.full((1, N_CTX), -jnp.inf, F32)
    for h in range(H):
        qh = q_ref[:, h * HEAD_PAD:(h + 1) * HEAD_PAD]
        kh = k_ref[:, h * HEAD_PAD:(h + 1) * HEAD_PAD]
        _, _, p = _probs(_dot_nt(kh, qh), m0)
        acc = _dot(vt_ref[h * V_AUG:(h + 1) * V_AUG, :], p)
        o_ref[:, h * V_DIM:(h + 1) * V_DIM] = _attn_finish(acc)


def _attn_ctx(q, k, vt):
    return pl.pallas_call(
        _attn_ctx_kernel,
        grid=(N_CTX_B,),
        in_specs=[pl.BlockSpec((N_CTX, H * HEAD_PAD), lambda b: (b, 0)),
                  pl.BlockSpec((N_CTX, H * HEAD_PAD), lambda b: (b, 0)),
                  pl.BlockSpec((H * V_AUG, N_CTX), lambda b: (0, b))],
        out_specs=pl.BlockSpec((N_CTX, H * V_DIM), lambda b: (b, 0)),
        out_shape=jax.ShapeDtypeStruct((T_CTX, H * V_DIM), BF16),
        compiler_params=_cp(("arbitrary",)),
        name="attn_ctx",
    )(q, k, vt)


def _attn_lat_kernel(q_ref, k_ref, vt_ref, kc_ref, vct_ref, o_ref):
    q = q_ref[...]
    n_lat = N_LAT // TK

    def keys(c):
        return k_ref[c * TK:(c + 1) * TK, :] if c < n_lat else kc_ref[...]

    def values_t(c):
        return vt_ref[:, c * TK:(c + 1) * TK] if c < n_lat else vct_ref[...]

    m = jnp.full((1, TQ), -jnp.inf, F32)
    acc = jnp.zeros((V_AUG, TQ), F32)
    s_next = _dot_nt(keys(0), q)
    p_prev = alpha_prev = None
    for c in range(n_lat + 1):
        s = s_next
        if c < n_lat:
            s_next = _dot_nt(keys(c + 1), q)
        if c > 0:
            acc = alpha_prev * acc + _dot(values_t(c - 1), p_prev)
        m, alpha_prev, p_prev = _probs(s, m)
    acc = alpha_prev * acc + _dot(values_t(n_lat), p_prev)
    o_ref[...] = _attn_finish(acc)


def _attn_lat(q, k, vt, kc, vct):
    qb0 = T_CTX // TQ
    qpb = N_LAT // TQ
    kb0 = T_CTX // N_LAT
    return pl.pallas_call(
        _attn_lat_kernel,
        grid=(N_LAT_B, H, qpb),
        in_specs=[pl.BlockSpec((TQ, HEAD_PAD), lambda b, h, qi: (qb0 + b * qpb + qi, h)),
                  pl.BlockSpec((N_LAT, HEAD_PAD), lambda b, h, qi: (kb0 + b, h)),
                  pl.BlockSpec((V_AUG, N_LAT), lambda b, h, qi: (h, kb0 + b)),
                  pl.BlockSpec((PAST, HEAD_PAD), lambda b, h, qi: (b, h)),
                  pl.BlockSpec((V_AUG, PAST), lambda b, h, qi: (h, b))],
        out_specs=pl.BlockSpec((TQ, V_DIM), lambda b, h, qi: (b * qpb + qi, h)),
        out_shape=jax.ShapeDtypeStruct((T_LAT, H * V_DIM), BF16),
        compiler_params=_cp(("arbitrary", "arbitrary", "arbitrary")),
        name="attn_lat",
    )(q, k, vt, kc, vct)


def _route(logits):
    lane_i = lax.broadcasted_iota(I32, logits.shape, 1)
    lane = lane_i.astype(F32)
    neg = jnp.float32(-jnp.inf)
    far = jnp.float32(LANES)
    gl = jnp.where(lane_i < N_GRP, logits, neg)
    gmax = jnp.max(gl, -1, keepdims=True)
    gidx = jnp.min(jnp.where(gl == gmax, lane, far), -1, keepdims=True)
    p_top = 1.0 / jnp.sum(jnp.exp(gl - gmax), -1, keepdims=True)
    grp_of_lane = ((lane_i - N_GRP) >> 3).astype(F32)
    in_grp = (lane_i >= N_GRP) & (lane_i < N_GRP + N_EXP) & (grp_of_lane == gidx)
    el = jnp.where(in_grp, logits, neg)
    m1 = jnp.max(el, -1, keepdims=True)
    i1 = jnp.min(jnp.where(el == m1, lane, far), -1, keepdims=True)
    el2 = jnp.where(lane == i1, neg, el)
    m2 = jnp.max(el2, -1, keepdims=True)
    i2 = jnp.min(jnp.where(el2 == m2, lane, far), -1, keepdims=True)
    e2 = jnp.exp(m2 - m1)
    w1 = p_top / (1.0 + e2)
    w2 = p_top * e2 / (1.0 + e2)
    return i1 - N_GRP, i2 - N_GRP, w1, w2


def _pair(a, b, dtype):
    two = lax.broadcasted_iota(I32, (a.shape[0], TOP_K), 1)
    return jnp.where(two == 0, a, b).astype(dtype)


def _post_kernel(ac_ref, al_ref, gm_ref, xc_ref, xl_ref, mods_ref, wout_ref, g1_ref, b1_ref, wrt_ref,
                 brt_ref, x1_ref, h2p_ref, ridx_ref, rwt_ref):
    i = pl.program_id(0)
    is_ctx = i < T_CTX // TP
    m = mods_ref[0]
    gate1, shift2, scale2 = m[2:3], m[3:4], m[4:5]

    def mix_of(s):
        rows = slice(s * SUB, (s + 1) * SUB)
        attn = jnp.where(is_ctx, ac_ref[rows, :], al_ref[rows, :])
        return _dot(jnp.concatenate([attn, gm_ref[rows, :]], axis=1), wout_ref[...])

    n_sub = TP // SUB
    mix_next = mix_of(0)
    for s in range(n_sub):
        rows = slice(s * SUB, (s + 1) * SUB)
        mix = mix_next
        if s + 1 < n_sub:
            mix_next = mix_of(s + 1)
        x = jnp.where(is_ctx, xc_ref[rows, :], xl_ref[rows, :])
        x1 = _layer_norm(ALPHA * x + gate1 * mix, g1_ref[...], b1_ref[...])
        x1_ref[rows, :] = x1
        h2 = x1 * (1.0 + scale2) + shift2
        h2p_ref[rows, :] = _pack_pair(h2[:, :D // 2], h2[:, D // 2:])
        e1, e2, w1, w2 = _route(_dot(h2.astype(BF16), wrt_ref[...]) + brt_ref[...])
        ridx_ref[rows, :] = _pair(e1, e2, I32)
        rwt_ref[rows, :] = _pair(w1, w2, F32)


def _post(attn_c, attn_l, gm, xc, xl, mods, wout, g1, b1, wrt, brt):
    npt = T_CTX // TP

    def tok(width):
        return pl.BlockSpec((TP, width), lambda i: (i, 0))

    def ctx(width):
        return pl.BlockSpec((TP, width), lambda i: (jnp.minimum(i, npt - 1), 0))

    def lat(width):
        return pl.BlockSpec((TP, width), lambda i: (jnp.maximum(i - npt, 0), 0))

    return pl.pallas_call(
        _post_kernel,
        grid=(T // TP,),
        in_specs=[ctx(H * V_DIM), lat(H * V_DIM), tok(GW), ctx(D), lat(D),
                  pl.BlockSpec((1, 6, D), lambda i: (_mods_index(i, TP), 0, 0)),
                  _const_spec((D, D)), _const_spec((1, D)), _const_spec((1, D)),
                  _const_spec((D, LANES)), _const_spec((1, LANES))],
        out_specs=[tok(D), tok(D // 2), tok(TOP_K), tok(TOP_K)],
        out_shape=[jax.ShapeDtypeStruct((T, D), F32),
                   jax.ShapeDtypeStruct((T, D // 2), U32),
                   jax.ShapeDtypeStruct((T, TOP_K), I32),
                   jax.ShapeDtypeStruct((T, TOP_K), F32)],
        compiler_params=_cp(("arbitrary",)),
        name="post",
    )(attn_c, attn_l, gm, xc, xl, mods, wout, g1, b1, wrt, brt)


def _rank_kernel(ridx_ref, tri_ref, rank_ref, hist_ref):
    lane = lax.broadcasted_iota(I32, (SUB, LANES), 1).astype(F32)
    seen = jnp.zeros((1, LANES), F32)
    for s in range(TP // SUB):
        rows = slice(s * SUB, (s + 1) * SUB)
        e = ridx_ref[rows, :].astype(F32)
        oh1 = (lane == e[:, 0:1]).astype(F32)
        oh2 = (lane == e[:, 1:2]).astype(F32)
        both = oh1 + oh2
        before = _dot(tri_ref[...], both.astype(BF16)) + seen
        rank_ref[rows, :] = _pair(jnp.sum(before * oh1, -1, keepdims=True),
                                  jnp.sum(before * oh2, -1, keepdims=True), I32)
        seen = seen + jnp.sum(both, 0, keepdims=True)
    hist_ref[0] = jnp.broadcast_to(seen, (8, LANES))


def _rank(ridx):
    return pl.pallas_call(
        _rank_kernel,
        grid=(T // TP,),
        in_specs=[pl.BlockSpec((TP, TOP_K), lambda i: (i, 0)), _const_spec((SUB, SUB))],
        out_specs=[pl.BlockSpec((TP, TOP_K), lambda i: (i, 0)),
                   pl.BlockSpec((1, 8, LANES), lambda i: (i, 0, 0))],
        out_shape=[jax.ShapeDtypeStruct((T, TOP_K), I32),
                   jax.ShapeDtypeStruct((T // TP, 8, LANES), F32)],
        compiler_params=_cp(("arbitrary",)),
        name="rank",
    )(ridx, jnp.asarray(np.tril(np.ones((SUB, SUB), np.float32), -1), BF16))


def _row_copy(src, src_row, dst, dst_row, sem):
    return pltpu.make_async_copy(src.at[pl.ds(src_row, 1)], dst.at[pl.ds(dst_row, 1)], sem)


def _dispatch_kernel(zb_ref, zv_ref, dest_ref, h2p_ref, xs_ref, zbuf, sem):
    i = pl.program_id(0)

    @pl.when(i == 0)
    def _():
        zbuf[...] = jnp.zeros_like(zbuf)

        def zero_block(j):
            return pltpu.make_async_copy(zbuf, xs_ref.at[pl.ds(zb_ref[j] * BM, BM)], sem)

        def start(j, carry):
            @pl.when(zv_ref[j] != 0)
            def _():
                zero_block(j).start()
            return carry

        def wait(j, carry):
            @pl.when(zv_ref[j] != 0)
            def _():
                zero_block(j).wait()
            return carry

        lax.fori_loop(0, N_ZERO, start, 0)
        lax.fori_loop(0, N_ZERO, wait, 0)

    def issue(r, carry):
        for k in range(TOP_K):
            _row_copy(h2p_ref, r, xs_ref, dest_ref[0, 0, TOP_K * r + k], sem).start()
        return carry

    lax.fori_loop(0, TD, issue, 0, unroll=8)
    for _ in range(TOP_K):
        pltpu.make_async_copy(h2p_ref, xs_ref.at[pl.ds(0, TD)], sem).wait()


def _dispatch(zero_blk, zero_valid, dest, h2p):
    grid_spec = pltpu.PrefetchScalarGridSpec(
        num_scalar_prefetch=2,
        grid=(T // TD,),
        in_specs=[pl.BlockSpec((1, 1, TOP_K * TD), lambda i, zb, zv: (i, 0, 0), memory_space=pltpu.SMEM),
                  pl.BlockSpec((TD, D // 2), lambda i, zb, zv: (i, 0))],
        out_specs=pl.BlockSpec(memory_space=pl.ANY),
        scratch_shapes=[pltpu.VMEM((BM, D // 2), U32), pltpu.SemaphoreType.DMA(())],
    )
    return pl.pallas_call(
        _dispatch_kernel,
        grid_spec=grid_spec,
        out_shape=jax.ShapeDtypeStruct((N_BLOCKS * BM, D // 2), U32),
        compiler_params=_cp(("arbitrary",)),
        name="dispatch",
    )(zero_blk, zero_valid, dest.reshape(T // TD, 1, TOP_K * TD), h2p)


def _experts_kernel(be_ref, bv_ref, sw_ref, es_ref, nx_ref, x_ref, w1_hbm, w3_hbm, w2_hbm, y_ref,
                    wf1, wf3, wf2, w1b, w3b, w2b, sems):
    i = pl.program_id(0)
    e = be_ref[i]
    nv = bv_ref[i]

    def weight_copies(expert, s):
        return (pltpu.make_async_copy(w1_hbm.at[expert], wf1.at[s], sems.at[s]),
                pltpu.make_async_copy(w3_hbm.at[expert], wf3.at[s], sems.at[s]),
                pltpu.make_async_copy(w2_hbm.at[expert], wf2.at[s], sems.at[s]))

    @pl.when(sw_ref[i] != 0)
    def _():
        s = es_ref[i]

        @pl.when(i == 0)
        def _():
            for c in weight_copies(e, s):
                c.start()

        for c in weight_copies(e, s):
            c.wait()
        nxt = nx_ref[i]

        @pl.when(nxt >= 0)
        def _():
            for c in weight_copies(nxt, 1 - s):
                c.start()

        w1b[...] = wf1[s].astype(BF16)
        w3b[...] = wf3[s].astype(BF16)
        w2b[...] = wf2[s].astype(BF16)

    @pl.when(nv > 0)
    def _():
        lo, hi = _unpack_pair(x_ref[...])
        xb = jnp.concatenate([lo.astype(BF16), hi.astype(BF16)], axis=1)
        a = _dot(xb, w1b[...])
        b = _dot(xb, w3b[...])
        hid = (jax.nn.silu(a) * b).astype(BF16)
        y = _dot(hid, w2b[...])
        y_ref[...] = _pack_pair(y[:, :D // 2], y[:, D // 2:])

    @pl.when(nv == 0)
    def _():
        y_ref[...] = jnp.zeros_like(y_ref)


def _experts(blk_expert, blk_valid, switch, wslot, next_expert, xs, w1, w3, w2):
    hbm = pl.BlockSpec(memory_space=pl.ANY)
    grid_spec = pltpu.PrefetchScalarGridSpec(
        num_scalar_prefetch=5,
        grid=(N_BLOCKS,),
        in_specs=[pl.BlockSpec((BM, D // 2), lambda i, *_: (i, 0)), hbm, hbm, hbm],
        out_specs=pl.BlockSpec((BM, D // 2), lambda i, *_: (i, 0)),
        scratch_shapes=[pltpu.VMEM((2, D, E_HID), F32), pltpu.VMEM((2, D, E_HID), F32),
                        pltpu.VMEM((2, E_HID, D), F32),
                        pltpu.VMEM((D, E_HID), BF16), pltpu.VMEM((D, E_HID), BF16),
                        pltpu.VMEM((E_HID, D), BF16), pltpu.SemaphoreType.DMA((2,))],
    )
    return pl.pallas_call(
        _experts_kernel,
        grid_spec=grid_spec,
        out_shape=jax.ShapeDtypeStruct((N_BLOCKS * BM, D // 2), U32),
        compiler_params=_cp(("arbitrary",)),
        name="experts",
    )(blk_expert, blk_valid, switch, wslot, next_expert, xs, w1, w3, w2)


def _final_kernel(dcur_ref, dnext_ref, ys_ref, x1_ref, rwt_ref, mods_ref, g2_ref, b2_ref, oc_ref, ol_ref,
                  buf, sems):
    i = pl.program_id(0)
    n = pl.num_programs(0)
    slot = i % 2

    def issue(d_ref, s):
        def body(r, carry):
            for k in range(TOP_K):
                _row_copy(ys_ref, d_ref[0, 0, TOP_K * r + k], buf.at[s, k], r, sems.at[s]).start()
            return carry

        lax.fori_loop(0, TM, body, 0, unroll=8)

    @pl.when(i == 0)
    def _():
        issue(dcur_ref, 0)

    @pl.when(i + 1 < n)
    def _():
        issue(dnext_ref, 1 - slot)

    for k in range(TOP_K):
        pltpu.make_async_copy(ys_ref.at[pl.ds(0, TM)], buf.at[slot, k], sems.at[slot]).wait()

    w = rwt_ref[...]
    lo0, hi0 = _unpack_pair(buf[slot, 0])
    lo1, hi1 = _unpack_pair(buf[slot, 1])
    w0, w1 = w[:, 0:1], w[:, 1:2]
    moe = jnp.concatenate([lo0 * w0 + lo1 * w1, hi0 * w0 + hi1 * w1], axis=1)
    gate2 = mods_ref[0][5:6]
    out = _layer_norm(ALPHA * x1_ref[...] + gate2 * moe, g2_ref[...], b2_ref[...])

    @pl.when(i < NP_TILES)
    def _():
        oc_ref[...] = out

    @pl.when(i >= NP_TILES)
    def _():
        ol_ref[...] = out


def _final(dest3, ys, x1, rwt, mods, g2, b2):
    n = T // TM
    return pl.pallas_call(
        _final_kernel,
        grid=(n,),
        in_specs=[pl.BlockSpec((1, 1, TOP_K * TM), lambda i: (i, 0, 0), memory_space=pltpu.SMEM),
                  pl.BlockSpec((1, 1, TOP_K * TM), lambda i: (jnp.minimum(i + 1, n - 1), 0, 0),
                               memory_space=pltpu.SMEM),
                  pl.BlockSpec(memory_space=pl.ANY),
                  pl.BlockSpec((TM, D), lambda i: (i, 0)),
                  pl.BlockSpec((TM, TOP_K), lambda i: (i, 0)),
                  pl.BlockSpec((1, 6, D), lambda i: (_mods_index(i), 0, 0)),
                  _const_spec((1, D)), _const_spec((1, D))],
        out_specs=[pl.BlockSpec((TM, D), lambda i: (jnp.minimum(i, NP_TILES - 1), 0)),
                   pl.BlockSpec((TM, D), lambda i: (jnp.maximum(i - NP_TILES, 0), 0))],
        out_shape=[jax.ShapeDtypeStruct((T_CTX, D), F32), jax.ShapeDtypeStruct((T_LAT, D), F32)],
        scratch_shapes=[pltpu.VMEM((2, TOP_K, TM, D // 2), U32), pltpu.SemaphoreType.DMA((2,))],
        compiler_params=_cp(("arbitrary",)),
        name="final",
    )(dest3, dest3, ys, x1, rwt, mods, g2, b2)


def _rope_tables():
    n = N_LAT
    rows = n // GRID_W
    f32 = np.float32
    row = np.repeat(np.arange(rows, dtype=f32), GRID_W)
    col = np.tile(np.arange(GRID_W, dtype=f32), rows)
    half = QK_ROPE // 2
    inv = (f32(ROPE_THETA) ** (-np.arange(0, half, 2, dtype=f32) / f32(half))).astype(f32)
    ang_r, ang_c = (row[:, None] * inv).astype(f32), (col[:, None] * inv).astype(f32)
    cr, sr = np.cos(ang_r).astype(f32), np.sin(ang_r).astype(f32)
    cc, sc = np.cos(ang_c).astype(f32), np.sin(ang_c).astype(f32)
    z16 = np.zeros_like(cr)
    pad = lambda v: np.full((n, LANES - QK_ROPE), v, f32)
    cos = np.concatenate([cr, cr, cc, cc, pad(1.0)], axis=1)
    s_plus = np.concatenate([z16, sr, z16, sc, pad(0.0)], axis=1)
    s_minus = np.concatenate([-sr, z16, -sc, z16, pad(0.0)], axis=1)
    ident = lambda v: np.full((TM, LANES), v, f32)
    return (jnp.asarray(np.concatenate([cos, ident(1.0)], 0)),
            jnp.asarray(np.concatenate([s_plus, ident(0.0)], 0)),
            jnp.asarray(np.concatenate([s_minus, ident(0.0)], 0)))


def _block_plan(ridx, rank, hist):
    counts = jnp.sum(hist, axis=0)
    nblk = (counts + BM - 1) // BM
    bends = jnp.cumsum(nblk)
    bstarts = bends - nblk
    base = bstarts[None, :] * BM + jnp.cumsum(hist, axis=0) - hist
    step = jnp.arange(T, dtype=I32)[:, None] // TP
    dest = (base.reshape(-1)[step * N_EXP + ridx] + rank).reshape(-1)
    blk = jnp.arange(N_BLOCKS, dtype=I32)
    blk_expert = jnp.minimum(jnp.sum((blk[:, None] >= bends[None, :]).astype(I32), axis=1), N_EXP - 1)
    left = counts[blk_expert] - (blk - bstarts[blk_expert]) * BM
    blk_valid = jnp.where(blk < bends[-1], jnp.clip(left, 0, BM), 0).astype(I32)
    unused = bends[-1] + jnp.arange(N_EXP, dtype=I32)
    zero_blk = jnp.concatenate([jnp.maximum(bends - 1, 0), jnp.minimum(unused, N_BLOCKS - 1)])
    zero_valid = jnp.concatenate([counts % BM != 0, unused < N_BLOCKS])
    prev_expert = jnp.concatenate([jnp.full((1,), -1, I32), blk_expert[:-1]])
    switch = (blk < bends[-1]) & (blk_expert != prev_expert)
    wslot = (jnp.cumsum(switch.astype(I32)) - 1) % 2
    eid = jnp.arange(N_EXP, dtype=I32)
    later = (eid[None, :] > eid[:, None]) & (nblk[None, :] > 0)
    next_of = jnp.min(jnp.where(later, eid[None, :], N_EXP), axis=1)
    next_expert = jnp.where(next_of < N_EXP, next_of, -1)[blk_expert]
    return (dest.astype(I32), blk_expert.astype(I32), blk_valid, zero_blk.astype(I32),
            zero_valid.astype(I32), switch.astype(I32), wslot.astype(I32), next_expert.astype(I32))


def kernel(x_prompt, x_sample, cache_ckv, cache_krope, c, c_ctx, w_ada, b_ada, w_in, q_norm_g, w_uq,
           kv_norm_g, w_ukv, gmlp_ln_g, gmlp_ln_b, w_spatial, b_spatial, w_out, ln1_g, ln1_b, w_group,
           b_group, w_router, b_router, w1, w3, w2, ln2_g, ln2_b):
    l = 0
    xc = x_prompt.reshape(T_CTX, D)
    xl = x_sample.reshape(T_LAT, D)

    cond16 = jnp.concatenate([c, c_ctx[None, :], jnp.zeros((16 - N_LAT_B - 1, D), F32)], axis=0)
    mods = _adaln(cond16, w_ada[l], b_ada[l][None, :]).reshape(16, 6, D)

    wi = w_in[l]
    o_kr = QL + KVL
    o_u = o_kr + QK_ROPE
    win_a = wi[:, :o_kr].astype(BF16)
    win_g = wi[:, o_u:].astype(BF16)
    win_r = jnp.pad(wi[:, o_kr:o_u], ((0, 0), (0, LANES - QK_ROPE))).astype(BF16)
    wuq = jnp.pad(w_uq[l].reshape(QL, H, QK_NOPE + QK_ROPE),
                  ((0, 0), (0, 0), (0, HEAD_PAD - QK_NOPE - QK_ROPE))).reshape(QL, H * HEAD_PAD).astype(BF16)
    wkv = w_ukv[l].reshape(KVL, H, QK_NOPE + V_DIM)
    wuk = wkv[:, :, :QK_NOPE].reshape(KVL, H * QK_NOPE).astype(BF16)
    wuvt = wkv[:, :, QK_NOPE:].reshape(KVL, H * V_DIM).T.astype(BF16)
    ws = w_spatial[l].astype(BF16)
    bsb = jnp.broadcast_to(b_spatial[l][:, :, None], (G_GROUPS, CHUNK, G_DIM))
    wrt = jnp.concatenate([w_group[l], w_router[l], jnp.zeros((D, LANES - N_GRP - N_EXP), F32)],
                          axis=1).astype(BF16)
    brt = jnp.concatenate([b_group[l], b_router[l], jnp.zeros((LANES - N_GRP - N_EXP,), F32)])[None, :]
    cos_t, sp_t, sm_t = _rope_tables()
    row = lambda v: v[l][None, :]

    q, k, vt, gm, ckv_n, kr = _proj(xc, xl, mods, win_a, win_g, win_r, row(q_norm_g), wuq, row(kv_norm_g), wuk, wuvt,
                                   row(gmlp_ln_g), row(gmlp_ln_b), ws, bsb, cos_t, sp_t, sm_t)
    kc, vct = _kvexp(cache_ckv[:, l].reshape(N_LAT_B * PAST, KVL),
                     cache_krope[:, l].reshape(N_LAT_B * PAST, QK_ROPE), wuk, wuvt)
    attn_c = _attn_ctx(q, k, vt)
    attn_l = _attn_lat(q, k, vt, kc, vct)

    x1, h2p, ridx, rwt = _post(attn_c, attn_l, gm, xc, xl, mods, w_out[l].astype(BF16), row(ln1_g), row(ln1_b),
                               wrt, brt)
    rank, hist = _rank(ridx)
    dest, blk_expert, blk_valid, zero_blk, zero_valid, switch, wslot, next_expert = _block_plan(
        ridx, rank, hist[:, 0, :N_EXP].astype(I32))
    xs = _dispatch(zero_blk, zero_valid, dest, h2p)
    ys = _experts(blk_expert, blk_valid, switch, wslot, next_expert, xs, w1[l], w3[l], w2[l])
    y_c, y_l = _final(dest.reshape(T // TM, 1, TOP_K * TM), ys, x1, rwt, mods, row(ln2_g), row(ln2_b))

    return (y_c.reshape(N_CTX_B, N_CTX, D), y_l.reshape(N_LAT_B, N_LAT, D),
            ckv_n.reshape(N_CTX_B, 1, N_CTX, KVL), kr.reshape(N_CTX_B, 1, N_CTX, QK_ROPE))
```

```python
import functools

import jax
import jax.numpy as jnp
import numpy as np
from jax import lax
from jax.experimental import pallas as pl
from jax.experimental.pallas import tpu as pltpu

F32 = jnp.float32
BF16 = jnp.bfloat16
U32 = jnp.uint32
I32 = jnp.int32

D = 2048
N_CTX_B, N_CTX = 32, 256
N_LAT_B, N_LAT = 4, 4096
PAST = 512
GRID_W = 64
H = 8
QK_NOPE, QK_ROPE, V_DIM = 128, 64, 128
QL, KVL = 512, 512
GW = 1024
G_GROUPS, G_DIM, CHUNK = 8, 128, 128
N_GRP, E_PER_GRP, N_EXP, TOP_K = 4, 8, 32, 2
E_HID = 512
ROPE_THETA = 10000.0
EPS = 1e-6
ALPHA = 2.0 ** 0.25
SM_SCALE = (QK_NOPE + QK_ROPE) ** -0.5
Q_SCALE = SM_SCALE * float(np.log2(np.e))

T_CTX = N_CTX_B * N_CTX
T_LAT = N_LAT_B * N_LAT
T = T_CTX + T_LAT
N_ASSIGN = T * TOP_K

LANES = 128
HEAD_PAD = 256
V_AUG = V_DIM + 16
TM = 256
TP = 512
SUB = 256
TD = 1024
TQ = 1024
TK = 512
BM = 256
N_BLOCKS = N_ASSIGN // BM + N_EXP
N_ZERO = 2 * N_EXP
VMEM_LIMIT = 56 * 1024 * 1024

NP_TILES = T_CTX // TM
LAT_TILES_PER_B = N_LAT // TM


def _cp(sem):
    return pltpu.CompilerParams(dimension_semantics=sem, vmem_limit_bytes=VMEM_LIMIT)


def _const_spec(shape):
    nd = len(shape)
    return pl.BlockSpec(shape, lambda *a: (0,) * nd, pipeline_mode=pl.Buffered(1))


def _dot(a, b):
    return jnp.dot(a, b, preferred_element_type=F32)


def _dot_nt(a, b):
    return lax.dot_general(a, b, (((1,), (1,)), ((), ())), preferred_element_type=F32)


def _layer_norm(x, g, b):
    mu = jnp.mean(x, -1, keepdims=True)
    xc = x - mu
    var = jnp.mean(xc * xc, -1, keepdims=True)
    return xc * lax.rsqrt(var + EPS) * g + b


def _rms_norm(x, g):
    return x * lax.rsqrt(jnp.mean(x * x, -1, keepdims=True) + EPS) * g


def _gelu(x):
    return 0.5 * x * (1.0 + lax.erf(x * np.float32(np.sqrt(0.5))))


def _pack_pair(lo, hi):
    lo_b = lax.bitcast_convert_type(lo.astype(BF16).astype(F32), U32)
    hi_b = lax.bitcast_convert_type(hi.astype(BF16).astype(F32), U32)
    return hi_b | (lo_b >> 16)


def _unpack_pair(w):
    lo = lax.bitcast_convert_type(w << 16, F32)
    hi = lax.bitcast_convert_type(w & jnp.uint32(0xFFFF0000), F32)
    return lo, hi


def _adaln_kernel(cond_ref, w_ref, b_ref, o_ref):
    s = jax.nn.silu(cond_ref[...])
    s_hi = s.astype(BF16)
    s_lo = (s - s_hi.astype(F32)).astype(BF16)
    w = w_ref[...]
    w_hi = w.astype(BF16)
    w_lo = (w - w_hi.astype(F32)).astype(BF16)
    o_ref[...] = _dot(s_hi, w_hi) + _dot(s_lo, w_hi) + _dot(s_hi, w_lo) + b_ref[...]


def _adaln(cond16, w_ada, b_ada):
    tn = 1024
    n = w_ada.shape[1]
    return pl.pallas_call(
        _adaln_kernel,
        grid=(n // tn,),
        in_specs=[pl.BlockSpec((16, D), lambda j: (0, 0)),
                  pl.BlockSpec((D, tn), lambda j: (0, j)),
                  pl.BlockSpec((1, tn), lambda j: (0, j))],
        out_specs=pl.BlockSpec((16, tn), lambda j: (0, j)),
        out_shape=jax.ShapeDtypeStruct((16, n), F32),
        compiler_params=_cp(("arbitrary",)),
        name="adaln",
    )(cond16, w_ada, b_ada)


def _expand_kv(ckv_n, kr128, wuk_ref, wuvt_ref, k_ref, vt_ref):
    cb = ckv_n.astype(BF16)
    kn = _dot(cb, wuk_ref[...])
    vt = _dot_nt(wuvt_ref[...], cb).astype(BF16)
    ones = jnp.ones((V_AUG - V_DIM, vt.shape[1]), BF16)
    for h in range(H):
        vt_ref[h * V_AUG:h * V_AUG + V_DIM, :] = vt[h * V_DIM:(h + 1) * V_DIM]
        vt_ref[h * V_AUG + V_DIM:(h + 1) * V_AUG, :] = ones
    krb = kr128.astype(BF16)
    for h in range(H):
        k_ref[:, h * HEAD_PAD:h * HEAD_PAD + QK_NOPE] = kn[:, h * QK_NOPE:(h + 1) * QK_NOPE].astype(BF16)
        k_ref[:, h * HEAD_PAD + QK_NOPE:(h + 1) * HEAD_PAD] = krb


def _rope128(x, cos, s_plus, s_minus):
    return (x * cos + pltpu.roll(x, 16, 1) * s_plus + pltpu.roll(x, LANES - 16, 1) * s_minus)


def _proj_kernel(xc_ref, xl_ref, mods_ref, win_a_ref, win_g_ref, win_r_ref, qg_ref, wuq_ref, kvg_ref,
                 wuk_ref, wuv_ref,
                 lng_ref, lnb_ref, ws_ref, bsb_ref, cos_ref, sp_ref, sm_ref,
                 q_ref, k_ref, v_ref, gm_ref, ckv_ref, kr_ref):
    i = pl.program_id(0)
    x = jnp.where(i < NP_TILES, xc_ref[...], xl_ref[...])
    m = mods_ref[0]
    shift1, scale1 = m[0:1], m[1:2]
    hmod = (x * (1.0 + scale1) + shift1).astype(BF16)
    pa = _dot(hmod, win_a_ref[...])
    pg = _dot(hmod, win_g_ref[...])
    kr = _dot(hmod, win_r_ref[...])
    cq, ckv = pa[:, :QL], pa[:, QL:]
    u, v = pg[:, :GW], pg[:, GW:]

    cos, s_plus, s_minus = cos_ref[...], sp_ref[...], sm_ref[...]

    cq_n = _rms_norm(cq, qg_ref[...]).astype(BF16)
    q = _dot(cq_n, wuq_ref[...]) * Q_SCALE
    for h in range(H):
        q_ref[:, h * HEAD_PAD:h * HEAD_PAD + QK_NOPE] = q[:, h * HEAD_PAD:h * HEAD_PAD + QK_NOPE].astype(BF16)
        q_ref[:, h * HEAD_PAD + QK_NOPE:(h + 1) * HEAD_PAD] = _rope128(
            q[:, h * HEAD_PAD + QK_NOPE:(h + 1) * HEAD_PAD], cos, s_plus, s_minus).astype(BF16)

    ckv_n = _rms_norm(ckv, kvg_ref[...])

    @pl.when(i < NP_TILES)
    def _():
        ckv_ref[...] = ckv_n
        kr_ref[...] = kr[:, :QK_ROPE]

    _expand_kv(ckv_n, _rope128(kr, cos, s_plus, s_minus), wuk_ref, wuv_ref, k_ref, v_ref)

    gu = _gelu(u)
    vn = _layer_norm(_gelu(v), lng_ref[...], lnb_ref[...]).astype(BF16)
    for c in range(TM // CHUNK):
        rows = slice(c * CHUNK, (c + 1) * CHUNK)
        for g in range(G_GROUPS):
            cols = slice(g * G_DIM, (g + 1) * G_DIM)
            s = _dot(ws_ref[g], vn[rows, cols]) + bsb_ref[g]
            gm_ref[rows, cols] = (gu[rows, cols] * s).astype(BF16)


def _mods_index(i, tile=TM):
    return jnp.where(i < T_CTX // tile, N_LAT_B, (i - T_CTX // tile) // (N_LAT // tile))


def _proj(xc, xl, mods, win_a, win_g, win_r, qg, wuq, kvg, wuk, wuv, lng, lnb, ws, bsb, cos_t, sp_t, sm_t):
    def tok(width):
        return pl.BlockSpec((TM, width), lambda i: (i, 0))

    def pos_block(i):
        return jnp.where(i < NP_TILES, LAT_TILES_PER_B, (i - NP_TILES) % LAT_TILES_PER_B)

    rope_spec = pl.BlockSpec((TM, LANES), lambda i: (pos_block(i), 0))
    return pl.pallas_call(
        _proj_kernel,
        grid=(T // TM,),
        in_specs=[
            pl.BlockSpec((TM, D), lambda i: (jnp.minimum(i, NP_TILES - 1), 0)),
            pl.BlockSpec((TM, D), lambda i: (jnp.maximum(i - NP_TILES, 0), 0)),
            pl.BlockSpec((1, 6, D), lambda i: (_mods_index(i), 0, 0)),
            _const_spec((D, QL + KVL)), _const_spec((D, 2 * GW)), _const_spec((D, LANES)),
            _const_spec((1, QL)), _const_spec((QL, H * HEAD_PAD)),
            _const_spec((1, KVL)), _const_spec((KVL, H * QK_NOPE)), _const_spec((H * V_DIM, KVL)),
            _const_spec((1, GW)), _const_spec((1, GW)),
            _const_spec((G_GROUPS, CHUNK, CHUNK)), _const_spec((G_GROUPS, CHUNK, G_DIM)),
            rope_spec, rope_spec, rope_spec,
        ],
        out_specs=[tok(H * HEAD_PAD), tok(H * HEAD_PAD),
                   pl.BlockSpec((H * V_AUG, TM), lambda i: (0, i)), tok(GW),
                   pl.BlockSpec((TM, KVL), lambda i: (jnp.minimum(i, NP_TILES - 1), 0)),
                   pl.BlockSpec((TM, QK_ROPE), lambda i: (jnp.minimum(i, NP_TILES - 1), 0))],
        out_shape=[
            jax.ShapeDtypeStruct((T, H * HEAD_PAD), BF16),
            jax.ShapeDtypeStruct((T, H * HEAD_PAD), BF16),
            jax.ShapeDtypeStruct((H * V_AUG, T), BF16),
            jax.ShapeDtypeStruct((T, GW), BF16),
            jax.ShapeDtypeStruct((T_CTX, KVL), F32),
            jax.ShapeDtypeStruct((T_CTX, QK_ROPE), F32),
        ],
        compiler_params=_cp(("arbitrary",)),
        name="proj",
    )(xc, xl, mods, win_a, win_g, win_r, qg, wuq, kvg, wuk, wuv, lng, lnb, ws, bsb, cos_t, sp_t, sm_t)


def _kvexp_kernel(ckv_ref, kr_ref, wuk_ref, wuv_ref, k_ref, v_ref):
    kr = kr_ref[...]
    kr128 = jnp.concatenate([kr, jnp.zeros_like(kr)], axis=1)
    _expand_kv(ckv_ref[...], kr128, wuk_ref, wuv_ref, k_ref, v_ref)


def _kvexp(ckv, kr, wuk, wuv):
    rows = ckv.shape[0]
    return pl.pallas_call(
        _kvexp_kernel,
        grid=(rows // TM,),
        in_specs=[pl.BlockSpec((TM, KVL), lambda i: (i, 0)), pl.BlockSpec((TM, QK_ROPE), lambda i: (i, 0)),
                  _const_spec((KVL, H * QK_NOPE)), _const_spec((H * V_DIM, KVL))],
        out_specs=[pl.BlockSpec((TM, H * HEAD_PAD), lambda i: (i, 0)),
                   pl.BlockSpec((H * V_AUG, TM), lambda i: (0, i))],
        out_shape=[jax.ShapeDtypeStruct((rows, H * HEAD_PAD), BF16),
                   jax.ShapeDtypeStruct((H * V_AUG, rows), BF16)],
        compiler_params=_cp(("arbitrary",)),
        name="kvexp",
    )(ckv, kr, wuk, wuv)


def _probs(s, m):
    m_new = jnp.maximum(m, jnp.max(s, 0, keepdims=True))
    return m_new, jnp.exp2(m - m_new), jnp.exp2(s - m_new).astype(BF16)


def _attn_finish(acc):
    return (acc[:V_DIM] / acc[V_DIM:V_DIM + 1]).T.astype(BF16)


def _attn_ctx_kernel(q_ref, k_ref, vt_ref, o_ref):
    m0 = jnp.full((1, N_CTX), -jnp.inf, F32)
    for h in range(H):
        qh = q_ref[:, h * HEAD_PAD:(h + 1) * HEAD_PAD]
        kh = k_ref[:, h * HEAD_PAD:(h + 1) * HEAD_PAD]
        _, _, p = _probs(_dot_nt(kh, qh), m0)
        acc = _dot(vt_ref[h * V_AUG:(h + 1) * V_AUG, :], p)
        o_ref[:, h * V_DIM:(h + 1) * V_DIM] = _attn_finish(acc)


def _attn_ctx(q, k, vt):
    return pl.pallas_call(
        _attn_ctx_kernel,
        grid=(N_CTX_B,),
        in_specs=[pl.BlockSpec((N_CTX, H * HEAD_PAD), lambda b: (b, 0)),
                  pl.BlockSpec((N_CTX, H * HEAD_PAD), lambda b: (b, 0)),
                  pl.BlockSpec((H * V_AUG, N_CTX), lambda b: (0, b))],
        out_specs=pl.BlockSpec((N_CTX, H * V_DIM), lambda b: (b, 0)),
        out_shape=jax.ShapeDtypeStruct((T_CTX, H * V_DIM), BF16),
        compiler_params=_cp(("arbitrary",)),
        name="attn_ctx",
    )(q, k, vt)


def _attn_lat_kernel(q_ref, k_ref, vt_ref, kc_ref, vct_ref, o_ref):
    q = q_ref[...]
    n_lat = N_LAT // TK

    def keys(c):
        return k_ref[c * TK:(c + 1) * TK, :] if c < n_lat else kc_ref[...]

    def values_t(c):
        return vt_ref[:, c * TK:(c + 1) * TK] if c < n_lat else vct_ref[...]

    m = jnp.full((1, TQ), -jnp.inf, F32)
    acc = jnp.zeros((V_AUG, TQ), F32)
    s_next = _dot_nt(keys(0), q)
    p_prev = alpha_prev = None
    for c in range(n_lat + 1):
        s = s_next
        if c < n_lat:
            s_next = _dot_nt(keys(c + 1), q)
        if c > 0:
            acc = alpha_prev * acc + _dot(values_t(c - 1), p_prev)
        m, alpha_prev, p_prev = _probs(s, m)
    acc = alpha_prev * acc + _dot(values_t(n_lat), p_prev)
    o_ref[...] = _attn_finish(acc)


def _attn_lat(q, k, vt, kc, vct):
    qb0 = T_CTX // TQ
    qpb = N_LAT // TQ
    kb0 = T_CTX // N_LAT
    return pl.pallas_call(
        _attn_lat_kernel,
        grid=(N_LAT_B, H, qpb),
        in_specs=[pl.BlockSpec((TQ, HEAD_PAD), lambda b, h, qi: (qb0 + b * qpb + qi, h)),
                  pl.BlockSpec((N_LAT, HEAD_PAD), lambda b, h, qi: (kb0 + b, h)),
                  pl.BlockSpec((V_AUG, N_LAT), lambda b, h, qi: (h, kb0 + b)),
                  pl.BlockSpec((PAST, HEAD_PAD), lambda b, h, qi: (b, h)),
                  pl.BlockSpec((V_AUG, PAST), lambda b, h, qi: (h, b))],
        out_specs=pl.BlockSpec((TQ, V_DIM), lambda b, h, qi: (b * qpb + qi, h)),
        out_shape=jax.ShapeDtypeStruct((T_LAT, H * V_DIM), BF16),
        compiler_params=_cp(("arbitrary", "arbitrary", "arbitrary")),
        name="attn_lat",
    )(q, k, vt, kc, vct)


def _route(logits):
    lane_i = lax.broadcasted_iota(I32, logits.shape, 1)
    lane = lane_i.astype(F32)
    neg = jnp.float32(-jnp.inf)
    far = jnp.float32(LANES)
    gl = jnp.where(lane_i < N_GRP, logits, neg)
    gmax = jnp.max(gl, -1, keepdims=True)
    gidx = jnp.min(jnp.where(gl == gmax, lane, far), -1, keepdims=True)
    p_top = 1.0 / jnp.sum(jnp.exp(gl - gmax), -1, keepdims=True)
    grp_of_lane = ((lane_i - N_GRP) >> 3).astype(F32)
    in_grp = (lane_i >= N_GRP) & (lane_i < N_GRP + N_EXP) & (grp_of_lane == gidx)
    el = jnp.where(in_grp, logits, neg)
    m1 = jnp.max(el, -1, keepdims=True)
    i1 = jnp.min(jnp.where(el == m1, lane, far), -1, keepdims=True)
    el2 = jnp.where(lane == i1, neg, el)
    m2 = jnp.max(el2, -1, keepdims=True)
    i2 = jnp.min(jnp.where(el2 == m2, lane, far), -1, keepdims=True)
    e2 = jnp.exp(m2 - m1)
    w1 = p_top / (1.0 + e2)
    w2 = p_top * e2 / (1.0 + e2)
    return i1 - N_GRP, i2 - N_GRP, w1, w2


def _pair(a, b, dtype):
    two = lax.broadcasted_iota(I32, (a.shape[0], TOP_K), 1)
    return jnp.where(two == 0, a, b).astype(dtype)


def _post_kernel(ac_ref, al_ref, gm_ref, xc_ref, xl_ref, mods_ref, wout_ref, g1_ref, b1_ref, wrt_ref,
                 brt_ref, x1_ref, h2p_ref, ridx_ref, rwt_ref):
    i = pl.program_id(0)
    is_ctx = i < T_CTX // TP
    m = mods_ref[0]
    gate1, shift2, scale2 = m[2:3], m[3:4], m[4:5]

    def mix_of(s):
        rows = slice(s * SUB, (s + 1) * SUB)
        attn = jnp.where(is_ctx, ac_ref[rows, :], al_ref[rows, :])
        return _dot(jnp.concatenate([attn, gm_ref[rows, :]], axis=1), wout_ref[...])

    n_sub = TP // SUB
    mix_next = mix_of(0)
    for s in range(n_sub):
        rows = slice(s * SUB, (s + 1) * SUB)
        mix = mix_next
        if s + 1 < n_sub:
            mix_next = mix_of(s + 1)
        x = jnp.where(is_ctx, xc_ref[rows, :], xl_ref[rows, :])
        x1 = _layer_norm(ALPHA * x + gate1 * mix, g1_ref[...], b1_ref[...])
        x1_ref[rows, :] = x1
        h2 = x1 * (1.0 + scale2) + shift2
        h2p_ref[rows, :] = _pack_pair(h2[:, :D // 2], h2[:, D // 2:])
        e1, e2, w1, w2 = _route(_dot(h2.astype(BF16), wrt_ref[...]) + brt_ref[...])
        ridx_ref[rows, :] = _pair(e1, e2, I32)
        rwt_ref[rows, :] = _pair(w1, w2, F32)


def _post(attn_c, attn_l, gm, xc, xl, mods, wout, g1, b1, wrt, brt):
    npt = T_CTX // TP

    def tok(width):
        return pl.BlockSpec((TP, width), lambda i: (i, 0))

    def ctx(width):
        return pl.BlockSpec((TP, width), lambda i: (jnp.minimum(i, npt - 1), 0))

    def lat(width):
        return pl.BlockSpec((TP, width), lambda i: (jnp.maximum(i - npt, 0), 0))

    return pl.pallas_call(
        _post_kernel,
        grid=(T // TP,),
        in_specs=[ctx(H * V_DIM), lat(H * V_DIM), tok(GW), ctx(D), lat(D),
                  pl.BlockSpec((1, 6, D), lambda i: (_mods_index(i, TP), 0, 0)),
                  _const_spec((D, D)), _const_spec((1, D)), _const_spec((1, D)),
                  _const_spec((D, LANES)), _const_spec((1, LANES))],
        out_specs=[tok(D), tok(D // 2), tok(TOP_K), tok(TOP_K)],
        out_shape=[jax.ShapeDtypeStruct((T, D), F32),
                   jax.ShapeDtypeStruct((T, D // 2), U32),
                   jax.ShapeDtypeStruct((T, TOP_K), I32),
                   jax.ShapeDtypeStruct((T, TOP_K), F32)],
        compiler_params=_cp(("arbitrary",)),
        name="post",
    )(attn_c, attn_l, gm, xc, xl, mods, wout, g1, b1, wrt, brt)


def _rank_kernel(ridx_ref, tri_ref, rank_ref, hist_ref):
    lane = lax.broadcasted_iota(I32, (SUB, LANES), 1).astype(F32)
    seen = jnp.zeros((1, LANES), F32)
    for s in range(TP // SUB):
        rows = slice(s * SUB, (s + 1) * SUB)
        e = ridx_ref[rows, :].astype(F32)
        oh1 = (lane == e[:, 0:1]).astype(F32)
        oh2 = (lane == e[:, 1:2]).astype(F32)
        both = oh1 + oh2
        before = _dot(tri_ref[...], both.astype(BF16)) + seen
        rank_ref[rows, :] = _pair(jnp.sum(before * oh1, -1, keepdims=True),
                                  jnp.sum(before * oh2, -1, keepdims=True), I32)
        seen = seen + jnp.sum(both, 0, keepdims=True)
    hist_ref[0] = jnp.broadcast_to(seen, (8, LANES))


def _rank(ridx):
    return pl.pallas_call(
        _rank_kernel,
        grid=(T // TP,),
        in_specs=[pl.BlockSpec((TP, TOP_K), lambda i: (i, 0)), _const_spec((SUB, SUB))],
        out_specs=[pl.BlockSpec((TP, TOP_K), lambda i: (i, 0)),
                   pl.BlockSpec((1, 8, LANES), lambda i: (i, 0, 0))],
        out_shape=[jax.ShapeDtypeStruct((T, TOP_K), I32),
                   jax.ShapeDtypeStruct((T // TP, 8, LANES), F32)],
        compiler_params=_cp(("arbitrary",)),
        name="rank",
    )(ridx, jnp.asarray(np.tril(np.ones((SUB, SUB), np.float32), -1), BF16))


def _row_copy(src, src_row, dst, dst_row, sem):
    return pltpu.make_async_copy(src.at[pl.ds(src_row, 1)], dst.at[pl.ds(dst_row, 1)], sem)


def _dispatch_kernel(zb_ref, zv_ref, dest_ref, h2p_ref, xs_ref, zbuf, sem):
    i = pl.program_id(0)

    @pl.when(i == 0)
    def _():
        zbuf[...] = jnp.zeros_like(zbuf)

        def zero_block(j):
            return pltpu.make_async_copy(zbuf, xs_ref.at[pl.ds(zb_ref[j] * BM, BM)], sem)

        def start(j, carry):
            @pl.when(zv_ref[j] != 0)
            def _():
                zero_block(j).start()
            return carry

        def wait(j, carry):
            @pl.when(zv_ref[j] != 0)
            def _():
                zero_block(j).wait()
            return carry

        lax.fori_loop(0, N_ZERO, start, 0)
        lax.fori_loop(0, N_ZERO, wait, 0)

    def issue(r, carry):
        for k in range(TOP_K):
            _row_copy(h2p_ref, r, xs_ref, dest_ref[0, 0, TOP_K * r + k], sem).start()
        return carry

    lax.fori_loop(0, TD, issue, 0, unroll=8)
    for _ in range(TOP_K):
        pltpu.make_async_copy(h2p_ref, xs_ref.at[pl.ds(0, TD)], sem).wait()


def _dispatch(zero_blk, zero_valid, dest, h2p):
    grid_spec = pltpu.PrefetchScalarGridSpec(
        num_scalar_prefetch=2,
        grid=(T // TD,),
        in_specs=[pl.BlockSpec((1, 1, TOP_K * TD), lambda i, zb, zv: (i, 0, 0), memory_space=pltpu.SMEM),
                  pl.BlockSpec((TD, D // 2), lambda i, zb, zv: (i, 0))],
        out_specs=pl.BlockSpec(memory_space=pl.ANY),
        scratch_shapes=[pltpu.VMEM((BM, D // 2), U32), pltpu.SemaphoreType.DMA(())],
    )
    return pl.pallas_call(
        _dispatch_kernel,
        grid_spec=grid_spec,
        out_shape=jax.ShapeDtypeStruct((N_BLOCKS * BM, D // 2), U32),
        compiler_params=_cp(("arbitrary",)),
        name="dispatch",
    )(zero_blk, zero_valid, dest.reshape(T // TD, 1, TOP_K * TD), h2p)


def _experts_kernel(be_ref, bv_ref, sw_ref, es_ref, nx_ref, x_ref, w1_hbm, w3_hbm, w2_hbm, y_ref,
                    wf1, wf3, wf2, w1b, w3b, w2b, sems):
    i = pl.program_id(0)
    e = be_ref[i]
    nv = bv_ref[i]

    def weight_copies(expert, s):
        return (pltpu.make_async_copy(w1_hbm.at[expert], wf1.at[s], sems.at[s]),
                pltpu.make_async_copy(w3_hbm.at[expert], wf3.at[s], sems.at[s]),
                pltpu.make_async_copy(w2_hbm.at[expert], wf2.at[s], sems.at[s]))

    @pl.when(sw_ref[i] != 0)
    def _():
        s = es_ref[i]

        @pl.when(i == 0)
        def _():
            for c in weight_copies(e, s):
                c.start()

        for c in weight_copies(e, s):
            c.wait()
        nxt = nx_ref[i]

        @pl.when(nxt >= 0)
        def _():
            for c in weight_copies(nxt, 1 - s):
                c.start()

        w1b[...] = wf1[s].astype(BF16)
        w3b[...] = wf3[s].astype(BF16)
        w2b[...] = wf2[s].astype(BF16)

    @pl.when(nv > 0)
    def _():
        lo, hi = _unpack_pair(x_ref[...])
        xb = jnp.concatenate([lo.astype(BF16), hi.astype(BF16)], axis=1)
        a = _dot(xb, w1b[...])
        b = _dot(xb, w3b[...])
        hid = (jax.nn.silu(a) * b).astype(BF16)
        y = _dot(hid, w2b[...])
        y_ref[...] = _pack_pair(y[:, :D // 2], y[:, D // 2:])

    @pl.when(nv == 0)
    def _():
        y_ref[...] = jnp.zeros_like(y_ref)


def _experts(blk_expert, blk_valid, switch, wslot, next_expert, xs, w1, w3, w2):
    hbm = pl.BlockSpec(memory_space=pl.ANY)
    grid_spec = pltpu.PrefetchScalarGridSpec(
        num_scalar_prefetch=5,
        grid=(N_BLOCKS,),
        in_specs=[pl.BlockSpec((BM, D // 2), lambda i, *_: (i, 0)), hbm, hbm, hbm],
        out_specs=pl.BlockSpec((BM, D // 2), lambda i, *_: (i, 0)),
        scratch_shapes=[pltpu.VMEM((2, D, E_HID), F32), pltpu.VMEM((2, D, E_HID), F32),
                        pltpu.VMEM((2, E_HID, D), F32),
                        pltpu.VMEM((D, E_HID), BF16), pltpu.VMEM((D, E_HID), BF16),
                        pltpu.VMEM((E_HID, D), BF16), pltpu.SemaphoreType.DMA((2,))],
    )
    return pl.pallas_call(
        _experts_kernel,
        grid_spec=grid_spec,
        out_shape=jax.ShapeDtypeStruct((N_BLOCKS * BM, D // 2), U32),
        compiler_params=_cp(("arbitrary",)),
        name="experts",
    )(blk_expert, blk_valid, switch, wslot, next_expert, xs, w1, w3, w2)


def _final_kernel(dcur_ref, dnext_ref, ys_ref, x1_ref, rwt_ref, mods_ref, g2_ref, b2_ref, oc_ref, ol_ref,
                  buf, sems):
    i = pl.program_id(0)
    n = pl.num_programs(0)
    slot = i % 2

    def issue(d_ref, s):
        def body(r, carry):
            for k in range(TOP_K):
                _row_copy(ys_ref, d_ref[0, 0, TOP_K * r + k], buf.at[s, k], r, sems.at[s]).start()
            return carry

        lax.fori_loop(0, TM, body, 0, unroll=8)

    @pl.when(i == 0)
    def _():
        issue(dcur_ref, 0)

    @pl.when(i + 1 < n)
    def _():
        issue(dnext_ref, 1 - slot)

    for k in range(TOP_K):
        pltpu.make_async_copy(ys_ref.at[pl.ds(0, TM)], buf.at[slot, k], sems.at[slot]).wait()

    w = rwt_ref[...]
    lo0, hi0 = _unpack_pair(buf[slot, 0])
    lo1, hi1 = _unpack_pair(buf[slot, 1])
    w0, w1 = w[:, 0:1], w[:, 1:2]
    moe = jnp.concatenate([lo0 * w0 + lo1 * w1, hi0 * w0 + hi1 * w1], axis=1)
    gate2 = mods_ref[0][5:6]
    out = _layer_norm(ALPHA * x1_ref[...] + gate2 * moe, g2_ref[...], b2_ref[...])

    @pl.when(i < NP_TILES)
    def _():
        oc_ref[...] = out

    @pl.when(i >= NP_TILES)
    def _():
        ol_ref[...] = out


def _final(dest3, ys, x1, rwt, mods, g2, b2):
    n = T // TM
    return pl.pallas_call(
        _final_kernel,
        grid=(n,),
        in_specs=[pl.BlockSpec((1, 1, TOP_K * TM), lambda i: (i, 0, 0), memory_space=pltpu.SMEM),
                  pl.BlockSpec((1, 1, TOP_K * TM), lambda i: (jnp.minimum(i + 1, n - 1), 0, 0),
                               memory_space=pltpu.SMEM),
                  pl.BlockSpec(memory_space=pl.ANY),
                  pl.BlockSpec((TM, D), lambda i: (i, 0)),
                  pl.BlockSpec((TM, TOP_K), lambda i: (i, 0)),
                  pl.BlockSpec((1, 6, D), lambda i: (_mods_index(i), 0, 0)),
                  _const_spec((1, D)), _const_spec((1, D))],
        out_specs=[pl.BlockSpec((TM, D), lambda i: (jnp.minimum(i, NP_TILES - 1), 0)),
                   pl.BlockSpec((TM, D), lambda i: (jnp.maximum(i - NP_TILES, 0), 0))],
        out_shape=[jax.ShapeDtypeStruct((T_CTX, D), F32), jax.ShapeDtypeStruct((T_LAT, D), F32)],
        scratch_shapes=[pltpu.VMEM((2, TOP_K, TM, D // 2), U32), pltpu.SemaphoreType.DMA((2,))],
        compiler_params=_cp(("arbitrary",)),
        name="final",
    )(dest3, dest3, ys, x1, rwt, mods, g2, b2)


def _rope_tables():
    n = N_LAT
    rows = n // GRID_W
    f32 = np.float32
    row = np.repeat(np.arange(rows, dtype=f32), GRID_W)
    col = np.tile(np.arange(GRID_W, dtype=f32), rows)
    half = QK_ROPE // 2
    inv = (f32(ROPE_THETA) ** (-np.arange(0, half, 2, dtype=f32) / f32(half))).astype(f32)
    ang_r, ang_c = (row[:, None] * inv).astype(f32), (col[:, None] * inv).astype(f32)
    cr, sr = np.cos(ang_r).astype(f32), np.sin(ang_r).astype(f32)
    cc, sc = np.cos(ang_c).astype(f32), np.sin(ang_c).astype(f32)
    z16 = np.zeros_like(cr)
    pad = lambda v: np.full((n, LANES - QK_ROPE), v, f32)
    cos = np.concatenate([cr, cr, cc, cc, pad(1.0)], axis=1)
    s_plus = np.concatenate([z16, sr, z16, sc, pad(0.0)], axis=1)
    s_minus = np.concatenate([-sr, z16, -sc, z16, pad(0.0)], axis=1)
    ident = lambda v: np.full((TM, LANES), v, f32)
    return (jnp.asarray(np.concatenate([cos, ident(1.0)], 0)),
            jnp.asarray(np.concatenate([s_plus, ident(0.0)], 0)),
            jnp.asarray(np.concatenate([s_minus, ident(0.0)], 0)))


def _block_plan(ridx, rank, hist):
    counts = jnp.sum(hist, axis=0)
    nblk = (counts + BM - 1) // BM
    bends = jnp.cumsum(nblk)
    bstarts = bends - nblk
    base = bstarts[None, :] * BM + jnp.cumsum(hist, axis=0) - hist
    hit = ridx.reshape(T // TP, TP, TOP_K, 1) == jnp.arange(N_EXP, dtype=I32)
    first = jnp.sum(jnp.where(hit, base[:, None, None, :], 0), axis=-1)
    dest = (first + rank.reshape(T // TP, TP, TOP_K)).reshape(-1)
    eid = jnp.arange(N_EXP, dtype=I32)
    blk = jnp.arange(N_BLOCKS, dtype=I32)
    blk_expert = jnp.minimum(jnp.sum((blk[:, None] >= bends[None, :]).astype(I32), axis=1), N_EXP - 1)
    of_blk = blk_expert[:, None] == eid[None, :]

    def per_block(v):
        return jnp.sum(jnp.where(of_blk, v[None, :], 0), axis=1)

    left = per_block(counts) - (blk - per_block(bstarts)) * BM
    blk_valid = jnp.where(blk < bends[-1], jnp.clip(left, 0, BM), 0).astype(I32)
    unused = bends[-1] + jnp.arange(N_EXP, dtype=I32)
    zero_blk = jnp.concatenate([jnp.maximum(bends - 1, 0), jnp.minimum(unused, N_BLOCKS - 1)])
    zero_valid = jnp.concatenate([counts % BM != 0, unused < N_BLOCKS])
    prev_expert = jnp.concatenate([jnp.full((1,), -1, I32), blk_expert[:-1]])
    switch = (blk < bends[-1]) & (blk_expert != prev_expert)
    wslot = (jnp.cumsum(switch.astype(I32)) - 1) % 2
    later = (eid[None, :] > eid[:, None]) & (nblk[None, :] > 0)
    next_of = jnp.min(jnp.where(later, eid[None, :], N_EXP), axis=1)
    next_expert = per_block(jnp.where(next_of < N_EXP, next_of, -1))
    return (dest.astype(I32), blk_expert.astype(I32), blk_valid, zero_blk.astype(I32),
            zero_valid.astype(I32), switch.astype(I32), wslot.astype(I32), next_expert.astype(I32))


def kernel(x_prompt, x_sample, cache_ckv, cache_krope, c, c_ctx, w_ada, b_ada, w_in, q_norm_g, w_uq,
           kv_norm_g, w_ukv, gmlp_ln_g, gmlp_ln_b, w_spatial, b_spatial, w_out, ln1_g, ln1_b, w_group,
           b_group, w_router, b_router, w1, w3, w2, ln2_g, ln2_b):
    l = 0
    xc = x_prompt.reshape(T_CTX, D)
    xl = x_sample.reshape(T_LAT, D)

    cond16 = jnp.concatenate([c, c_ctx[None, :], jnp.zeros((16 - N_LAT_B - 1, D), F32)], axis=0)
    mods = _adaln(cond16, w_ada[l], b_ada[l][None, :]).reshape(16, 6, D)

    wi = w_in[l]
    o_kr = QL + KVL
    o_u = o_kr + QK_ROPE
    win_a = wi[:, :o_kr].astype(BF16)
    win_g = wi[:, o_u:].astype(BF16)
    win_r = jnp.pad(wi[:, o_kr:o_u], ((0, 0), (0, LANES - QK_ROPE))).astype(BF16)
    wuq = jnp.pad(w_uq[l].reshape(QL, H, QK_NOPE + QK_ROPE),
                  ((0, 0), (0, 0), (0, HEAD_PAD - QK_NOPE - QK_ROPE))).reshape(QL, H * HEAD_PAD).astype(BF16)
    wkv = w_ukv[l].reshape(KVL, H, QK_NOPE + V_DIM)
    wuk = wkv[:, :, :QK_NOPE].reshape(KVL, H * QK_NOPE).astype(BF16)
    wuvt = wkv[:, :, QK_NOPE:].reshape(KVL, H * V_DIM).T.astype(BF16)
    ws = w_spatial[l].astype(BF16)
    bsb = jnp.broadcast_to(b_spatial[l][:, :, None], (G_GROUPS, CHUNK, G_DIM))
    wrt = jnp.concatenate([w_group[l], w_router[l], jnp.zeros((D, LANES - N_GRP - N_EXP), F32)],
                          axis=1).astype(BF16)
    brt = jnp.concatenate([b_group[l], b_router[l], jnp.zeros((LANES - N_GRP - N_EXP,), F32)])[None, :]
    cos_t, sp_t, sm_t = _rope_tables()
    row = lambda v: v[l][None, :]

    q, k, vt, gm, ckv_n, kr = _proj(xc, xl, mods, win_a, win_g, win_r, row(q_norm_g), wuq, row(kv_norm_g), wuk, wuvt,
                                   row(gmlp_ln_g), row(gmlp_ln_b), ws, bsb, cos_t, sp_t, sm_t)
    kc, vct = _kvexp(cache_ckv[:, l].reshape(N_LAT_B * PAST, KVL),
                     cache_krope[:, l].reshape(N_LAT_B * PAST, QK_ROPE), wuk, wuvt)
    attn_c = _attn_ctx(q, k, vt)
    attn_l = _attn_lat(q, k, vt, kc, vct)

    x1, h2p, ridx, rwt = _post(attn_c, attn_l, gm, xc, xl, mods, w_out[l].astype(BF16), row(ln1_g), row(ln1_b),
                               wrt, brt)
    rank, hist = _rank(ridx)
    dest, blk_expert, blk_valid, zero_blk, zero_valid, switch, wslot, next_expert = _block_plan(
        ridx, rank, hist[:, 0, :N_EXP].astype(I32))
    xs = _dispatch(zero_blk, zero_valid, dest, h2p)
    ys = _experts(blk_expert, blk_valid, switch, wslot, next_expert, xs, w1[l], w3[l], w2[l])
    y_c, y_l = _final(dest.reshape(T // TM, 1, TOP_K * TM), ys, x1, rwt, mods, row(ln2_g), row(ln2_b))

    return (y_c.reshape(N_CTX_B, N_CTX, D), y_l.reshape(N_LAT_B, N_LAT, D),
            ckv_n.reshape(N_CTX_B, 1, N_CTX, KVL), kr.reshape(N_CTX_B, 1, N_CTX, QK_ROPE))
```

```python
import functools

import jax
import jax.numpy as jnp
import numpy as np
from jax import lax
from jax.experimental import pallas as pl
from jax.experimental.pallas import tpu as pltpu

F32 = jnp.float32
BF16 = jnp.bfloat16
U32 = jnp.uint32
I32 = jnp.int32

D = 2048
N_CTX_B, N_CTX = 32, 256
N_LAT_B, N_LAT = 4, 4096
PAST = 512
GRID_W = 64
H = 8
QK_NOPE, QK_ROPE, V_DIM = 128, 64, 128
QL, KVL = 512, 512
GW = 1024
G_GROUPS, G_DIM, CHUNK = 8, 128, 128
N_GRP, E_PER_GRP, N_EXP, TOP_K = 4, 8, 32, 2
E_HID = 512
ROPE_THETA = 10000.0
EPS = 1e-6
ALPHA = 2.0 ** 0.25
SM_SCALE = (QK_NOPE + QK_ROPE) ** -0.5
Q_SCALE = SM_SCALE * float(np.log2(np.e))

T_CTX = N_CTX_B * N_CTX
T_LAT = N_LAT_B * N_LAT
T = T_CTX + T_LAT
N_ASSIGN = T * TOP_K

LANES = 128
HEAD_PAD = 256
V_AUG = V_DIM + 16
TM = 256
TP = 512
SUB = 256
TD = 1024
TQ = 1024
TK = 512
BM = 256
N_BLOCKS = N_ASSIGN // BM + N_EXP
N_ZERO = 2 * N_EXP
VMEM_LIMIT = 56 * 1024 * 1024

NP_TILES = T_CTX // TM
LAT_TILES_PER_B = N_LAT // TM


def _cp(sem):
    return pltpu.CompilerParams(dimension_semantics=sem, vmem_limit_bytes=VMEM_LIMIT)


def _const_spec(shape):
    nd = len(shape)
    return pl.BlockSpec(shape, lambda *a: (0,) * nd, pipeline_mode=pl.Buffered(1))


def _dot(a, b):
    return jnp.dot(a, b, preferred_element_type=F32)


def _dot_nt(a, b):
    return lax.dot_general(a, b, (((1,), (1,)), ((), ())), preferred_element_type=F32)


def _layer_norm(x, g, b):
    mu = jnp.mean(x, -1, keepdims=True)
    xc = x - mu
    var = jnp.mean(xc * xc, -1, keepdims=True)
    return xc * lax.rsqrt(var + EPS) * g + b


def _rms_norm(x, g):
    return x * lax.rsqrt(jnp.mean(x * x, -1, keepdims=True) + EPS) * g


def _gelu(x):
    return 0.5 * x * (1.0 + lax.erf(x * np.float32(np.sqrt(0.5))))


def _pack_pair(lo, hi):
    lo_b = lax.bitcast_convert_type(lo.astype(BF16).astype(F32), U32)
    hi_b = lax.bitcast_convert_type(hi.astype(BF16).astype(F32), U32)
    return hi_b | (lo_b >> 16)


def _unpack_pair(w):
    lo = lax.bitcast_convert_type(w << 16, F32)
    hi = lax.bitcast_convert_type(w & jnp.uint32(0xFFFF0000), F32)
    return lo, hi


RT = 8


def _store_rows_as_tiles(ref, row0, val):
    n = val.shape[0]
    for j in range(RT):
        ref[pl.ds(row0 * RT + j, n, stride=RT), :] = val[:, j * LANES:(j + 1) * LANES]


def _load_tiles_as_rows(ref, row0, n):
    return jnp.concatenate([ref[pl.ds(row0 * RT + j, n, stride=RT), :] for j in range(RT)], axis=1)


def _adaln_kernel(cond_ref, w_ref, b_ref, o_ref):
    s = jax.nn.silu(cond_ref[...])
    s_hi = s.astype(BF16)
    s_lo = (s - s_hi.astype(F32)).astype(BF16)
    w = w_ref[...]
    w_hi = w.astype(BF16)
    w_lo = (w - w_hi.astype(F32)).astype(BF16)
    o_ref[...] = _dot(s_hi, w_hi) + _dot(s_lo, w_hi) + _dot(s_hi, w_lo) + b_ref[...]


def _adaln(cond16, w_ada, b_ada):
    tn = 1024
    n = w_ada.shape[1]
    return pl.pallas_call(
        _adaln_kernel,
        grid=(n // tn,),
        in_specs=[pl.BlockSpec((16, D), lambda j: (0, 0)),
                  pl.BlockSpec((D, tn), lambda j: (0, j)),
                  pl.BlockSpec((1, tn), lambda j: (0, j))],
        out_specs=pl.BlockSpec((16, tn), lambda j: (0, j)),
        out_shape=jax.ShapeDtypeStruct((16, n), F32),
        compiler_params=_cp(("arbitrary",)),
        name="adaln",
    )(cond16, w_ada, b_ada)


def _expand_kv(ckv_n, kr128, wuk_ref, wuvt_ref, k_ref, vt_ref):
    cb = ckv_n.astype(BF16)
    kn = _dot(cb, wuk_ref[...])
    vt = _dot_nt(wuvt_ref[...], cb).astype(BF16)
    ones = jnp.ones((V_AUG - V_DIM, vt.shape[1]), BF16)
    for h in range(H):
        vt_ref[h * V_AUG:h * V_AUG + V_DIM, :] = vt[h * V_DIM:(h + 1) * V_DIM]
        vt_ref[h * V_AUG + V_DIM:(h + 1) * V_AUG, :] = ones
    krb = kr128.astype(BF16)
    for h in range(H):
        k_ref[:, h * HEAD_PAD:h * HEAD_PAD + QK_NOPE] = kn[:, h * QK_NOPE:(h + 1) * QK_NOPE].astype(BF16)
        k_ref[:, h * HEAD_PAD + QK_NOPE:(h + 1) * HEAD_PAD] = krb


def _rope128(x, cos, s_plus, s_minus):
    return (x * cos + pltpu.roll(x, 16, 1) * s_plus + pltpu.roll(x, LANES - 16, 1) * s_minus)


def _proj_kernel(xc_ref, xl_ref, mods_ref, win_a_ref, win_g_ref, win_r_ref, qg_ref, wuq_ref, kvg_ref,
                 wuk_ref, wuv_ref,
                 lng_ref, lnb_ref, ws_ref, bsb_ref, cos_ref, sp_ref, sm_ref,
                 q_ref, k_ref, v_ref, gm_ref, ckv_ref, kr_ref):
    i = pl.program_id(0)
    x = jnp.where(i < NP_TILES, xc_ref[...], xl_ref[...])
    m = mods_ref[0]
    shift1, scale1 = m[0:1], m[1:2]
    hmod = (x * (1.0 + scale1) + shift1).astype(BF16)
    pa = _dot(hmod, win_a_ref[...])
    pg = _dot(hmod, win_g_ref[...])
    kr = _dot(hmod, win_r_ref[...])
    cq, ckv = pa[:, :QL], pa[:, QL:]
    u, v = pg[:, :GW], pg[:, GW:]

    cos, s_plus, s_minus = cos_ref[...], sp_ref[...], sm_ref[...]

    cq_n = _rms_norm(cq, qg_ref[...]).astype(BF16)
    q = _dot(cq_n, wuq_ref[...]) * Q_SCALE
    for h in range(H):
        q_ref[:, h * HEAD_PAD:h * HEAD_PAD + QK_NOPE] = q[:, h * HEAD_PAD:h * HEAD_PAD + QK_NOPE].astype(BF16)
        q_ref[:, h * HEAD_PAD + QK_NOPE:(h + 1) * HEAD_PAD] = _rope128(
            q[:, h * HEAD_PAD + QK_NOPE:(h + 1) * HEAD_PAD], cos, s_plus, s_minus).astype(BF16)

    ckv_n = _rms_norm(ckv, kvg_ref[...])

    @pl.when(i < NP_TILES)
    def _():
        ckv_ref[...] = ckv_n
        kr_ref[...] = kr[:, :QK_ROPE]

    _expand_kv(ckv_n, _rope128(kr, cos, s_plus, s_minus), wuk_ref, wuv_ref, k_ref, v_ref)

    gu = _gelu(u)
    vn = _layer_norm(_gelu(v), lng_ref[...], lnb_ref[...]).astype(BF16)
    for c in range(TM // CHUNK):
        rows = slice(c * CHUNK, (c + 1) * CHUNK)
        for g in range(G_GROUPS):
            cols = slice(g * G_DIM, (g + 1) * G_DIM)
            s = _dot(ws_ref[g], vn[rows, cols]) + bsb_ref[g]
            gm_ref[rows, cols] = (gu[rows, cols] * s).astype(BF16)


def _mods_index(i, tile=TM):
    return jnp.where(i < T_CTX // tile, N_LAT_B, (i - T_CTX // tile) // (N_LAT // tile))


def _proj(xc, xl, mods, win_a, win_g, win_r, qg, wuq, kvg, wuk, wuv, lng, lnb, ws, bsb, cos_t, sp_t, sm_t):
    def tok(width):
        return pl.BlockSpec((TM, width), lambda i: (i, 0))

    def pos_block(i):
        return jnp.where(i < NP_TILES, LAT_TILES_PER_B, (i - NP_TILES) % LAT_TILES_PER_B)

    rope_spec = pl.BlockSpec((TM, LANES), lambda i: (pos_block(i), 0))
    return pl.pallas_call(
        _proj_kernel,
        grid=(T // TM,),
        in_specs=[
            pl.BlockSpec((TM, D), lambda i: (jnp.minimum(i, NP_TILES - 1), 0)),
            pl.BlockSpec((TM, D), lambda i: (jnp.maximum(i - NP_TILES, 0), 0)),
            pl.BlockSpec((1, 6, D), lambda i: (_mods_index(i), 0, 0)),
            _const_spec((D, QL + KVL)), _const_spec((D, 2 * GW)), _const_spec((D, LANES)),
            _const_spec((1, QL)), _const_spec((QL, H * HEAD_PAD)),
            _const_spec((1, KVL)), _const_spec((KVL, H * QK_NOPE)), _const_spec((H * V_DIM, KVL)),
            _const_spec((1, GW)), _const_spec((1, GW)),
            _const_spec((G_GROUPS, CHUNK, CHUNK)), _const_spec((G_GROUPS, CHUNK, G_DIM)),
            rope_spec, rope_spec, rope_spec,
        ],
        out_specs=[tok(H * HEAD_PAD), tok(H * HEAD_PAD),
                   pl.BlockSpec((H * V_AUG, TM), lambda i: (0, i)), tok(GW),
                   pl.BlockSpec((TM, KVL), lambda i: (jnp.minimum(i, NP_TILES - 1), 0)),
                   pl.BlockSpec((TM, QK_ROPE), lambda i: (jnp.minimum(i, NP_TILES - 1), 0))],
        out_shape=[
            jax.ShapeDtypeStruct((T, H * HEAD_PAD), BF16),
            jax.ShapeDtypeStruct((T, H * HEAD_PAD), BF16),
            jax.ShapeDtypeStruct((H * V_AUG, T), BF16),
            jax.ShapeDtypeStruct((T, GW), BF16),
            jax.ShapeDtypeStruct((T_CTX, KVL), F32),
            jax.ShapeDtypeStruct((T_CTX, QK_ROPE), F32),
        ],
        compiler_params=_cp(("arbitrary",)),
        name="proj",
    )(xc, xl, mods, win_a, win_g, win_r, qg, wuq, kvg, wuk, wuv, lng, lnb, ws, bsb, cos_t, sp_t, sm_t)


def _kvexp_kernel(ckv_ref, kr_ref, wuk_ref, wuv_ref, k_ref, v_ref):
    kr = kr_ref[...]
    kr128 = jnp.concatenate([kr, jnp.zeros_like(kr)], axis=1)
    _expand_kv(ckv_ref[...], kr128, wuk_ref, wuv_ref, k_ref, v_ref)


def _kvexp(ckv, kr, wuk, wuv):
    rows = ckv.shape[0]
    return pl.pallas_call(
        _kvexp_kernel,
        grid=(rows // TM,),
        in_specs=[pl.BlockSpec((TM, KVL), lambda i: (i, 0)), pl.BlockSpec((TM, QK_ROPE), lambda i: (i, 0)),
                  _const_spec((KVL, H * QK_NOPE)), _const_spec((H * V_DIM, KVL))],
        out_specs=[pl.BlockSpec((TM, H * HEAD_PAD), lambda i: (i, 0)),
                   pl.BlockSpec((H * V_AUG, TM), lambda i: (0, i))],
        out_shape=[jax.ShapeDtypeStruct((rows, H * HEAD_PAD), BF16),
                   jax.ShapeDtypeStruct((H * V_AUG, rows), BF16)],
        compiler_params=_cp(("arbitrary",)),
        name="kvexp",
    )(ckv, kr, wuk, wuv)


def _probs(s, m):
    m_new = jnp.maximum(m, jnp.max(s, 0, keepdims=True))
    return m_new, jnp.exp2(m - m_new), jnp.exp2(s - m_new).astype(BF16)


def _attn_finish(acc):
    return (acc[:V_DIM] / acc[V_DIM:V_DIM + 1]).T.astype(BF16)


def _attn_ctx_kernel(q_ref, k_ref, vt_ref, o_ref):
    m0 = jnp.full((1, N_CTX), -jnp.inf, F32)
    for h in range(H):
        qh = q_ref[:, h * HEAD_PAD:(h + 1) * HEAD_PAD]
        kh = k_ref[:, h * HEAD_PAD:(h + 1) * HEAD_PAD]
        _, _, p = _probs(_dot_nt(kh, qh), m0)
        acc = _dot(vt_ref[h * V_AUG:(h + 1) * V_AUG, :], p)
        o_ref[:, h * V_DIM:(h + 1) * V_DIM] = _attn_finish(acc)


def _attn_ctx(q, k, vt):
    return pl.pallas_call(
        _attn_ctx_kernel,
        grid=(N_CTX_B,),
        in_specs=[pl.BlockSpec((N_CTX, H * HEAD_PAD), lambda b: (b, 0)),
                  pl.BlockSpec((N_CTX, H * HEAD_PAD), lambda b: (b, 0)),
                  pl.BlockSpec((H * V_AUG, N_CTX), lambda b: (0, b))],
        out_specs=pl.BlockSpec((N_CTX, H * V_DIM), lambda b: (b, 0)),
        out_shape=jax.ShapeDtypeStruct((T_CTX, H * V_DIM), BF16),
        compiler_params=_cp(("arbitrary",)),
        name="attn_ctx",
    )(q, k, vt)


def _attn_lat_kernel(q_ref, k_ref, vt_ref, kc_ref, vct_ref, o_ref):
    q = q_ref[...]
    n_lat = N_LAT // TK

    def keys(c):
        return k_ref[c * TK:(c + 1) * TK, :] if c < n_lat else kc_ref[...]

    def values_t(c):
        return vt_ref[:, c * TK:(c + 1) * TK] if c < n_lat else vct_ref[...]

    m = jnp.full((1, TQ), -jnp.inf, F32)
    acc = jnp.zeros((V_AUG, TQ), F32)
    s_next = _dot_nt(keys(0), q)
    m_next = jnp.maximum(m, jnp.max(s_next, 0, keepdims=True))
    p_prev = alpha_prev = None
    for c in range(n_lat + 1):
        s, m_new = s_next, m_next
        if c < n_lat:
            s_next = _dot_nt(keys(c + 1), q)
            m_next = jnp.maximum(m_new, jnp.max(s_next, 0, keepdims=True))
        if c > 0:
            acc = alpha_prev * acc + _dot(values_t(c - 1), p_prev)
        alpha_prev = jnp.exp2(m - m_new)
        p_prev = jnp.exp2(s - m_new).astype(BF16)
        m = m_new
    acc = alpha_prev * acc + _dot(values_t(n_lat), p_prev)
    o_ref[...] = _attn_finish(acc)


def _attn_lat(q, k, vt, kc, vct):
    qb0 = T_CTX // TQ
    qpb = N_LAT // TQ
    kb0 = T_CTX // N_LAT
    return pl.pallas_call(
        _attn_lat_kernel,
        grid=(N_LAT_B, H, qpb),
        in_specs=[pl.BlockSpec((TQ, HEAD_PAD), lambda b, h, qi: (qb0 + b * qpb + qi, h)),
                  pl.BlockSpec((N_LAT, HEAD_PAD), lambda b, h, qi: (kb0 + b, h)),
                  pl.BlockSpec((V_AUG, N_LAT), lambda b, h, qi: (h, kb0 + b)),
                  pl.BlockSpec((PAST, HEAD_PAD), lambda b, h, qi: (b, h)),
                  pl.BlockSpec((V_AUG, PAST), lambda b, h, qi: (h, b))],
        out_specs=pl.BlockSpec((TQ, V_DIM), lambda b, h, qi: (b * qpb + qi, h)),
        out_shape=jax.ShapeDtypeStruct((T_LAT, H * V_DIM), BF16),
        compiler_params=_cp(("arbitrary", "arbitrary", "arbitrary")),
        name="attn_lat",
    )(q, k, vt, kc, vct)


def _route(logits):
    lane_i = lax.broadcasted_iota(I32, logits.shape, 1)
    lane = lane_i.astype(F32)
    neg = jnp.float32(-jnp.inf)
    far = jnp.float32(LANES)
    gl = jnp.where(lane_i < N_GRP, logits, neg)
    gmax = jnp.max(gl, -1, keepdims=True)
    gidx = jnp.min(jnp.where(gl == gmax, lane, far), -1, keepdims=True)
    p_top = 1.0 / jnp.sum(jnp.exp(gl - gmax), -1, keepdims=True)
    grp_of_lane = ((lane_i - N_GRP) >> 3).astype(F32)
    in_grp = (lane_i >= N_GRP) & (lane_i < N_GRP + N_EXP) & (grp_of_lane == gidx)
    el = jnp.where(in_grp, logits, neg)
    m1 = jnp.max(el, -1, keepdims=True)
    i1 = jnp.min(jnp.where(el == m1, lane, far), -1, keepdims=True)
    el2 = jnp.where(lane == i1, neg, el)
    m2 = jnp.max(el2, -1, keepdims=True)
    i2 = jnp.min(jnp.where(el2 == m2, lane, far), -1, keepdims=True)
    e2 = jnp.exp(m2 - m1)
    w1 = p_top / (1.0 + e2)
    w2 = p_top * e2 / (1.0 + e2)
    return i1 - N_GRP, i2 - N_GRP, w1, w2


def _pair(a, b, dtype):
    two = lax.broadcasted_iota(I32, (a.shape[0], TOP_K), 1)
    return jnp.where(two == 0, a, b).astype(dtype)


def _post_kernel(ac_ref, al_ref, gm_ref, xc_ref, xl_ref, mods_ref, wout_ref, g1_ref, b1_ref, wrt_ref,
                 brt_ref, x1_ref, h2p_ref, ridx_ref, rwt_ref):
    i = pl.program_id(0)
    is_ctx = i < T_CTX // TP
    m = mods_ref[0]
    gate1, shift2, scale2 = m[2:3], m[3:4], m[4:5]

    def mix_of(s):
        rows = slice(s * SUB, (s + 1) * SUB)
        attn = jnp.where(is_ctx, ac_ref[rows, :], al_ref[rows, :])
        return _dot(jnp.concatenate([attn, gm_ref[rows, :]], axis=1), wout_ref[...])

    n_sub = TP // SUB
    mix_next = mix_of(0)
    for s in range(n_sub):
        rows = slice(s * SUB, (s + 1) * SUB)
        mix = mix_next
        if s + 1 < n_sub:
            mix_next = mix_of(s + 1)
        x = jnp.where(is_ctx, xc_ref[rows, :], xl_ref[rows, :])
        x1 = _layer_norm(ALPHA * x + gate1 * mix, g1_ref[...], b1_ref[...])
        x1_ref[rows, :] = x1
        h2 = x1 * (1.0 + scale2) + shift2
        _store_rows_as_tiles(h2p_ref, s * SUB, _pack_pair(h2[:, :D // 2], h2[:, D // 2:]))
        e1, e2, w1, w2 = _route(_dot(h2.astype(BF16), wrt_ref[...]) + brt_ref[...])
        ridx_ref[rows, :] = _pair(e1, e2, I32)
        rwt_ref[rows, :] = _pair(w1, w2, F32)


def _post(attn_c, attn_l, gm, xc, xl, mods, wout, g1, b1, wrt, brt):
    npt = T_CTX // TP

    def tok(width):
        return pl.BlockSpec((TP, width), lambda i: (i, 0))

    def ctx(width):
        return pl.BlockSpec((TP, width), lambda i: (jnp.minimum(i, npt - 1), 0))

    def lat(width):
        return pl.BlockSpec((TP, width), lambda i: (jnp.maximum(i - npt, 0), 0))

    return pl.pallas_call(
        _post_kernel,
        grid=(T // TP,),
        in_specs=[ctx(H * V_DIM), lat(H * V_DIM), tok(GW), ctx(D), lat(D),
                  pl.BlockSpec((1, 6, D), lambda i: (_mods_index(i, TP), 0, 0)),
                  _const_spec((D, D)), _const_spec((1, D)), _const_spec((1, D)),
                  _const_spec((D, LANES)), _const_spec((1, LANES))],
        out_specs=[tok(D), pl.BlockSpec((TP * RT, LANES), lambda i: (i, 0)), tok(TOP_K), tok(TOP_K)],
        out_shape=[jax.ShapeDtypeStruct((T, D), F32),
                   jax.ShapeDtypeStruct((T * RT, LANES), U32),
                   jax.ShapeDtypeStruct((T, TOP_K), I32),
                   jax.ShapeDtypeStruct((T, TOP_K), F32)],
        compiler_params=_cp(("arbitrary",)),
        name="post",
    )(attn_c, attn_l, gm, xc, xl, mods, wout, g1, b1, wrt, brt)


def _rank_kernel(ridx_ref, tri_ref, rank_ref, hist_ref):
    lane = lax.broadcasted_iota(I32, (SUB, LANES), 1).astype(F32)
    seen = jnp.zeros((1, LANES), F32)
    for s in range(TP // SUB):
        rows = slice(s * SUB, (s + 1) * SUB)
        e = ridx_ref[rows, :].astype(F32)
        oh1 = (lane == e[:, 0:1]).astype(F32)
        oh2 = (lane == e[:, 1:2]).astype(F32)
        both = oh1 + oh2
        before = _dot(tri_ref[...], both.astype(BF16)) + seen
        rank_ref[rows, :] = _pair(jnp.sum(before * oh1, -1, keepdims=True),
                                  jnp.sum(before * oh2, -1, keepdims=True), I32)
        seen = seen + jnp.sum(both, 0, keepdims=True)
    hist_ref[0] = jnp.broadcast_to(seen, (8, LANES))


def _rank(ridx):
    return pl.pallas_call(
        _rank_kernel,
        grid=(T // TP,),
        in_specs=[pl.BlockSpec((TP, TOP_K), lambda i: (i, 0)), _const_spec((SUB, SUB))],
        out_specs=[pl.BlockSpec((TP, TOP_K), lambda i: (i, 0)),
                   pl.BlockSpec((1, 8, LANES), lambda i: (i, 0, 0))],
        out_shape=[jax.ShapeDtypeStruct((T, TOP_K), I32),
                   jax.ShapeDtypeStruct((T // TP, 8, LANES), F32)],
        compiler_params=_cp(("arbitrary",)),
        name="rank",
    )(ridx, jnp.asarray(np.tril(np.ones((SUB, SUB), np.float32), -1), BF16))


def _row_copy(src, src_row, dst, dst_row, sem):
    def tile(ref, row):
        return ref.at[pl.ds(pl.multiple_of(row * RT, RT), RT)]

    return pltpu.make_async_copy(tile(src, src_row), tile(dst, dst_row), sem)


def _dispatch_kernel(zb_ref, zv_ref, dest_ref, h2p_ref, xs_ref, zbuf, sem):
    i = pl.program_id(0)

    @pl.when(i == 0)
    def _():
        zbuf[...] = jnp.zeros_like(zbuf)

        def zero_block(j):
            start = pl.multiple_of(zb_ref[j] * (BM * RT), BM * RT)
            return pltpu.make_async_copy(zbuf, xs_ref.at[pl.ds(start, BM * RT)], sem)

        def start(j, carry):
            @pl.when(zv_ref[j] != 0)
            def _():
                zero_block(j).start()
            return carry

        def wait(j, carry):
            @pl.when(zv_ref[j] != 0)
            def _():
                zero_block(j).wait()
            return carry

        lax.fori_loop(0, N_ZERO, start, 0)
        lax.fori_loop(0, N_ZERO, wait, 0)

    def issue(r, carry):
        for k in range(TOP_K):
            _row_copy(h2p_ref, r, xs_ref, dest_ref[0, 0, TOP_K * r + k], sem).start()
        return carry

    lax.fori_loop(0, TD, issue, 0, unroll=8)
    for _ in range(TOP_K):
        pltpu.make_async_copy(h2p_ref, xs_ref.at[pl.ds(0, TD * RT)], sem).wait()


def _dispatch(zero_blk, zero_valid, dest, h2p):
    grid_spec = pltpu.PrefetchScalarGridSpec(
        num_scalar_prefetch=2,
        grid=(T // TD,),
        in_specs=[pl.BlockSpec((1, 1, TOP_K * TD), lambda i, zb, zv: (i, 0, 0), memory_space=pltpu.SMEM),
                  pl.BlockSpec((TD * RT, LANES), lambda i, zb, zv: (i, 0))],
        out_specs=pl.BlockSpec(memory_space=pl.ANY),
        scratch_shapes=[pltpu.VMEM((BM * RT, LANES), U32), pltpu.SemaphoreType.DMA(())],
    )
    return pl.pallas_call(
        _dispatch_kernel,
        grid_spec=grid_spec,
        out_shape=jax.ShapeDtypeStruct((N_BLOCKS * BM * RT, LANES), U32),
        compiler_params=_cp(("arbitrary",)),
        name="dispatch",
    )(zero_blk, zero_valid, dest.reshape(T // TD, 1, TOP_K * TD), h2p)


def _experts_kernel(be_ref, bv_ref, sw_ref, es_ref, nx_ref, x_ref, w1_hbm, w3_hbm, w2_hbm, y_ref,
                    wf1, wf3, wf2, w1b, w3b, w2b, sems):
    i = pl.program_id(0)
    e = be_ref[i]
    nv = bv_ref[i]

    def weight_copies(expert, s):
        return (pltpu.make_async_copy(w1_hbm.at[expert], wf1.at[s], sems.at[s]),
                pltpu.make_async_copy(w3_hbm.at[expert], wf3.at[s], sems.at[s]),
                pltpu.make_async_copy(w2_hbm.at[expert], wf2.at[s], sems.at[s]))

    @pl.when(sw_ref[i] != 0)
    def _():
        s = es_ref[i]

        @pl.when(i == 0)
        def _():
            for c in weight_copies(e, s):
                c.start()

        for c in weight_copies(e, s):
            c.wait()
        nxt = nx_ref[i]

        @pl.when(nxt >= 0)
        def _():
            for c in weight_copies(nxt, 1 - s):
                c.start()

        w1b[...] = wf1[s].astype(BF16)
        w3b[...] = wf3[s].astype(BF16)
        w2b[...] = wf2[s].astype(BF16)

    @pl.when(nv > 0)
    def _():
        lo, hi = _unpack_pair(_load_tiles_as_rows(x_ref, 0, BM))
        xb = jnp.concatenate([lo.astype(BF16), hi.astype(BF16)], axis=1)
        a = _dot(xb, w1b[...])
        b = _dot(xb, w3b[...])
        hid = (jax.nn.silu(a) * b).astype(BF16)
        y = _dot(hid, w2b[...])
        _store_rows_as_tiles(y_ref, 0, _pack_pair(y[:, :D // 2], y[:, D // 2:]))

    @pl.when(nv == 0)
    def _():
        y_ref[...] = jnp.zeros_like(y_ref)


def _experts(blk_expert, blk_valid, switch, wslot, next_expert, xs, w1, w3, w2):
    hbm = pl.BlockSpec(memory_space=pl.ANY)
    grid_spec = pltpu.PrefetchScalarGridSpec(
        num_scalar_prefetch=5,
        grid=(N_BLOCKS,),
        in_specs=[pl.BlockSpec((BM * RT, LANES), lambda i, *_: (i, 0)), hbm, hbm, hbm],
        out_specs=pl.BlockSpec((BM * RT, LANES), lambda i, *_: (i, 0)),
        scratch_shapes=[pltpu.VMEM((2, D, E_HID), F32), pltpu.VMEM((2, D, E_HID), F32),
                        pltpu.VMEM((2, E_HID, D), F32),
                        pltpu.VMEM((D, E_HID), BF16), pltpu.VMEM((D, E_HID), BF16),
                        pltpu.VMEM((E_HID, D), BF16), pltpu.SemaphoreType.DMA((2,))],
    )
    return pl.pallas_call(
        _experts_kernel,
        grid_spec=grid_spec,
        out_shape=jax.ShapeDtypeStruct((N_BLOCKS * BM * RT, LANES), U32),
        compiler_params=_cp(("arbitrary",)),
        name="experts",
    )(blk_expert, blk_valid, switch, wslot, next_expert, xs, w1, w3, w2)


def _final_kernel(dcur_ref, dnext_ref, ys_ref, x1_ref, rwt_ref, mods_ref, g2_ref, b2_ref, oc_ref, ol_ref,
                  buf, sems):
    i = pl.program_id(0)
    n = pl.num_programs(0)
    slot = i % 2

    def first_row(s, k):
        return (s * TOP_K + k) * TM

    def issue(d_ref, s):
        def body(r, carry):
            for k in range(TOP_K):
                _row_copy(ys_ref, d_ref[0, 0, TOP_K * r + k], buf, first_row(s, k) + r, sems.at[s]).start()
            return carry

        lax.fori_loop(0, TM, body, 0, unroll=8)

    @pl.when(i == 0)
    def _():
        issue(dcur_ref, 0)

    @pl.when(i + 1 < n)
    def _():
        issue(dnext_ref, 1 - slot)

    for k in range(TOP_K):
        pltpu.make_async_copy(ys_ref.at[pl.ds(0, TM * RT)], buf.at[pl.ds(0, TM * RT)], sems.at[slot]).wait()

    w = rwt_ref[...]
    lo0, hi0 = _unpack_pair(_load_tiles_as_rows(buf, first_row(slot, 0), TM))
    lo1, hi1 = _unpack_pair(_load_tiles_as_rows(buf, first_row(slot, 1), TM))
    w0, w1 = w[:, 0:1], w[:, 1:2]
    moe = jnp.concatenate([lo0 * w0 + lo1 * w1, hi0 * w0 + hi1 * w1], axis=1)
    gate2 = mods_ref[0][5:6]
    out = _layer_norm(ALPHA * x1_ref[...] + gate2 * moe, g2_ref[...], b2_ref[...])

    @pl.when(i < NP_TILES)
    def _():
        oc_ref[...] = out

    @pl.when(i >= NP_TILES)
    def _():
        ol_ref[...] = out


def _final(dest3, ys, x1, rwt, mods, g2, b2):
    n = T // TM
    return pl.pallas_call(
        _final_kernel,
        grid=(n,),
        in_specs=[pl.BlockSpec((1, 1, TOP_K * TM), lambda i: (i, 0, 0), memory_space=pltpu.SMEM),
                  pl.BlockSpec((1, 1, TOP_K * TM), lambda i: (jnp.minimum(i + 1, n - 1), 0, 0),
                               memory_space=pltpu.SMEM),
                  pl.BlockSpec(memory_space=pl.ANY),
                  pl.BlockSpec((TM, D), lambda i: (i, 0)),
                  pl.BlockSpec((TM, TOP_K), lambda i: (i, 0)),
                  pl.BlockSpec((1, 6, D), lambda i: (_mods_index(i), 0, 0)),
                  _const_spec((1, D)), _const_spec((1, D))],
        out_specs=[pl.BlockSpec((TM, D), lambda i: (jnp.minimum(i, NP_TILES - 1), 0)),
                   pl.BlockSpec((TM, D), lambda i: (jnp.maximum(i - NP_TILES, 0), 0))],
        out_shape=[jax.ShapeDtypeStruct((T_CTX, D), F32), jax.ShapeDtypeStruct((T_LAT, D), F32)],
        scratch_shapes=[pltpu.VMEM((2 * TOP_K * TM * RT, LANES), U32), pltpu.SemaphoreType.DMA((2,))],
        compiler_params=_cp(("arbitrary",)),
        name="final",
    )(dest3, dest3, ys, x1, rwt, mods, g2, b2)


def _rope_tables():
    n = N_LAT
    rows = n // GRID_W
    f32 = np.float32
    row = np.repeat(np.arange(rows, dtype=f32), GRID_W)
    col = np.tile(np.arange(GRID_W, dtype=f32), rows)
    half = QK_ROPE // 2
    inv = (f32(ROPE_THETA) ** (-np.arange(0, half, 2, dtype=f32) / f32(half))).astype(f32)
    ang_r, ang_c = (row[:, None] * inv).astype(f32), (col[:, None] * inv).astype(f32)
    cr, sr = np.cos(ang_r).astype(f32), np.sin(ang_r).astype(f32)
    cc, sc = np.cos(ang_c).astype(f32), np.sin(ang_c).astype(f32)
    z16 = np.zeros_like(cr)
    pad = lambda v: np.full((n, LANES - QK_ROPE), v, f32)
    cos = np.concatenate([cr, cr, cc, cc, pad(1.0)], axis=1)
    s_plus = np.concatenate([z16, sr, z16, sc, pad(0.0)], axis=1)
    s_minus = np.concatenate([-sr, z16, -sc, z16, pad(0.0)], axis=1)
    ident = lambda v: np.full((TM, LANES), v, f32)
    return (jnp.asarray(np.concatenate([cos, ident(1.0)], 0)),
            jnp.asarray(np.concatenate([s_plus, ident(0.0)], 0)),
            jnp.asarray(np.concatenate([s_minus, ident(0.0)], 0)))


def _block_plan(ridx, rank, hist):
    counts = jnp.sum(hist, axis=0)
    nblk = (counts + BM - 1) // BM
    bends = jnp.cumsum(nblk)
    bstarts = bends - nblk
    base = bstarts[None, :] * BM + jnp.cumsum(hist, axis=0) - hist
    hit = ridx.reshape(T // TP, TP, TOP_K, 1) == jnp.arange(N_EXP, dtype=I32)
    first = jnp.sum(jnp.where(hit, base[:, None, None, :], 0), axis=-1)
    dest = (first + rank.reshape(T // TP, TP, TOP_K)).reshape(-1)
    eid = jnp.arange(N_EXP, dtype=I32)
    blk = jnp.arange(N_BLOCKS, dtype=I32)
    blk_expert = jnp.minimum(jnp.sum((blk[:, None] >= bends[None, :]).astype(I32), axis=1), N_EXP - 1)
    of_blk = blk_expert[:, None] == eid[None, :]

    def per_block(v):
        return jnp.sum(jnp.where(of_blk, v[None, :], 0), axis=1)

    left = per_block(counts) - (blk - per_block(bstarts)) * BM
    blk_valid = jnp.where(blk < bends[-1], jnp.clip(left, 0, BM), 0).astype(I32)
    unused = bends[-1] + jnp.arange(N_EXP, dtype=I32)
    zero_blk = jnp.concatenate([jnp.maximum(bends - 1, 0), jnp.minimum(unused, N_BLOCKS - 1)])
    zero_valid = jnp.concatenate([counts % BM != 0, unused < N_BLOCKS])
    prev_expert = jnp.concatenate([jnp.full((1,), -1, I32), blk_expert[:-1]])
    switch = (blk < bends[-1]) & (blk_expert != prev_expert)
    wslot = (jnp.cumsum(switch.astype(I32)) - 1) % 2
    later = (eid[None, :] > eid[:, None]) & (nblk[None, :] > 0)
    next_of = jnp.min(jnp.where(later, eid[None, :], N_EXP), axis=1)
    next_expert = per_block(jnp.where(next_of < N_EXP, next_of, -1))
    return (dest.astype(I32), blk_expert.astype(I32), blk_valid, zero_blk.astype(I32),
            zero_valid.astype(I32), switch.astype(I32), wslot.astype(I32), next_expert.astype(I32))


def kernel(x_prompt, x_sample, cache_ckv, cache_krope, c, c_ctx, w_ada, b_ada, w_in, q_norm_g, w_uq,
           kv_norm_g, w_ukv, gmlp_ln_g, gmlp_ln_b, w_spatial, b_spatial, w_out, ln1_g, ln1_b, w_group,
           b_group, w_router, b_router, w1, w3, w2, ln2_g, ln2_b):
    l = 0
    xc = x_prompt.reshape(T_CTX, D)
    xl = x_sample.reshape(T_LAT, D)

    cond16 = jnp.concatenate([c, c_ctx[None, :], jnp.zeros((16 - N_LAT_B - 1, D), F32)], axis=0)
    mods = _adaln(cond16, w_ada[l], b_ada[l][None, :]).reshape(16, 6, D)

    wi = w_in[l]
    o_kr = QL + KVL
    o_u = o_kr + QK_ROPE
    win_a = wi[:, :o_kr].astype(BF16)
    win_g = wi[:, o_u:].astype(BF16)
    win_r = jnp.pad(wi[:, o_kr:o_u], ((0, 0), (0, LANES - QK_ROPE))).astype(BF16)
    wuq = jnp.pad(w_uq[l].reshape(QL, H, QK_NOPE + QK_ROPE),
                  ((0, 0), (0, 0), (0, HEAD_PAD - QK_NOPE - QK_ROPE))).reshape(QL, H * HEAD_PAD).astype(BF16)
    wkv = w_ukv[l].reshape(KVL, H, QK_NOPE + V_DIM)
    wuk = wkv[:, :, :QK_NOPE].reshape(KVL, H * QK_NOPE).astype(BF16)
    wuvt = wkv[:, :, QK_NOPE:].reshape(KVL, H * V_DIM).T.astype(BF16)
    ws = w_spatial[l].astype(BF16)
    bsb = jnp.broadcast_to(b_spatial[l][:, :, None], (G_GROUPS, CHUNK, G_DIM))
    wrt = jnp.concatenate([w_group[l], w_router[l], jnp.zeros((D, LANES - N_GRP - N_EXP), F32)],
                          axis=1).astype(BF16)
    brt = jnp.concatenate([b_group[l], b_router[l], jnp.zeros((LANES - N_GRP - N_EXP,), F32)])[None, :]
    cos_t, sp_t, sm_t = _rope_tables()
    row = lambda v: v[l][None, :]

    q, k, vt, gm, ckv_n, kr = _proj(xc, xl, mods, win_a, win_g, win_r, row(q_norm_g), wuq, row(kv_norm_g), wuk, wuvt,
                                   row(gmlp_ln_g), row(gmlp_ln_b), ws, bsb, cos_t, sp_t, sm_t)
    kc, vct = _kvexp(cache_ckv[:, l].reshape(N_LAT_B * PAST, KVL),
                     cache_krope[:, l].reshape(N_LAT_B * PAST, QK_ROPE), wuk, wuvt)
    attn_c = _attn_ctx(q, k, vt)
    attn_l = _attn_lat(q, k, vt, kc, vct)

    x1, h2p, ridx, rwt = _post(attn_c, attn_l, gm, xc, xl, mods, w_out[l].astype(BF16), row(ln1_g), row(ln1_b),
                               wrt, brt)
    rank, hist = _rank(ridx)
    dest, blk_expert, blk_valid, zero_blk, zero_valid, switch, wslot, next_expert = _block_plan(
        ridx, rank, hist[:, 0, :N_EXP].astype(I32))
    xs = _dispatch(zero_blk, zero_valid, dest, h2p)
    ys = _experts(blk_expert, blk_valid, switch, wslot, next_expert, xs, w1[l], w3[l], w2[l])
    y_c, y_l = _final(dest.reshape(T // TM, 1, TOP_K * TM), ys, x1, rwt, mods, row(ln2_g), row(ln2_b))

    return (y_c.reshape(N_CTX_B, N_CTX, D), y_l.reshape(N_LAT_B, N_LAT, D),
            ckv_n.reshape(N_CTX_B, 1, N_CTX, KVL), kr.reshape(N_CTX_B, 1, N_CTX, QK_ROPE))
```

```python
import functools

import jax
import jax.numpy as jnp
import numpy as np
from jax import lax
from jax.experimental import pallas as pl
from jax.experimental.pallas import tpu as pltpu

F32 = jnp.float32
BF16 = jnp.bfloat16
U32 = jnp.uint32
I32 = jnp.int32

D = 2048
N_CTX_B, N_CTX = 32, 256
N_LAT_B, N_LAT = 4, 4096
PAST = 512
GRID_W = 64
H = 8
QK_NOPE, QK_ROPE, V_DIM = 128, 64, 128
QL, KVL = 512, 512
GW = 1024
G_GROUPS, G_DIM, CHUNK = 8, 128, 128
N_GRP, E_PER_GRP, N_EXP, TOP_K = 4, 8, 32, 2
E_HID = 512
ROPE_THETA = 10000.0
EPS = 1e-6
ALPHA = 2.0 ** 0.25
SM_SCALE = (QK_NOPE + QK_ROPE) ** -0.5
Q_SCALE = SM_SCALE * float(np.log2(np.e))

T_CTX = N_CTX_B * N_CTX
T_LAT = N_LAT_B * N_LAT
T = T_CTX + T_LAT
N_ASSIGN = T * TOP_K

LANES = 128
HEAD_PAD = 256
V_AUG = V_DIM + 16
TM = 256
TP = 512
SUB = 256
TD = 1024
TQ = 1024
TK = 512
BM = 256
N_BLOCKS = N_ASSIGN // BM + N_EXP
N_ZERO = 2 * N_EXP
VMEM_LIMIT = 56 * 1024 * 1024

NP_TILES = T_CTX // TM
LAT_TILES_PER_B = N_LAT // TM


def _cp(sem):
    return pltpu.CompilerParams(dimension_semantics=sem, vmem_limit_bytes=VMEM_LIMIT)


def _const_spec(shape):
    nd = len(shape)
    return pl.BlockSpec(shape, lambda *a: (0,) * nd, pipeline_mode=pl.Buffered(1))


def _dot(a, b):
    return jnp.dot(a, b, preferred_element_type=F32)


def _dot_nt(a, b):
    return lax.dot_general(a, b, (((1,), (1,)), ((), ())), preferred_element_type=F32)


def _layer_norm(x, g, b):
    mu = jnp.mean(x, -1, keepdims=True)
    xc = x - mu
    var = jnp.mean(xc * xc, -1, keepdims=True)
    return xc * lax.rsqrt(var + EPS) * g + b


def _rms_norm(x, g):
    return x * lax.rsqrt(jnp.mean(x * x, -1, keepdims=True) + EPS) * g


def _gelu(x):
    return 0.5 * x * (1.0 + lax.erf(x * np.float32(np.sqrt(0.5))))


def _pack_pair(lo, hi):
    lo_b = lax.bitcast_convert_type(lo.astype(BF16).astype(F32), U32)
    hi_b = lax.bitcast_convert_type(hi.astype(BF16).astype(F32), U32)
    return hi_b | (lo_b >> 16)


def _unpack_pair(w):
    lo = lax.bitcast_convert_type(w << 16, F32)
    hi = lax.bitcast_convert_type(w & jnp.uint32(0xFFFF0000), F32)
    return lo, hi


RT = 8


def _store_rows_as_tiles(ref, row0, val):
    n = val.shape[0]
    for j in range(RT):
        ref[pl.ds(row0 * RT + j, n, stride=RT), :] = val[:, j * LANES:(j + 1) * LANES]


def _load_tiles_as_rows(ref, row0, n):
    return jnp.concatenate([ref[pl.ds(row0 * RT + j, n, stride=RT), :] for j in range(RT)], axis=1)


def _adaln_kernel(cond_ref, w_ref, b_ref, o_ref):
    s = jax.nn.silu(cond_ref[...])
    s_hi = s.astype(BF16)
    s_lo = (s - s_hi.astype(F32)).astype(BF16)
    w = w_ref[...]
    w_hi = w.astype(BF16)
    w_lo = (w - w_hi.astype(F32)).astype(BF16)
    o_ref[...] = _dot(s_hi, w_hi) + _dot(s_lo, w_hi) + _dot(s_hi, w_lo) + b_ref[...]


def _adaln(cond16, w_ada, b_ada):
    tn = 1024
    n = w_ada.shape[1]
    return pl.pallas_call(
        _adaln_kernel,
        grid=(n // tn,),
        in_specs=[pl.BlockSpec((16, D), lambda j: (0, 0)),
                  pl.BlockSpec((D, tn), lambda j: (0, j)),
                  pl.BlockSpec((1, tn), lambda j: (0, j))],
        out_specs=pl.BlockSpec((16, tn), lambda j: (0, j)),
        out_shape=jax.ShapeDtypeStruct((16, n), F32),
        compiler_params=_cp(("arbitrary",)),
        name="adaln",
    )(cond16, w_ada, b_ada)


def _expand_kv(ckv_n, kr128, wuk_ref, wuvt_ref, k_ref, vt_ref):
    cb = ckv_n.astype(BF16)
    kn = _dot(cb, wuk_ref[...])
    vt = _dot_nt(wuvt_ref[...], cb).astype(BF16)
    ones = jnp.ones((V_AUG - V_DIM, vt.shape[1]), BF16)
    for h in range(H):
        vt_ref[h * V_AUG:h * V_AUG + V_DIM, :] = vt[h * V_DIM:(h + 1) * V_DIM]
        vt_ref[h * V_AUG + V_DIM:(h + 1) * V_AUG, :] = ones
    krb = kr128.astype(BF16)
    for h in range(H):
        k_ref[:, h * HEAD_PAD:h * HEAD_PAD + QK_NOPE] = kn[:, h * QK_NOPE:(h + 1) * QK_NOPE].astype(BF16)
        k_ref[:, h * HEAD_PAD + QK_NOPE:(h + 1) * HEAD_PAD] = krb


def _rope128(x, cos, s_plus, s_minus):
    return (x * cos + pltpu.roll(x, 16, 1) * s_plus + pltpu.roll(x, LANES - 16, 1) * s_minus)


def _proj_kernel(xc_ref, xl_ref, mods_ref, win_a_ref, win_g_ref, win_r_ref, qg_ref, wuq_ref, kvg_ref,
                 wuk_ref, wuv_ref,
                 lng_ref, lnb_ref, ws_ref, bsb_ref, cos_ref, sp_ref, sm_ref,
                 q_ref, k_ref, v_ref, gm_ref, ckv_ref, kr_ref):
    i = pl.program_id(0)
    x = jnp.where(i < NP_TILES, xc_ref[...], xl_ref[...])
    m = mods_ref[0]
    shift1, scale1 = m[0:1], m[1:2]
    hmod = (x * (1.0 + scale1) + shift1).astype(BF16)
    pa = _dot(hmod, win_a_ref[...])
    pg = _dot(hmod, win_g_ref[...])
    kr = _dot(hmod, win_r_ref[...])
    cq, ckv = pa[:, :QL], pa[:, QL:]
    u, v = pg[:, :GW], pg[:, GW:]

    cos, s_plus, s_minus = cos_ref[...], sp_ref[...], sm_ref[...]

    cq_n = _rms_norm(cq, qg_ref[...]).astype(BF16)
    q = _dot(cq_n, wuq_ref[...]) * Q_SCALE
    for h in range(H):
        q_ref[:, h * HEAD_PAD:h * HEAD_PAD + QK_NOPE] = q[:, h * HEAD_PAD:h * HEAD_PAD + QK_NOPE].astype(BF16)
        q_ref[:, h * HEAD_PAD + QK_NOPE:(h + 1) * HEAD_PAD] = _rope128(
            q[:, h * HEAD_PAD + QK_NOPE:(h + 1) * HEAD_PAD], cos, s_plus, s_minus).astype(BF16)

    ckv_n = _rms_norm(ckv, kvg_ref[...])

    @pl.when(i < NP_TILES)
    def _():
        ckv_ref[...] = ckv_n
        kr_ref[...] = kr[:, :QK_ROPE]

    _expand_kv(ckv_n, _rope128(kr, cos, s_plus, s_minus), wuk_ref, wuv_ref, k_ref, v_ref)

    gu = _gelu(u)
    vn = _layer_norm(_gelu(v), lng_ref[...], lnb_ref[...]).astype(BF16)
    for c in range(TM // CHUNK):
        rows = slice(c * CHUNK, (c + 1) * CHUNK)
        for g in range(G_GROUPS):
            cols = slice(g * G_DIM, (g + 1) * G_DIM)
            s = _dot(ws_ref[g], vn[rows, cols]) + bsb_ref[g]
            gm_ref[rows, cols] = (gu[rows, cols] * s).astype(BF16)


def _mods_index(i, tile=TM):
    return jnp.where(i < T_CTX // tile, N_LAT_B, (i - T_CTX // tile) // (N_LAT // tile))


def _proj(xc, xl, mods, win_a, win_g, win_r, qg, wuq, kvg, wuk, wuv, lng, lnb, ws, bsb, cos_t, sp_t, sm_t):
    def tok(width):
        return pl.BlockSpec((TM, width), lambda i: (i, 0))

    def pos_block(i):
        return jnp.where(i < NP_TILES, LAT_TILES_PER_B, (i - NP_TILES) % LAT_TILES_PER_B)

    rope_spec = pl.BlockSpec((TM, LANES), lambda i: (pos_block(i), 0))
    return pl.pallas_call(
        _proj_kernel,
        grid=(T // TM,),
        in_specs=[
            pl.BlockSpec((TM, D), lambda i: (jnp.minimum(i, NP_TILES - 1), 0)),
            pl.BlockSpec((TM, D), lambda i: (jnp.maximum(i - NP_TILES, 0), 0)),
            pl.BlockSpec((1, 6, D), lambda i: (_mods_index(i), 0, 0)),
            _const_spec((D, QL + KVL)), _const_spec((D, 2 * GW)), _const_spec((D, LANES)),
            _const_spec((1, QL)), _const_spec((QL, H * HEAD_PAD)),
            _const_spec((1, KVL)), _const_spec((KVL, H * QK_NOPE)), _const_spec((H * V_DIM, KVL)),
            _const_spec((1, GW)), _const_spec((1, GW)),
            _const_spec((G_GROUPS, CHUNK, CHUNK)), _const_spec((G_GROUPS, CHUNK, G_DIM)),
            rope_spec, rope_spec, rope_spec,
        ],
        out_specs=[tok(H * HEAD_PAD), tok(H * HEAD_PAD),
                   pl.BlockSpec((H * V_AUG, TM), lambda i: (0, i)), tok(GW),
                   pl.BlockSpec((TM, KVL), lambda i: (jnp.minimum(i, NP_TILES - 1), 0)),
                   pl.BlockSpec((TM, QK_ROPE), lambda i: (jnp.minimum(i, NP_TILES - 1), 0))],
        out_shape=[
            jax.ShapeDtypeStruct((T, H * HEAD_PAD), BF16),
            jax.ShapeDtypeStruct((T, H * HEAD_PAD), BF16),
            jax.ShapeDtypeStruct((H * V_AUG, T), BF16),
            jax.ShapeDtypeStruct((T, GW), BF16),
            jax.ShapeDtypeStruct((T_CTX, KVL), F32),
            jax.ShapeDtypeStruct((T_CTX, QK_ROPE), F32),
        ],
        compiler_params=_cp(("arbitrary",)),
        name="proj",
    )(xc, xl, mods, win_a, win_g, win_r, qg, wuq, kvg, wuk, wuv, lng, lnb, ws, bsb, cos_t, sp_t, sm_t)


def _kvexp_kernel(ckv_ref, kr_ref, wuk_ref, wuv_ref, k_ref, v_ref):
    kr = kr_ref[...]
    kr128 = jnp.concatenate([kr, jnp.zeros_like(kr)], axis=1)
    _expand_kv(ckv_ref[...], kr128, wuk_ref, wuv_ref, k_ref, v_ref)


def _kvexp(ckv, kr, wuk, wuv):
    rows = ckv.shape[0]
    return pl.pallas_call(
        _kvexp_kernel,
        grid=(rows // TM,),
        in_specs=[pl.BlockSpec((TM, KVL), lambda i: (i, 0)), pl.BlockSpec((TM, QK_ROPE), lambda i: (i, 0)),
                  _const_spec((KVL, H * QK_NOPE)), _const_spec((H * V_DIM, KVL))],
        out_specs=[pl.BlockSpec((TM, H * HEAD_PAD), lambda i: (i, 0)),
                   pl.BlockSpec((H * V_AUG, TM), lambda i: (0, i))],
        out_shape=[jax.ShapeDtypeStruct((rows, H * HEAD_PAD), BF16),
                   jax.ShapeDtypeStruct((H * V_AUG, rows), BF16)],
        compiler_params=_cp(("arbitrary",)),
        name="kvexp",
    )(ckv, kr, wuk, wuv)


def _probs(s, m):
    m_new = jnp.maximum(m, jnp.max(s, 0, keepdims=True))
    return m_new, jnp.exp2(m - m_new), jnp.exp2(s - m_new).astype(BF16)


def _attn_finish(acc):
    return (acc[:V_DIM] / acc[V_DIM:V_DIM + 1]).T.astype(BF16)


def _attn_ctx_kernel(q_ref, k_ref, vt_ref, o_ref):
    m0 = jnp.full((1, N_CTX), -jnp.inf, F32)
    for h in range(H):
        qh = q_ref[:, h * HEAD_PAD:(h + 1) * HEAD_PAD]
        kh = k_ref[:, h * HEAD_PAD:(h + 1) * HEAD_PAD]
        _, _, p = _probs(_dot_nt(kh, qh), m0)
        acc = _dot(vt_ref[h * V_AUG:(h + 1) * V_AUG, :], p)
        o_ref[:, h * V_DIM:(h + 1) * V_DIM] = _attn_finish(acc)


def _attn_ctx(q, k, vt):
    return pl.pallas_call(
        _attn_ctx_kernel,
        grid=(N_CTX_B,),
        in_specs=[pl.BlockSpec((N_CTX, H * HEAD_PAD), lambda b: (b, 0)),
                  pl.BlockSpec((N_CTX, H * HEAD_PAD), lambda b: (b, 0)),
                  pl.BlockSpec((H * V_AUG, N_CTX), lambda b: (0, b))],
        out_specs=pl.BlockSpec((N_CTX, H * V_DIM), lambda b: (b, 0)),
        out_shape=jax.ShapeDtypeStruct((T_CTX, H * V_DIM), BF16),
        compiler_params=_cp(("arbitrary",)),
        name="attn_ctx",
    )(q, k, vt)


def _attn_lat_kernel(q_ref, k_ref, vt_ref, kc_ref, vct_ref, o_ref):
    q = q_ref[...]
    n_lat = N_LAT // TK

    def keys(c):
        return k_ref[c * TK:(c + 1) * TK, :] if c < n_lat else kc_ref[...]

    def values_t(c):
        return vt_ref[:, c * TK:(c + 1) * TK] if c < n_lat else vct_ref[...]

    m = jnp.full((1, TQ), -jnp.inf, F32)
    acc = jnp.zeros((V_AUG, TQ), F32)
    s_next = _dot_nt(keys(0), q)
    m_next = jnp.maximum(m, jnp.max(s_next, 0, keepdims=True))
    p_prev = alpha_prev = None
    for c in range(n_lat + 1):
        s, m_new = s_next, m_next
        if c < n_lat:
            s_next = _dot_nt(keys(c + 1), q)
            m_next = jnp.maximum(m_new, jnp.max(s_next, 0, keepdims=True))
        if c > 0:
            acc = alpha_prev * acc + _dot(values_t(c - 1), p_prev)
        alpha_prev = jnp.exp2(m - m_new)
        p_prev = jnp.exp2(s - m_new).astype(BF16)
        m = m_new
    acc = alpha_prev * acc + _dot(values_t(n_lat), p_prev)
    o_ref[...] = _attn_finish(acc)


def _attn_lat(q, k, vt, kc, vct):
    qb0 = T_CTX // TQ
    qpb = N_LAT // TQ
    kb0 = T_CTX // N_LAT
    return pl.pallas_call(
        _attn_lat_kernel,
        grid=(N_LAT_B, H, qpb),
        in_specs=[pl.BlockSpec((TQ, HEAD_PAD), lambda b, h, qi: (qb0 + b * qpb + qi, h)),
                  pl.BlockSpec((N_LAT, HEAD_PAD), lambda b, h, qi: (kb0 + b, h)),
                  pl.BlockSpec((V_AUG, N_LAT), lambda b, h, qi: (h, kb0 + b)),
                  pl.BlockSpec((PAST, HEAD_PAD), lambda b, h, qi: (b, h)),
                  pl.BlockSpec((V_AUG, PAST), lambda b, h, qi: (h, b))],
        out_specs=pl.BlockSpec((TQ, V_DIM), lambda b, h, qi: (b * qpb + qi, h)),
        out_shape=jax.ShapeDtypeStruct((T_LAT, H * V_DIM), BF16),
        compiler_params=_cp(("arbitrary", "arbitrary", "arbitrary")),
        name="attn_lat",
    )(q, k, vt, kc, vct)


def _route(logits):
    lane_i = lax.broadcasted_iota(I32, logits.shape, 1)
    lane = lane_i.astype(F32)
    neg = jnp.float32(-jnp.inf)
    far = jnp.float32(LANES)
    gl = jnp.where(lane_i < N_GRP, logits, neg)
    gmax = jnp.max(gl, -1, keepdims=True)
    gidx = jnp.min(jnp.where(gl == gmax, lane, far), -1, keepdims=True)
    p_top = 1.0 / jnp.sum(jnp.exp(gl - gmax), -1, keepdims=True)
    grp_of_lane = ((lane_i - N_GRP) >> 3).astype(F32)
    in_grp = (lane_i >= N_GRP) & (lane_i < N_GRP + N_EXP) & (grp_of_lane == gidx)
    el = jnp.where(in_grp, logits, neg)
    m1 = jnp.max(el, -1, keepdims=True)
    i1 = jnp.min(jnp.where(el == m1, lane, far), -1, keepdims=True)
    el2 = jnp.where(lane == i1, neg, el)
    m2 = jnp.max(el2, -1, keepdims=True)
    i2 = jnp.min(jnp.where(el2 == m2, lane, far), -1, keepdims=True)
    e2 = jnp.exp(m2 - m1)
    w1 = p_top / (1.0 + e2)
    w2 = p_top * e2 / (1.0 + e2)
    return i1 - N_GRP, i2 - N_GRP, w1, w2


def _pair(a, b, dtype):
    two = lax.broadcasted_iota(I32, (a.shape[0], TOP_K), 1)
    return jnp.where(two == 0, a, b).astype(dtype)


def _post_kernel(ac_ref, al_ref, gm_ref, xc_ref, xl_ref, mods_ref, wout_ref, g1_ref, b1_ref, wrt_ref,
                 brt_ref, x1_ref, h2p_ref, ridx_ref, rwt_ref):
    i = pl.program_id(0)
    is_ctx = i < T_CTX // TP
    m = mods_ref[0]
    gate1, shift2, scale2 = m[2:3], m[3:4], m[4:5]

    def mix_of(s):
        rows = slice(s * SUB, (s + 1) * SUB)
        attn = jnp.where(is_ctx, ac_ref[rows, :], al_ref[rows, :])
        return _dot(jnp.concatenate([attn, gm_ref[rows, :]], axis=1), wout_ref[...])

    n_sub = TP // SUB
    mix_next = mix_of(0)
    for s in range(n_sub):
        rows = slice(s * SUB, (s + 1) * SUB)
        mix = mix_next
        if s + 1 < n_sub:
            mix_next = mix_of(s + 1)
        x = jnp.where(is_ctx, xc_ref[rows, :], xl_ref[rows, :])
        x1 = _layer_norm(ALPHA * x + gate1 * mix, g1_ref[...], b1_ref[...])
        x1_ref[rows, :] = x1
        h2 = x1 * (1.0 + scale2) + shift2
        _store_rows_as_tiles(h2p_ref, s * SUB, _pack_pair(h2[:, :D // 2], h2[:, D // 2:]))
        e1, e2, w1, w2 = _route(_dot(h2.astype(BF16), wrt_ref[...]) + brt_ref[...])
        ridx_ref[rows, :] = _pair(e1, e2, I32)
        rwt_ref[rows, :] = _pair(w1, w2, F32)


def _post(attn_c, attn_l, gm, xc, xl, mods, wout, g1, b1, wrt, brt):
    npt = T_CTX // TP

    def tok(width):
        return pl.BlockSpec((TP, width), lambda i: (i, 0))

    def ctx(width):
        return pl.BlockSpec((TP, width), lambda i: (jnp.minimum(i, npt - 1), 0))

    def lat(width):
        return pl.BlockSpec((TP, width), lambda i: (jnp.maximum(i - npt, 0), 0))

    return pl.pallas_call(
        _post_kernel,
        grid=(T // TP,),
        in_specs=[ctx(H * V_DIM), lat(H * V_DIM), tok(GW), ctx(D), lat(D),
                  pl.BlockSpec((1, 6, D), lambda i: (_mods_index(i, TP), 0, 0)),
                  _const_spec((D, D)), _const_spec((1, D)), _const_spec((1, D)),
                  _const_spec((D, LANES)), _const_spec((1, LANES))],
        out_specs=[tok(D), pl.BlockSpec((TP * RT, LANES), lambda i: (i, 0)), tok(TOP_K), tok(TOP_K)],
        out_shape=[jax.ShapeDtypeStruct((T, D), F32),
                   jax.ShapeDtypeStruct((T * RT, LANES), U32),
                   jax.ShapeDtypeStruct((T, TOP_K), I32),
                   jax.ShapeDtypeStruct((T, TOP_K), F32)],
        compiler_params=_cp(("arbitrary",)),
        name="post",
    )(attn_c, attn_l, gm, xc, xl, mods, wout, g1, b1, wrt, brt)


def _rank_kernel(ridx_ref, tri_ref, rank_ref, hist_ref):
    lane = lax.broadcasted_iota(I32, (SUB, LANES), 1).astype(F32)
    seen = jnp.zeros((1, LANES), F32)
    for s in range(TP // SUB):
        rows = slice(s * SUB, (s + 1) * SUB)
        e = ridx_ref[rows, :].astype(F32)
        oh1 = (lane == e[:, 0:1]).astype(F32)
        oh2 = (lane == e[:, 1:2]).astype(F32)
        both = oh1 + oh2
        before = _dot(tri_ref[...], both.astype(BF16)) + seen
        rank_ref[rows, :] = _pair(jnp.sum(before * oh1, -1, keepdims=True),
                                  jnp.sum(before * oh2, -1, keepdims=True), I32)
        seen = seen + jnp.sum(both, 0, keepdims=True)
    hist_ref[0] = jnp.broadcast_to(seen, (8, LANES))


def _rank(ridx):
    return pl.pallas_call(
        _rank_kernel,
        grid=(T // TP,),
        in_specs=[pl.BlockSpec((TP, TOP_K), lambda i: (i, 0)), _const_spec((SUB, SUB))],
        out_specs=[pl.BlockSpec((TP, TOP_K), lambda i: (i, 0)),
                   pl.BlockSpec((1, 8, LANES), lambda i: (i, 0, 0))],
        out_shape=[jax.ShapeDtypeStruct((T, TOP_K), I32),
                   jax.ShapeDtypeStruct((T // TP, 8, LANES), F32)],
        compiler_params=_cp(("arbitrary",)),
        name="rank",
    )(ridx, jnp.asarray(np.tril(np.ones((SUB, SUB), np.float32), -1), BF16))


def _row_copy(src, src_row, dst, dst_row, sem):
    def tile(ref, row):
        return ref.at[pl.ds(pl.multiple_of(row * RT, RT), RT)]

    return pltpu.make_async_copy(tile(src, src_row), tile(dst, dst_row), sem)


def _dispatch_kernel(zb_ref, zv_ref, dest_ref, h2p_ref, xs_ref, zbuf, sem):
    i = pl.program_id(0)

    @pl.when(i == 0)
    def _():
        zbuf[...] = jnp.zeros_like(zbuf)

        def zero_block(j):
            start = pl.multiple_of(zb_ref[j] * (BM * RT), BM * RT)
            return pltpu.make_async_copy(zbuf, xs_ref.at[pl.ds(start, BM * RT)], sem)

        def start(j, carry):
            @pl.when(zv_ref[j] != 0)
            def _():
                zero_block(j).start()
            return carry

        def wait(j, carry):
            @pl.when(zv_ref[j] != 0)
            def _():
                zero_block(j).wait()
            return carry

        lax.fori_loop(0, N_ZERO, start, 0)
        lax.fori_loop(0, N_ZERO, wait, 0)

    def issue(r, carry):
        for k in range(TOP_K):
            _row_copy(h2p_ref, r, xs_ref, dest_ref[0, 0, TOP_K * r + k], sem).start(priority=k)
        return carry

    lax.fori_loop(0, TD, issue, 0, unroll=8)
    for _ in range(TOP_K):
        pltpu.make_async_copy(h2p_ref, xs_ref.at[pl.ds(0, TD * RT)], sem).wait()


def _dispatch(zero_blk, zero_valid, dest, h2p):
    grid_spec = pltpu.PrefetchScalarGridSpec(
        num_scalar_prefetch=2,
        grid=(T // TD,),
        in_specs=[pl.BlockSpec((1, 1, TOP_K * TD), lambda i, zb, zv: (i, 0, 0), memory_space=pltpu.SMEM),
                  pl.BlockSpec((TD * RT, LANES), lambda i, zb, zv: (i, 0))],
        out_specs=pl.BlockSpec(memory_space=pl.ANY),
        scratch_shapes=[pltpu.VMEM((BM * RT, LANES), U32), pltpu.SemaphoreType.DMA(())],
    )
    return pl.pallas_call(
        _dispatch_kernel,
        grid_spec=grid_spec,
        out_shape=jax.ShapeDtypeStruct((N_BLOCKS * BM * RT, LANES), U32),
        compiler_params=_cp(("arbitrary",)),
        name="dispatch",
    )(zero_blk, zero_valid, dest.reshape(T // TD, 1, TOP_K * TD), h2p)


def _experts_kernel(be_ref, bv_ref, sw_ref, es_ref, nx_ref, x_ref, w1_hbm, w3_hbm, w2_hbm, y_ref,
                    wf1, wf3, wf2, w1b, w3b, w2b, sems):
    i = pl.program_id(0)
    e = be_ref[i]
    nv = bv_ref[i]

    def weight_copies(expert, s):
        return (pltpu.make_async_copy(w1_hbm.at[expert], wf1.at[s], sems.at[s]),
                pltpu.make_async_copy(w3_hbm.at[expert], wf3.at[s], sems.at[s]),
                pltpu.make_async_copy(w2_hbm.at[expert], wf2.at[s], sems.at[s]))

    @pl.when(sw_ref[i] != 0)
    def _():
        s = es_ref[i]

        @pl.when(i == 0)
        def _():
            for c in weight_copies(e, s):
                c.start()

        for c in weight_copies(e, s):
            c.wait()
        nxt = nx_ref[i]

        @pl.when(nxt >= 0)
        def _():
            for c in weight_copies(nxt, 1 - s):
                c.start()

        w1b[...] = wf1[s].astype(BF16)
        w3b[...] = wf3[s].astype(BF16)
        w2b[...] = wf2[s].astype(BF16)

    @pl.when(nv > 0)
    def _():
        lo, hi = _unpack_pair(_load_tiles_as_rows(x_ref, 0, BM))
        xb = jnp.concatenate([lo.astype(BF16), hi.astype(BF16)], axis=1)
        a = _dot(xb, w1b[...])
        b = _dot(xb, w3b[...])
        hid = (jax.nn.silu(a) * b).astype(BF16)
        y = _dot(hid, w2b[...])
        _store_rows_as_tiles(y_ref, 0, _pack_pair(y[:, :D // 2], y[:, D // 2:]))

    @pl.when(nv == 0)
    def _():
        y_ref[...] = jnp.zeros_like(y_ref)


def _experts(blk_expert, blk_valid, switch, wslot, next_expert, xs, w1, w3, w2):
    hbm = pl.BlockSpec(memory_space=pl.ANY)
    grid_spec = pltpu.PrefetchScalarGridSpec(
        num_scalar_prefetch=5,
        grid=(N_BLOCKS,),
        in_specs=[pl.BlockSpec((BM * RT, LANES), lambda i, *_: (i, 0)), hbm, hbm, hbm],
        out_specs=pl.BlockSpec((BM * RT, LANES), lambda i, *_: (i, 0)),
        scratch_shapes=[pltpu.VMEM((2, D, E_HID), F32), pltpu.VMEM((2, D, E_HID), F32),
                        pltpu.VMEM((2, E_HID, D), F32),
                        pltpu.VMEM((D, E_HID), BF16), pltpu.VMEM((D, E_HID), BF16),
                        pltpu.VMEM((E_HID, D), BF16), pltpu.SemaphoreType.DMA((2,))],
    )
    return pl.pallas_call(
        _experts_kernel,
        grid_spec=grid_spec,
        out_shape=jax.ShapeDtypeStruct((N_BLOCKS * BM * RT, LANES), U32),
        compiler_params=_cp(("arbitrary",)),
        name="experts",
    )(blk_expert, blk_valid, switch, wslot, next_expert, xs, w1, w3, w2)


def _final_kernel(dcur_ref, dnext_ref, ys_ref, x1_ref, rwt_ref, mods_ref, g2_ref, b2_ref, oc_ref, ol_ref,
                  buf, sems):
    i = pl.program_id(0)
    n = pl.num_programs(0)
    slot = i % 2

    def first_row(s, k):
        return (s * TOP_K + k) * TM

    def issue(d_ref, s):
        def body(r, carry):
            for k in range(TOP_K):
                _row_copy(ys_ref, d_ref[0, 0, TOP_K * r + k], buf, first_row(s, k) + r, sems.at[s]).start()
            return carry

        lax.fori_loop(0, TM, body, 0, unroll=8)

    @pl.when(i == 0)
    def _():
        issue(dcur_ref, 0)

    @pl.when(i + 1 < n)
    def _():
        issue(dnext_ref, 1 - slot)

    for k in range(TOP_K):
        pltpu.make_async_copy(ys_ref.at[pl.ds(0, TM * RT)], buf.at[pl.ds(0, TM * RT)], sems.at[slot]).wait()

    w = rwt_ref[...]
    lo0, hi0 = _unpack_pair(_load_tiles_as_rows(buf, first_row(slot, 0), TM))
    lo1, hi1 = _unpack_pair(_load_tiles_as_rows(buf, first_row(slot, 1), TM))
    w0, w1 = w[:, 0:1], w[:, 1:2]
    moe = jnp.concatenate([lo0 * w0 + lo1 * w1, hi0 * w0 + hi1 * w1], axis=1)
    gate2 = mods_ref[0][5:6]
    out = _layer_norm(ALPHA * x1_ref[...] + gate2 * moe, g2_ref[...], b2_ref[...])

    @pl.when(i < NP_TILES)
    def _():
        oc_ref[...] = out

    @pl.when(i >= NP_TILES)
    def _():
        ol_ref[...] = out


def _final(dest3, ys, x1, rwt, mods, g2, b2):
    n = T // TM
    return pl.pallas_call(
        _final_kernel,
        grid=(n,),
        in_specs=[pl.BlockSpec((1, 1, TOP_K * TM), lambda i: (i, 0, 0), memory_space=pltpu.SMEM),
                  pl.BlockSpec((1, 1, TOP_K * TM), lambda i: (jnp.minimum(i + 1, n - 1), 0, 0),
                               memory_space=pltpu.SMEM),
                  pl.BlockSpec(memory_space=pl.ANY),
                  pl.BlockSpec((TM, D), lambda i: (i, 0)),
                  pl.BlockSpec((TM, TOP_K), lambda i: (i, 0)),
                  pl.BlockSpec((1, 6, D), lambda i: (_mods_index(i), 0, 0)),
                  _const_spec((1, D)), _const_spec((1, D))],
        out_specs=[pl.BlockSpec((TM, D), lambda i: (jnp.minimum(i, NP_TILES - 1), 0)),
                   pl.BlockSpec((TM, D), lambda i: (jnp.maximum(i - NP_TILES, 0), 0))],
        out_shape=[jax.ShapeDtypeStruct((T_CTX, D), F32), jax.ShapeDtypeStruct((T_LAT, D), F32)],
        scratch_shapes=[pltpu.VMEM((2 * TOP_K * TM * RT, LANES), U32), pltpu.SemaphoreType.DMA((2,))],
        compiler_params=_cp(("arbitrary",)),
        name="final",
    )(dest3, dest3, ys, x1, rwt, mods, g2, b2)


def _rope_tables():
    n = N_LAT
    rows = n // GRID_W
    f32 = np.float32
    row = np.repeat(np.arange(rows, dtype=f32), GRID_W)
    col = np.tile(np.arange(GRID_W, dtype=f32), rows)
    half = QK_ROPE // 2
    inv = (f32(ROPE_THETA) ** (-np.arange(0, half, 2, dtype=f32) / f32(half))).astype(f32)
    ang_r, ang_c = (row[:, None] * inv).astype(f32), (col[:, None] * inv).astype(f32)
    cr, sr = np.cos(ang_r).astype(f32), np.sin(ang_r).astype(f32)
    cc, sc = np.cos(ang_c).astype(f32), np.sin(ang_c).astype(f32)
    z16 = np.zeros_like(cr)
    pad = lambda v: np.full((n, LANES - QK_ROPE), v, f32)
    cos = np.concatenate([cr, cr, cc, cc, pad(1.0)], axis=1)
    s_plus = np.concatenate([z16, sr, z16, sc, pad(0.0)], axis=1)
    s_minus = np.concatenate([-sr, z16, -sc, z16, pad(0.0)], axis=1)
    ident = lambda v: np.full((TM, LANES), v, f32)
    return (jnp.asarray(np.concatenate([cos, ident(1.0)], 0)),
            jnp.asarray(np.concatenate([s_plus, ident(0.0)], 0)),
            jnp.asarray(np.concatenate([s_minus, ident(0.0)], 0)))


def _block_plan(ridx, rank, hist):
    counts = jnp.sum(hist, axis=0)
    nblk = (counts + BM - 1) // BM
    bends = jnp.cumsum(nblk)
    bstarts = bends - nblk
    base = bstarts[None, :] * BM + jnp.cumsum(hist, axis=0) - hist
    hit = ridx.reshape(T // TP, TP, TOP_K, 1) == jnp.arange(N_EXP, dtype=I32)
    first = jnp.sum(jnp.where(hit, base[:, None, None, :], 0), axis=-1)
    dest = (first + rank.reshape(T // TP, TP, TOP_K)).reshape(-1)
    eid = jnp.arange(N_EXP, dtype=I32)
    blk = jnp.arange(N_BLOCKS, dtype=I32)
    blk_expert = jnp.minimum(jnp.sum((blk[:, None] >= bends[None, :]).astype(I32), axis=1), N_EXP - 1)
    of_blk = blk_expert[:, None] == eid[None, :]

    def per_block(v):
        return jnp.sum(jnp.where(of_blk, v[None, :], 0), axis=1)

    left = per_block(counts) - (blk - per_block(bstarts)) * BM
    blk_valid = jnp.where(blk < bends[-1], jnp.clip(left, 0, BM), 0).astype(I32)
    unused = bends[-1] + jnp.arange(N_EXP, dtype=I32)
    zero_blk = jnp.concatenate([jnp.maximum(bends - 1, 0), jnp.minimum(unused, N_BLOCKS - 1)])
    zero_valid = jnp.concatenate([counts % BM != 0, unused < N_BLOCKS])
    prev_expert = jnp.concatenate([jnp.full((1,), -1, I32), blk_expert[:-1]])
    switch = (blk < bends[-1]) & (blk_expert != prev_expert)
    wslot = (jnp.cumsum(switch.astype(I32)) - 1) % 2
    later = (eid[None, :] > eid[:, None]) & (nblk[None, :] > 0)
    next_of = jnp.min(jnp.where(later, eid[None, :], N_EXP), axis=1)
    next_expert = per_block(jnp.where(next_of < N_EXP, next_of, -1))
    return (dest.astype(I32), blk_expert.astype(I32), blk_valid, zero_blk.astype(I32),
            zero_valid.astype(I32), switch.astype(I32), wslot.astype(I32), next_expert.astype(I32))


def kernel(x_prompt, x_sample, cache_ckv, cache_krope, c, c_ctx, w_ada, b_ada, w_in, q_norm_g, w_uq,
           kv_norm_g, w_ukv, gmlp_ln_g, gmlp_ln_b, w_spatial, b_spatial, w_out, ln1_g, ln1_b, w_group,
           b_group, w_router, b_router, w1, w3, w2, ln2_g, ln2_b):
    l = 0
    xc = x_prompt.reshape(T_CTX, D)
    xl = x_sample.reshape(T_LAT, D)

    cond16 = jnp.concatenate([c, c_ctx[None, :], jnp.zeros((16 - N_LAT_B - 1, D), F32)], axis=0)
    mods = _adaln(cond16, w_ada[l], b_ada[l][None, :]).reshape(16, 6, D)

    wi = w_in[l]
    o_kr = QL + KVL
    o_u = o_kr + QK_ROPE
    win_a = wi[:, :o_kr].astype(BF16)
    win_g = wi[:, o_u:].astype(BF16)
    win_r = jnp.pad(wi[:, o_kr:o_u], ((0, 0), (0, LANES - QK_ROPE))).astype(BF16)
    wuq = jnp.pad(w_uq[l].reshape(QL, H, QK_NOPE + QK_ROPE),
                  ((0, 0), (0, 0), (0, HEAD_PAD - QK_NOPE - QK_ROPE))).reshape(QL, H * HEAD_PAD).astype(BF16)
    wkv = w_ukv[l].reshape(KVL, H, QK_NOPE + V_DIM)
    wuk = wkv[:, :, :QK_NOPE].reshape(KVL, H * QK_NOPE).astype(BF16)
    wuvt = wkv[:, :, QK_NOPE:].reshape(KVL, H * V_DIM).T.astype(BF16)
    ws = w_spatial[l].astype(BF16)
    bsb = jnp.broadcast_to(b_spatial[l][:, :, None], (G_GROUPS, CHUNK, G_DIM))
    wrt = jnp.concatenate([w_group[l], w_router[l], jnp.zeros((D, LANES - N_GRP - N_EXP), F32)],
                          axis=1).astype(BF16)
    brt = jnp.concatenate([b_group[l], b_router[l], jnp.zeros((LANES - N_GRP - N_EXP,), F32)])[None, :]
    cos_t, sp_t, sm_t = _rope_tables()
    row = lambda v: v[l][None, :]

    q, k, vt, gm, ckv_n, kr = _proj(xc, xl, mods, win_a, win_g, win_r, row(q_norm_g), wuq, row(kv_norm_g), wuk, wuvt,
                                   row(gmlp_ln_g), row(gmlp_ln_b), ws, bsb, cos_t, sp_t, sm_t)
    kc, vct = _kvexp(cache_ckv[:, l].reshape(N_LAT_B * PAST, KVL),
                     cache_krope[:, l].reshape(N_LAT_B * PAST, QK_ROPE), wuk, wuvt)
    attn_c = _attn_ctx(q, k, vt)
    attn_l = _attn_lat(q, k, vt, kc, vct)

    x1, h2p, ridx, rwt = _post(attn_c, attn_l, gm, xc, xl, mods, w_out[l].astype(BF16), row(ln1_g), row(ln1_b),
                               wrt, brt)
    rank, hist = _rank(ridx)
    dest, blk_expert, blk_valid, zero_blk, zero_valid, switch, wslot, next_expert = _block_plan(
        ridx, rank, hist[:, 0, :N_EXP].astype(I32))
    xs = _dispatch(zero_blk, zero_valid, dest, h2p)
    ys = _experts(blk_expert, blk_valid, switch, wslot, next_expert, xs, w1[l], w3[l], w2[l])
    y_c, y_l = _final(dest.reshape(T // TM, 1, TOP_K * TM), ys, x1, rwt, mods, row(ln2_g), row(ln2_b))

    return (y_c.reshape(N_CTX_B, N_CTX, D), y_l.reshape(N_LAT_B, N_LAT, D),
            ckv_n.reshape(N_CTX_B, 1, N_CTX, KVL), kr.reshape(N_CTX_B, 1, N_CTX, QK_ROPE))
```

```python
import functools

import jax
import jax.numpy as jnp
import numpy as np
from jax import lax
from jax.experimental import pallas as pl
from jax.experimental.pallas import tpu as pltpu

F32 = jnp.float32
BF16 = jnp.bfloat16
U32 = jnp.uint32
I32 = jnp.int32

D = 2048
N_CTX_B, N_CTX = 32, 256
N_LAT_B, N_LAT = 4, 4096
PAST = 512
GRID_W = 64
H = 8
QK_NOPE, QK_ROPE, V_DIM = 128, 64, 128
QL, KVL = 512, 512
GW = 1024
G_GROUPS, G_DIM, CHUNK = 8, 128, 128
N_GRP, E_PER_GRP, N_EXP, TOP_K = 4, 8, 32, 2
E_HID = 512
ROPE_THETA = 10000.0
EPS = 1e-6
ALPHA = 2.0 ** 0.25
SM_SCALE = (QK_NOPE + QK_ROPE) ** -0.5
Q_SCALE = SM_SCALE * float(np.log2(np.e))

T_CTX = N_CTX_B * N_CTX
T_LAT = N_LAT_B * N_LAT
T = T_CTX + T_LAT
N_ASSIGN = T * TOP_K

LANES = 128
HEAD_PAD = 256
V_AUG = V_DIM + 16
TM = 256
TP = 512
POST_SUBS = (256, 256)
SUB = 256
TD = 1024
TQ = 1024
TK = 512
BM = 256
N_BLOCKS = N_ASSIGN // BM + N_EXP
N_ZERO = 2 * N_EXP
VMEM_LIMIT = 56 * 1024 * 1024

NP_TILES = T_CTX // TM
LAT_TILES_PER_B = N_LAT // TM


def _cp(sem):
    return pltpu.CompilerParams(dimension_semantics=sem, vmem_limit_bytes=VMEM_LIMIT)


def _const_spec(shape):
    nd = len(shape)
    return pl.BlockSpec(shape, lambda *a: (0,) * nd, pipeline_mode=pl.Buffered(1))


def _dot(a, b):
    return jnp.dot(a, b, preferred_element_type=F32)


def _dot_nt(a, b):
    return lax.dot_general(a, b, (((1,), (1,)), ((), ())), preferred_element_type=F32)


def _layer_norm(x, g, b):
    mu = jnp.mean(x, -1, keepdims=True)
    xc = x - mu
    var = jnp.mean(xc * xc, -1, keepdims=True)
    return xc * lax.rsqrt(var + EPS) * g + b


def _rms_norm(x, g):
    return x * lax.rsqrt(jnp.mean(x * x, -1, keepdims=True) + EPS) * g


def _gelu(x):
    return 0.5 * x * (1.0 + lax.erf(x * np.float32(np.sqrt(0.5))))


def _pack_pair(lo, hi):
    lo_b = lax.bitcast_convert_type(lo.astype(BF16).astype(F32), U32)
    hi_b = lax.bitcast_convert_type(hi.astype(BF16).astype(F32), U32)
    return hi_b | (lo_b >> 16)


def _unpack_pair(w):
    lo = lax.bitcast_convert_type(w << 16, F32)
    hi = lax.bitcast_convert_type(w & jnp.uint32(0xFFFF0000), F32)
    return lo, hi


RT = 8


def _store_rows_as_tiles(ref, row0, val):
    n = val.shape[0]
    for j in range(RT):
        ref[pl.ds(row0 * RT + j, n, stride=RT), :] = val[:, j * LANES:(j + 1) * LANES]


def _load_tiles_as_rows(ref, row0, n):
    return jnp.concatenate([ref[pl.ds(row0 * RT + j, n, stride=RT), :] for j in range(RT)], axis=1)


def _adaln_kernel(cond_ref, w_ref, b_ref, o_ref):
    s = jax.nn.silu(cond_ref[...])
    s_hi = s.astype(BF16)
    s_lo = (s - s_hi.astype(F32)).astype(BF16)
    w = w_ref[...]
    w_hi = w.astype(BF16)
    w_lo = (w - w_hi.astype(F32)).astype(BF16)
    o_ref[...] = _dot(s_hi, w_hi) + _dot(s_lo, w_hi) + _dot(s_hi, w_lo) + b_ref[...]


def _adaln(cond16, w_ada, b_ada):
    tn = 1024
    n = w_ada.shape[1]
    return pl.pallas_call(
        _adaln_kernel,
        grid=(n // tn,),
        in_specs=[pl.BlockSpec((16, D), lambda j: (0, 0)),
                  pl.BlockSpec((D, tn), lambda j: (0, j)),
                  pl.BlockSpec((1, tn), lambda j: (0, j))],
        out_specs=pl.BlockSpec((16, tn), lambda j: (0, j)),
        out_shape=jax.ShapeDtypeStruct((16, n), F32),
        compiler_params=_cp(("arbitrary",)),
        name="adaln",
    )(cond16, w_ada, b_ada)


def _expand_kv(ckv_n, kr128, wuk_ref, wuvt_ref, k_ref, vt_ref):
    cb = ckv_n.astype(BF16)
    kn = _dot(cb, wuk_ref[...])
    vt = _dot_nt(wuvt_ref[...], cb).astype(BF16)
    ones = jnp.ones((V_AUG - V_DIM, vt.shape[1]), BF16)
    for h in range(H):
        vt_ref[h * V_AUG:h * V_AUG + V_DIM, :] = vt[h * V_DIM:(h + 1) * V_DIM]
        vt_ref[h * V_AUG + V_DIM:(h + 1) * V_AUG, :] = ones
    krb = kr128.astype(BF16)
    for h in range(H):
        k_ref[:, h * HEAD_PAD:h * HEAD_PAD + QK_NOPE] = kn[:, h * QK_NOPE:(h + 1) * QK_NOPE].astype(BF16)
        k_ref[:, h * HEAD_PAD + QK_NOPE:(h + 1) * HEAD_PAD] = krb


def _rope128(x, cos, s_plus, s_minus):
    return (x * cos + pltpu.roll(x, 16, 1) * s_plus + pltpu.roll(x, LANES - 16, 1) * s_minus)


def _proj_kernel(xc_ref, xl_ref, mods_ref, win_a_ref, win_g_ref, win_r_ref, qg_ref, wuq_ref, kvg_ref,
                 wuk_ref, wuv_ref,
                 lng_ref, lnb_ref, ws_ref, bsb_ref, cos_ref, sp_ref, sm_ref,
                 q_ref, k_ref, v_ref, gm_ref, ckv_ref, kr_ref):
    i = pl.program_id(0)
    x = jnp.where(i < NP_TILES, xc_ref[...], xl_ref[...])
    m = mods_ref[0]
    shift1, scale1 = m[0:1], m[1:2]
    hmod = (x * (1.0 + scale1) + shift1).astype(BF16)
    pg = _dot(hmod, win_g_ref[...])
    u, v = pg[:, :GW], pg[:, GW:]
    gu = _gelu(u)
    vn = _layer_norm(_gelu(v), lng_ref[...], lnb_ref[...]).astype(BF16)
    pa = _dot(hmod, win_a_ref[...])
    kr = _dot(hmod, win_r_ref[...])
    cq, ckv = pa[:, :QL], pa[:, QL:]

    cos, s_plus, s_minus = cos_ref[...], sp_ref[...], sm_ref[...]

    cq_n = _rms_norm(cq, qg_ref[...]).astype(BF16)
    q = _dot(cq_n, wuq_ref[...]) * Q_SCALE
    for h in range(H):
        q_ref[:, h * HEAD_PAD:h * HEAD_PAD + QK_NOPE] = q[:, h * HEAD_PAD:h * HEAD_PAD + QK_NOPE].astype(BF16)
        q_ref[:, h * HEAD_PAD + QK_NOPE:(h + 1) * HEAD_PAD] = _rope128(
            q[:, h * HEAD_PAD + QK_NOPE:(h + 1) * HEAD_PAD], cos, s_plus, s_minus).astype(BF16)

    ckv_n = _rms_norm(ckv, kvg_ref[...])
    _expand_kv(ckv_n, _rope128(kr, cos, s_plus, s_minus), wuk_ref, wuv_ref, k_ref, v_ref)

    for c in range(TM // CHUNK):
        rows = slice(c * CHUNK, (c + 1) * CHUNK)
        for g in range(G_GROUPS):
            cols = slice(g * G_DIM, (g + 1) * G_DIM)
            s = _dot(ws_ref[g], vn[rows, cols]) + bsb_ref[g]
            gm_ref[rows, cols] = (gu[rows, cols] * s).astype(BF16)

    @pl.when(i < NP_TILES)
    def _():
        ckv_ref[...] = ckv_n
        kr_ref[...] = kr[:, :QK_ROPE]


def _mods_index(i, tile=TM):
    return jnp.where(i < T_CTX // tile, N_LAT_B, (i - T_CTX // tile) // (N_LAT // tile))


def _proj(xc, xl, mods, win_a, win_g, win_r, qg, wuq, kvg, wuk, wuv, lng, lnb, ws, bsb, cos_t, sp_t, sm_t):
    def tok(width):
        return pl.BlockSpec((TM, width), lambda i: (i, 0))

    def pos_block(i):
        return jnp.where(i < NP_TILES, LAT_TILES_PER_B, (i - NP_TILES) % LAT_TILES_PER_B)

    rope_spec = pl.BlockSpec((TM, LANES), lambda i: (pos_block(i), 0))
    return pl.pallas_call(
        _proj_kernel,
        grid=(T // TM,),
        in_specs=[
            pl.BlockSpec((TM, D), lambda i: (jnp.minimum(i, NP_TILES - 1), 0)),
            pl.BlockSpec((TM, D), lambda i: (jnp.maximum(i - NP_TILES, 0), 0)),
            pl.BlockSpec((1, 6, D), lambda i: (_mods_index(i), 0, 0)),
            _const_spec((D, QL + KVL)), _const_spec((D, 2 * GW)), _const_spec((D, LANES)),
            _const_spec((1, QL)), _const_spec((QL, H * HEAD_PAD)),
            _const_spec((1, KVL)), _const_spec((KVL, H * QK_NOPE)), _const_spec((H * V_DIM, KVL)),
            _const_spec((1, GW)), _const_spec((1, GW)),
            _const_spec((G_GROUPS, CHUNK, CHUNK)), _const_spec((G_GROUPS, CHUNK, G_DIM)),
            rope_spec, rope_spec, rope_spec,
        ],
        out_specs=[tok(H * HEAD_PAD), tok(H * HEAD_PAD),
                   pl.BlockSpec((H * V_AUG, TM), lambda i: (0, i)), tok(GW),
                   pl.BlockSpec((TM, KVL), lambda i: (jnp.minimum(i, NP_TILES - 1), 0)),
                   pl.BlockSpec((TM, QK_ROPE), lambda i: (jnp.minimum(i, NP_TILES - 1), 0))],
        out_shape=[
            jax.ShapeDtypeStruct((T, H * HEAD_PAD), BF16),
            jax.ShapeDtypeStruct((T, H * HEAD_PAD), BF16),
            jax.ShapeDtypeStruct((H * V_AUG, T), BF16),
            jax.ShapeDtypeStruct((T, GW), BF16),
            jax.ShapeDtypeStruct((T_CTX, KVL), F32),
            jax.ShapeDtypeStruct((T_CTX, QK_ROPE), F32),
        ],
        compiler_params=_cp(("arbitrary",)),
        name="proj",
    )(xc, xl, mods, win_a, win_g, win_r, qg, wuq, kvg, wuk, wuv, lng, lnb, ws, bsb, cos_t, sp_t, sm_t)


def _kvexp_kernel(ckv_ref, kr_ref, wuk_ref, wuv_ref, k_ref, v_ref):
    kr = kr_ref[...]
    kr128 = jnp.concatenate([kr, jnp.zeros_like(kr)], axis=1)
    _expand_kv(ckv_ref[...], kr128, wuk_ref, wuv_ref, k_ref, v_ref)


def _kvexp(ckv, kr, wuk, wuv):
    rows = ckv.shape[0]
    return pl.pallas_call(
        _kvexp_kernel,
        grid=(rows // TM,),
        in_specs=[pl.BlockSpec((TM, KVL), lambda i: (i, 0)), pl.BlockSpec((TM, QK_ROPE), lambda i: (i, 0)),
                  _const_spec((KVL, H * QK_NOPE)), _const_spec((H * V_DIM, KVL))],
        out_specs=[pl.BlockSpec((TM, H * HEAD_PAD), lambda i: (i, 0)),
                   pl.BlockSpec((H * V_AUG, TM), lambda i: (0, i))],
        out_shape=[jax.ShapeDtypeStruct((rows, H * HEAD_PAD), BF16),
                   jax.ShapeDtypeStruct((H * V_AUG, rows), BF16)],
        compiler_params=_cp(("arbitrary",)),
        name="kvexp",
    )(ckv, kr, wuk, wuv)


def _probs(s, m):
    m_new = jnp.maximum(m, jnp.max(s, 0, keepdims=True))
    return m_new, jnp.exp2(m - m_new), jnp.exp2(s - m_new).astype(BF16)


def _attn_finish(acc):
    return (acc[:V_DIM] / acc[V_DIM:V_DIM + 1]).T.astype(BF16)


def _attn_ctx_kernel(q_ref, k_ref, vt_ref, o_ref):
    m0 = jnp.full((1, N_CTX), -jnp.inf, F32)
    for h in range(H):
        qh = q_ref[:, h * HEAD_PAD:(h + 1) * HEAD_PAD]
        kh = k_ref[:, h * HEAD_PAD:(h + 1) * HEAD_PAD]
        _, _, p = _probs(_dot_nt(kh, qh), m0)
        acc = _dot(vt_ref[h * V_AUG:(h + 1) * V_AUG, :], p)
        o_ref[:, h * V_DIM:(h + 1) * V_DIM] = _attn_finish(acc)


def _attn_ctx(q, k, vt):
    return pl.pallas_call(
        _attn_ctx_kernel,
        grid=(N_CTX_B,),
        in_specs=[pl.BlockSpec((N_CTX, H * HEAD_PAD), lambda b: (b, 0)),
                  pl.BlockSpec((N_CTX, H * HEAD_PAD), lambda b: (b, 0)),
                  pl.BlockSpec((H * V_AUG, N_CTX), lambda b: (0, b))],
        out_specs=pl.BlockSpec((N_CTX, H * V_DIM), lambda b: (b, 0)),
        out_shape=jax.ShapeDtypeStruct((T_CTX, H * V_DIM), BF16),
        compiler_params=_cp(("arbitrary",)),
        name="attn_ctx",
    )(q, k, vt)


def _attn_lat_kernel(q_ref, k_ref, vt_ref, kc_ref, vct_ref, o_ref):
    q = q_ref[...]
    n_lat = N_LAT // TK

    def keys(c):
        return k_ref[c * TK:(c + 1) * TK, :] if c < n_lat else kc_ref[...]

    def values_t(c):
        return vt_ref[:, c * TK:(c + 1) * TK] if c < n_lat else vct_ref[...]

    m = jnp.full((1, TQ), -jnp.inf, F32)
    acc = jnp.zeros((V_AUG, TQ), F32)
    s_next = _dot_nt(keys(0), q)
    m_next = jnp.maximum(m, jnp.max(s_next, 0, keepdims=True))
    p_prev = alpha_prev = None
    for c in range(n_lat + 1):
        s, m_new = s_next, m_next
        if c < n_lat:
            s_next = _dot_nt(keys(c + 1), q)
            m_next = jnp.maximum(m_new, jnp.max(s_next, 0, keepdims=True))
        if c > 0:
            acc = alpha_prev * acc + _dot(values_t(c - 1), p_prev)
        alpha_prev = jnp.exp2(m - m_new)
        p_prev = jnp.exp2(s - m_new).astype(BF16)
        m = m_new
    acc = alpha_prev * acc + _dot(values_t(n_lat), p_prev)
    o_ref[...] = _attn_finish(acc)


def _attn_lat(q, k, vt, kc, vct):
    qb0 = T_CTX // TQ
    qpb = N_LAT // TQ
    kb0 = T_CTX // N_LAT
    return pl.pallas_call(
        _attn_lat_kernel,
        grid=(N_LAT_B, H, qpb),
        in_specs=[pl.BlockSpec((TQ, HEAD_PAD), lambda b, h, qi: (qb0 + b * qpb + qi, h)),
                  pl.BlockSpec((N_LAT, HEAD_PAD), lambda b, h, qi: (kb0 + b, h)),
                  pl.BlockSpec((V_AUG, N_LAT), lambda b, h, qi: (h, kb0 + b)),
                  pl.BlockSpec((PAST, HEAD_PAD), lambda b, h, qi: (b, h)),
                  pl.BlockSpec((V_AUG, PAST), lambda b, h, qi: (h, b))],
        out_specs=pl.BlockSpec((TQ, V_DIM), lambda b, h, qi: (b * qpb + qi, h)),
        out_shape=jax.ShapeDtypeStruct((T_LAT, H * V_DIM), BF16),
        compiler_params=_cp(("arbitrary", "arbitrary", "arbitrary")),
        name="attn_lat",
    )(q, k, vt, kc, vct)


def _route(logits):
    lane_i = lax.broadcasted_iota(I32, logits.shape, 1)
    lane = lane_i.astype(F32)
    neg = jnp.float32(-jnp.inf)
    far = jnp.float32(LANES)
    gl = jnp.where(lane_i < N_GRP, logits, neg)
    gmax = jnp.max(gl, -1, keepdims=True)
    gidx = jnp.min(jnp.where(gl == gmax, lane, far), -1, keepdims=True)
    p_top = 1.0 / jnp.sum(jnp.exp(gl - gmax), -1, keepdims=True)
    grp_of_lane = ((lane_i - N_GRP) >> 3).astype(F32)
    in_grp = (lane_i >= N_GRP) & (lane_i < N_GRP + N_EXP) & (grp_of_lane == gidx)
    el = jnp.where(in_grp, logits, neg)
    m1 = jnp.max(el, -1, keepdims=True)
    i1 = jnp.min(jnp.where(el == m1, lane, far), -1, keepdims=True)
    el2 = jnp.where(lane == i1, neg, el)
    m2 = jnp.max(el2, -1, keepdims=True)
    i2 = jnp.min(jnp.where(el2 == m2, lane, far), -1, keepdims=True)
    e2 = jnp.exp(m2 - m1)
    w1 = p_top / (1.0 + e2)
    w2 = p_top * e2 / (1.0 + e2)
    return i1 - N_GRP, i2 - N_GRP, w1, w2


def _pair(a, b, dtype):
    two = lax.broadcasted_iota(I32, (a.shape[0], TOP_K), 1)
    return jnp.where(two == 0, a, b).astype(dtype)


def _post_kernel(ac_ref, al_ref, gm_ref, xc_ref, xl_ref, mods_ref, wout_ref, g1_ref, b1_ref, wrt_ref,
                 brt_ref, x1_ref, h2p_ref, ridx_ref, rwt_ref):
    i = pl.program_id(0)
    is_ctx = i < T_CTX // TP
    m = mods_ref[0]
    gate1, shift2, scale2 = m[2:3], m[3:4], m[4:5]

    def mix_of(rows):
        attn = jnp.where(is_ctx, ac_ref[rows, :], al_ref[rows, :])
        return _dot(jnp.concatenate([attn, gm_ref[rows, :]], axis=1), wout_ref[...])

    bounds = np.cumsum((0,) + POST_SUBS)
    subs = [slice(int(a), int(b)) for a, b in zip(bounds[:-1], bounds[1:])]
    mix_next = mix_of(subs[0])
    for s, rows in enumerate(subs):
        mix = mix_next
        if s + 1 < len(subs):
            mix_next = mix_of(subs[s + 1])
        x = jnp.where(is_ctx, xc_ref[rows, :], xl_ref[rows, :])
        x1 = _layer_norm(ALPHA * x + gate1 * mix, g1_ref[...], b1_ref[...])
        x1_ref[rows, :] = x1
        h2 = x1 * (1.0 + scale2) + shift2
        _store_rows_as_tiles(h2p_ref, rows.start, _pack_pair(h2[:, :D // 2], h2[:, D // 2:]))
        e1, e2, w1, w2 = _route(_dot(h2.astype(BF16), wrt_ref[...]) + brt_ref[...])
        ridx_ref[rows, :] = _pair(e1, e2, I32)
        rwt_ref[rows, :] = _pair(w1, w2, F32)


def _post(attn_c, attn_l, gm, xc, xl, mods, wout, g1, b1, wrt, brt):
    npt = T_CTX // TP

    def tok(width):
        return pl.BlockSpec((TP, width), lambda i: (i, 0))

    def ctx(width):
        return pl.BlockSpec((TP, width), lambda i: (jnp.minimum(i, npt - 1), 0))

    def lat(width):
        return pl.BlockSpec((TP, width), lambda i: (jnp.maximum(i - npt, 0), 0))

    return pl.pallas_call(
        _post_kernel,
        grid=(T // TP,),
        in_specs=[ctx(H * V_DIM), lat(H * V_DIM), tok(GW), ctx(D), lat(D),
                  pl.BlockSpec((1, 6, D), lambda i: (_mods_index(i, TP), 0, 0)),
                  _const_spec((D, D)), _const_spec((1, D)), _const_spec((1, D)),
                  _const_spec((D, LANES)), _const_spec((1, LANES))],
        out_specs=[tok(D), pl.BlockSpec((TP * RT, LANES), lambda i: (i, 0)), tok(TOP_K), tok(TOP_K)],
        out_shape=[jax.ShapeDtypeStruct((T, D), F32),
                   jax.ShapeDtypeStruct((T * RT, LANES), U32),
                   jax.ShapeDtypeStruct((T, TOP_K), I32),
                   jax.ShapeDtypeStruct((T, TOP_K), F32)],
        compiler_params=_cp(("arbitrary",)),
        name="post",
    )(attn_c, attn_l, gm, xc, xl, mods, wout, g1, b1, wrt, brt)


def _rank_kernel(ridx_ref, tri_ref, rank_ref, hist_ref):
    lane = lax.broadcasted_iota(I32, (SUB, LANES), 1).astype(F32)
    seen = jnp.zeros((1, LANES), F32)
    for s in range(TP // SUB):
        rows = slice(s * SUB, (s + 1) * SUB)
        e = ridx_ref[rows, :].astype(F32)
        oh1 = (lane == e[:, 0:1]).astype(F32)
        oh2 = (lane == e[:, 1:2]).astype(F32)
        both = oh1 + oh2
        before = _dot(tri_ref[...], both.astype(BF16)) + seen
        rank_ref[rows, :] = _pair(jnp.sum(before * oh1, -1, keepdims=True),
                                  jnp.sum(before * oh2, -1, keepdims=True), I32)
        seen = seen + jnp.sum(both, 0, keepdims=True)
    hist_ref[0] = jnp.broadcast_to(seen, (8, LANES))


def _rank(ridx):
    return pl.pallas_call(
        _rank_kernel,
        grid=(T // TP,),
        in_specs=[pl.BlockSpec((TP, TOP_K), lambda i: (i, 0)), _const_spec((SUB, SUB))],
        out_specs=[pl.BlockSpec((TP, TOP_K), lambda i: (i, 0)),
                   pl.BlockSpec((1, 8, LANES), lambda i: (i, 0, 0))],
        out_shape=[jax.ShapeDtypeStruct((T, TOP_K), I32),
                   jax.ShapeDtypeStruct((T // TP, 8, LANES), F32)],
        compiler_params=_cp(("arbitrary",)),
        name="rank",
    )(ridx, jnp.asarray(np.tril(np.ones((SUB, SUB), np.float32), -1), BF16))


def _row_copy(src, src_row, dst, dst_row, sem):
    def tile(ref, row):
        return ref.at[pl.ds(pl.multiple_of(row * RT, RT), RT)]

    return pltpu.make_async_copy(tile(src, src_row), tile(dst, dst_row), sem)


def _dispatch_kernel(zb_ref, zv_ref, dest_ref, h2p_ref, xs_ref, zbuf, sem):
    i = pl.program_id(0)

    @pl.when(i == 0)
    def _():
        zbuf[...] = jnp.zeros_like(zbuf)

        def zero_block(j):
            start = pl.multiple_of(zb_ref[j] * (BM * RT), BM * RT)
            return pltpu.make_async_copy(zbuf, xs_ref.at[pl.ds(start, BM * RT)], sem)

        def start(j, carry):
            @pl.when(zv_ref[j] != 0)
            def _():
                zero_block(j).start()
            return carry

        def wait(j, carry):
            @pl.when(zv_ref[j] != 0)
            def _():
                zero_block(j).wait()
            return carry

        lax.fori_loop(0, N_ZERO, start, 0)
        lax.fori_loop(0, N_ZERO, wait, 0)

    def issue(r, carry):
        for k in range(TOP_K):
            _row_copy(h2p_ref, r, xs_ref, dest_ref[0, 0, TOP_K * r + k], sem).start(priority=k)
        return carry

    lax.fori_loop(0, TD, issue, 0, unroll=8)
    for _ in range(TOP_K):
        pltpu.make_async_copy(h2p_ref, xs_ref.at[pl.ds(0, TD * RT)], sem).wait()


def _dispatch(zero_blk, zero_valid, dest, h2p):
    grid_spec = pltpu.PrefetchScalarGridSpec(
        num_scalar_prefetch=2,
        grid=(T // TD,),
        in_specs=[pl.BlockSpec((1, 1, TOP_K * TD), lambda i, zb, zv: (i, 0, 0), memory_space=pltpu.SMEM),
                  pl.BlockSpec((TD * RT, LANES), lambda i, zb, zv: (i, 0))],
        out_specs=pl.BlockSpec(memory_space=pl.ANY),
        scratch_shapes=[pltpu.VMEM((BM * RT, LANES), U32), pltpu.SemaphoreType.DMA(())],
    )
    return pl.pallas_call(
        _dispatch_kernel,
        grid_spec=grid_spec,
        out_shape=jax.ShapeDtypeStruct((N_BLOCKS * BM * RT, LANES), U32),
        compiler_params=_cp(("arbitrary",)),
        name="dispatch",
    )(zero_blk, zero_valid, dest.reshape(T // TD, 1, TOP_K * TD), h2p)


def _experts_kernel(be_ref, bv_ref, sw_ref, es_ref, nx_ref, x_ref, w1_hbm, w3_hbm, w2_hbm, y_ref,
                    wf1, wf3, wf2, w1b, w3b, w2b, sems):
    i = pl.program_id(0)
    e = be_ref[i]
    nv = bv_ref[i]

    def weight_copies(expert, s):
        return (pltpu.make_async_copy(w1_hbm.at[expert], wf1.at[s], sems.at[s]),
                pltpu.make_async_copy(w3_hbm.at[expert], wf3.at[s], sems.at[s]),
                pltpu.make_async_copy(w2_hbm.at[expert], wf2.at[s], sems.at[s]))

    @pl.when(sw_ref[i] != 0)
    def _():
        s = es_ref[i]

        @pl.when(i == 0)
        def _():
            for c in weight_copies(e, s):
                c.start()

        for c in weight_copies(e, s):
            c.wait()
        nxt = nx_ref[i]

        @pl.when(nxt >= 0)
        def _():
            for c in weight_copies(nxt, 1 - s):
                c.start()

        w1b[...] = wf1[s].astype(BF16)
        w3b[...] = wf3[s].astype(BF16)
        w2b[...] = wf2[s].astype(BF16)

    @pl.when(nv > 0)
    def _():
        lo, hi = _unpack_pair(_load_tiles_as_rows(x_ref, 0, BM))
        xb = jnp.concatenate([lo.astype(BF16), hi.astype(BF16)], axis=1)
        a = _dot(xb, w1b[...])
        b = _dot(xb, w3b[...])
        hid = (jax.nn.silu(a) * b).astype(BF16)
        y = _dot(hid, w2b[...])
        _store_rows_as_tiles(y_ref, 0, _pack_pair(y[:, :D // 2], y[:, D // 2:]))

    @pl.when(nv == 0)
    def _():
        y_ref[...] = jnp.zeros_like(y_ref)


def _experts(blk_expert, blk_valid, switch, wslot, next_expert, xs, w1, w3, w2):
    hbm = pl.BlockSpec(memory_space=pl.ANY)
    grid_spec = pltpu.PrefetchScalarGridSpec(
        num_scalar_prefetch=5,
        grid=(N_BLOCKS,),
        in_specs=[pl.BlockSpec((BM * RT, LANES), lambda i, *_: (i, 0)), hbm, hbm, hbm],
        out_specs=pl.BlockSpec((BM * RT, LANES), lambda i, *_: (i, 0)),
        scratch_shapes=[pltpu.VMEM((2, D, E_HID), F32), pltpu.VMEM((2, D, E_HID), F32),
                        pltpu.VMEM((2, E_HID, D), F32),
                        pltpu.VMEM((D, E_HID), BF16), pltpu.VMEM((D, E_HID), BF16),
                        pltpu.VMEM((E_HID, D), BF16), pltpu.SemaphoreType.DMA((2,))],
    )
    return pl.pallas_call(
        _experts_kernel,
        grid_spec=grid_spec,
        out_shape=jax.ShapeDtypeStruct((N_BLOCKS * BM * RT, LANES), U32),
        compiler_params=_cp(("arbitrary",)),
        name="experts",
    )(blk_expert, blk_valid, switch, wslot, next_expert, xs, w1, w3, w2)


def _final_kernel(dcur_ref, dnext_ref, ys_ref, x1_ref, rwt_ref, mods_ref, g2_ref, b2_ref, oc_ref, ol_ref,
                  buf, sems):
    i = pl.program_id(0)
    n = pl.num_programs(0)
    slot = i % 2

    def first_row(s, k):
        return (s * TOP_K + k) * TM

    def issue(d_ref, s):
        def body(r, carry):
            for k in range(TOP_K):
                _row_copy(ys_ref, d_ref[0, 0, TOP_K * r + k], buf, first_row(s, k) + r, sems.at[s]).start()
            return carry

        lax.fori_loop(0, TM, body, 0, unroll=8)

    @pl.when(i == 0)
    def _():
        issue(dcur_ref, 0)

    @pl.when(i + 1 < n)
    def _():
        issue(dnext_ref, 1 - slot)

    for k in range(TOP_K):
        pltpu.make_async_copy(ys_ref.at[pl.ds(0, TM * RT)], buf.at[pl.ds(0, TM * RT)], sems.at[slot]).wait()

    w = rwt_ref[...]
    lo0, hi0 = _unpack_pair(_load_tiles_as_rows(buf, first_row(slot, 0), TM))
    lo1, hi1 = _unpack_pair(_load_tiles_as_rows(buf, first_row(slot, 1), TM))
    w0, w1 = w[:, 0:1], w[:, 1:2]
    moe = jnp.concatenate([lo0 * w0 + lo1 * w1, hi0 * w0 + hi1 * w1], axis=1)
    gate2 = mods_ref[0][5:6]
    out = _layer_norm(ALPHA * x1_ref[...] + gate2 * moe, g2_ref[...], b2_ref[...])

    @pl.when(i < NP_TILES)
    def _():
        oc_ref[...] = out

    @pl.when(i >= NP_TILES)
    def _():
        ol_ref[...] = out


def _final(dest3, ys, x1, rwt, mods, g2, b2):
    n = T // TM
    return pl.pallas_call(
        _final_kernel,
        grid=(n,),
        in_specs=[pl.BlockSpec((1, 1, TOP_K * TM), lambda i: (i, 0, 0), memory_space=pltpu.SMEM),
                  pl.BlockSpec((1, 1, TOP_K * TM), lambda i: (jnp.minimum(i + 1, n - 1), 0, 0),
                               memory_space=pltpu.SMEM),
                  pl.BlockSpec(memory_space=pl.ANY),
                  pl.BlockSpec((TM, D), lambda i: (i, 0)),
                  pl.BlockSpec((TM, TOP_K), lambda i: (i, 0)),
                  pl.BlockSpec((1, 6, D), lambda i: (_mods_index(i), 0, 0)),
                  _const_spec((1, D)), _const_spec((1, D))],
        out_specs=[pl.BlockSpec((TM, D), lambda i: (jnp.minimum(i, NP_TILES - 1), 0)),
                   pl.BlockSpec((TM, D), lambda i: (jnp.maximum(i - NP_TILES, 0), 0))],
        out_shape=[jax.ShapeDtypeStruct((T_CTX, D), F32), jax.ShapeDtypeStruct((T_LAT, D), F32)],
        scratch_shapes=[pltpu.VMEM((2 * TOP_K * TM * RT, LANES), U32), pltpu.SemaphoreType.DMA((2,))],
        compiler_params=_cp(("arbitrary",)),
        name="final",
    )(dest3, dest3, ys, x1, rwt, mods, g2, b2)


def _rope_tables():
    n = N_LAT
    rows = n // GRID_W
    f32 = np.float32
    row = np.repeat(np.arange(rows, dtype=f32), GRID_W)
    col = np.tile(np.arange(GRID_W, dtype=f32), rows)
    half = QK_ROPE // 2
    inv = (f32(ROPE_THETA) ** (-np.arange(0, half, 2, dtype=f32) / f32(half))).astype(f32)
    ang_r, ang_c = (row[:, None] * inv).astype(f32), (col[:, None] * inv).astype(f32)
    cr, sr = np.cos(ang_r).astype(f32), np.sin(ang_r).astype(f32)
    cc, sc = np.cos(ang_c).astype(f32), np.sin(ang_c).astype(f32)
    z16 = np.zeros_like(cr)
    pad = lambda v: np.full((n, LANES - QK_ROPE), v, f32)
    cos = np.concatenate([cr, cr, cc, cc, pad(1.0)], axis=1)
    s_plus = np.concatenate([z16, sr, z16, sc, pad(0.0)], axis=1)
    s_minus = np.concatenate([-sr, z16, -sc, z16, pad(0.0)], axis=1)
    ident = lambda v: np.full((TM, LANES), v, f32)
    return (jnp.asarray(np.concatenate([cos, ident(1.0)], 0)),
            jnp.asarray(np.concatenate([s_plus, ident(0.0)], 0)),
            jnp.asarray(np.concatenate([s_minus, ident(0.0)], 0)))


def _block_plan(ridx, rank, hist):
    counts = jnp.sum(hist, axis=0)
    nblk = (counts + BM - 1) // BM
    bends = jnp.cumsum(nblk)
    bstarts = bends - nblk
    base = bstarts[None, :] * BM + jnp.cumsum(hist, axis=0) - hist
    hit = ridx.reshape(T // TP, TP, TOP_K, 1) == jnp.arange(N_EXP, dtype=I32)
    first = jnp.sum(jnp.where(hit, base[:, None, None, :], 0), axis=-1)
    dest = (first + rank.reshape(T // TP, TP, TOP_K)).reshape(-1)
    eid = jnp.arange(N_EXP, dtype=I32)
    blk = jnp.arange(N_BLOCKS, dtype=I32)
    blk_expert = jnp.minimum(jnp.sum((blk[:, None] >= bends[None, :]).astype(I32), axis=1), N_EXP - 1)
    of_blk = blk_expert[:, None] == eid[None, :]

    def per_block(v):
        return jnp.sum(jnp.where(of_blk, v[None, :], 0), axis=1)

    left = per_block(counts) - (blk - per_block(bstarts)) * BM
    blk_valid = jnp.where(blk < bends[-1], jnp.clip(left, 0, BM), 0).astype(I32)
    unused = bends[-1] + jnp.arange(N_EXP, dtype=I32)
    zero_blk = jnp.concatenate([jnp.maximum(bends - 1, 0), jnp.minimum(unused, N_BLOCKS - 1)])
    zero_valid = jnp.concatenate([counts % BM != 0, unused < N_BLOCKS])
    prev_expert = jnp.concatenate([jnp.full((1,), -1, I32), blk_expert[:-1]])
    switch = (blk < bends[-1]) & (blk_expert != prev_expert)
    wslot = (jnp.cumsum(switch.astype(I32)) - 1) % 2
    later = (eid[None, :] > eid[:, None]) & (nblk[None, :] > 0)
    next_of = jnp.min(jnp.where(later, eid[None, :], N_EXP), axis=1)
    next_expert = per_block(jnp.where(next_of < N_EXP, next_of, -1))
    return (dest.astype(I32), blk_expert.astype(I32), blk_valid, zero_blk.astype(I32),
            zero_valid.astype(I32), switch.astype(I32), wslot.astype(I32), next_expert.astype(I32))


def kernel(x_prompt, x_sample, cache_ckv, cache_krope, c, c_ctx, w_ada, b_ada, w_in, q_norm_g, w_uq,
           kv_norm_g, w_ukv, gmlp_ln_g, gmlp_ln_b, w_spatial, b_spatial, w_out, ln1_g, ln1_b, w_group,
           b_group, w_router, b_router, w1, w3, w2, ln2_g, ln2_b):
    l = 0
    xc = x_prompt.reshape(T_CTX, D)
    xl = x_sample.reshape(T_LAT, D)

    cond16 = jnp.concatenate([c, c_ctx[None, :], jnp.zeros((16 - N_LAT_B - 1, D), F32)], axis=0)
    mods = _adaln(cond16, w_ada[l], b_ada[l][None, :]).reshape(16, 6, D)

    wi = w_in[l]
    o_kr = QL + KVL
    o_u = o_kr + QK_ROPE
    win_a = wi[:, :o_kr].astype(BF16)
    win_g = wi[:, o_u:].astype(BF16)
    win_r = jnp.pad(wi[:, o_kr:o_u], ((0, 0), (0, LANES - QK_ROPE))).astype(BF16)
    wuq = jnp.pad(w_uq[l].reshape(QL, H, QK_NOPE + QK_ROPE),
                  ((0, 0), (0, 0), (0, HEAD_PAD - QK_NOPE - QK_ROPE))).reshape(QL, H * HEAD_PAD).astype(BF16)
    wkv = w_ukv[l].reshape(KVL, H, QK_NOPE + V_DIM)
    wuk = wkv[:, :, :QK_NOPE].reshape(KVL, H * QK_NOPE).astype(BF16)
    wuvt = wkv[:, :, QK_NOPE:].reshape(KVL, H * V_DIM).T.astype(BF16)
    ws = w_spatial[l].astype(BF16)
    bsb = jnp.broadcast_to(b_spatial[l][:, :, None], (G_GROUPS, CHUNK, G_DIM))
    wrt = jnp.concatenate([w_group[l], w_router[l], jnp.zeros((D, LANES - N_GRP - N_EXP), F32)],
                          axis=1).astype(BF16)
    brt = jnp.concatenate([b_group[l], b_router[l], jnp.zeros((LANES - N_GRP - N_EXP,), F32)])[None, :]
    cos_t, sp_t, sm_t = _rope_tables()
    row = lambda v: v[l][None, :]

    q, k, vt, gm, ckv_n, kr = _proj(xc, xl, mods, win_a, win_g, win_r, row(q_norm_g), wuq, row(kv_norm_g), wuk, wuvt,
                                   row(gmlp_ln_g), row(gmlp_ln_b), ws, bsb, cos_t, sp_t, sm_t)
    kc, vct = _kvexp(cache_ckv[:, l].reshape(N_LAT_B * PAST, KVL),
                     cache_krope[:, l].reshape(N_LAT_B * PAST, QK_ROPE), wuk, wuvt)
    attn_c = _attn_ctx(q, k, vt)
    attn_l = _attn_lat(q, k, vt, kc, vct)

    x1, h2p, ridx, rwt = _post(attn_c, attn_l, gm, xc, xl, mods, w_out[l].astype(BF16), row(ln1_g), row(ln1_b),
                               wrt, brt)
    rank, hist = _rank(ridx)
    dest, blk_expert, blk_valid, zero_blk, zero_valid, switch, wslot, next_expert = _block_plan(
        ridx, rank, hist[:, 0, :N_EXP].astype(I32))
    xs = _dispatch(zero_blk, zero_valid, dest, h2p)
    ys = _experts(blk_expert, blk_valid, switch, wslot, next_expert, xs, w1[l], w3[l], w2[l])
    y_c, y_l = _final(dest.reshape(T // TM, 1, TOP_K * TM), ys, x1, rwt, mods, row(ln2_g), row(ln2_b))

    return (y_c.reshape(N_CTX_B, N_CTX, D), y_l.reshape(N_LAT_B, N_LAT, D),
            ckv_n.reshape(N_CTX_B, 1, N_CTX, KVL), kr.reshape(N_CTX_B, 1, N_CTX, QK_ROPE))
```

```python
import functools

import jax
import jax.numpy as jnp
import numpy as np
from jax import lax
from jax.experimental import pallas as pl
from jax.experimental.pallas import tpu as pltpu

F32 = jnp.float32
BF16 = jnp.bfloat16
U32 = jnp.uint32
I32 = jnp.int32

D = 2048
N_CTX_B, N_CTX = 32, 256
N_LAT_B, N_LAT = 4, 4096
PAST = 512
GRID_W = 64
H = 8
QK_NOPE, QK_ROPE, V_DIM = 128, 64, 128
QL, KVL = 512, 512
GW = 1024
G_GROUPS, G_DIM, CHUNK = 8, 128, 128
N_GRP, E_PER_GRP, N_EXP, TOP_K = 4, 8, 32, 2
E_HID = 512
ROPE_THETA = 10000.0
EPS = 1e-6
ALPHA = 2.0 ** 0.25
SM_SCALE = (QK_NOPE + QK_ROPE) ** -0.5
Q_SCALE = SM_SCALE * float(np.log2(np.e))

T_CTX = N_CTX_B * N_CTX
T_LAT = N_LAT_B * N_LAT
T = T_CTX + T_LAT
N_ASSIGN = T * TOP_K

LANES = 128
HEAD_PAD = 256
V_AUG = V_DIM + 16
TM = 256
TP = 512
POST_SUBS = (256, 256)
SUB = 256
TD = 1024
TQ = 1024
TK = 512
BM = 256
N_BLOCKS = N_ASSIGN // BM + N_EXP
N_ZERO = 2 * N_EXP
VMEM_LIMIT = 56 * 1024 * 1024

NP_TILES = T_CTX // TM
LAT_TILES_PER_B = N_LAT // TM


def _cp(sem):
    return pltpu.CompilerParams(dimension_semantics=sem, vmem_limit_bytes=VMEM_LIMIT)


def _const_spec(shape):
    nd = len(shape)
    return pl.BlockSpec(shape, lambda *a: (0,) * nd, pipeline_mode=pl.Buffered(1))


def _dot(a, b):
    return jnp.dot(a, b, preferred_element_type=F32)


def _dot_nt(a, b):
    return lax.dot_general(a, b, (((1,), (1,)), ((), ())), preferred_element_type=F32)


def _layer_norm(x, g, b):
    mu = jnp.mean(x, -1, keepdims=True)
    xc = x - mu
    var = jnp.mean(xc * xc, -1, keepdims=True)
    return xc * lax.rsqrt(var + EPS) * g + b


def _rms_norm(x, g):
    return x * lax.rsqrt(jnp.mean(x * x, -1, keepdims=True) + EPS) * g


def _gelu(x):
    return 0.5 * x * (1.0 + lax.erf(x * np.float32(np.sqrt(0.5))))


def _pack_pair(lo, hi):
    lo_b = lax.bitcast_convert_type(lo.astype(BF16).astype(F32), U32)
    hi_b = lax.bitcast_convert_type(hi.astype(BF16).astype(F32), U32)
    return hi_b | (lo_b >> 16)


def _unpack_pair(w):
    lo = lax.bitcast_convert_type(w << 16, F32)
    hi = lax.bitcast_convert_type(w & jnp.uint32(0xFFFF0000), F32)
    return lo, hi


RT = 8


def _store_rows_as_tiles(ref, row0, val):
    n = val.shape[0]
    for j in range(RT):
        ref[pl.ds(row0 * RT + j, n, stride=RT), :] = val[:, j * LANES:(j + 1) * LANES]


def _load_tiles_as_rows(ref, row0, n):
    return jnp.concatenate([ref[pl.ds(row0 * RT + j, n, stride=RT), :] for j in range(RT)], axis=1)


def _adaln_kernel(cond_ref, w_ref, b_ref, o_ref):
    s = jax.nn.silu(cond_ref[...])
    s_hi = s.astype(BF16)
    s_lo = (s - s_hi.astype(F32)).astype(BF16)
    w = w_ref[...]
    w_hi = w.astype(BF16)
    w_lo = (w - w_hi.astype(F32)).astype(BF16)
    o_ref[...] = _dot(s_hi, w_hi) + _dot(s_lo, w_hi) + _dot(s_hi, w_lo) + b_ref[...]


def _adaln(cond16, w_ada, b_ada):
    tn = 1024
    n = w_ada.shape[1]
    return pl.pallas_call(
        _adaln_kernel,
        grid=(n // tn,),
        in_specs=[pl.BlockSpec((16, D), lambda j: (0, 0)),
                  pl.BlockSpec((D, tn), lambda j: (0, j)),
                  pl.BlockSpec((1, tn), lambda j: (0, j))],
        out_specs=pl.BlockSpec((16, tn), lambda j: (0, j)),
        out_shape=jax.ShapeDtypeStruct((16, n), F32),
        compiler_params=_cp(("arbitrary",)),
        name="adaln",
    )(cond16, w_ada, b_ada)


def _expand_kv(ckv_n, kr128, wuk_ref, wuvt_ref, k_ref, vt_ref):
    cb = ckv_n.astype(BF16)
    kn = _dot(cb, wuk_ref[...])
    vt = _dot_nt(wuvt_ref[...], cb).astype(BF16)
    ones = jnp.ones((V_AUG - V_DIM, vt.shape[1]), BF16)
    for h in range(H):
        vt_ref[h * V_AUG:h * V_AUG + V_DIM, :] = vt[h * V_DIM:(h + 1) * V_DIM]
        vt_ref[h * V_AUG + V_DIM:(h + 1) * V_AUG, :] = ones
    krb = kr128.astype(BF16)
    for h in range(H):
        k_ref[:, h * HEAD_PAD:h * HEAD_PAD + QK_NOPE] = kn[:, h * QK_NOPE:(h + 1) * QK_NOPE].astype(BF16)
        k_ref[:, h * HEAD_PAD + QK_NOPE:(h + 1) * HEAD_PAD] = krb


def _rope128(x, cos, s_plus, s_minus):
    return (x * cos + pltpu.roll(x, 16, 1) * s_plus + pltpu.roll(x, LANES - 16, 1) * s_minus)


def _proj_kernel(xc_ref, xl_ref, mods_ref, win_a_ref, win_g_ref, win_r_ref, qg_ref, wuq_ref, kvg_ref,
                 wuk_ref, wuv_ref,
                 lng_ref, lnb_ref, ws_ref, bsb_ref, cos_ref, sp_ref, sm_ref,
                 q_ref, k_ref, v_ref, gm_ref, ckv_ref, kr_ref):
    i = pl.program_id(0)
    x = jnp.where(i < NP_TILES, xc_ref[...], xl_ref[...])
    m = mods_ref[0]
    shift1, scale1 = m[0:1], m[1:2]
    hmod = (x * (1.0 + scale1) + shift1).astype(BF16)
    pg = _dot(hmod, win_g_ref[...])
    u, v = pg[:, :GW], pg[:, GW:]
    gu = _gelu(u)
    vn = _layer_norm(_gelu(v), lng_ref[...], lnb_ref[...]).astype(BF16)
    pa = _dot(hmod, win_a_ref[...])
    kr = _dot(hmod, win_r_ref[...])
    cq, ckv = pa[:, :QL], pa[:, QL:]

    cos, s_plus, s_minus = cos_ref[...], sp_ref[...], sm_ref[...]

    cq_n = _rms_norm(cq, qg_ref[...]).astype(BF16)
    q = _dot(cq_n, wuq_ref[...]) * Q_SCALE
    for h in range(H):
        q_ref[:, h * HEAD_PAD:h * HEAD_PAD + QK_NOPE] = q[:, h * HEAD_PAD:h * HEAD_PAD + QK_NOPE].astype(BF16)
        q_ref[:, h * HEAD_PAD + QK_NOPE:(h + 1) * HEAD_PAD] = _rope128(
            q[:, h * HEAD_PAD + QK_NOPE:(h + 1) * HEAD_PAD], cos, s_plus, s_minus).astype(BF16)

    ckv_n = _rms_norm(ckv, kvg_ref[...])
    _expand_kv(ckv_n, _rope128(kr, cos, s_plus, s_minus), wuk_ref, wuv_ref, k_ref, v_ref)

    for c in range(TM // CHUNK):
        rows = slice(c * CHUNK, (c + 1) * CHUNK)
        for g in range(G_GROUPS):
            cols = slice(g * G_DIM, (g + 1) * G_DIM)
            s = _dot(ws_ref[g], vn[rows, cols]) + bsb_ref[g]
            gm_ref[rows, cols] = (gu[rows, cols] * s).astype(BF16)

    @pl.when(i < NP_TILES)
    def _():
        ckv_ref[...] = ckv_n
        kr_ref[...] = kr[:, :QK_ROPE]


def _mods_index(i, tile=TM):
    return jnp.where(i < T_CTX // tile, N_LAT_B, (i - T_CTX // tile) // (N_LAT // tile))


def _proj(xc, xl, mods, win_a, win_g, win_r, qg, wuq, kvg, wuk, wuv, lng, lnb, ws, bsb, cos_t, sp_t, sm_t):
    def tok(width):
        return pl.BlockSpec((TM, width), lambda i: (i, 0))

    def pos_block(i):
        return jnp.where(i < NP_TILES, LAT_TILES_PER_B, (i - NP_TILES) % LAT_TILES_PER_B)

    rope_spec = pl.BlockSpec((TM, LANES), lambda i: (pos_block(i), 0))
    return pl.pallas_call(
        _proj_kernel,
        grid=(T // TM,),
        in_specs=[
            pl.BlockSpec((TM, D), lambda i: (jnp.minimum(i, NP_TILES - 1), 0)),
            pl.BlockSpec((TM, D), lambda i: (jnp.maximum(i - NP_TILES, 0), 0)),
            pl.BlockSpec((1, 6, D), lambda i: (_mods_index(i), 0, 0)),
            _const_spec((D, QL + KVL)), _const_spec((D, 2 * GW)), _const_spec((D, LANES)),
            _const_spec((1, QL)), _const_spec((QL, H * HEAD_PAD)),
            _const_spec((1, KVL)), _const_spec((KVL, H * QK_NOPE)), _const_spec((H * V_DIM, KVL)),
            _const_spec((1, GW)), _const_spec((1, GW)),
            _const_spec((G_GROUPS, CHUNK, CHUNK)), _const_spec((G_GROUPS, CHUNK, G_DIM)),
            rope_spec, rope_spec, rope_spec,
        ],
        out_specs=[tok(H * HEAD_PAD), tok(H * HEAD_PAD),
                   pl.BlockSpec((H * V_AUG, TM), lambda i: (0, i)), tok(GW),
                   pl.BlockSpec((TM, KVL), lambda i: (jnp.minimum(i, NP_TILES - 1), 0)),
                   pl.BlockSpec((TM, QK_ROPE), lambda i: (jnp.minimum(i, NP_TILES - 1), 0))],
        out_shape=[
            jax.ShapeDtypeStruct((T, H * HEAD_PAD), BF16),
            jax.ShapeDtypeStruct((T, H * HEAD_PAD), BF16),
            jax.ShapeDtypeStruct((H * V_AUG, T), BF16),
            jax.ShapeDtypeStruct((T, GW), BF16),
            jax.ShapeDtypeStruct((T_CTX, KVL), F32),
            jax.ShapeDtypeStruct((T_CTX, QK_ROPE), F32),
        ],
        compiler_params=_cp(("arbitrary",)),
        name="proj",
    )(xc, xl, mods, win_a, win_g, win_r, qg, wuq, kvg, wuk, wuv, lng, lnb, ws, bsb, cos_t, sp_t, sm_t)


def _kvexp_kernel(ckv_ref, kr_ref, wuk_ref, wuv_ref, k_ref, v_ref):
    kr = kr_ref[...]
    kr128 = jnp.concatenate([kr, jnp.zeros_like(kr)], axis=1)
    _expand_kv(ckv_ref[...], kr128, wuk_ref, wuv_ref, k_ref, v_ref)


def _kvexp(ckv, kr, wuk, wuv):
    rows = ckv.shape[0]
    return pl.pallas_call(
        _kvexp_kernel,
        grid=(rows // TM,),
        in_specs=[pl.BlockSpec((TM, KVL), lambda i: (i, 0)), pl.BlockSpec((TM, QK_ROPE), lambda i: (i, 0)),
                  _const_spec((KVL, H * QK_NOPE)), _const_spec((H * V_DIM, KVL))],
        out_specs=[pl.BlockSpec((TM, H * HEAD_PAD), lambda i: (i, 0)),
                   pl.BlockSpec((H * V_AUG, TM), lambda i: (0, i))],
        out_shape=[jax.ShapeDtypeStruct((rows, H * HEAD_PAD), BF16),
                   jax.ShapeDtypeStruct((H * V_AUG, rows), BF16)],
        compiler_params=_cp(("arbitrary",)),
        name="kvexp",
    )(ckv, kr, wuk, wuv)


def _probs(s, m):
    m_new = jnp.maximum(m, jnp.max(s, 0, keepdims=True))
    return m_new, jnp.exp2(m - m_new), jnp.exp2(s - m_new).astype(BF16)


def _attn_finish(acc):
    return (acc[:V_DIM] / acc[V_DIM:V_DIM + 1]).T.astype(BF16)


def _attn_ctx_kernel(q_ref, k_ref, vt_ref, o_ref):
    m0 = jnp.full((1, N_CTX), -jnp.inf, F32)
    for h in range(H):
        qh = q_ref[:, h * HEAD_PAD:(h + 1) * HEAD_PAD]
        kh = k_ref[:, h * HEAD_PAD:(h + 1) * HEAD_PAD]
        _, _, p = _probs(_dot_nt(kh, qh), m0)
        acc = _dot(vt_ref[h * V_AUG:(h + 1) * V_AUG, :], p)
        o_ref[:, h * V_DIM:(h + 1) * V_DIM] = _attn_finish(acc)


def _attn_ctx(q, k, vt):
    return pl.pallas_call(
        _attn_ctx_kernel,
        grid=(N_CTX_B,),
        in_specs=[pl.BlockSpec((N_CTX, H * HEAD_PAD), lambda b: (b, 0)),
                  pl.BlockSpec((N_CTX, H * HEAD_PAD), lambda b: (b, 0)),
                  pl.BlockSpec((H * V_AUG, N_CTX), lambda b: (0, b))],
        out_specs=pl.BlockSpec((N_CTX, H * V_DIM), lambda b: (b, 0)),
        out_shape=jax.ShapeDtypeStruct((T_CTX, H * V_DIM), BF16),
        compiler_params=_cp(("arbitrary",)),
        name="attn_ctx",
    )(q, k, vt)


def _attn_lat_kernel(q_ref, k_ref, vt_ref, kc_ref, vct_ref, o_ref):
    q = q_ref[...]
    n_lat = N_LAT // TK

    def keys(c):
        return k_ref[c * TK:(c + 1) * TK, :] if c < n_lat else kc_ref[...]

    def values_t(c):
        return vt_ref[:, c * TK:(c + 1) * TK] if c < n_lat else vct_ref[...]

    m = jnp.full((1, TQ), -jnp.inf, F32)
    acc = jnp.zeros((V_AUG, TQ), F32)
    s_next = _dot_nt(keys(0), q)
    m_next = jnp.maximum(m, jnp.max(s_next, 0, keepdims=True))
    p_prev = alpha_prev = None
    for c in range(n_lat + 1):
        s, m_new = s_next, m_next
        if c < n_lat:
            s_next = _dot_nt(keys(c + 1), q)
            m_next = jnp.maximum(m_new, jnp.max(s_next, 0, keepdims=True))
        if c > 0:
            acc = alpha_prev * acc + _dot(values_t(c - 1), p_prev)
        alpha_prev = jnp.exp2(m - m_new)
        p_prev = jnp.exp2(s - m_new).astype(BF16)
        m = m_new
    acc = alpha_prev * acc + _dot(values_t(n_lat), p_prev)
    o_ref[...] = _attn_finish(acc)


def _attn_lat(q, k, vt, kc, vct):
    qb0 = T_CTX // TQ
    qpb = N_LAT // TQ
    kb0 = T_CTX // N_LAT
    return pl.pallas_call(
        _attn_lat_kernel,
        grid=(N_LAT_B, H, qpb),
        in_specs=[pl.BlockSpec((TQ, HEAD_PAD), lambda b, h, qi: (qb0 + b * qpb + qi, h)),
                  pl.BlockSpec((N_LAT, HEAD_PAD), lambda b, h, qi: (kb0 + b, h)),
                  pl.BlockSpec((V_AUG, N_LAT), lambda b, h, qi: (h, kb0 + b)),
                  pl.BlockSpec((PAST, HEAD_PAD), lambda b, h, qi: (b, h)),
                  pl.BlockSpec((V_AUG, PAST), lambda b, h, qi: (h, b))],
        out_specs=pl.BlockSpec((TQ, V_DIM), lambda b, h, qi: (b * qpb + qi, h)),
        out_shape=jax.ShapeDtypeStruct((T_LAT, H * V_DIM), BF16),
        compiler_params=_cp(("arbitrary", "arbitrary", "arbitrary")),
        name="attn_lat",
    )(q, k, vt, kc, vct)


def _route(logits):
    lane_i = lax.broadcasted_iota(I32, logits.shape, 1)
    lane = lane_i.astype(F32)
    neg = jnp.float32(-jnp.inf)
    far = jnp.float32(LANES)
    gl = jnp.where(lane_i < N_GRP, logits, neg)
    gmax = jnp.max(gl, -1, keepdims=True)
    gidx = jnp.min(jnp.where(gl == gmax, lane, far), -1, keepdims=True)
    p_top = 1.0 / jnp.sum(jnp.exp(gl - gmax), -1, keepdims=True)
    grp_of_lane = ((lane_i - N_GRP) >> 3).astype(F32)
    in_grp = (lane_i >= N_GRP) & (lane_i < N_GRP + N_EXP) & (grp_of_lane == gidx)
    el = jnp.where(in_grp, logits, neg)
    m1 = jnp.max(el, -1, keepdims=True)
    i1 = jnp.min(jnp.where(el == m1, lane, far), -1, keepdims=True)
    el2 = jnp.where(lane == i1, neg, el)
    m2 = jnp.max(el2, -1, keepdims=True)
    i2 = jnp.min(jnp.where(el2 == m2, lane, far), -1, keepdims=True)
    e2 = jnp.exp(m2 - m1)
    w1 = p_top / (1.0 + e2)
    w2 = p_top * e2 / (1.0 + e2)
    return i1 - N_GRP, i2 - N_GRP, w1, w2


def _pair(a, b, dtype):
    two = lax.broadcasted_iota(I32, (a.shape[0], TOP_K), 1)
    return jnp.where(two == 0, a, b).astype(dtype)


def _post_kernel(ac_ref, al_ref, gm_ref, xc_ref, xl_ref, mods_ref, wout_ref, g1_ref, b1_ref, wrt_ref,
                 brt_ref, x1_ref, h2p_ref, ridx_ref, rwt_ref):
    i = pl.program_id(0)
    is_ctx = i < T_CTX // TP
    m = mods_ref[0]
    gate1, shift2, scale2 = m[2:3], m[3:4], m[4:5]

    def mix_of(rows):
        attn = jnp.where(is_ctx, ac_ref[rows, :], al_ref[rows, :])
        return _dot(jnp.concatenate([attn, gm_ref[rows, :]], axis=1), wout_ref[...])

    bounds = np.cumsum((0,) + POST_SUBS)
    subs = [slice(int(a), int(b)) for a, b in zip(bounds[:-1], bounds[1:])]
    mix_next = mix_of(subs[0])
    for s, rows in enumerate(subs):
        mix = mix_next
        if s + 1 < len(subs):
            mix_next = mix_of(subs[s + 1])
        x = jnp.where(is_ctx, xc_ref[rows, :], xl_ref[rows, :])
        x1 = _layer_norm(ALPHA * x + gate1 * mix, g1_ref[...], b1_ref[...])
        x1_ref[rows, :] = x1
        h2 = x1 * (1.0 + scale2) + shift2
        _store_rows_as_tiles(h2p_ref, rows.start, _pack_pair(h2[:, :D // 2], h2[:, D // 2:]))
        e1, e2, w1, w2 = _route(_dot(h2.astype(BF16), wrt_ref[...]) + brt_ref[...])
        ridx_ref[rows, :] = _pair(e1, e2, I32)
        rwt_ref[rows, :] = _pair(w1, w2, F32)


def _post(attn_c, attn_l, gm, xc, xl, mods, wout, g1, b1, wrt, brt):
    npt = T_CTX // TP

    def tok(width):
        return pl.BlockSpec((TP, width), lambda i: (i, 0))

    def ctx(width):
        return pl.BlockSpec((TP, width), lambda i: (jnp.minimum(i, npt - 1), 0))

    def lat(width):
        return pl.BlockSpec((TP, width), lambda i: (jnp.maximum(i - npt, 0), 0))

    return pl.pallas_call(
        _post_kernel,
        grid=(T // TP,),
        in_specs=[ctx(H * V_DIM), lat(H * V_DIM), tok(GW), ctx(D), lat(D),
                  pl.BlockSpec((1, 6, D), lambda i: (_mods_index(i, TP), 0, 0)),
                  _const_spec((D, D)), _const_spec((1, D)), _const_spec((1, D)),
                  _const_spec((D, LANES)), _const_spec((1, LANES))],
        out_specs=[tok(D), pl.BlockSpec((TP * RT, LANES), lambda i: (i, 0)), tok(TOP_K), tok(TOP_K)],
        out_shape=[jax.ShapeDtypeStruct((T, D), F32),
                   jax.ShapeDtypeStruct((T * RT, LANES), U32),
                   jax.ShapeDtypeStruct((T, TOP_K), I32),
                   jax.ShapeDtypeStruct((T, TOP_K), F32)],
        compiler_params=_cp(("arbitrary",)),
        name="post",
    )(attn_c, attn_l, gm, xc, xl, mods, wout, g1, b1, wrt, brt)


def _as_lane_row(col):
    return jnp.broadcast_to(col, (col.shape[0], LANES)).T[0:1, :]


def _rank_kernel(ridx_ref, tri_ref, e_ref, rank_ref, hist_ref):
    lane = lax.broadcasted_iota(I32, (SUB, LANES), 1).astype(F32)
    seen = jnp.zeros((1, LANES), F32)
    for s in range(TP // SUB):
        rows = slice(s * SUB, (s + 1) * SUB)
        e = ridx_ref[rows, :].astype(F32)
        ohs = [(lane == e[:, k:k + 1]).astype(F32) for k in range(TOP_K)]
        both = ohs[0] + ohs[1]
        before = _dot(tri_ref[...], both.astype(BF16)) + seen
        for k in range(TOP_K):
            rank_k = jnp.sum(before * ohs[k], -1, keepdims=True)
            e_ref[0, k:k + 1, rows] = _as_lane_row(e[:, k:k + 1]).astype(I32)
            rank_ref[0, k:k + 1, rows] = _as_lane_row(rank_k).astype(I32)
        seen = seen + jnp.sum(both, 0, keepdims=True)
    hist_ref[0] = jnp.broadcast_to(seen, (8, LANES))


def _rank(ridx):
    dense = pl.BlockSpec((1, TOP_K, TP), lambda i: (i, 0, 0))
    return pl.pallas_call(
        _rank_kernel,
        grid=(T // TP,),
        in_specs=[pl.BlockSpec((TP, TOP_K), lambda i: (i, 0)), _const_spec((SUB, SUB))],
        out_specs=[dense, dense, pl.BlockSpec((1, 8, LANES), lambda i: (i, 0, 0))],
        out_shape=[jax.ShapeDtypeStruct((T // TP, TOP_K, TP), I32),
                   jax.ShapeDtypeStruct((T // TP, TOP_K, TP), I32),
                   jax.ShapeDtypeStruct((T // TP, 8, LANES), F32)],
        compiler_params=_cp(("arbitrary",)),
        name="rank",
    )(ridx, jnp.asarray(np.tril(np.ones((SUB, SUB), np.float32), -1), BF16))


def _row_copy(src, src_row, dst, dst_row, sem):
    def tile(ref, row):
        return ref.at[pl.ds(pl.multiple_of(row * RT, RT), RT)]

    return pltpu.make_async_copy(tile(src, src_row), tile(dst, dst_row), sem)


def _dispatch_kernel(zb_ref, zv_ref, dest_ref, h2p_ref, xs_ref, zbuf, sem):
    i = pl.program_id(0)

    @pl.when(i == 0)
    def _():
        zbuf[...] = jnp.zeros_like(zbuf)

        def zero_block(j):
            start = pl.multiple_of(zb_ref[j] * (BM * RT), BM * RT)
            return pltpu.make_async_copy(zbuf, xs_ref.at[pl.ds(start, BM * RT)], sem)

        def start(j, carry):
            @pl.when(zv_ref[j] != 0)
            def _():
                zero_block(j).start()
            return carry

        def wait(j, carry):
            @pl.when(zv_ref[j] != 0)
            def _():
                zero_block(j).wait()
            return carry

        lax.fori_loop(0, N_ZERO, start, 0)
        lax.fori_loop(0, N_ZERO, wait, 0)

    def issue(r, carry):
        for k in range(TOP_K):
            _row_copy(h2p_ref, r, xs_ref, dest_ref[0, k, r], sem).start(priority=k)
        return carry

    lax.fori_loop(0, TD, issue, 0, unroll=8)
    for _ in range(TOP_K):
        pltpu.make_async_copy(h2p_ref, xs_ref.at[pl.ds(0, TD * RT)], sem).wait()


def _regroup_steps(dest, rows):
    if rows >= TP:
        g = rows // TP
        return dest.reshape(T // rows, g, TOP_K, TP).transpose(0, 2, 1, 3).reshape(T // rows, TOP_K, rows)
    g = TP // rows
    return dest.reshape(T // TP, TOP_K, g, rows).transpose(0, 2, 1, 3).reshape(T // rows, TOP_K, rows)


def _dispatch(zero_blk, zero_valid, dest, h2p):
    grid_spec = pltpu.PrefetchScalarGridSpec(
        num_scalar_prefetch=2,
        grid=(T // TD,),
        in_specs=[pl.BlockSpec((1, TOP_K, TD), lambda i, zb, zv: (i, 0, 0), memory_space=pltpu.SMEM),
                  pl.BlockSpec((TD * RT, LANES), lambda i, zb, zv: (i, 0))],
        out_specs=pl.BlockSpec(memory_space=pl.ANY),
        scratch_shapes=[pltpu.VMEM((BM * RT, LANES), U32), pltpu.SemaphoreType.DMA(())],
    )
    return pl.pallas_call(
        _dispatch_kernel,
        grid_spec=grid_spec,
        out_shape=jax.ShapeDtypeStruct((N_BLOCKS * BM * RT, LANES), U32),
        compiler_params=_cp(("arbitrary",)),
        name="dispatch",
    )(zero_blk, zero_valid, _regroup_steps(dest, TD), h2p)


def _experts_kernel(be_ref, bv_ref, sw_ref, es_ref, nx_ref, x_ref, w1_hbm, w3_hbm, w2_hbm, y_ref,
                    wf1, wf3, wf2, w1b, w3b, w2b, sems):
    i = pl.program_id(0)
    e = be_ref[i]
    nv = bv_ref[i]

    def weight_copies(expert, s):
        return (pltpu.make_async_copy(w1_hbm.at[expert], wf1.at[s], sems.at[s]),
                pltpu.make_async_copy(w3_hbm.at[expert], wf3.at[s], sems.at[s]),
                pltpu.make_async_copy(w2_hbm.at[expert], wf2.at[s], sems.at[s]))

    @pl.when(sw_ref[i] != 0)
    def _():
        s = es_ref[i]

        @pl.when(i == 0)
        def _():
            for c in weight_copies(e, s):
                c.start()

        for c in weight_copies(e, s):
            c.wait()
        nxt = nx_ref[i]

        @pl.when(nxt >= 0)
        def _():
            for c in weight_copies(nxt, 1 - s):
                c.start()

        w1b[...] = wf1[s].astype(BF16)
        w3b[...] = wf3[s].astype(BF16)
        w2b[...] = wf2[s].astype(BF16)

    @pl.when(nv > 0)
    def _():
        lo, hi = _unpack_pair(_load_tiles_as_rows(x_ref, 0, BM))
        xb = jnp.concatenate([lo.astype(BF16), hi.astype(BF16)], axis=1)
        a = _dot(xb, w1b[...])
        b = _dot(xb, w3b[...])
        hid = (jax.nn.silu(a) * b).astype(BF16)
        y = _dot(hid, w2b[...])
        _store_rows_as_tiles(y_ref, 0, _pack_pair(y[:, :D // 2], y[:, D // 2:]))

    @pl.when(nv == 0)
    def _():
        y_ref[...] = jnp.zeros_like(y_ref)


def _experts(blk_expert, blk_valid, switch, wslot, next_expert, xs, w1, w3, w2):
    hbm = pl.BlockSpec(memory_space=pl.ANY)
    grid_spec = pltpu.PrefetchScalarGridSpec(
        num_scalar_prefetch=5,
        grid=(N_BLOCKS,),
        in_specs=[pl.BlockSpec((BM * RT, LANES), lambda i, *_: (i, 0)), hbm, hbm, hbm],
        out_specs=pl.BlockSpec((BM * RT, LANES), lambda i, *_: (i, 0)),
        scratch_shapes=[pltpu.VMEM((2, D, E_HID), F32), pltpu.VMEM((2, D, E_HID), F32),
                        pltpu.VMEM((2, E_HID, D), F32),
                        pltpu.VMEM((D, E_HID), BF16), pltpu.VMEM((D, E_HID), BF16),
                        pltpu.VMEM((E_HID, D), BF16), pltpu.SemaphoreType.DMA((2,))],
    )
    return pl.pallas_call(
        _experts_kernel,
        grid_spec=grid_spec,
        out_shape=jax.ShapeDtypeStruct((N_BLOCKS * BM * RT, LANES), U32),
        compiler_params=_cp(("arbitrary",)),
        name="experts",
    )(blk_expert, blk_valid, switch, wslot, next_expert, xs, w1, w3, w2)


def _final_kernel(dcur_ref, dnext_ref, ys_ref, x1_ref, rwt_ref, mods_ref, g2_ref, b2_ref, oc_ref, ol_ref,
                  buf, sems):
    i = pl.program_id(0)
    n = pl.num_programs(0)
    slot = i % 2

    def first_row(s, k):
        return (s * TOP_K + k) * TM

    def issue(d_ref, s):
        def body(r, carry):
            for k in range(TOP_K):
                _row_copy(ys_ref, d_ref[0, k, r], buf, first_row(s, k) + r, sems.at[s]).start()
            return carry

        lax.fori_loop(0, TM, body, 0, unroll=8)

    @pl.when(i == 0)
    def _():
        issue(dcur_ref, 0)

    @pl.when(i + 1 < n)
    def _():
        issue(dnext_ref, 1 - slot)

    for k in range(TOP_K):
        pltpu.make_async_copy(ys_ref.at[pl.ds(0, TM * RT)], buf.at[pl.ds(0, TM * RT)], sems.at[slot]).wait()

    w = rwt_ref[...]
    lo0, hi0 = _unpack_pair(_load_tiles_as_rows(buf, first_row(slot, 0), TM))
    lo1, hi1 = _unpack_pair(_load_tiles_as_rows(buf, first_row(slot, 1), TM))
    w0, w1 = w[:, 0:1], w[:, 1:2]
    moe = jnp.concatenate([lo0 * w0 + lo1 * w1, hi0 * w0 + hi1 * w1], axis=1)
    gate2 = mods_ref[0][5:6]
    out = _layer_norm(ALPHA * x1_ref[...] + gate2 * moe, g2_ref[...], b2_ref[...])

    @pl.when(i < NP_TILES)
    def _():
        oc_ref[...] = out

    @pl.when(i >= NP_TILES)
    def _():
        ol_ref[...] = out


def _final(dest3, ys, x1, rwt, mods, g2, b2):
    n = T // TM
    return pl.pallas_call(
        _final_kernel,
        grid=(n,),
        in_specs=[pl.BlockSpec((1, TOP_K, TM), lambda i: (i, 0, 0), memory_space=pltpu.SMEM),
                  pl.BlockSpec((1, TOP_K, TM), lambda i: (jnp.minimum(i + 1, n - 1), 0, 0),
                               memory_space=pltpu.SMEM),
                  pl.BlockSpec(memory_space=pl.ANY),
                  pl.BlockSpec((TM, D), lambda i: (i, 0)),
                  pl.BlockSpec((TM, TOP_K), lambda i: (i, 0)),
                  pl.BlockSpec((1, 6, D), lambda i: (_mods_index(i), 0, 0)),
                  _const_spec((1, D)), _const_spec((1, D))],
        out_specs=[pl.BlockSpec((TM, D), lambda i: (jnp.minimum(i, NP_TILES - 1), 0)),
                   pl.BlockSpec((TM, D), lambda i: (jnp.maximum(i - NP_TILES, 0), 0))],
        out_shape=[jax.ShapeDtypeStruct((T_CTX, D), F32), jax.ShapeDtypeStruct((T_LAT, D), F32)],
        scratch_shapes=[pltpu.VMEM((2 * TOP_K * TM * RT, LANES), U32), pltpu.SemaphoreType.DMA((2,))],
        compiler_params=_cp(("arbitrary",)),
        name="final",
    )(dest3, dest3, ys, x1, rwt, mods, g2, b2)


def _rope_tables():
    n = N_LAT
    rows = n // GRID_W
    f32 = np.float32
    row = np.repeat(np.arange(rows, dtype=f32), GRID_W)
    col = np.tile(np.arange(GRID_W, dtype=f32), rows)
    half = QK_ROPE // 2
    inv = (f32(ROPE_THETA) ** (-np.arange(0, half, 2, dtype=f32) / f32(half))).astype(f32)
    ang_r, ang_c = (row[:, None] * inv).astype(f32), (col[:, None] * inv).astype(f32)
    cr, sr = np.cos(ang_r).astype(f32), np.sin(ang_r).astype(f32)
    cc, sc = np.cos(ang_c).astype(f32), np.sin(ang_c).astype(f32)
    z16 = np.zeros_like(cr)
    pad = lambda v: np.full((n, LANES - QK_ROPE), v, f32)
    cos = np.concatenate([cr, cr, cc, cc, pad(1.0)], axis=1)
    s_plus = np.concatenate([z16, sr, z16, sc, pad(0.0)], axis=1)
    s_minus = np.concatenate([-sr, z16, -sc, z16, pad(0.0)], axis=1)
    ident = lambda v: np.full((TM, LANES), v, f32)
    return (jnp.asarray(np.concatenate([cos, ident(1.0)], 0)),
            jnp.asarray(np.concatenate([s_plus, ident(0.0)], 0)),
            jnp.asarray(np.concatenate([s_minus, ident(0.0)], 0)))


def _block_plan(eid_k, rank, hist):
    counts = jnp.sum(hist, axis=0)
    nblk = (counts + BM - 1) // BM
    bends = jnp.cumsum(nblk)
    bstarts = bends - nblk
    base = bstarts[None, :] * BM + jnp.cumsum(hist, axis=0) - hist
    eid = jnp.arange(N_EXP, dtype=I32)
    hit = eid_k[:, :, None, :] == eid[None, None, :, None]
    first = jnp.sum(jnp.where(hit, base[:, None, :, None], 0), axis=2)
    dest = (first + rank).astype(I32)
    blk = jnp.arange(N_BLOCKS, dtype=I32)
    blk_expert = jnp.minimum(jnp.sum((blk[:, None] >= bends[None, :]).astype(I32), axis=1), N_EXP - 1)
    of_blk = blk_expert[:, None] == eid[None, :]

    def per_block(v):
        return jnp.sum(jnp.where(of_blk, v[None, :], 0), axis=1)

    left = per_block(counts) - (blk - per_block(bstarts)) * BM
    blk_valid = jnp.where(blk < bends[-1], jnp.clip(left, 0, BM), 0).astype(I32)
    unused = bends[-1] + jnp.arange(N_EXP, dtype=I32)
    zero_blk = jnp.concatenate([jnp.maximum(bends - 1, 0), jnp.minimum(unused, N_BLOCKS - 1)])
    zero_valid = jnp.concatenate([counts % BM != 0, unused < N_BLOCKS])
    prev_expert = jnp.concatenate([jnp.full((1,), -1, I32), blk_expert[:-1]])
    switch = (blk < bends[-1]) & (blk_expert != prev_expert)
    wslot = (jnp.cumsum(switch.astype(I32)) - 1) % 2
    later = (eid[None, :] > eid[:, None]) & (nblk[None, :] > 0)
    next_of = jnp.min(jnp.where(later, eid[None, :], N_EXP), axis=1)
    next_expert = per_block(jnp.where(next_of < N_EXP, next_of, -1))
    return (dest, blk_expert.astype(I32), blk_valid, zero_blk.astype(I32),
            zero_valid.astype(I32), switch.astype(I32), wslot.astype(I32), next_expert.astype(I32))


def kernel(x_prompt, x_sample, cache_ckv, cache_krope, c, c_ctx, w_ada, b_ada, w_in, q_norm_g, w_uq,
           kv_norm_g, w_ukv, gmlp_ln_g, gmlp_ln_b, w_spatial, b_spatial, w_out, ln1_g, ln1_b, w_group,
           b_group, w_router, b_router, w1, w3, w2, ln2_g, ln2_b):
    l = 0
    xc = x_prompt.reshape(T_CTX, D)
    xl = x_sample.reshape(T_LAT, D)

    cond16 = jnp.concatenate([c, c_ctx[None, :], jnp.zeros((16 - N_LAT_B - 1, D), F32)], axis=0)
    mods = _adaln(cond16, w_ada[l], b_ada[l][None, :]).reshape(16, 6, D)

    wi = w_in[l]
    o_kr = QL + KVL
    o_u = o_kr + QK_ROPE
    win_a = wi[:, :o_kr].astype(BF16)
    win_g = wi[:, o_u:].astype(BF16)
    win_r = jnp.pad(wi[:, o_kr:o_u], ((0, 0), (0, LANES - QK_ROPE))).astype(BF16)
    wuq = jnp.pad(w_uq[l].reshape(QL, H, QK_NOPE + QK_ROPE),
                  ((0, 0), (0, 0), (0, HEAD_PAD - QK_NOPE - QK_ROPE))).reshape(QL, H * HEAD_PAD).astype(BF16)
    wkv = w_ukv[l].reshape(KVL, H, QK_NOPE + V_DIM)
    wuk = wkv[:, :, :QK_NOPE].reshape(KVL, H * QK_NOPE).astype(BF16)
    wuvt = wkv[:, :, QK_NOPE:].reshape(KVL, H * V_DIM).T.astype(BF16)
    ws = w_spatial[l].astype(BF16)
    bsb = jnp.broadcast_to(b_spatial[l][:, :, None], (G_GROUPS, CHUNK, G_DIM))
    wrt = jnp.concatenate([w_group[l], w_router[l], jnp.zeros((D, LANES - N_GRP - N_EXP), F32)],
                          axis=1).astype(BF16)
    brt = jnp.concatenate([b_group[l], b_router[l], jnp.zeros((LANES - N_GRP - N_EXP,), F32)])[None, :]
    cos_t, sp_t, sm_t = _rope_tables()
    row = lambda v: v[l][None, :]

    q, k, vt, gm, ckv_n, kr = _proj(xc, xl, mods, win_a, win_g, win_r, row(q_norm_g), wuq, row(kv_norm_g), wuk, wuvt,
                                   row(gmlp_ln_g), row(gmlp_ln_b), ws, bsb, cos_t, sp_t, sm_t)
    kc, vct = _kvexp(cache_ckv[:, l].reshape(N_LAT_B * PAST, KVL),
                     cache_krope[:, l].reshape(N_LAT_B * PAST, QK_ROPE), wuk, wuvt)
    attn_c = _attn_ctx(q, k, vt)
    attn_l = _attn_lat(q, k, vt, kc, vct)

    x1, h2p, ridx, rwt = _post(attn_c, attn_l, gm, xc, xl, mods, w_out[l].astype(BF16), row(ln1_g), row(ln1_b),
                               wrt, brt)
    eid_k, rank, hist = _rank(ridx)
    dest, blk_expert, blk_valid, zero_blk, zero_valid, switch, wslot, next_expert = _block_plan(
        eid_k, rank, hist[:, 0, :N_EXP].astype(I32))
    xs = _dispatch(zero_blk, zero_valid, dest, h2p)
    ys = _experts(blk_expert, blk_valid, switch, wslot, next_expert, xs, w1[l], w3[l], w2[l])
    y_c, y_l = _final(_regroup_steps(dest, TM), ys, x1, rwt, mods, row(ln2_g), row(ln2_b))

    return (y_c.reshape(N_CTX_B, N_CTX, D), y_l.reshape(N_LAT_B, N_LAT, D),
            ckv_n.reshape(N_CTX_B, 1, N_CTX, KVL), kr.reshape(N_CTX_B, 1, N_CTX, QK_ROPE))
```

```python
import functools

import jax
import jax.numpy as jnp
import numpy as np
from jax import lax
from jax.experimental import pallas as pl
from jax.experimental.pallas import tpu as pltpu

F32 = jnp.float32
BF16 = jnp.bfloat16
U32 = jnp.uint32
I32 = jnp.int32

D = 2048
N_CTX_B, N_CTX = 32, 256
N_LAT_B, N_LAT = 4, 4096
PAST = 512
GRID_W = 64
H = 8
QK_NOPE, QK_ROPE, V_DIM = 128, 64, 128
QL, KVL = 512, 512
GW = 1024
G_GROUPS, G_DIM, CHUNK = 8, 128, 128
N_GRP, E_PER_GRP, N_EXP, TOP_K = 4, 8, 32, 2
E_HID = 512
ROPE_THETA = 10000.0
EPS = 1e-6
ALPHA = 2.0 ** 0.25
SM_SCALE = (QK_NOPE + QK_ROPE) ** -0.5
Q_SCALE = SM_SCALE * float(np.log2(np.e))

T_CTX = N_CTX_B * N_CTX
T_LAT = N_LAT_B * N_LAT
T = T_CTX + T_LAT
N_ASSIGN = T * TOP_K

LANES = 128
HEAD_PAD = 256
V_AUG = V_DIM + 16
TM = 256
TP = 512
POST_SUBS = (256, 256)
SUB = 256
TD = 1024
TQ = 1024
TK = 512
BM = 256
N_BLOCKS = N_ASSIGN // BM + N_EXP
N_ZERO = 2 * N_EXP
VMEM_LIMIT = 56 * 1024 * 1024

NP_TILES = T_CTX // TM
LAT_TILES_PER_B = N_LAT // TM


def _cp(sem):
    return pltpu.CompilerParams(dimension_semantics=sem, vmem_limit_bytes=VMEM_LIMIT)


def _const_spec(shape):
    nd = len(shape)
    return pl.BlockSpec(shape, lambda *a: (0,) * nd, pipeline_mode=pl.Buffered(1))


def _dot(a, b):
    return jnp.dot(a, b, preferred_element_type=F32)


def _dot_nt(a, b):
    return lax.dot_general(a, b, (((1,), (1,)), ((), ())), preferred_element_type=F32)


def _layer_norm(x, g, b):
    mu = jnp.mean(x, -1, keepdims=True)
    xc = x - mu
    var = jnp.mean(xc * xc, -1, keepdims=True)
    return xc * lax.rsqrt(var + EPS) * g + b


def _rms_norm(x, g):
    return x * lax.rsqrt(jnp.mean(x * x, -1, keepdims=True) + EPS) * g


def _gelu(x):
    return 0.5 * x * (1.0 + lax.erf(x * np.float32(np.sqrt(0.5))))


def _pack_pair(lo, hi):
    lo_b = lax.bitcast_convert_type(lo.astype(BF16).astype(F32), U32)
    hi_b = lax.bitcast_convert_type(hi.astype(BF16).astype(F32), U32)
    return hi_b | (lo_b >> 16)


def _unpack_pair(w):
    lo = lax.bitcast_convert_type(w << 16, F32)
    hi = lax.bitcast_convert_type(w & jnp.uint32(0xFFFF0000), F32)
    return lo, hi


RT = 8


def _store_rows_as_tiles(ref, row0, val):
    n = val.shape[0]
    for j in range(RT):
        ref[pl.ds(row0 * RT + j, n, stride=RT), :] = val[:, j * LANES:(j + 1) * LANES]


def _load_tiles_as_rows(ref, row0, n):
    return jnp.concatenate([ref[pl.ds(row0 * RT + j, n, stride=RT), :] for j in range(RT)], axis=1)


def _adaln_kernel(cond_ref, w_ref, b_ref, o_ref):
    s = jax.nn.silu(cond_ref[...])
    s_hi = s.astype(BF16)
    s_lo = (s - s_hi.astype(F32)).astype(BF16)
    w = w_ref[...]
    w_hi = w.astype(BF16)
    w_lo = (w - w_hi.astype(F32)).astype(BF16)
    o_ref[...] = _dot(s_hi, w_hi) + _dot(s_lo, w_hi) + _dot(s_hi, w_lo) + b_ref[...]


def _adaln(cond16, w_ada, b_ada):
    tn = 1024
    n = w_ada.shape[1]
    return pl.pallas_call(
        _adaln_kernel,
        grid=(n // tn,),
        in_specs=[pl.BlockSpec((16, D), lambda j: (0, 0)),
                  pl.BlockSpec((D, tn), lambda j: (0, j)),
                  pl.BlockSpec((1, tn), lambda j: (0, j))],
        out_specs=pl.BlockSpec((16, tn), lambda j: (0, j)),
        out_shape=jax.ShapeDtypeStruct((16, n), F32),
        compiler_params=_cp(("arbitrary",)),
        name="adaln",
    )(cond16, w_ada, b_ada)


def _expand_kv(ckv_n, kr128, wuk_ref, wuvt_ref, k_ref, vt_ref):
    cb = ckv_n.astype(BF16)
    kn = _dot(cb, wuk_ref[...])
    vt = _dot_nt(wuvt_ref[...], cb).astype(BF16)
    ones = jnp.ones((V_AUG - V_DIM, vt.shape[1]), BF16)
    for h in range(H):
        vt_ref[h * V_AUG:h * V_AUG + V_DIM, :] = vt[h * V_DIM:(h + 1) * V_DIM]
        vt_ref[h * V_AUG + V_DIM:(h + 1) * V_AUG, :] = ones
    krb = kr128.astype(BF16)
    for h in range(H):
        k_ref[:, h * HEAD_PAD:h * HEAD_PAD + QK_NOPE] = kn[:, h * QK_NOPE:(h + 1) * QK_NOPE].astype(BF16)
        k_ref[:, h * HEAD_PAD + QK_NOPE:(h + 1) * HEAD_PAD] = krb


def _rope128(x, cos, s_plus, s_minus):
    return (x * cos + pltpu.roll(x, 16, 1) * s_plus + pltpu.roll(x, LANES - 16, 1) * s_minus)


def _proj_kernel(xc_ref, xl_ref, mods_ref, win_a_ref, win_g_ref, win_r_ref, qg_ref, wuq_ref, kvg_ref,
                 wuk_ref, wuv_ref,
                 lng_ref, lnb_ref, ws_ref, bsb_ref, cos_ref, sp_ref, sm_ref,
                 q_ref, k_ref, v_ref, gm_ref, ckv_ref, kr_ref):
    i = pl.program_id(0)
    x = jnp.where(i < NP_TILES, xc_ref[...], xl_ref[...])
    m = mods_ref[0]
    shift1, scale1 = m[0:1], m[1:2]
    hmod = (x * (1.0 + scale1) + shift1).astype(BF16)
    pg = _dot(hmod, win_g_ref[...])
    u, v = pg[:, :GW], pg[:, GW:]
    gu = _gelu(u)
    vn = _layer_norm(_gelu(v), lng_ref[...], lnb_ref[...]).astype(BF16)
    pa = _dot(hmod, win_a_ref[...])
    kr = _dot(hmod, win_r_ref[...])
    cq, ckv = pa[:, :QL], pa[:, QL:]

    cos, s_plus, s_minus = cos_ref[...], sp_ref[...], sm_ref[...]

    cq_n = _rms_norm(cq, qg_ref[...]).astype(BF16)
    q = _dot(cq_n, wuq_ref[...]) * Q_SCALE
    for h in range(H):
        q_ref[:, h * HEAD_PAD:h * HEAD_PAD + QK_NOPE] = q[:, h * HEAD_PAD:h * HEAD_PAD + QK_NOPE].astype(BF16)
        q_ref[:, h * HEAD_PAD + QK_NOPE:(h + 1) * HEAD_PAD] = _rope128(
            q[:, h * HEAD_PAD + QK_NOPE:(h + 1) * HEAD_PAD], cos, s_plus, s_minus).astype(BF16)

    ckv_n = _rms_norm(ckv, kvg_ref[...])
    _expand_kv(ckv_n, _rope128(kr, cos, s_plus, s_minus), wuk_ref, wuv_ref, k_ref, v_ref)

    for c in range(TM // CHUNK):
        rows = slice(c * CHUNK, (c + 1) * CHUNK)
        for g in range(G_GROUPS):
            cols = slice(g * G_DIM, (g + 1) * G_DIM)
            s = _dot(ws_ref[g], vn[rows, cols]) + bsb_ref[g]
            gm_ref[rows, cols] = (gu[rows, cols] * s).astype(BF16)

    @pl.when(i < NP_TILES)
    def _():
        ckv_ref[...] = ckv_n
        kr_ref[...] = kr[:, :QK_ROPE]


def _mods_index(i, tile=TM):
    return jnp.where(i < T_CTX // tile, N_LAT_B, (i - T_CTX // tile) // (N_LAT // tile))


def _proj(xc, xl, mods, win_a, win_g, win_r, qg, wuq, kvg, wuk, wuv, lng, lnb, ws, bsb, cos_t, sp_t, sm_t):
    def tok(width):
        return pl.BlockSpec((TM, width), lambda i: (i, 0))

    def pos_block(i):
        return jnp.where(i < NP_TILES, LAT_TILES_PER_B, (i - NP_TILES) % LAT_TILES_PER_B)

    rope_spec = pl.BlockSpec((TM, LANES), lambda i: (pos_block(i), 0))
    return pl.pallas_call(
        _proj_kernel,
        grid=(T // TM,),
        in_specs=[
            pl.BlockSpec((TM, D), lambda i: (jnp.minimum(i, NP_TILES - 1), 0)),
            pl.BlockSpec((TM, D), lambda i: (jnp.maximum(i - NP_TILES, 0), 0)),
            pl.BlockSpec((1, 6, D), lambda i: (_mods_index(i), 0, 0)),
            _const_spec((D, QL + KVL)), _const_spec((D, 2 * GW)), _const_spec((D, LANES)),
            _const_spec((1, QL)), _const_spec((QL, H * HEAD_PAD)),
            _const_spec((1, KVL)), _const_spec((KVL, H * QK_NOPE)), _const_spec((H * V_DIM, KVL)),
            _const_spec((1, GW)), _const_spec((1, GW)),
            _const_spec((G_GROUPS, CHUNK, CHUNK)), _const_spec((G_GROUPS, CHUNK, G_DIM)),
            rope_spec, rope_spec, rope_spec,
        ],
        out_specs=[tok(H * HEAD_PAD), tok(H * HEAD_PAD),
                   pl.BlockSpec((H * V_AUG, TM), lambda i: (0, i)), tok(GW),
                   pl.BlockSpec((TM, KVL), lambda i: (jnp.minimum(i, NP_TILES - 1), 0)),
                   pl.BlockSpec((TM, QK_ROPE), lambda i: (jnp.minimum(i, NP_TILES - 1), 0))],
        out_shape=[
            jax.ShapeDtypeStruct((T, H * HEAD_PAD), BF16),
            jax.ShapeDtypeStruct((T, H * HEAD_PAD), BF16),
            jax.ShapeDtypeStruct((H * V_AUG, T), BF16),
            jax.ShapeDtypeStruct((T, GW), BF16),
            jax.ShapeDtypeStruct((T_CTX, KVL), F32),
            jax.ShapeDtypeStruct((T_CTX, QK_ROPE), F32),
        ],
        compiler_params=_cp(("arbitrary",)),
        name="proj",
    )(xc, xl, mods, win_a, win_g, win_r, qg, wuq, kvg, wuk, wuv, lng, lnb, ws, bsb, cos_t, sp_t, sm_t)


def _kvexp_kernel(ckv_ref, kr_ref, wuk_ref, wuv_ref, k_ref, v_ref):
    kr = kr_ref[...]
    kr128 = jnp.concatenate([kr, jnp.zeros_like(kr)], axis=1)
    _expand_kv(ckv_ref[...], kr128, wuk_ref, wuv_ref, k_ref, v_ref)


def _kvexp(ckv, kr, wuk, wuv):
    rows = ckv.shape[0]
    return pl.pallas_call(
        _kvexp_kernel,
        grid=(rows // TM,),
        in_specs=[pl.BlockSpec((TM, KVL), lambda i: (i, 0)), pl.BlockSpec((TM, QK_ROPE), lambda i: (i, 0)),
                  _const_spec((KVL, H * QK_NOPE)), _const_spec((H * V_DIM, KVL))],
        out_specs=[pl.BlockSpec((TM, H * HEAD_PAD), lambda i: (i, 0)),
                   pl.BlockSpec((H * V_AUG, TM), lambda i: (0, i))],
        out_shape=[jax.ShapeDtypeStruct((rows, H * HEAD_PAD), BF16),
                   jax.ShapeDtypeStruct((H * V_AUG, rows), BF16)],
        compiler_params=_cp(("arbitrary",)),
        name="kvexp",
    )(ckv, kr, wuk, wuv)


def _probs(s, m):
    m_new = jnp.maximum(m, jnp.max(s, 0, keepdims=True))
    return m_new, jnp.exp2(m - m_new), jnp.exp2(s - m_new).astype(BF16)


def _attn_finish(acc):
    return (acc[:V_DIM] / acc[V_DIM:V_DIM + 1]).T.astype(BF16)


def _attn_ctx_kernel(q_ref, k_ref, vt_ref, o_ref):
    m0 = jnp.full((1, N_CTX), -jnp.inf, F32)
    for h in range(H):
        qh = q_ref[:, h * HEAD_PAD:(h + 1) * HEAD_PAD]
        kh = k_ref[:, h * HEAD_PAD:(h + 1) * HEAD_PAD]
        _, _, p = _probs(_dot_nt(kh, qh), m0)
        acc = _dot(vt_ref[h * V_AUG:(h + 1) * V_AUG, :], p)
        o_ref[:, h * V_DIM:(h + 1) * V_DIM] = _attn_finish(acc)


def _attn_ctx(q, k, vt):
    return pl.pallas_call(
        _attn_ctx_kernel,
        grid=(N_CTX_B,),
        in_specs=[pl.BlockSpec((N_CTX, H * HEAD_PAD), lambda b: (b, 0)),
                  pl.BlockSpec((N_CTX, H * HEAD_PAD), lambda b: (b, 0)),
                  pl.BlockSpec((H * V_AUG, N_CTX), lambda b: (0, b))],
        out_specs=pl.BlockSpec((N_CTX, H * V_DIM), lambda b: (b, 0)),
        out_shape=jax.ShapeDtypeStruct((T_CTX, H * V_DIM), BF16),
        compiler_params=_cp(("arbitrary",)),
        name="attn_ctx",
    )(q, k, vt)


def _attn_lat_kernel(q_ref, k_ref, vt_ref, kc_ref, vct_ref, o_ref):
    q = q_ref[...]
    n_lat = N_LAT // TK

    def keys(c):
        return k_ref[c * TK:(c + 1) * TK, :] if c < n_lat else kc_ref[...]

    def values_t(c):
        return vt_ref[:, c * TK:(c + 1) * TK] if c < n_lat else vct_ref[...]

    m = jnp.full((1, TQ), -jnp.inf, F32)
    acc = jnp.zeros((V_AUG, TQ), F32)
    s_next = _dot_nt(keys(0), q)
    m_next = jnp.maximum(m, jnp.max(s_next, 0, keepdims=True))
    p_prev = alpha_prev = None
    for c in range(n_lat + 1):
        s, m_new = s_next, m_next
        if c < n_lat:
            s_next = _dot_nt(keys(c + 1), q)
            m_next = jnp.maximum(m_new, jnp.max(s_next, 0, keepdims=True))
        if c > 0:
            acc = alpha_prev * acc + _dot(values_t(c - 1), p_prev)
        alpha_prev = jnp.exp2(m - m_new)
        p_prev = jnp.exp2(s - m_new).astype(BF16)
        m = m_new
    acc = alpha_prev * acc + _dot(values_t(n_lat), p_prev)
    o_ref[...] = _attn_finish(acc)


def _attn_lat(q, k, vt, kc, vct):
    qb0 = T_CTX // TQ
    qpb = N_LAT // TQ
    kb0 = T_CTX // N_LAT
    return pl.pallas_call(
        _attn_lat_kernel,
        grid=(N_LAT_B, H, qpb),
        in_specs=[pl.BlockSpec((TQ, HEAD_PAD), lambda b, h, qi: (qb0 + b * qpb + qi, h)),
                  pl.BlockSpec((N_LAT, HEAD_PAD), lambda b, h, qi: (kb0 + b, h)),
                  pl.BlockSpec((V_AUG, N_LAT), lambda b, h, qi: (h, kb0 + b)),
                  pl.BlockSpec((PAST, HEAD_PAD), lambda b, h, qi: (b, h)),
                  pl.BlockSpec((V_AUG, PAST), lambda b, h, qi: (h, b))],
        out_specs=pl.BlockSpec((TQ, V_DIM), lambda b, h, qi: (b * qpb + qi, h)),
        out_shape=jax.ShapeDtypeStruct((T_LAT, H * V_DIM), BF16),
        compiler_params=_cp(("arbitrary", "arbitrary", "arbitrary")),
        name="attn_lat",
    )(q, k, vt, kc, vct)


def _route(logits):
    lane_i = lax.broadcasted_iota(I32, logits.shape, 1)
    lane = lane_i.astype(F32)
    neg = jnp.float32(-jnp.inf)
    far = jnp.float32(LANES)
    gl = jnp.where(lane_i < N_GRP, logits, neg)
    gmax = jnp.max(gl, -1, keepdims=True)
    gidx = jnp.min(jnp.where(gl == gmax, lane, far), -1, keepdims=True)
    p_top = 1.0 / jnp.sum(jnp.exp(gl - gmax), -1, keepdims=True)
    grp_of_lane = ((lane_i - N_GRP) >> 3).astype(F32)
    in_grp = (lane_i >= N_GRP) & (lane_i < N_GRP + N_EXP) & (grp_of_lane == gidx)
    el = jnp.where(in_grp, logits, neg)
    m1 = jnp.max(el, -1, keepdims=True)
    i1 = jnp.min(jnp.where(el == m1, lane, far), -1, keepdims=True)
    el2 = jnp.where(lane == i1, neg, el)
    m2 = jnp.max(el2, -1, keepdims=True)
    i2 = jnp.min(jnp.where(el2 == m2, lane, far), -1, keepdims=True)
    e2 = jnp.exp(m2 - m1)
    w1 = p_top / (1.0 + e2)
    w2 = p_top * e2 / (1.0 + e2)
    return i1 - N_GRP, i2 - N_GRP, w1, w2


def _pair(a, b, dtype):
    two = lax.broadcasted_iota(I32, (a.shape[0], TOP_K), 1)
    return jnp.where(two == 0, a, b).astype(dtype)


def _post_kernel(ac_ref, al_ref, gm_ref, xc_ref, xl_ref, mods_ref, wout_ref, g1_ref, b1_ref, wrt_ref,
                 brt_ref, x1_ref, h2p_ref, ridx_ref, rwt_ref):
    i = pl.program_id(0)
    is_ctx = i < T_CTX // TP
    m = mods_ref[0]
    gate1, shift2, scale2 = m[2:3], m[3:4], m[4:5]

    def mix_of(rows):
        attn = jnp.where(is_ctx, ac_ref[rows, :], al_ref[rows, :])
        return _dot(jnp.concatenate([attn, gm_ref[rows, :]], axis=1), wout_ref[...])

    bounds = np.cumsum((0,) + POST_SUBS)
    subs = [slice(int(a), int(b)) for a, b in zip(bounds[:-1], bounds[1:])]
    mix_next = mix_of(subs[0])
    for s, rows in enumerate(subs):
        mix = mix_next
        if s + 1 < len(subs):
            mix_next = mix_of(subs[s + 1])
        x = jnp.where(is_ctx, xc_ref[rows, :], xl_ref[rows, :])
        x1 = _layer_norm(ALPHA * x + gate1 * mix, g1_ref[...], b1_ref[...])
        x1_ref[rows, :] = x1
        h2 = x1 * (1.0 + scale2) + shift2
        _store_rows_as_tiles(h2p_ref, rows.start, _pack_pair(h2[:, :D // 2], h2[:, D // 2:]))
        e1, e2, w1, w2 = _route(_dot(h2.astype(BF16), wrt_ref[...]) + brt_ref[...])
        ridx_ref[rows, :] = _pair(e1, e2, I32)
        rwt_ref[rows, :] = _pair(w1, w2, F32)


def _post(attn_c, attn_l, gm, xc, xl, mods, wout, g1, b1, wrt, brt):
    npt = T_CTX // TP

    def tok(width):
        return pl.BlockSpec((TP, width), lambda i: (i, 0))

    def ctx(width):
        return pl.BlockSpec((TP, width), lambda i: (jnp.minimum(i, npt - 1), 0))

    def lat(width):
        return pl.BlockSpec((TP, width), lambda i: (jnp.maximum(i - npt, 0), 0))

    return pl.pallas_call(
        _post_kernel,
        grid=(T // TP,),
        in_specs=[ctx(H * V_DIM), lat(H * V_DIM), tok(GW), ctx(D), lat(D),
                  pl.BlockSpec((1, 6, D), lambda i: (_mods_index(i, TP), 0, 0)),
                  _const_spec((D, D)), _const_spec((1, D)), _const_spec((1, D)),
                  _const_spec((D, LANES)), _const_spec((1, LANES))],
        out_specs=[tok(D), pl.BlockSpec((TP * RT, LANES), lambda i: (i, 0)), tok(TOP_K), tok(TOP_K)],
        out_shape=[jax.ShapeDtypeStruct((T, D), F32),
                   jax.ShapeDtypeStruct((T * RT, LANES), U32),
                   jax.ShapeDtypeStruct((T, TOP_K), I32),
                   jax.ShapeDtypeStruct((T, TOP_K), F32)],
        compiler_params=_cp(("arbitrary",)),
        name="post",
    )(attn_c, attn_l, gm, xc, xl, mods, wout, g1, b1, wrt, brt)


def _as_lane_row(col):
    return jnp.broadcast_to(col, (col.shape[0], LANES)).T[0:1, :]


def _rank_kernel(ridx_ref, tri_ref, e_ref, rank_ref, hist_ref):
    lane = lax.broadcasted_iota(I32, (SUB, LANES), 1).astype(F32)
    seen = jnp.zeros((1, LANES), F32)
    for s in range(TP // SUB):
        rows = slice(s * SUB, (s + 1) * SUB)
        e = ridx_ref[rows, :].astype(F32)
        ohs = [(lane == e[:, k:k + 1]).astype(F32) for k in range(TOP_K)]
        both = ohs[0] + ohs[1]
        before = _dot(tri_ref[...], both.astype(BF16)) + seen
        for k in range(TOP_K):
            rank_k = jnp.sum(before * ohs[k], -1, keepdims=True)
            e_ref[0, k:k + 1, rows] = _as_lane_row(e[:, k:k + 1]).astype(I32)
            rank_ref[0, k:k + 1, rows] = _as_lane_row(rank_k).astype(I32)
        seen = seen + jnp.sum(both, 0, keepdims=True)
    hist_ref[0] = jnp.broadcast_to(seen, (8, LANES))


def _rank(ridx):
    dense = pl.BlockSpec((1, TOP_K, TP), lambda i: (i, 0, 0))
    return pl.pallas_call(
        _rank_kernel,
        grid=(T // TP,),
        in_specs=[pl.BlockSpec((TP, TOP_K), lambda i: (i, 0)), _const_spec((SUB, SUB))],
        out_specs=[dense, dense, pl.BlockSpec((1, 8, LANES), lambda i: (i, 0, 0))],
        out_shape=[jax.ShapeDtypeStruct((T // TP, TOP_K, TP), I32),
                   jax.ShapeDtypeStruct((T // TP, TOP_K, TP), I32),
                   jax.ShapeDtypeStruct((T // TP, 8, LANES), F32)],
        compiler_params=_cp(("arbitrary",)),
        name="rank",
    )(ridx, jnp.asarray(np.tril(np.ones((SUB, SUB), np.float32), -1), BF16))


def _row_copy(src, src_row, dst, dst_row, sem):
    def tile(ref, row):
        return ref.at[pl.ds(pl.multiple_of(row * RT, RT), RT)]

    return pltpu.make_async_copy(tile(src, src_row), tile(dst, dst_row), sem)


def _dispatch_kernel(zb_ref, zv_ref, d0_ref, d1_ref, h2p_ref, xs_ref, zbuf, sem):
    i = pl.program_id(0)
    dest_refs = (d0_ref, d1_ref)

    @pl.when(i == 0)
    def _():
        zbuf[...] = jnp.zeros_like(zbuf)

        def zero_block(j):
            start = pl.multiple_of(zb_ref[j] * (BM * RT), BM * RT)
            return pltpu.make_async_copy(zbuf, xs_ref.at[pl.ds(start, BM * RT)], sem)

        def start(j, carry):
            @pl.when(zv_ref[j] != 0)
            def _():
                zero_block(j).start()
            return carry

        def wait(j, carry):
            @pl.when(zv_ref[j] != 0)
            def _():
                zero_block(j).wait()
            return carry

        lax.fori_loop(0, N_ZERO, start, 0)
        lax.fori_loop(0, N_ZERO, wait, 0)

    def issue(r, carry):
        for k in range(TOP_K):
            _row_copy(h2p_ref, r, xs_ref, dest_refs[k][0, 0, r], sem).start(priority=k)
        return carry

    lax.fori_loop(0, TD, issue, 0, unroll=8)
    for _ in range(TOP_K):
        pltpu.make_async_copy(h2p_ref, xs_ref.at[pl.ds(0, TD * RT)], sem).wait()


def _regroup_steps(dest, rows):
    return [dest[:, k, :].reshape(T // rows, 1, rows) for k in range(TOP_K)]


def _dispatch(zero_blk, zero_valid, dest, h2p):
    grid_spec = pltpu.PrefetchScalarGridSpec(
        num_scalar_prefetch=2,
        grid=(T // TD,),
        in_specs=[pl.BlockSpec((1, 1, TD), lambda i, zb, zv: (i, 0, 0), memory_space=pltpu.SMEM),
                  pl.BlockSpec((1, 1, TD), lambda i, zb, zv: (i, 0, 0), memory_space=pltpu.SMEM),
                  pl.BlockSpec((TD * RT, LANES), lambda i, zb, zv: (i, 0))],
        out_specs=pl.BlockSpec(memory_space=pl.ANY),
        scratch_shapes=[pltpu.VMEM((BM * RT, LANES), U32), pltpu.SemaphoreType.DMA(())],
    )
    return pl.pallas_call(
        _dispatch_kernel,
        grid_spec=grid_spec,
        out_shape=jax.ShapeDtypeStruct((N_BLOCKS * BM * RT, LANES), U32),
        compiler_params=_cp(("arbitrary",)),
        name="dispatch",
    )(zero_blk, zero_valid, *_regroup_steps(dest, TD), h2p)


def _experts_kernel(be_ref, bv_ref, sw_ref, es_ref, nx_ref, x_ref, w1_hbm, w3_hbm, w2_hbm, y_ref,
                    wf1, wf3, wf2, w1b, w3b, w2b, sems):
    i = pl.program_id(0)
    e = be_ref[i]
    nv = bv_ref[i]

    def weight_copies(expert, s):
        return (pltpu.make_async_copy(w1_hbm.at[expert], wf1.at[s], sems.at[s]),
                pltpu.make_async_copy(w3_hbm.at[expert], wf3.at[s], sems.at[s]),
                pltpu.make_async_copy(w2_hbm.at[expert], wf2.at[s], sems.at[s]))

    @pl.when(sw_ref[i] != 0)
    def _():
        s = es_ref[i]

        @pl.when(i == 0)
        def _():
            for c in weight_copies(e, s):
                c.start()

        for c in weight_copies(e, s):
            c.wait()
        nxt = nx_ref[i]

        @pl.when(nxt >= 0)
        def _():
            for c in weight_copies(nxt, 1 - s):
                c.start()

        w1b[...] = wf1[s].astype(BF16)
        w3b[...] = wf3[s].astype(BF16)
        w2b[...] = wf2[s].astype(BF16)

    @pl.when(nv > 0)
    def _():
        lo, hi = _unpack_pair(_load_tiles_as_rows(x_ref, 0, BM))
        xb = jnp.concatenate([lo.astype(BF16), hi.astype(BF16)], axis=1)
        a = _dot(xb, w1b[...])
        b = _dot(xb, w3b[...])
        hid = (jax.nn.silu(a) * b).astype(BF16)
        y = _dot(hid, w2b[...])
        _store_rows_as_tiles(y_ref, 0, _pack_pair(y[:, :D // 2], y[:, D // 2:]))

    @pl.when(nv == 0)
    def _():
        y_ref[...] = jnp.zeros_like(y_ref)


def _experts(blk_expert, blk_valid, switch, wslot, next_expert, xs, w1, w3, w2):
    hbm = pl.BlockSpec(memory_space=pl.ANY)
    grid_spec = pltpu.PrefetchScalarGridSpec(
        num_scalar_prefetch=5,
        grid=(N_BLOCKS,),
        in_specs=[pl.BlockSpec((BM * RT, LANES), lambda i, *_: (i, 0)), hbm, hbm, hbm],
        out_specs=pl.BlockSpec((BM * RT, LANES), lambda i, *_: (i, 0)),
        scratch_shapes=[pltpu.VMEM((2, D, E_HID), F32), pltpu.VMEM((2, D, E_HID), F32),
                        pltpu.VMEM((2, E_HID, D), F32),
                        pltpu.VMEM((D, E_HID), BF16), pltpu.VMEM((D, E_HID), BF16),
                        pltpu.VMEM((E_HID, D), BF16), pltpu.SemaphoreType.DMA((2,))],
    )
    return pl.pallas_call(
        _experts_kernel,
        grid_spec=grid_spec,
        out_shape=jax.ShapeDtypeStruct((N_BLOCKS * BM * RT, LANES), U32),
        compiler_params=_cp(("arbitrary",)),
        name="experts",
    )(blk_expert, blk_valid, switch, wslot, next_expert, xs, w1, w3, w2)


def _final_kernel(dcur0_ref, dcur1_ref, dnext0_ref, dnext1_ref, ys_ref, x1_ref, rwt_ref, mods_ref,
                  g2_ref, b2_ref, oc_ref, ol_ref, buf, sems):
    i = pl.program_id(0)
    dcur_ref, dnext_ref = (dcur0_ref, dcur1_ref), (dnext0_ref, dnext1_ref)
    n = pl.num_programs(0)
    slot = i % 2

    def first_row(s, k):
        return (s * TOP_K + k) * TM

    def issue(d_ref, s):
        def body(r, carry):
            for k in range(TOP_K):
                _row_copy(ys_ref, d_ref[k][0, 0, r], buf, first_row(s, k) + r, sems.at[s]).start()
            return carry

        lax.fori_loop(0, TM, body, 0, unroll=8)

    @pl.when(i == 0)
    def _():
        issue(dcur_ref, 0)

    @pl.when(i + 1 < n)
    def _():
        issue(dnext_ref, 1 - slot)

    for k in range(TOP_K):
        pltpu.make_async_copy(ys_ref.at[pl.ds(0, TM * RT)], buf.at[pl.ds(0, TM * RT)], sems.at[slot]).wait()

    w = rwt_ref[...]
    lo0, hi0 = _unpack_pair(_load_tiles_as_rows(buf, first_row(slot, 0), TM))
    lo1, hi1 = _unpack_pair(_load_tiles_as_rows(buf, first_row(slot, 1), TM))
    w0, w1 = w[:, 0:1], w[:, 1:2]
    moe = jnp.concatenate([lo0 * w0 + lo1 * w1, hi0 * w0 + hi1 * w1], axis=1)
    gate2 = mods_ref[0][5:6]
    out = _layer_norm(ALPHA * x1_ref[...] + gate2 * moe, g2_ref[...], b2_ref[...])

    @pl.when(i < NP_TILES)
    def _():
        oc_ref[...] = out

    @pl.when(i >= NP_TILES)
    def _():
        ol_ref[...] = out


def _final(dest_k, ys, x1, rwt, mods, g2, b2):
    n = T // TM
    cur = pl.BlockSpec((1, 1, TM), lambda i: (i, 0, 0), memory_space=pltpu.SMEM)
    nxt = pl.BlockSpec((1, 1, TM), lambda i: (jnp.minimum(i + 1, n - 1), 0, 0), memory_space=pltpu.SMEM)
    return pl.pallas_call(
        _final_kernel,
        grid=(n,),
        in_specs=[cur, cur, nxt, nxt,
                  pl.BlockSpec(memory_space=pl.ANY),
                  pl.BlockSpec((TM, D), lambda i: (i, 0)),
                  pl.BlockSpec((TM, TOP_K), lambda i: (i, 0)),
                  pl.BlockSpec((1, 6, D), lambda i: (_mods_index(i), 0, 0)),
                  _const_spec((1, D)), _const_spec((1, D))],
        out_specs=[pl.BlockSpec((TM, D), lambda i: (jnp.minimum(i, NP_TILES - 1), 0)),
                   pl.BlockSpec((TM, D), lambda i: (jnp.maximum(i - NP_TILES, 0), 0))],
        out_shape=[jax.ShapeDtypeStruct((T_CTX, D), F32), jax.ShapeDtypeStruct((T_LAT, D), F32)],
        scratch_shapes=[pltpu.VMEM((2 * TOP_K * TM * RT, LANES), U32), pltpu.SemaphoreType.DMA((2,))],
        compiler_params=_cp(("arbitrary",)),
        name="final",
    )(*dest_k, *dest_k, ys, x1, rwt, mods, g2, b2)


def _rope_tables():
    n = N_LAT
    rows = n // GRID_W
    f32 = np.float32
    row = np.repeat(np.arange(rows, dtype=f32), GRID_W)
    col = np.tile(np.arange(GRID_W, dtype=f32), rows)
    half = QK_ROPE // 2
    inv = (f32(ROPE_THETA) ** (-np.arange(0, half, 2, dtype=f32) / f32(half))).astype(f32)
    ang_r, ang_c = (row[:, None] * inv).astype(f32), (col[:, None] * inv).astype(f32)
    cr, sr = np.cos(ang_r).astype(f32), np.sin(ang_r).astype(f32)
    cc, sc = np.cos(ang_c).astype(f32), np.sin(ang_c).astype(f32)
    z16 = np.zeros_like(cr)
    pad = lambda v: np.full((n, LANES - QK_ROPE), v, f32)
    cos = np.concatenate([cr, cr, cc, cc, pad(1.0)], axis=1)
    s_plus = np.concatenate([z16, sr, z16, sc, pad(0.0)], axis=1)
    s_minus = np.concatenate([-sr, z16, -sc, z16, pad(0.0)], axis=1)
    ident = lambda v: np.full((TM, LANES), v, f32)
    return (jnp.asarray(np.concatenate([cos, ident(1.0)], 0)),
            jnp.asarray(np.concatenate([s_plus, ident(0.0)], 0)),
            jnp.asarray(np.concatenate([s_minus, ident(0.0)], 0)))


def _block_plan(eid_k, rank, hist):
    counts = jnp.sum(hist, axis=0)
    nblk = (counts + BM - 1) // BM
    bends = jnp.cumsum(nblk)
    bstarts = bends - nblk
    base = bstarts[None, :] * BM + jnp.cumsum(hist, axis=0) - hist
    eid = jnp.arange(N_EXP, dtype=I32)
    hit = eid_k[:, :, None, :] == eid[None, None, :, None]
    first = jnp.sum(jnp.where(hit, base[:, None, :, None], 0), axis=2)
    dest = (first + rank).astype(I32)
    blk = jnp.arange(N_BLOCKS, dtype=I32)
    blk_expert = jnp.minimum(jnp.sum((blk[:, None] >= bends[None, :]).astype(I32), axis=1), N_EXP - 1)
    of_blk = blk_expert[:, None] == eid[None, :]

    def per_block(v):
        return jnp.sum(jnp.where(of_blk, v[None, :], 0), axis=1)

    left = per_block(counts) - (blk - per_block(bstarts)) * BM
    blk_valid = jnp.where(blk < bends[-1], jnp.clip(left, 0, BM), 0).astype(I32)
    unused = bends[-1] + jnp.arange(N_EXP, dtype=I32)
    zero_blk = jnp.concatenate([jnp.maximum(bends - 1, 0), jnp.minimum(unused, N_BLOCKS - 1)])
    zero_valid = jnp.concatenate([counts % BM != 0, unused < N_BLOCKS])
    prev_expert = jnp.concatenate([jnp.full((1,), -1, I32), blk_expert[:-1]])
    switch = (blk < bends[-1]) & (blk_expert != prev_expert)
    wslot = (jnp.cumsum(switch.astype(I32)) - 1) % 2
    later = (eid[None, :] > eid[:, None]) & (nblk[None, :] > 0)
    next_of = jnp.min(jnp.where(later, eid[None, :], N_EXP), axis=1)
    next_expert = per_block(jnp.where(next_of < N_EXP, next_of, -1))
    return (dest, blk_expert.astype(I32), blk_valid, zero_blk.astype(I32),
            zero_valid.astype(I32), switch.astype(I32), wslot.astype(I32), next_expert.astype(I32))


def kernel(x_prompt, x_sample, cache_ckv, cache_krope, c, c_ctx, w_ada, b_ada, w_in, q_norm_g, w_uq,
           kv_norm_g, w_ukv, gmlp_ln_g, gmlp_ln_b, w_spatial, b_spatial, w_out, ln1_g, ln1_b, w_group,
           b_group, w_router, b_router, w1, w3, w2, ln2_g, ln2_b):
    l = 0
    xc = x_prompt.reshape(T_CTX, D)
    xl = x_sample.reshape(T_LAT, D)

    cond16 = jnp.concatenate([c, c_ctx[None, :], jnp.zeros((16 - N_LAT_B - 1, D), F32)], axis=0)
    mods = _adaln(cond16, w_ada[l], b_ada[l][None, :]).reshape(16, 6, D)

    wi = w_in[l]
    o_kr = QL + KVL
    o_u = o_kr + QK_ROPE
    win_a = wi[:, :o_kr].astype(BF16)
    win_g = wi[:, o_u:].astype(BF16)
    win_r = jnp.pad(wi[:, o_kr:o_u], ((0, 0), (0, LANES - QK_ROPE))).astype(BF16)
    wuq = jnp.pad(w_uq[l].reshape(QL, H, QK_NOPE + QK_ROPE),
                  ((0, 0), (0, 0), (0, HEAD_PAD - QK_NOPE - QK_ROPE))).reshape(QL, H * HEAD_PAD).astype(BF16)
    wkv = w_ukv[l].reshape(KVL, H, QK_NOPE + V_DIM)
    wuk = wkv[:, :, :QK_NOPE].reshape(KVL, H * QK_NOPE).astype(BF16)
    wuvt = wkv[:, :, QK_NOPE:].reshape(KVL, H * V_DIM).T.astype(BF16)
    ws = w_spatial[l].astype(BF16)
    bsb = jnp.broadcast_to(b_spatial[l][:, :, None], (G_GROUPS, CHUNK, G_DIM))
    wrt = jnp.concatenate([w_group[l], w_router[l], jnp.zeros((D, LANES - N_GRP - N_EXP), F32)],
                          axis=1).astype(BF16)
    brt = jnp.concatenate([b_group[l], b_router[l], jnp.zeros((LANES - N_GRP - N_EXP,), F32)])[None, :]
    cos_t, sp_t, sm_t = _rope_tables()
    row = lambda v: v[l][None, :]

    q, k, vt, gm, ckv_n, kr = _proj(xc, xl, mods, win_a, win_g, win_r, row(q_norm_g), wuq, row(kv_norm_g), wuk, wuvt,
                                   row(gmlp_ln_g), row(gmlp_ln_b), ws, bsb, cos_t, sp_t, sm_t)
    kc, vct = _kvexp(cache_ckv[:, l].reshape(N_LAT_B * PAST, KVL),
                     cache_krope[:, l].reshape(N_LAT_B * PAST, QK_ROPE), wuk, wuvt)
    attn_c = _attn_ctx(q, k, vt)
    attn_l = _attn_lat(q, k, vt, kc, vct)

    x1, h2p, ridx, rwt = _post(attn_c, attn_l, gm, xc, xl, mods, w_out[l].astype(BF16), row(ln1_g), row(ln1_b),
                               wrt, brt)
    eid_k, rank, hist = _rank(ridx)
    dest, blk_expert, blk_valid, zero_blk, zero_valid, switch, wslot, next_expert = _block_plan(
        eid_k, rank, hist[:, 0, :N_EXP].astype(I32))
    xs = _dispatch(zero_blk, zero_valid, dest, h2p)
    ys = _experts(blk_expert, blk_valid, switch, wslot, next_expert, xs, w1[l], w3[l], w2[l])
    y_c, y_l = _final(_regroup_steps(dest, TM), ys, x1, rwt, mods, row(ln2_g), row(ln2_b))

    return (y_c.reshape(N_CTX_B, N_CTX, D), y_l.reshape(N_LAT_B, N_LAT, D),
            ckv_n.reshape(N_CTX_B, 1, N_CTX, KVL), kr.reshape(N_CTX_B, 1, N_CTX, QK_ROPE))
```

```python
import functools

import jax
import jax.numpy as jnp
import numpy as np
from jax import lax
from jax.experimental import pallas as pl
from jax.experimental.pallas import tpu as pltpu

F32 = jnp.float32
BF16 = jnp.bfloat16
U32 = jnp.uint32
I32 = jnp.int32

D = 2048
N_CTX_B, N_CTX = 32, 256
N_LAT_B, N_LAT = 4, 4096
PAST = 512
GRID_W = 64
H = 8
QK_NOPE, QK_ROPE, V_DIM = 128, 64, 128
QL, KVL = 512, 512
GW = 1024
G_GROUPS, G_DIM, CHUNK = 8, 128, 128
N_GRP, E_PER_GRP, N_EXP, TOP_K = 4, 8, 32, 2
E_HID = 512
ROPE_THETA = 10000.0
EPS = 1e-6
ALPHA = 2.0 ** 0.25
SM_SCALE = (QK_NOPE + QK_ROPE) ** -0.5
Q_SCALE = SM_SCALE * float(np.log2(np.e))

T_CTX = N_CTX_B * N_CTX
T_LAT = N_LAT_B * N_LAT
T = T_CTX + T_LAT
N_ASSIGN = T * TOP_K

LANES = 128
HEAD_PAD = 256
V_AUG = V_DIM + 16
TM = 256
TP = 512
POST_SUBS = (256, 256)
SUB = 256
TD = 1024
CTX_PER_STEP = 2
TQ = 1024
TK = 512
BM = 256
N_BLOCKS = N_ASSIGN // BM + N_EXP
N_ZERO = 2 * N_EXP
VMEM_LIMIT = 56 * 1024 * 1024

NP_TILES = T_CTX // TM
LAT_TILES_PER_B = N_LAT // TM


def _cp(sem):
    return pltpu.CompilerParams(dimension_semantics=sem, vmem_limit_bytes=VMEM_LIMIT)


def _const_spec(shape):
    nd = len(shape)
    return pl.BlockSpec(shape, lambda *a: (0,) * nd, pipeline_mode=pl.Buffered(1))


def _dot(a, b):
    return jnp.dot(a, b, preferred_element_type=F32)


def _dot_nt(a, b):
    return lax.dot_general(a, b, (((1,), (1,)), ((), ())), preferred_element_type=F32)


def _layer_norm(x, g, b):
    mu = jnp.mean(x, -1, keepdims=True)
    xc = x - mu
    var = jnp.mean(xc * xc, -1, keepdims=True)
    return xc * lax.rsqrt(var + EPS) * g + b


def _rms_norm(x, g):
    return x * lax.rsqrt(jnp.mean(x * x, -1, keepdims=True) + EPS) * g


def _gelu(x):
    return 0.5 * x * (1.0 + lax.erf(x * np.float32(np.sqrt(0.5))))


def _pack_pair(lo, hi):
    lo_b = lax.bitcast_convert_type(lo.astype(BF16).astype(F32), U32)
    hi_b = lax.bitcast_convert_type(hi.astype(BF16).astype(F32), U32)
    return hi_b | (lo_b >> 16)


def _unpack_pair(w):
    lo = lax.bitcast_convert_type(w << 16, F32)
    hi = lax.bitcast_convert_type(w & jnp.uint32(0xFFFF0000), F32)
    return lo, hi


RT = 8


def _store_rows_as_tiles(ref, row0, val):
    n = val.shape[0]
    for j in range(RT):
        ref[pl.ds(row0 * RT + j, n, stride=RT), :] = val[:, j * LANES:(j + 1) * LANES]


def _load_tiles_as_rows(ref, row0, n):
    return jnp.concatenate([ref[pl.ds(row0 * RT + j, n, stride=RT), :] for j in range(RT)], axis=1)


def _adaln_kernel(cond_ref, w_ref, b_ref, o_ref):
    s = jax.nn.silu(cond_ref[...])
    s_hi = s.astype(BF16)
    s_lo = (s - s_hi.astype(F32)).astype(BF16)
    w = w_ref[...]
    w_hi = w.astype(BF16)
    w_lo = (w - w_hi.astype(F32)).astype(BF16)
    o_ref[...] = _dot(s_hi, w_hi) + _dot(s_lo, w_hi) + _dot(s_hi, w_lo) + b_ref[...]


def _adaln(cond16, w_ada, b_ada):
    tn = 1024
    n = w_ada.shape[1]
    return pl.pallas_call(
        _adaln_kernel,
        grid=(n // tn,),
        in_specs=[pl.BlockSpec((16, D), lambda j: (0, 0)),
                  pl.BlockSpec((D, tn), lambda j: (0, j)),
                  pl.BlockSpec((1, tn), lambda j: (0, j))],
        out_specs=pl.BlockSpec((16, tn), lambda j: (0, j)),
        out_shape=jax.ShapeDtypeStruct((16, n), F32),
        compiler_params=_cp(("arbitrary",)),
        name="adaln",
    )(cond16, w_ada, b_ada)


def _expand_kv(ckv_n, kr128, wuk_ref, wuvt_ref, k_ref, vt_ref):
    cb = ckv_n.astype(BF16)
    kn = _dot(cb, wuk_ref[...])
    vt = _dot_nt(wuvt_ref[...], cb).astype(BF16)
    ones = jnp.ones((V_AUG - V_DIM, vt.shape[1]), BF16)
    for h in range(H):
        vt_ref[h * V_AUG:h * V_AUG + V_DIM, :] = vt[h * V_DIM:(h + 1) * V_DIM]
        vt_ref[h * V_AUG + V_DIM:(h + 1) * V_AUG, :] = ones
    krb = kr128.astype(BF16)
    for h in range(H):
        k_ref[:, h * HEAD_PAD:h * HEAD_PAD + QK_NOPE] = kn[:, h * QK_NOPE:(h + 1) * QK_NOPE].astype(BF16)
        k_ref[:, h * HEAD_PAD + QK_NOPE:(h + 1) * HEAD_PAD] = krb


def _rope128(x, cos, s_plus, s_minus):
    return (x * cos + pltpu.roll(x, 16, 1) * s_plus + pltpu.roll(x, LANES - 16, 1) * s_minus)


def _proj_kernel(xc_ref, xl_ref, mods_ref, win_a_ref, win_g_ref, win_r_ref, qg_ref, wuq_ref, kvg_ref,
                 wuk_ref, wuv_ref,
                 lng_ref, lnb_ref, ws_ref, bsb_ref, cos_ref, sp_ref, sm_ref,
                 q_ref, k_ref, v_ref, gm_ref, ckv_ref, kr_ref):
    i = pl.program_id(0)
    x = jnp.where(i < NP_TILES, xc_ref[...], xl_ref[...])
    m = mods_ref[0]
    shift1, scale1 = m[0:1], m[1:2]
    hmod = (x * (1.0 + scale1) + shift1).astype(BF16)
    pg = _dot(hmod, win_g_ref[...])
    u, v = pg[:, :GW], pg[:, GW:]
    gu = _gelu(u)
    vn = _layer_norm(_gelu(v), lng_ref[...], lnb_ref[...]).astype(BF16)
    pa = _dot(hmod, win_a_ref[...])
    kr = _dot(hmod, win_r_ref[...])
    cq, ckv = pa[:, :QL], pa[:, QL:]

    cos, s_plus, s_minus = cos_ref[...], sp_ref[...], sm_ref[...]

    cq_n = _rms_norm(cq, qg_ref[...]).astype(BF16)
    q = _dot(cq_n, wuq_ref[...]) * Q_SCALE
    for h in range(H):
        q_ref[:, h * HEAD_PAD:h * HEAD_PAD + QK_NOPE] = q[:, h * HEAD_PAD:h * HEAD_PAD + QK_NOPE].astype(BF16)
        q_ref[:, h * HEAD_PAD + QK_NOPE:(h + 1) * HEAD_PAD] = _rope128(
            q[:, h * HEAD_PAD + QK_NOPE:(h + 1) * HEAD_PAD], cos, s_plus, s_minus).astype(BF16)

    ckv_n = _rms_norm(ckv, kvg_ref[...])
    _expand_kv(ckv_n, _rope128(kr, cos, s_plus, s_minus), wuk_ref, wuv_ref, k_ref, v_ref)

    for c in range(TM // CHUNK):
        rows = slice(c * CHUNK, (c + 1) * CHUNK)
        for g in range(G_GROUPS):
            cols = slice(g * G_DIM, (g + 1) * G_DIM)
            s = _dot(ws_ref[g], vn[rows, cols]) + bsb_ref[g]
            gm_ref[rows, cols] = (gu[rows, cols] * s).astype(BF16)

    @pl.when(i < NP_TILES)
    def _():
        ckv_ref[...] = ckv_n
        kr_ref[...] = kr[:, :QK_ROPE]


def _mods_index(i, tile=TM):
    return jnp.where(i < T_CTX // tile, N_LAT_B, (i - T_CTX // tile) // (N_LAT // tile))


def _proj(xc, xl, mods, win_a, win_g, win_r, qg, wuq, kvg, wuk, wuv, lng, lnb, ws, bsb, cos_t, sp_t, sm_t):
    def tok(width):
        return pl.BlockSpec((TM, width), lambda i: (i, 0))

    def pos_block(i):
        return jnp.where(i < NP_TILES, LAT_TILES_PER_B, (i - NP_TILES) % LAT_TILES_PER_B)

    rope_spec = pl.BlockSpec((TM, LANES), lambda i: (pos_block(i), 0))
    return pl.pallas_call(
        _proj_kernel,
        grid=(T // TM,),
        in_specs=[
            pl.BlockSpec((TM, D), lambda i: (jnp.minimum(i, NP_TILES - 1), 0)),
            pl.BlockSpec((TM, D), lambda i: (jnp.maximum(i - NP_TILES, 0), 0)),
            pl.BlockSpec((1, 6, D), lambda i: (_mods_index(i), 0, 0)),
            _const_spec((D, QL + KVL)), _const_spec((D, 2 * GW)), _const_spec((D, LANES)),
            _const_spec((1, QL)), _const_spec((QL, H * HEAD_PAD)),
            _const_spec((1, KVL)), _const_spec((KVL, H * QK_NOPE)), _const_spec((H * V_DIM, KVL)),
            _const_spec((1, GW)), _const_spec((1, GW)),
            _const_spec((G_GROUPS, CHUNK, CHUNK)), _const_spec((G_GROUPS, CHUNK, G_DIM)),
            rope_spec, rope_spec, rope_spec,
        ],
        out_specs=[tok(H * HEAD_PAD), tok(H * HEAD_PAD),
                   pl.BlockSpec((H * V_AUG, TM), lambda i: (0, i)), tok(GW),
                   pl.BlockSpec((TM, KVL), lambda i: (jnp.minimum(i, NP_TILES - 1), 0)),
                   pl.BlockSpec((TM, QK_ROPE), lambda i: (jnp.minimum(i, NP_TILES - 1), 0))],
        out_shape=[
            jax.ShapeDtypeStruct((T, H * HEAD_PAD), BF16),
            jax.ShapeDtypeStruct((T, H * HEAD_PAD), BF16),
            jax.ShapeDtypeStruct((H * V_AUG, T), BF16),
            jax.ShapeDtypeStruct((T, GW), BF16),
            jax.ShapeDtypeStruct((T_CTX, KVL), F32),
            jax.ShapeDtypeStruct((T_CTX, QK_ROPE), F32),
        ],
        compiler_params=_cp(("arbitrary",)),
        name="proj",
    )(xc, xl, mods, win_a, win_g, win_r, qg, wuq, kvg, wuk, wuv, lng, lnb, ws, bsb, cos_t, sp_t, sm_t)


def _kvexp_kernel(ckv_ref, kr_ref, wuk_ref, wuv_ref, k_ref, v_ref):
    kr = kr_ref[...]
    kr128 = jnp.concatenate([kr, jnp.zeros_like(kr)], axis=1)
    _expand_kv(ckv_ref[...], kr128, wuk_ref, wuv_ref, k_ref, v_ref)


def _kvexp(ckv, kr, wuk, wuv):
    rows = ckv.shape[0]
    return pl.pallas_call(
        _kvexp_kernel,
        grid=(rows // TM,),
        in_specs=[pl.BlockSpec((TM, KVL), lambda i: (i, 0)), pl.BlockSpec((TM, QK_ROPE), lambda i: (i, 0)),
                  _const_spec((KVL, H * QK_NOPE)), _const_spec((H * V_DIM, KVL))],
        out_specs=[pl.BlockSpec((TM, H * HEAD_PAD), lambda i: (i, 0)),
                   pl.BlockSpec((H * V_AUG, TM), lambda i: (0, i))],
        out_shape=[jax.ShapeDtypeStruct((rows, H * HEAD_PAD), BF16),
                   jax.ShapeDtypeStruct((H * V_AUG, rows), BF16)],
        compiler_params=_cp(("arbitrary",)),
        name="kvexp",
    )(ckv, kr, wuk, wuv)


def _probs(s, m):
    m_new = jnp.maximum(m, jnp.max(s, 0, keepdims=True))
    return m_new, jnp.exp2(m - m_new), jnp.exp2(s - m_new).astype(BF16)


def _attn_finish(acc):
    return (acc[:V_DIM] / acc[V_DIM:V_DIM + 1]).T.astype(BF16)


def _attn_ctx_kernel(q_ref, k_ref, vt_ref, o_ref):
    m0 = jnp.full((1, N_CTX), -jnp.inf, F32)
    for b in range(CTX_PER_STEP):
        rows = slice(b * N_CTX, (b + 1) * N_CTX)
        for h in range(H):
            qh = q_ref[rows, h * HEAD_PAD:(h + 1) * HEAD_PAD]
            kh = k_ref[rows, h * HEAD_PAD:(h + 1) * HEAD_PAD]
            _, _, p = _probs(_dot_nt(kh, qh), m0)
            acc = _dot(vt_ref[h * V_AUG:(h + 1) * V_AUG, rows], p)
            o_ref[rows, h * V_DIM:(h + 1) * V_DIM] = _attn_finish(acc)


def _attn_ctx(q, k, vt):
    n = CTX_PER_STEP * N_CTX
    return pl.pallas_call(
        _attn_ctx_kernel,
        grid=(N_CTX_B // CTX_PER_STEP,),
        in_specs=[pl.BlockSpec((n, H * HEAD_PAD), lambda b: (b, 0)),
                  pl.BlockSpec((n, H * HEAD_PAD), lambda b: (b, 0)),
                  pl.BlockSpec((H * V_AUG, n), lambda b: (0, b))],
        out_specs=pl.BlockSpec((n, H * V_DIM), lambda b: (b, 0)),
        out_shape=jax.ShapeDtypeStruct((T_CTX, H * V_DIM), BF16),
        compiler_params=_cp(("arbitrary",)),
        name="attn_ctx",
    )(q, k, vt)


def _attn_lat_kernel(q_ref, k_ref, vt_ref, kc_ref, vct_ref, o_ref):
    q = q_ref[...]
    n_lat = N_LAT // TK

    def keys(c):
        return k_ref[c * TK:(c + 1) * TK, :] if c < n_lat else kc_ref[...]

    def values_t(c):
        return vt_ref[:, c * TK:(c + 1) * TK] if c < n_lat else vct_ref[...]

    m = jnp.full((1, TQ), -jnp.inf, F32)
    acc = jnp.zeros((V_AUG, TQ), F32)
    s_next = _dot_nt(keys(0), q)
    m_next = jnp.maximum(m, jnp.max(s_next, 0, keepdims=True))
    p_prev = alpha_prev = None
    for c in range(n_lat + 1):
        s, m_new = s_next, m_next
        if c < n_lat:
            s_next = _dot_nt(keys(c + 1), q)
            m_next = jnp.maximum(m_new, jnp.max(s_next, 0, keepdims=True))
        if c > 0:
            acc = alpha_prev * acc + _dot(values_t(c - 1), p_prev)
        alpha_prev = jnp.exp2(m - m_new)
        p_prev = jnp.exp2(s - m_new).astype(BF16)
        m = m_new
    acc = alpha_prev * acc + _dot(values_t(n_lat), p_prev)
    o_ref[...] = _attn_finish(acc)


def _attn_lat(q, k, vt, kc, vct):
    qb0 = T_CTX // TQ
    qpb = N_LAT // TQ
    kb0 = T_CTX // N_LAT
    return pl.pallas_call(
        _attn_lat_kernel,
        grid=(N_LAT_B, H, qpb),
        in_specs=[pl.BlockSpec((TQ, HEAD_PAD), lambda b, h, qi: (qb0 + b * qpb + qi, h)),
                  pl.BlockSpec((N_LAT, HEAD_PAD), lambda b, h, qi: (kb0 + b, h)),
                  pl.BlockSpec((V_AUG, N_LAT), lambda b, h, qi: (h, kb0 + b)),
                  pl.BlockSpec((PAST, HEAD_PAD), lambda b, h, qi: (b, h)),
                  pl.BlockSpec((V_AUG, PAST), lambda b, h, qi: (h, b))],
        out_specs=pl.BlockSpec((TQ, V_DIM), lambda b, h, qi: (b * qpb + qi, h)),
        out_shape=jax.ShapeDtypeStruct((T_LAT, H * V_DIM), BF16),
        compiler_params=_cp(("arbitrary", "arbitrary", "arbitrary")),
        name="attn_lat",
    )(q, k, vt, kc, vct)


def _route(logits):
    lane_i = lax.broadcasted_iota(I32, logits.shape, 1)
    lane = lane_i.astype(F32)
    neg = jnp.float32(-jnp.inf)
    far = jnp.float32(LANES)
    gl = jnp.where(lane_i < N_GRP, logits, neg)
    gmax = jnp.max(gl, -1, keepdims=True)
    gidx = jnp.min(jnp.where(gl == gmax, lane, far), -1, keepdims=True)
    p_top = 1.0 / jnp.sum(jnp.exp(gl - gmax), -1, keepdims=True)
    grp_of_lane = ((lane_i - N_GRP) >> 3).astype(F32)
    in_grp = (lane_i >= N_GRP) & (lane_i < N_GRP + N_EXP) & (grp_of_lane == gidx)
    el = jnp.where(in_grp, logits, neg)
    m1 = jnp.max(el, -1, keepdims=True)
    i1 = jnp.min(jnp.where(el == m1, lane, far), -1, keepdims=True)
    el2 = jnp.where(lane == i1, neg, el)
    m2 = jnp.max(el2, -1, keepdims=True)
    i2 = jnp.min(jnp.where(el2 == m2, lane, far), -1, keepdims=True)
    e2 = jnp.exp(m2 - m1)
    w1 = p_top / (1.0 + e2)
    w2 = p_top * e2 / (1.0 + e2)
    return i1 - N_GRP, i2 - N_GRP, w1, w2


def _pair(a, b, dtype):
    two = lax.broadcasted_iota(I32, (a.shape[0], TOP_K), 1)
    return jnp.where(two == 0, a, b).astype(dtype)


def _post_kernel(ac_ref, al_ref, gm_ref, xc_ref, xl_ref, mods_ref, wout_ref, g1_ref, b1_ref, wrt_ref,
                 brt_ref, x1_ref, h2p_ref, ridx_ref, rwt_ref):
    i = pl.program_id(0)
    is_ctx = i < T_CTX // TP
    m = mods_ref[0]
    gate1, shift2, scale2 = m[2:3], m[3:4], m[4:5]

    def mix_of(rows):
        attn = jnp.where(is_ctx, ac_ref[rows, :], al_ref[rows, :])
        return _dot(jnp.concatenate([attn, gm_ref[rows, :]], axis=1), wout_ref[...])

    bounds = np.cumsum((0,) + POST_SUBS)
    subs = [slice(int(a), int(b)) for a, b in zip(bounds[:-1], bounds[1:])]
    mix_next = mix_of(subs[0])
    for s, rows in enumerate(subs):
        mix = mix_next
        if s + 1 < len(subs):
            mix_next = mix_of(subs[s + 1])
        x = jnp.where(is_ctx, xc_ref[rows, :], xl_ref[rows, :])
        x1 = _layer_norm(ALPHA * x + gate1 * mix, g1_ref[...], b1_ref[...])
        x1_ref[rows, :] = x1
        h2 = x1 * (1.0 + scale2) + shift2
        _store_rows_as_tiles(h2p_ref, rows.start, _pack_pair(h2[:, :D // 2], h2[:, D // 2:]))
        e1, e2, w1, w2 = _route(_dot(h2.astype(BF16), wrt_ref[...]) + brt_ref[...])
        ridx_ref[rows, :] = _pair(e1, e2, I32)
        rwt_ref[rows, :] = _pair(w1, w2, F32)


def _post(attn_c, attn_l, gm, xc, xl, mods, wout, g1, b1, wrt, brt):
    npt = T_CTX // TP

    def tok(width):
        return pl.BlockSpec((TP, width), lambda i: (i, 0))

    def ctx(width):
        return pl.BlockSpec((TP, width), lambda i: (jnp.minimum(i, npt - 1), 0))

    def lat(width):
        return pl.BlockSpec((TP, width), lambda i: (jnp.maximum(i - npt, 0), 0))

    return pl.pallas_call(
        _post_kernel,
        grid=(T // TP,),
        in_specs=[ctx(H * V_DIM), lat(H * V_DIM), tok(GW), ctx(D), lat(D),
                  pl.BlockSpec((1, 6, D), lambda i: (_mods_index(i, TP), 0, 0)),
                  _const_spec((D, D)), _const_spec((1, D)), _const_spec((1, D)),
                  _const_spec((D, LANES)), _const_spec((1, LANES))],
        out_specs=[tok(D), pl.BlockSpec((TP * RT, LANES), lambda i: (i, 0)), tok(TOP_K), tok(TOP_K)],
        out_shape=[jax.ShapeDtypeStruct((T, D), F32),
                   jax.ShapeDtypeStruct((T * RT, LANES), U32),
                   jax.ShapeDtypeStruct((T, TOP_K), I32),
                   jax.ShapeDtypeStruct((T, TOP_K), F32)],
        compiler_params=_cp(("arbitrary",)),
        name="post",
    )(attn_c, attn_l, gm, xc, xl, mods, wout, g1, b1, wrt, brt)


def _as_lane_row(col):
    return jnp.broadcast_to(col, (col.shape[0], LANES)).T[0:1, :]


def _rank_kernel(ridx_ref, tri_ref, e_ref, rank_ref, hist_ref):
    lane = lax.broadcasted_iota(I32, (SUB, LANES), 1).astype(F32)
    seen = jnp.zeros((1, LANES), F32)
    for s in range(TP // SUB):
        rows = slice(s * SUB, (s + 1) * SUB)
        e = ridx_ref[rows, :].astype(F32)
        ohs = [(lane == e[:, k:k + 1]).astype(F32) for k in range(TOP_K)]
        both = ohs[0] + ohs[1]
        before = _dot(tri_ref[...], both.astype(BF16)) + seen
        for k in range(TOP_K):
            rank_k = jnp.sum(before * ohs[k], -1, keepdims=True)
            e_ref[0, k:k + 1, rows] = _as_lane_row(e[:, k:k + 1]).astype(I32)
            rank_ref[0, k:k + 1, rows] = _as_lane_row(rank_k).astype(I32)
        seen = seen + jnp.sum(both, 0, keepdims=True)
    hist_ref[0] = jnp.broadcast_to(seen, (8, LANES))


def _rank(ridx):
    dense = pl.BlockSpec((1, TOP_K, TP), lambda i: (i, 0, 0))
    return pl.pallas_call(
        _rank_kernel,
        grid=(T // TP,),
        in_specs=[pl.BlockSpec((TP, TOP_K), lambda i: (i, 0)), _const_spec((SUB, SUB))],
        out_specs=[dense, dense, pl.BlockSpec((1, 8, LANES), lambda i: (i, 0, 0))],
        out_shape=[jax.ShapeDtypeStruct((T // TP, TOP_K, TP), I32),
                   jax.ShapeDtypeStruct((T // TP, TOP_K, TP), I32),
                   jax.ShapeDtypeStruct((T // TP, 8, LANES), F32)],
        compiler_params=_cp(("arbitrary",)),
        name="rank",
    )(ridx, jnp.asarray(np.tril(np.ones((SUB, SUB), np.float32), -1), BF16))


def _row_copy(src, src_row, dst, dst_row, sem):
    def tile(ref, row):
        return ref.at[pl.ds(pl.multiple_of(row * RT, RT), RT)]

    return pltpu.make_async_copy(tile(src, src_row), tile(dst, dst_row), sem)


def _dispatch_kernel(zb_ref, zv_ref, d0_ref, d1_ref, h2p_ref, xs_ref, zbuf, sem):
    i = pl.program_id(0)
    dest_refs = (d0_ref, d1_ref)

    @pl.when(i == 0)
    def _():
        zbuf[...] = jnp.zeros_like(zbuf)

        def zero_block(j):
            start = pl.multiple_of(zb_ref[j] * (BM * RT), BM * RT)
            return pltpu.make_async_copy(zbuf, xs_ref.at[pl.ds(start, BM * RT)], sem)

        def start(j, carry):
            @pl.when(zv_ref[j] != 0)
            def _():
                zero_block(j).start()
            return carry

        def wait(j, carry):
            @pl.when(zv_ref[j] != 0)
            def _():
                zero_block(j).wait()
            return carry

        lax.fori_loop(0, N_ZERO, start, 0)
        lax.fori_loop(0, N_ZERO, wait, 0)

    def issue(r, carry):
        for k in range(TOP_K):
            _row_copy(h2p_ref, r, xs_ref, dest_refs[k][0, 0, r], sem).start(priority=k)
        return carry

    lax.fori_loop(0, TD, issue, 0, unroll=8)
    for _ in range(TOP_K):
        pltpu.make_async_copy(h2p_ref, xs_ref.at[pl.ds(0, TD * RT)], sem).wait()


def _regroup_steps(dest, rows):
    return [dest[:, k, :].reshape(T // rows, 1, rows) for k in range(TOP_K)]


def _dispatch(zero_blk, zero_valid, dest, h2p):
    grid_spec = pltpu.PrefetchScalarGridSpec(
        num_scalar_prefetch=2,
        grid=(T // TD,),
        in_specs=[pl.BlockSpec((1, 1, TD), lambda i, zb, zv: (i, 0, 0), memory_space=pltpu.SMEM),
                  pl.BlockSpec((1, 1, TD), lambda i, zb, zv: (i, 0, 0), memory_space=pltpu.SMEM),
                  pl.BlockSpec((TD * RT, LANES), lambda i, zb, zv: (i, 0))],
        out_specs=pl.BlockSpec(memory_space=pl.ANY),
        scratch_shapes=[pltpu.VMEM((BM * RT, LANES), U32), pltpu.SemaphoreType.DMA(())],
    )
    return pl.pallas_call(
        _dispatch_kernel,
        grid_spec=grid_spec,
        out_shape=jax.ShapeDtypeStruct((N_BLOCKS * BM * RT, LANES), U32),
        compiler_params=_cp(("arbitrary",)),
        name="dispatch",
    )(zero_blk, zero_valid, *_regroup_steps(dest, TD), h2p)


def _experts_kernel(be_ref, bv_ref, sw_ref, es_ref, nx_ref, x_ref, w1_hbm, w3_hbm, w2_hbm, y_ref,
                    wf1, wf3, wf2, w1b, w3b, w2b, sems):
    i = pl.program_id(0)
    e = be_ref[i]
    nv = bv_ref[i]

    def weight_copies(expert, s):
        return (pltpu.make_async_copy(w1_hbm.at[expert], wf1.at[s], sems.at[s]),
                pltpu.make_async_copy(w3_hbm.at[expert], wf3.at[s], sems.at[s]),
                pltpu.make_async_copy(w2_hbm.at[expert], wf2.at[s], sems.at[s]))

    @pl.when(sw_ref[i] != 0)
    def _():
        s = es_ref[i]

        @pl.when(i == 0)
        def _():
            for c in weight_copies(e, s):
                c.start()

        for c in weight_copies(e, s):
            c.wait()
        nxt = nx_ref[i]

        @pl.when(nxt >= 0)
        def _():
            for c in weight_copies(nxt, 1 - s):
                c.start(priority=1)

        w1b[...] = wf1[s].astype(BF16)
        w3b[...] = wf3[s].astype(BF16)
        w2b[...] = wf2[s].astype(BF16)

    @pl.when(nv > 0)
    def _():
        lo, hi = _unpack_pair(_load_tiles_as_rows(x_ref, 0, BM))
        xb = jnp.concatenate([lo.astype(BF16), hi.astype(BF16)], axis=1)
        a = _dot(xb, w1b[...])
        b = _dot(xb, w3b[...])
        hid = (jax.nn.silu(a) * b).astype(BF16)
        y = _dot(hid, w2b[...])
        _store_rows_as_tiles(y_ref, 0, _pack_pair(y[:, :D // 2], y[:, D // 2:]))

    @pl.when(nv == 0)
    def _():
        y_ref[...] = jnp.zeros_like(y_ref)


def _experts(blk_expert, blk_valid, switch, wslot, next_expert, xs, w1, w3, w2):
    hbm = pl.BlockSpec(memory_space=pl.ANY)
    grid_spec = pltpu.PrefetchScalarGridSpec(
        num_scalar_prefetch=5,
        grid=(N_BLOCKS,),
        in_specs=[pl.BlockSpec((BM * RT, LANES), lambda i, *_: (i, 0)), hbm, hbm, hbm],
        out_specs=pl.BlockSpec((BM * RT, LANES), lambda i, *_: (i, 0)),
        scratch_shapes=[pltpu.VMEM((2, D, E_HID), F32), pltpu.VMEM((2, D, E_HID), F32),
                        pltpu.VMEM((2, E_HID, D), F32),
                        pltpu.VMEM((D, E_HID), BF16), pltpu.VMEM((D, E_HID), BF16),
                        pltpu.VMEM((E_HID, D), BF16), pltpu.SemaphoreType.DMA((2,))],
    )
    return pl.pallas_call(
        _experts_kernel,
        grid_spec=grid_spec,
        out_shape=jax.ShapeDtypeStruct((N_BLOCKS * BM * RT, LANES), U32),
        compiler_params=_cp(("arbitrary",)),
        name="experts",
    )(blk_expert, blk_valid, switch, wslot, next_expert, xs, w1, w3, w2)


def _final_kernel(dcur0_ref, dcur1_ref, dnext0_ref, dnext1_ref, ys_ref, x1_ref, rwt_ref, mods_ref,
                  g2_ref, b2_ref, oc_ref, ol_ref, buf, sems):
    i = pl.program_id(0)
    dcur_ref, dnext_ref = (dcur0_ref, dcur1_ref), (dnext0_ref, dnext1_ref)
    n = pl.num_programs(0)
    slot = i % 2

    def first_row(s, k):
        return (s * TOP_K + k) * TM

    def issue(d_ref, s):
        def body(r, carry):
            for k in range(TOP_K):
                _row_copy(ys_ref, d_ref[k][0, 0, r], buf, first_row(s, k) + r, sems.at[s]).start()
            return carry

        lax.fori_loop(0, TM, body, 0, unroll=8)

    @pl.when(i == 0)
    def _():
        issue(dcur_ref, 0)

    @pl.when(i + 1 < n)
    def _():
        issue(dnext_ref, 1 - slot)

    for k in range(TOP_K):
        pltpu.make_async_copy(ys_ref.at[pl.ds(0, TM * RT)], buf.at[pl.ds(0, TM * RT)], sems.at[slot]).wait()

    w = rwt_ref[...]
    lo0, hi0 = _unpack_pair(_load_tiles_as_rows(buf, first_row(slot, 0), TM))
    lo1, hi1 = _unpack_pair(_load_tiles_as_rows(buf, first_row(slot, 1), TM))
    w0, w1 = w[:, 0:1], w[:, 1:2]
    moe = jnp.concatenate([lo0 * w0 + lo1 * w1, hi0 * w0 + hi1 * w1], axis=1)
    gate2 = mods_ref[0][5:6]
    out = _layer_norm(ALPHA * x1_ref[...] + gate2 * moe, g2_ref[...], b2_ref[...])

    @pl.when(i < NP_TILES)
    def _():
        oc_ref[...] = out

    @pl.when(i >= NP_TILES)
    def _():
        ol_ref[...] = out


def _final(dest_k, ys, x1, rwt, mods, g2, b2):
    n = T // TM
    cur = pl.BlockSpec((1, 1, TM), lambda i: (i, 0, 0), memory_space=pltpu.SMEM)
    nxt = pl.BlockSpec((1, 1, TM), lambda i: (jnp.minimum(i + 1, n - 1), 0, 0), memory_space=pltpu.SMEM)
    return pl.pallas_call(
        _final_kernel,
        grid=(n,),
        in_specs=[cur, cur, nxt, nxt,
                  pl.BlockSpec(memory_space=pl.ANY),
                  pl.BlockSpec((TM, D), lambda i: (i, 0)),
                  pl.BlockSpec((TM, TOP_K), lambda i: (i, 0)),
                  pl.BlockSpec((1, 6, D), lambda i: (_mods_index(i), 0, 0)),
                  _const_spec((1, D)), _const_spec((1, D))],
        out_specs=[pl.BlockSpec((TM, D), lambda i: (jnp.minimum(i, NP_TILES - 1), 0)),
                   pl.BlockSpec((TM, D), lambda i: (jnp.maximum(i - NP_TILES, 0), 0))],
        out_shape=[jax.ShapeDtypeStruct((T_CTX, D), F32), jax.ShapeDtypeStruct((T_LAT, D), F32)],
        scratch_shapes=[pltpu.VMEM((2 * TOP_K * TM * RT, LANES), U32), pltpu.SemaphoreType.DMA((2,))],
        compiler_params=_cp(("arbitrary",)),
        name="final",
    )(*dest_k, *dest_k, ys, x1, rwt, mods, g2, b2)


def _rope_tables():
    n = N_LAT
    rows = n // GRID_W
    f32 = np.float32
    row = np.repeat(np.arange(rows, dtype=f32), GRID_W)
    col = np.tile(np.arange(GRID_W, dtype=f32), rows)
    half = QK_ROPE // 2
    inv = (f32(ROPE_THETA) ** (-np.arange(0, half, 2, dtype=f32) / f32(half))).astype(f32)
    ang_r, ang_c = (row[:, None] * inv).astype(f32), (col[:, None] * inv).astype(f32)
    cr, sr = np.cos(ang_r).astype(f32), np.sin(ang_r).astype(f32)
    cc, sc = np.cos(ang_c).astype(f32), np.sin(ang_c).astype(f32)
    z16 = np.zeros_like(cr)
    pad = lambda v: np.full((n, LANES - QK_ROPE), v, f32)
    cos = np.concatenate([cr, cr, cc, cc, pad(1.0)], axis=1)
    s_plus = np.concatenate([z16, sr, z16, sc, pad(0.0)], axis=1)
    s_minus = np.concatenate([-sr, z16, -sc, z16, pad(0.0)], axis=1)
    ident = lambda v: np.full((TM, LANES), v, f32)
    return (jnp.asarray(np.concatenate([cos, ident(1.0)], 0)),
            jnp.asarray(np.concatenate([s_plus, ident(0.0)], 0)),
            jnp.asarray(np.concatenate([s_minus, ident(0.0)], 0)))


def _block_plan(eid_k, rank, hist):
    counts = jnp.sum(hist, axis=0)
    nblk = (counts + BM - 1) // BM
    bends = jnp.cumsum(nblk)
    bstarts = bends - nblk
    base = bstarts[None, :] * BM + jnp.cumsum(hist, axis=0) - hist
    eid = jnp.arange(N_EXP, dtype=I32)
    hit = eid_k[:, :, None, :] == eid[None, None, :, None]
    first = jnp.sum(jnp.where(hit, base[:, None, :, None], 0), axis=2)
    dest = (first + rank).astype(I32)
    blk = jnp.arange(N_BLOCKS, dtype=I32)
    blk_expert = jnp.minimum(jnp.sum((blk[:, None] >= bends[None, :]).astype(I32), axis=1), N_EXP - 1)
    of_blk = blk_expert[:, None] == eid[None, :]

    def per_block(v):
        return jnp.sum(jnp.where(of_blk, v[None, :], 0), axis=1)

    left = per_block(counts) - (blk - per_block(bstarts)) * BM
    blk_valid = jnp.where(blk < bends[-1], jnp.clip(left, 0, BM), 0).astype(I32)
    unused = bends[-1] + jnp.arange(N_EXP, dtype=I32)
    zero_blk = jnp.concatenate([jnp.maximum(bends - 1, 0), jnp.minimum(unused, N_BLOCKS - 1)])
    zero_valid = jnp.concatenate([counts % BM != 0, unused < N_BLOCKS])
    prev_expert = jnp.concatenate([jnp.full((1,), -1, I32), blk_expert[:-1]])
    switch = (blk < bends[-1]) & (blk_expert != prev_expert)
    wslot = (jnp.cumsum(switch.astype(I32)) - 1) % 2
    later = (eid[None, :] > eid[:, None]) & (nblk[None, :] > 0)
    next_of = jnp.min(jnp.where(later, eid[None, :], N_EXP), axis=1)
    next_expert = per_block(jnp.where(next_of < N_EXP, next_of, -1))
    return (dest, blk_expert.astype(I32), blk_valid, zero_blk.astype(I32),
            zero_valid.astype(I32), switch.astype(I32), wslot.astype(I32), next_expert.astype(I32))


def kernel(x_prompt, x_sample, cache_ckv, cache_krope, c, c_ctx, w_ada, b_ada, w_in, q_norm_g, w_uq,
           kv_norm_g, w_ukv, gmlp_ln_g, gmlp_ln_b, w_spatial, b_spatial, w_out, ln1_g, ln1_b, w_group,
           b_group, w_router, b_router, w1, w3, w2, ln2_g, ln2_b):
    l = 0
    xc = x_prompt.reshape(T_CTX, D)
    xl = x_sample.reshape(T_LAT, D)

    cond16 = jnp.concatenate([c, c_ctx[None, :], jnp.zeros((16 - N_LAT_B - 1, D), F32)], axis=0)
    mods = _adaln(cond16, w_ada[l], b_ada[l][None, :]).reshape(16, 6, D)

    wi = w_in[l]
    o_kr = QL + KVL
    o_u = o_kr + QK_ROPE
    win_a = wi[:, :o_kr].astype(BF16)
    win_g = wi[:, o_u:].astype(BF16)
    win_r = jnp.pad(wi[:, o_kr:o_u], ((0, 0), (0, LANES - QK_ROPE))).astype(BF16)
    wuq = jnp.pad(w_uq[l].reshape(QL, H, QK_NOPE + QK_ROPE),
                  ((0, 0), (0, 0), (0, HEAD_PAD - QK_NOPE - QK_ROPE))).reshape(QL, H * HEAD_PAD).astype(BF16)
    wkv = w_ukv[l].reshape(KVL, H, QK_NOPE + V_DIM)
    wuk = wkv[:, :, :QK_NOPE].reshape(KVL, H * QK_NOPE).astype(BF16)
    wuvt = wkv[:, :, QK_NOPE:].reshape(KVL, H * V_DIM).T.astype(BF16)
    ws = w_spatial[l].astype(BF16)
    bsb = jnp.broadcast_to(b_spatial[l][:, :, None], (G_GROUPS, CHUNK, G_DIM))
    wrt = jnp.concatenate([w_group[l], w_router[l], jnp.zeros((D, LANES - N_GRP - N_EXP), F32)],
                          axis=1).astype(BF16)
    brt = jnp.concatenate([b_group[l], b_router[l], jnp.zeros((LANES - N_GRP - N_EXP,), F32)])[None, :]
    cos_t, sp_t, sm_t = _rope_tables()
    row = lambda v: v[l][None, :]

    q, k, vt, gm, ckv_n, kr = _proj(xc, xl, mods, win_a, win_g, win_r, row(q_norm_g), wuq, row(kv_norm_g), wuk, wuvt,
                                   row(gmlp_ln_g), row(gmlp_ln_b), ws, bsb, cos_t, sp_t, sm_t)
    kc, vct = _kvexp(cache_ckv[:, l].reshape(N_LAT_B * PAST, KVL),
                     cache_krope[:, l].reshape(N_LAT_B * PAST, QK_ROPE), wuk, wuvt)
    attn_c = _attn_ctx(q, k, vt)
    attn_l = _attn_lat(q, k, vt, kc, vct)

    x1, h2p, ridx, rwt = _post(attn_c, attn_l, gm, xc, xl, mods, w_out[l].astype(BF16), row(ln1_g), row(ln1_b),
                               wrt, brt)
    eid_k, rank, hist = _rank(ridx)
    dest, blk_expert, blk_valid, zero_blk, zero_valid, switch, wslot, next_expert = _block_plan(
        eid_k, rank, hist[:, 0, :N_EXP].astype(I32))
    xs = _dispatch(zero_blk, zero_valid, dest, h2p)
    ys = _experts(blk_expert, blk_valid, switch, wslot, next_expert, xs, w1[l], w3[l], w2[l])
    y_c, y_l = _final(_regroup_steps(dest, TM), ys, x1, rwt, mods, row(ln2_g), row(ln2_b))

    return (y_c.reshape(N_CTX_B, N_CTX, D), y_l.reshape(N_LAT_B, N_LAT, D),
            ckv_n.reshape(N_CTX_B, 1, N_CTX, KVL), kr.reshape(N_CTX_B, 1, N_CTX, QK_ROPE))
```

```python
import functools

import jax
import jax.numpy as jnp
import numpy as np
from jax import lax
from jax.experimental import pallas as pl
from jax.experimental.pallas import tpu as pltpu

F32 = jnp.float32
BF16 = jnp.bfloat16
U32 = jnp.uint32
I32 = jnp.int32

D = 2048
N_CTX_B, N_CTX = 32, 256
N_LAT_B, N_LAT = 4, 4096
PAST = 512
GRID_W = 64
H = 8
QK_NOPE, QK_ROPE, V_DIM = 128, 64, 128
QL, KVL = 512, 512
GW = 1024
G_GROUPS, G_DIM, CHUNK = 8, 128, 128
N_GRP, E_PER_GRP, N_EXP, TOP_K = 4, 8, 32, 2
E_HID = 512
ROPE_THETA = 10000.0
EPS = 1e-6
ALPHA = 2.0 ** 0.25
SM_SCALE = (QK_NOPE + QK_ROPE) ** -0.5
Q_SCALE = SM_SCALE * float(np.log2(np.e))

T_CTX = N_CTX_B * N_CTX
T_LAT = N_LAT_B * N_LAT
T = T_CTX + T_LAT
N_ASSIGN = T * TOP_K

LANES = 128
HEAD_PAD = 256
V_AUG = V_DIM + 16
TM = 256
TP = 512
POST_SUBS = (256, 256)
SUB = 256
TD = 1024
CTX_PER_STEP = 2
TQ = 1024
TK = 512
BM = 256
N_BLOCKS = N_ASSIGN // BM + N_EXP
N_ZERO = 2 * N_EXP
VMEM_LIMIT = 56 * 1024 * 1024

NP_TILES = T_CTX // TM
LAT_TILES_PER_B = N_LAT // TM


def _cp(sem):
    return pltpu.CompilerParams(dimension_semantics=sem, vmem_limit_bytes=VMEM_LIMIT)


def _const_spec(shape):
    nd = len(shape)
    return pl.BlockSpec(shape, lambda *a: (0,) * nd, pipeline_mode=pl.Buffered(1))


def _dot(a, b):
    return jnp.dot(a, b, preferred_element_type=F32)


def _dot_nt(a, b):
    return lax.dot_general(a, b, (((1,), (1,)), ((), ())), preferred_element_type=F32)


def _layer_norm(x, g, b):
    mu = jnp.mean(x, -1, keepdims=True)
    xc = x - mu
    var = jnp.mean(xc * xc, -1, keepdims=True)
    return xc * lax.rsqrt(var + EPS) * g + b


def _rms_norm(x, g):
    return x * lax.rsqrt(jnp.mean(x * x, -1, keepdims=True) + EPS) * g


def _gelu(x):
    return 0.5 * x * (1.0 + lax.erf(x * np.float32(np.sqrt(0.5))))


def _pack_pair(lo, hi):
    lo_b = lax.bitcast_convert_type(lo.astype(BF16).astype(F32), U32)
    hi_b = lax.bitcast_convert_type(hi.astype(BF16).astype(F32), U32)
    return hi_b | (lo_b >> 16)


def _unpack_pair(w):
    lo = lax.bitcast_convert_type(w << 16, F32)
    hi = lax.bitcast_convert_type(w & jnp.uint32(0xFFFF0000), F32)
    return lo, hi


RT = 8


def _store_rows_as_tiles(ref, row0, val):
    n = val.shape[0]
    for j in range(RT):
        ref[pl.ds(row0 * RT + j, n, stride=RT), :] = val[:, j * LANES:(j + 1) * LANES]


def _load_tiles_as_rows(ref, row0, n):
    return jnp.concatenate([ref[pl.ds(row0 * RT + j, n, stride=RT), :] for j in range(RT)], axis=1)


def _adaln_kernel(cond_ref, w_ref, b_ref, o_ref):
    s = jax.nn.silu(cond_ref[...])
    s_hi = s.astype(BF16)
    s_lo = (s - s_hi.astype(F32)).astype(BF16)
    w = w_ref[...]
    w_hi = w.astype(BF16)
    w_lo = (w - w_hi.astype(F32)).astype(BF16)
    o_ref[...] = _dot(s_hi, w_hi) + _dot(s_lo, w_hi) + _dot(s_hi, w_lo) + b_ref[...]


def _adaln(cond16, w_ada, b_ada):
    tn = 1024
    n = w_ada.shape[1]
    return pl.pallas_call(
        _adaln_kernel,
        grid=(n // tn,),
        in_specs=[pl.BlockSpec((16, D), lambda j: (0, 0)),
                  pl.BlockSpec((D, tn), lambda j: (0, j)),
                  pl.BlockSpec((1, tn), lambda j: (0, j))],
        out_specs=pl.BlockSpec((16, tn), lambda j: (0, j)),
        out_shape=jax.ShapeDtypeStruct((16, n), F32),
        compiler_params=_cp(("arbitrary",)),
        name="adaln",
    )(cond16, w_ada, b_ada)


def _expand_kv(ckv_n, kr128, wuk_ref, wuvt_ref, k_ref, vt_ref):
    cb = ckv_n.astype(BF16)
    kn = _dot(cb, wuk_ref[...])
    vt = _dot_nt(wuvt_ref[...], cb).astype(BF16)
    ones = jnp.ones((V_AUG - V_DIM, vt.shape[1]), BF16)
    for h in range(H):
        vt_ref[h * V_AUG:h * V_AUG + V_DIM, :] = vt[h * V_DIM:(h + 1) * V_DIM]
        vt_ref[h * V_AUG + V_DIM:(h + 1) * V_AUG, :] = ones
    krb = kr128.astype(BF16)
    for h in range(H):
        k_ref[:, h * HEAD_PAD:h * HEAD_PAD + QK_NOPE] = kn[:, h * QK_NOPE:(h + 1) * QK_NOPE].astype(BF16)
        k_ref[:, h * HEAD_PAD + QK_NOPE:(h + 1) * HEAD_PAD] = krb


def _rope128(x, cos, s_plus, s_minus):
    return (x * cos + pltpu.roll(x, 16, 1) * s_plus + pltpu.roll(x, LANES - 16, 1) * s_minus)


def _proj_kernel(xc_ref, xl_ref, mods_ref, win_a_ref, win_g_ref, win_r_ref, qg_ref, wuq_ref, kvg_ref,
                 wuk_ref, wuv_ref,
                 lng_ref, lnb_ref, ws_ref, bsb_ref, cos_ref, sp_ref, sm_ref,
                 q_ref, k_ref, v_ref, gm_ref, ckv_ref, kr_ref):
    i = pl.program_id(0)
    x = jnp.where(i < NP_TILES, xc_ref[...], xl_ref[...])
    m = mods_ref[0]
    shift1, scale1 = m[0:1], m[1:2]
    hmod = (x * (1.0 + scale1) + shift1).astype(BF16)
    pg = _dot(hmod, win_g_ref[...])
    u, v = pg[:, :GW], pg[:, GW:]
    gu = _gelu(u)
    vn = _layer_norm(_gelu(v), lng_ref[...], lnb_ref[...]).astype(BF16)
    pa = _dot(hmod, win_a_ref[...])
    kr = _dot(hmod, win_r_ref[...])
    cq, ckv = pa[:, :QL], pa[:, QL:]

    cos, s_plus, s_minus = cos_ref[...], sp_ref[...], sm_ref[...]

    cq_n = _rms_norm(cq, qg_ref[...]).astype(BF16)
    q = _dot(cq_n, wuq_ref[...]) * Q_SCALE
    for h in range(H):
        q_ref[:, h * HEAD_PAD:h * HEAD_PAD + QK_NOPE] = q[:, h * HEAD_PAD:h * HEAD_PAD + QK_NOPE].astype(BF16)
        q_ref[:, h * HEAD_PAD + QK_NOPE:(h + 1) * HEAD_PAD] = _rope128(
            q[:, h * HEAD_PAD + QK_NOPE:(h + 1) * HEAD_PAD], cos, s_plus, s_minus).astype(BF16)

    ckv_n = _rms_norm(ckv, kvg_ref[...])
    _expand_kv(ckv_n, _rope128(kr, cos, s_plus, s_minus), wuk_ref, wuv_ref, k_ref, v_ref)

    for c in range(TM // CHUNK):
        rows = slice(c * CHUNK, (c + 1) * CHUNK)
        for g in range(G_GROUPS):
            cols = slice(g * G_DIM, (g + 1) * G_DIM)
            s = _dot(ws_ref[g], vn[rows, cols]) + bsb_ref[g]
            gm_ref[rows, cols] = (gu[rows, cols] * s).astype(BF16)

    @pl.when(i < NP_TILES)
    def _():
        ckv_ref[...] = ckv_n
        kr_ref[...] = kr[:, :QK_ROPE]


def _mods_index(i, tile=TM):
    return jnp.where(i < T_CTX // tile, N_LAT_B, (i - T_CTX // tile) // (N_LAT // tile))


def _proj(xc, xl, mods, win_a, win_g, win_r, qg, wuq, kvg, wuk, wuv, lng, lnb, ws, bsb, cos_t, sp_t, sm_t):
    def tok(width):
        return pl.BlockSpec((TM, width), lambda i: (i, 0))

    def pos_block(i):
        return jnp.where(i < NP_TILES, LAT_TILES_PER_B, (i - NP_TILES) % LAT_TILES_PER_B)

    rope_spec = pl.BlockSpec((TM, LANES), lambda i: (pos_block(i), 0))
    return pl.pallas_call(
        _proj_kernel,
        grid=(T // TM,),
        in_specs=[
            pl.BlockSpec((TM, D), lambda i: (jnp.minimum(i, NP_TILES - 1), 0)),
            pl.BlockSpec((TM, D), lambda i: (jnp.maximum(i - NP_TILES, 0), 0)),
            pl.BlockSpec((1, 6, D), lambda i: (_mods_index(i), 0, 0)),
            _const_spec((D, QL + KVL)), _const_spec((D, 2 * GW)), _const_spec((D, LANES)),
            _const_spec((1, QL)), _const_spec((QL, H * HEAD_PAD)),
            _const_spec((1, KVL)), _const_spec((KVL, H * QK_NOPE)), _const_spec((H * V_DIM, KVL)),
            _const_spec((1, GW)), _const_spec((1, GW)),
            _const_spec((G_GROUPS, CHUNK, CHUNK)), _const_spec((G_GROUPS, CHUNK, G_DIM)),
            rope_spec, rope_spec, rope_spec,
        ],
        out_specs=[tok(H * HEAD_PAD), tok(H * HEAD_PAD),
                   pl.BlockSpec((H * V_AUG, TM), lambda i: (0, i)), tok(GW),
                   pl.BlockSpec((TM, KVL), lambda i: (jnp.minimum(i, NP_TILES - 1), 0)),
                   pl.BlockSpec((TM, QK_ROPE), lambda i: (jnp.minimum(i, NP_TILES - 1), 0))],
        out_shape=[
            jax.ShapeDtypeStruct((T, H * HEAD_PAD), BF16),
            jax.ShapeDtypeStruct((T, H * HEAD_PAD), BF16),
            jax.ShapeDtypeStruct((H * V_AUG, T), BF16),
            jax.ShapeDtypeStruct((T, GW), BF16),
            jax.ShapeDtypeStruct((T_CTX, KVL), F32),
            jax.ShapeDtypeStruct((T_CTX, QK_ROPE), F32),
        ],
        compiler_params=_cp(("arbitrary",)),
        name="proj",
    )(xc, xl, mods, win_a, win_g, win_r, qg, wuq, kvg, wuk, wuv, lng, lnb, ws, bsb, cos_t, sp_t, sm_t)


def _kvexp_kernel(ckv_ref, kr_ref, wuk_ref, wuv_ref, k_ref, v_ref):
    kr = kr_ref[...]
    kr128 = jnp.concatenate([kr, jnp.zeros_like(kr)], axis=1)
    _expand_kv(ckv_ref[...], kr128, wuk_ref, wuv_ref, k_ref, v_ref)


def _kvexp(ckv, kr, wuk, wuv):
    rows = ckv.shape[0]
    return pl.pallas_call(
        _kvexp_kernel,
        grid=(rows // TM,),
        in_specs=[pl.BlockSpec((TM, KVL), lambda i: (i, 0)), pl.BlockSpec((TM, QK_ROPE), lambda i: (i, 0)),
                  _const_spec((KVL, H * QK_NOPE)), _const_spec((H * V_DIM, KVL))],
        out_specs=[pl.BlockSpec((TM, H * HEAD_PAD), lambda i: (i, 0)),
                   pl.BlockSpec((H * V_AUG, TM), lambda i: (0, i))],
        out_shape=[jax.ShapeDtypeStruct((rows, H * HEAD_PAD), BF16),
                   jax.ShapeDtypeStruct((H * V_AUG, rows), BF16)],
        compiler_params=_cp(("arbitrary",)),
        name="kvexp",
    )(ckv, kr, wuk, wuv)


def _probs(s, m):
    m_new = jnp.maximum(m, jnp.max(s, 0, keepdims=True))
    return m_new, jnp.exp2(m - m_new), jnp.exp2(s - m_new).astype(BF16)


def _attn_finish(acc):
    return (acc[:V_DIM] / acc[V_DIM:V_DIM + 1]).T.astype(BF16)


def _attn_ctx_kernel(q_ref, k_ref, vt_ref, o_ref):
    m0 = jnp.full((1, N_CTX), -jnp.inf, F32)
    for b in range(CTX_PER_STEP):
        rows = slice(b * N_CTX, (b + 1) * N_CTX)
        for h in range(H):
            qh = q_ref[rows, h * HEAD_PAD:(h + 1) * HEAD_PAD]
            kh = k_ref[rows, h * HEAD_PAD:(h + 1) * HEAD_PAD]
            _, _, p = _probs(_dot_nt(kh, qh), m0)
            acc = _dot(vt_ref[h * V_AUG:(h + 1) * V_AUG, rows], p)
            o_ref[rows, h * V_DIM:(h + 1) * V_DIM] = _attn_finish(acc)


def _attn_ctx(q, k, vt):
    n = CTX_PER_STEP * N_CTX
    return pl.pallas_call(
        _attn_ctx_kernel,
        grid=(N_CTX_B // CTX_PER_STEP,),
        in_specs=[pl.BlockSpec((n, H * HEAD_PAD), lambda b: (b, 0)),
                  pl.BlockSpec((n, H * HEAD_PAD), lambda b: (b, 0)),
                  pl.BlockSpec((H * V_AUG, n), lambda b: (0, b))],
        out_specs=pl.BlockSpec((n, H * V_DIM), lambda b: (b, 0)),
        out_shape=jax.ShapeDtypeStruct((T_CTX, H * V_DIM), BF16),
        compiler_params=_cp(("arbitrary",)),
        name="attn_ctx",
    )(q, k, vt)


def _attn_lat_kernel(q_ref, k_ref, vt_ref, kc_ref, vct_ref, o_ref):
    q = q_ref[...]
    n_lat = N_LAT // TK

    def keys(c):
        return k_ref[c * TK:(c + 1) * TK, :] if c < n_lat else kc_ref[...]

    def values_t(c):
        return vt_ref[:, c * TK:(c + 1) * TK] if c < n_lat else vct_ref[...]

    m = jnp.full((1, TQ), -jnp.inf, F32)
    acc = jnp.zeros((V_AUG, TQ), F32)
    s_next = _dot_nt(keys(0), q)
    m_next = jnp.maximum(m, jnp.max(s_next, 0, keepdims=True))
    p_prev = alpha_prev = None
    for c in range(n_lat + 1):
        s, m_new = s_next, m_next
        if c < n_lat:
            s_next = _dot_nt(keys(c + 1), q)
            m_next = jnp.maximum(m_new, jnp.max(s_next, 0, keepdims=True))
        if c > 0:
            acc = alpha_prev * acc + _dot(values_t(c - 1), p_prev)
        alpha_prev = jnp.exp2(m - m_new)
        p_prev = jnp.exp2(s - m_new).astype(BF16)
        m = m_new
    acc = alpha_prev * acc + _dot(values_t(n_lat), p_prev)
    o_ref[...] = _attn_finish(acc)


def _attn_lat(q, k, vt, kc, vct):
    qb0 = T_CTX // TQ
    qpb = N_LAT // TQ
    kb0 = T_CTX // N_LAT
    return pl.pallas_call(
        _attn_lat_kernel,
        grid=(N_LAT_B, H, qpb),
        in_specs=[pl.BlockSpec((TQ, HEAD_PAD), lambda b, h, qi: (qb0 + b * qpb + qi, h)),
                  pl.BlockSpec((N_LAT, HEAD_PAD), lambda b, h, qi: (kb0 + b, h)),
                  pl.BlockSpec((V_AUG, N_LAT), lambda b, h, qi: (h, kb0 + b)),
                  pl.BlockSpec((PAST, HEAD_PAD), lambda b, h, qi: (b, h)),
                  pl.BlockSpec((V_AUG, PAST), lambda b, h, qi: (h, b))],
        out_specs=pl.BlockSpec((TQ, V_DIM), lambda b, h, qi: (b * qpb + qi, h)),
        out_shape=jax.ShapeDtypeStruct((T_LAT, H * V_DIM), BF16),
        compiler_params=_cp(("arbitrary", "arbitrary", "arbitrary")),
        name="attn_lat",
    )(q, k, vt, kc, vct)


def _route(logits):
    lane_i = lax.broadcasted_iota(I32, logits.shape, 1)
    lane = lane_i.astype(F32)
    neg = jnp.float32(-jnp.inf)
    far = jnp.float32(LANES)
    gl = jnp.where(lane_i < N_GRP, logits, neg)
    gmax = jnp.max(gl, -1, keepdims=True)
    gidx = jnp.min(jnp.where(gl == gmax, lane, far), -1, keepdims=True)
    p_top = 1.0 / jnp.sum(jnp.exp(gl - gmax), -1, keepdims=True)
    grp_of_lane = ((lane_i - N_GRP) >> 3).astype(F32)
    in_grp = (lane_i >= N_GRP) & (lane_i < N_GRP + N_EXP) & (grp_of_lane == gidx)
    el = jnp.where(in_grp, logits, neg)
    m1 = jnp.max(el, -1, keepdims=True)
    i1 = jnp.min(jnp.where(el == m1, lane, far), -1, keepdims=True)
    el2 = jnp.where(lane == i1, neg, el)
    m2 = jnp.max(el2, -1, keepdims=True)
    i2 = jnp.min(jnp.where(el2 == m2, lane, far), -1, keepdims=True)
    e2 = jnp.exp(m2 - m1)
    w1 = p_top / (1.0 + e2)
    w2 = p_top * e2 / (1.0 + e2)
    return i1 - N_GRP, i2 - N_GRP, w1, w2


def _pair(a, b, dtype):
    two = lax.broadcasted_iota(I32, (a.shape[0], TOP_K), 1)
    return jnp.where(two == 0, a, b).astype(dtype)


def _post_kernel(ac_ref, al_ref, gm_ref, xc_ref, xl_ref, mods_ref, wout_ref, g1_ref, b1_ref, wrt_ref,
                 brt_ref, x1_ref, h2p_ref, ridx_ref, rwt_ref):
    i = pl.program_id(0)
    is_ctx = i < T_CTX // TP
    m = mods_ref[0]
    gate1, shift2, scale2 = m[2:3], m[3:4], m[4:5]

    def mix_of(rows):
        attn = jnp.where(is_ctx, ac_ref[rows, :], al_ref[rows, :])
        return _dot(jnp.concatenate([attn, gm_ref[rows, :]], axis=1), wout_ref[...])

    bounds = np.cumsum((0,) + POST_SUBS)
    subs = [slice(int(a), int(b)) for a, b in zip(bounds[:-1], bounds[1:])]
    mix_next = mix_of(subs[0])
    for s, rows in enumerate(subs):
        mix = mix_next
        if s + 1 < len(subs):
            mix_next = mix_of(subs[s + 1])
        x = jnp.where(is_ctx, xc_ref[rows, :], xl_ref[rows, :])
        x1 = _layer_norm(ALPHA * x + gate1 * mix, g1_ref[...], b1_ref[...])
        x1_ref[rows, :] = x1
        h2 = x1 * (1.0 + scale2) + shift2
        _store_rows_as_tiles(h2p_ref, rows.start, _pack_pair(h2[:, :D // 2], h2[:, D // 2:]))
        e1, e2, w1, w2 = _route(_dot(h2.astype(BF16), wrt_ref[...]) + brt_ref[...])
        ridx_ref[rows, :] = _pair(e1, e2, I32)
        rwt_ref[rows, :] = _pair(w1, w2, F32)


def _post(attn_c, attn_l, gm, xc, xl, mods, wout, g1, b1, wrt, brt):
    npt = T_CTX // TP

    def tok(width):
        return pl.BlockSpec((TP, width), lambda i: (i, 0))

    def ctx(width):
        return pl.BlockSpec((TP, width), lambda i: (jnp.minimum(i, npt - 1), 0))

    def lat(width):
        return pl.BlockSpec((TP, width), lambda i: (jnp.maximum(i - npt, 0), 0))

    return pl.pallas_call(
        _post_kernel,
        grid=(T // TP,),
        in_specs=[ctx(H * V_DIM), lat(H * V_DIM), tok(GW), ctx(D), lat(D),
                  pl.BlockSpec((1, 6, D), lambda i: (_mods_index(i, TP), 0, 0)),
                  _const_spec((D, D)), _const_spec((1, D)), _const_spec((1, D)),
                  _const_spec((D, LANES)), _const_spec((1, LANES))],
        out_specs=[tok(D), pl.BlockSpec((TP * RT, LANES), lambda i: (i, 0)), tok(TOP_K), tok(TOP_K)],
        out_shape=[jax.ShapeDtypeStruct((T, D), F32),
                   jax.ShapeDtypeStruct((T * RT, LANES), U32),
                   jax.ShapeDtypeStruct((T, TOP_K), I32),
                   jax.ShapeDtypeStruct((T, TOP_K), F32)],
        compiler_params=_cp(("arbitrary",)),
        name="post",
    )(attn_c, attn_l, gm, xc, xl, mods, wout, g1, b1, wrt, brt)


def _as_lane_row(col):
    return jnp.broadcast_to(col, (col.shape[0], LANES)).T[0:1, :]


def _rank_kernel(ridx_ref, tri_ref, e_ref, rank_ref, hist_ref):
    lane = lax.broadcasted_iota(I32, (SUB, LANES), 1).astype(F32)
    seen = jnp.zeros((1, LANES), F32)
    for s in range(TP // SUB):
        rows = slice(s * SUB, (s + 1) * SUB)
        e = ridx_ref[rows, :].astype(F32)
        ohs = [(lane == e[:, k:k + 1]).astype(F32) for k in range(TOP_K)]
        both = ohs[0] + ohs[1]
        before = _dot(tri_ref[...], both.astype(BF16)) + seen
        for k in range(TOP_K):
            rank_k = jnp.sum(before * ohs[k], -1, keepdims=True)
            e_ref[0, k:k + 1, rows] = _as_lane_row(e[:, k:k + 1]).astype(I32)
            rank_ref[0, k:k + 1, rows] = _as_lane_row(rank_k).astype(I32)
        seen = seen + jnp.sum(both, 0, keepdims=True)
    hist_ref[0] = jnp.broadcast_to(seen, (8, LANES))


def _rank(ridx):
    dense = pl.BlockSpec((1, TOP_K, TP), lambda i: (i, 0, 0))
    return pl.pallas_call(
        _rank_kernel,
        grid=(T // TP,),
        in_specs=[pl.BlockSpec((TP, TOP_K), lambda i: (i, 0)), _const_spec((SUB, SUB))],
        out_specs=[dense, dense, pl.BlockSpec((1, 8, LANES), lambda i: (i, 0, 0))],
        out_shape=[jax.ShapeDtypeStruct((T // TP, TOP_K, TP), I32),
                   jax.ShapeDtypeStruct((T // TP, TOP_K, TP), I32),
                   jax.ShapeDtypeStruct((T // TP, 8, LANES), F32)],
        compiler_params=_cp(("arbitrary",)),
        name="rank",
    )(ridx, jnp.asarray(np.tril(np.ones((SUB, SUB), np.float32), -1), BF16))


def _row_copy(src, src_row, dst, dst_row, sem):
    def tile(ref, row):
        return ref.at[pl.ds(pl.multiple_of(row * RT, RT), RT)]

    return pltpu.make_async_copy(tile(src, src_row), tile(dst, dst_row), sem)


def _dispatch_kernel(zb_ref, zv_ref, d0_ref, d1_ref, h2p_ref, xs_ref, zbuf, sem):
    i = pl.program_id(0)
    dest_refs = (d0_ref, d1_ref)

    @pl.when(i == 0)
    def _():
        zbuf[...] = jnp.zeros_like(zbuf)

        def zero_block(j):
            start = pl.multiple_of(zb_ref[j] * (BM * RT), BM * RT)
            return pltpu.make_async_copy(zbuf, xs_ref.at[pl.ds(start, BM * RT)], sem)

        def start(j, carry):
            @pl.when(zv_ref[j] != 0)
            def _():
                zero_block(j).start()
            return carry

        def wait(j, carry):
            @pl.when(zv_ref[j] != 0)
            def _():
                zero_block(j).wait()
            return carry

        lax.fori_loop(0, N_ZERO, start, 0)
        lax.fori_loop(0, N_ZERO, wait, 0)

    def issue(r, carry):
        for k in range(TOP_K):
            _row_copy(h2p_ref, r, xs_ref, dest_refs[k][0, 0, r], sem).start(priority=k)
        return carry

    lax.fori_loop(0, TD, issue, 0, unroll=8)
    for _ in range(TOP_K):
        pltpu.make_async_copy(h2p_ref, xs_ref.at[pl.ds(0, TD * RT)], sem).wait()


def _regroup_steps(dest, rows):
    return [dest[:, k, :].reshape(T // rows, 1, rows) for k in range(TOP_K)]


def _dispatch(zero_blk, zero_valid, dest, h2p):
    grid_spec = pltpu.PrefetchScalarGridSpec(
        num_scalar_prefetch=2,
        grid=(T // TD,),
        in_specs=[pl.BlockSpec((1, 1, TD), lambda i, zb, zv: (i, 0, 0), memory_space=pltpu.SMEM),
                  pl.BlockSpec((1, 1, TD), lambda i, zb, zv: (i, 0, 0), memory_space=pltpu.SMEM),
                  pl.BlockSpec((TD * RT, LANES), lambda i, zb, zv: (i, 0))],
        out_specs=pl.BlockSpec(memory_space=pl.ANY),
        scratch_shapes=[pltpu.VMEM((BM * RT, LANES), U32), pltpu.SemaphoreType.DMA(())],
    )
    return pl.pallas_call(
        _dispatch_kernel,
        grid_spec=grid_spec,
        out_shape=jax.ShapeDtypeStruct((N_BLOCKS * BM * RT, LANES), U32),
        compiler_params=_cp(("arbitrary",)),
        name="dispatch",
    )(zero_blk, zero_valid, *_regroup_steps(dest, TD), h2p)


def _experts_kernel(be_ref, bv_ref, sw_ref, es_ref, nx_ref, x_ref, w1_hbm, w3_hbm, w2_hbm, y_ref,
                    wf1, wf3, wf2, w1b, w3b, w2b, sems):
    i = pl.program_id(0)
    e = be_ref[i]
    nv = bv_ref[i]

    def weight_copies(expert, s):
        return (pltpu.make_async_copy(w1_hbm.at[expert], wf1.at[s], sems.at[s]),
                pltpu.make_async_copy(w3_hbm.at[expert], wf3.at[s], sems.at[s]),
                pltpu.make_async_copy(w2_hbm.at[expert], wf2.at[s], sems.at[s]))

    @pl.when(sw_ref[i] != 0)
    def _():
        s = es_ref[i]

        @pl.when(i == 0)
        def _():
            for c in weight_copies(e, s):
                c.start()

        for c in weight_copies(e, s):
            c.wait()
        nxt = nx_ref[i]

        @pl.when(nxt >= 0)
        def _():
            for c in weight_copies(nxt, 1 - s):
                c.start(priority=1)

        w1b[...] = wf1[s].astype(BF16)
        w3b[...] = wf3[s].astype(BF16)
        w2b[...] = wf2[s].astype(BF16)

    @pl.when(nv > 0)
    def _():
        lo, hi = _unpack_pair(_load_tiles_as_rows(x_ref, 0, BM))
        xb = jnp.concatenate([lo.astype(BF16), hi.astype(BF16)], axis=1)
        a = _dot(xb, w1b[...])
        b = _dot(xb, w3b[...])
        hid = (jax.nn.silu(a) * b).astype(BF16)
        y = _dot(hid, w2b[...])
        _store_rows_as_tiles(y_ref, 0, _pack_pair(y[:, :D // 2], y[:, D // 2:]))

    @pl.when(nv == 0)
    def _():
        y_ref[...] = jnp.zeros_like(y_ref)


def _experts(blk_expert, blk_valid, switch, wslot, next_expert, xs, w1, w3, w2):
    hbm = pl.BlockSpec(memory_space=pl.ANY)
    grid_spec = pltpu.PrefetchScalarGridSpec(
        num_scalar_prefetch=5,
        grid=(N_BLOCKS,),
        in_specs=[pl.BlockSpec((BM * RT, LANES), lambda i, *_: (i, 0)), hbm, hbm, hbm],
        out_specs=pl.BlockSpec((BM * RT, LANES), lambda i, *_: (i, 0)),
        scratch_shapes=[pltpu.VMEM((2, D, E_HID), F32), pltpu.VMEM((2, D, E_HID), F32),
                        pltpu.VMEM((2, E_HID, D), F32),
                        pltpu.VMEM((D, E_HID), BF16), pltpu.VMEM((D, E_HID), BF16),
                        pltpu.VMEM((E_HID, D), BF16), pltpu.SemaphoreType.DMA((2,))],
    )
    return pl.pallas_call(
        _experts_kernel,
        grid_spec=grid_spec,
        out_shape=jax.ShapeDtypeStruct((N_BLOCKS * BM * RT, LANES), U32),
        compiler_params=_cp(("arbitrary",)),
        name="experts",
    )(blk_expert, blk_valid, switch, wslot, next_expert, xs, w1, w3, w2)


def _final_kernel(dcur0_ref, dcur1_ref, dnext0_ref, dnext1_ref, ys_ref, x1_ref, rwt_ref, mods_ref,
                  g2_ref, b2_ref, oc_ref, ol_ref, buf, sems):
    i = pl.program_id(0)
    dcur_ref, dnext_ref = (dcur0_ref, dcur1_ref), (dnext0_ref, dnext1_ref)
    n = pl.num_programs(0)
    slot = i % 2

    def first_row(s, k):
        return (s * TOP_K + k) * TM

    def issue(d_ref, s):
        def body(r, carry):
            for k in range(TOP_K):
                _row_copy(ys_ref, d_ref[k][0, 0, r], buf, first_row(s, k) + r, sems.at[s]).start(priority=k)
            return carry

        lax.fori_loop(0, TM, body, 0, unroll=8)

    @pl.when(i == 0)
    def _():
        issue(dcur_ref, 0)

    @pl.when(i + 1 < n)
    def _():
        issue(dnext_ref, 1 - slot)

    for k in range(TOP_K):
        pltpu.make_async_copy(ys_ref.at[pl.ds(0, TM * RT)], buf.at[pl.ds(0, TM * RT)], sems.at[slot]).wait()

    w = rwt_ref[...]
    lo0, hi0 = _unpack_pair(_load_tiles_as_rows(buf, first_row(slot, 0), TM))
    lo1, hi1 = _unpack_pair(_load_tiles_as_rows(buf, first_row(slot, 1), TM))
    w0, w1 = w[:, 0:1], w[:, 1:2]
    moe = jnp.concatenate([lo0 * w0 + lo1 * w1, hi0 * w0 + hi1 * w1], axis=1)
    gate2 = mods_ref[0][5:6]
    out = _layer_norm(ALPHA * x1_ref[...] + gate2 * moe, g2_ref[...], b2_ref[...])

    @pl.when(i < NP_TILES)
    def _():
        oc_ref[...] = out

    @pl.when(i >= NP_TILES)
    def _():
        ol_ref[...] = out


def _final(dest_k, ys, x1, rwt, mods, g2, b2):
    n = T // TM
    cur = pl.BlockSpec((1, 1, TM), lambda i: (i, 0, 0), memory_space=pltpu.SMEM)
    nxt = pl.BlockSpec((1, 1, TM), lambda i: (jnp.minimum(i + 1, n - 1), 0, 0), memory_space=pltpu.SMEM)
    return pl.pallas_call(
        _final_kernel,
        grid=(n,),
        in_specs=[cur, cur, nxt, nxt,
                  pl.BlockSpec(memory_space=pl.ANY),
                  pl.BlockSpec((TM, D), lambda i: (i, 0)),
                  pl.BlockSpec((TM, TOP_K), lambda i: (i, 0)),
                  pl.BlockSpec((1, 6, D), lambda i: (_mods_index(i), 0, 0)),
                  _const_spec((1, D)), _const_spec((1, D))],
        out_specs=[pl.BlockSpec((TM, D), lambda i: (jnp.minimum(i, NP_TILES - 1), 0)),
                   pl.BlockSpec((TM, D), lambda i: (jnp.maximum(i - NP_TILES, 0), 0))],
        out_shape=[jax.ShapeDtypeStruct((T_CTX, D), F32), jax.ShapeDtypeStruct((T_LAT, D), F32)],
        scratch_shapes=[pltpu.VMEM((2 * TOP_K * TM * RT, LANES), U32), pltpu.SemaphoreType.DMA((2,))],
        compiler_params=_cp(("arbitrary",)),
        name="final",
    )(*dest_k, *dest_k, ys, x1, rwt, mods, g2, b2)


def _rope_tables():
    n = N_LAT
    rows = n // GRID_W
    f32 = np.float32
    row = np.repeat(np.arange(rows, dtype=f32), GRID_W)
    col = np.tile(np.arange(GRID_W, dtype=f32), rows)
    half = QK_ROPE // 2
    inv = (f32(ROPE_THETA) ** (-np.arange(0, half, 2, dtype=f32) / f32(half))).astype(f32)
    ang_r, ang_c = (row[:, None] * inv).astype(f32), (col[:, None] * inv).astype(f32)
    cr, sr = np.cos(ang_r).astype(f32), np.sin(ang_r).astype(f32)
    cc, sc = np.cos(ang_c).astype(f32), np.sin(ang_c).astype(f32)
    z16 = np.zeros_like(cr)
    pad = lambda v: np.full((n, LANES - QK_ROPE), v, f32)
    cos = np.concatenate([cr, cr, cc, cc, pad(1.0)], axis=1)
    s_plus = np.concatenate([z16, sr, z16, sc, pad(0.0)], axis=1)
    s_minus = np.concatenate([-sr, z16, -sc, z16, pad(0.0)], axis=1)
    ident = lambda v: np.full((TM, LANES), v, f32)
    return (jnp.asarray(np.concatenate([cos, ident(1.0)], 0)),
            jnp.asarray(np.concatenate([s_plus, ident(0.0)], 0)),
            jnp.asarray(np.concatenate([s_minus, ident(0.0)], 0)))


def _block_plan(eid_k, rank, hist):
    counts = jnp.sum(hist, axis=0)
    nblk = (counts + BM - 1) // BM
    bends = jnp.cumsum(nblk)
    bstarts = bends - nblk
    base = bstarts[None, :] * BM + jnp.cumsum(hist, axis=0) - hist
    eid = jnp.arange(N_EXP, dtype=I32)
    hit = eid_k[:, :, None, :] == eid[None, None, :, None]
    first = jnp.sum(jnp.where(hit, base[:, None, :, None], 0), axis=2)
    dest = (first + rank).astype(I32)
    blk = jnp.arange(N_BLOCKS, dtype=I32)
    blk_expert = jnp.minimum(jnp.sum((blk[:, None] >= bends[None, :]).astype(I32), axis=1), N_EXP - 1)
    of_blk = blk_expert[:, None] == eid[None, :]

    def per_block(v):
        return jnp.sum(jnp.where(of_blk, v[None, :], 0), axis=1)

    left = per_block(counts) - (blk - per_block(bstarts)) * BM
    blk_valid = jnp.where(blk < bends[-1], jnp.clip(left, 0, BM), 0).astype(I32)
    unused = bends[-1] + jnp.arange(N_EXP, dtype=I32)
    zero_blk = jnp.concatenate([jnp.maximum(bends - 1, 0), jnp.minimum(unused, N_BLOCKS - 1)])
    zero_valid = jnp.concatenate([counts % BM != 0, unused < N_BLOCKS])
    prev_expert = jnp.concatenate([jnp.full((1,), -1, I32), blk_expert[:-1]])
    switch = (blk < bends[-1]) & (blk_expert != prev_expert)
    wslot = (jnp.cumsum(switch.astype(I32)) - 1) % 2
    later = (eid[None, :] > eid[:, None]) & (nblk[None, :] > 0)
    next_of = jnp.min(jnp.where(later, eid[None, :], N_EXP), axis=1)
    next_expert = per_block(jnp.where(next_of < N_EXP, next_of, -1))
    return (dest, blk_expert.astype(I32), blk_valid, zero_blk.astype(I32),
            zero_valid.astype(I32), switch.astype(I32), wslot.astype(I32), next_expert.astype(I32))


def kernel(x_prompt, x_sample, cache_ckv, cache_krope, c, c_ctx, w_ada, b_ada, w_in, q_norm_g, w_uq,
           kv_norm_g, w_ukv, gmlp_ln_g, gmlp_ln_b, w_spatial, b_spatial, w_out, ln1_g, ln1_b, w_group,
           b_group, w_router, b_router, w1, w3, w2, ln2_g, ln2_b):
    l = 0
    xc = x_prompt.reshape(T_CTX, D)
    xl = x_sample.reshape(T_LAT, D)

    cond16 = jnp.concatenate([c, c_ctx[None, :], jnp.zeros((16 - N_LAT_B - 1, D), F32)], axis=0)
    mods = _adaln(cond16, w_ada[l], b_ada[l][None, :]).reshape(16, 6, D)

    wi = w_in[l]
    o_kr = QL + KVL
    o_u = o_kr + QK_ROPE
    win_a = wi[:, :o_kr].astype(BF16)
    win_g = wi[:, o_u:].astype(BF16)
    win_r = jnp.pad(wi[:, o_kr:o_u], ((0, 0), (0, LANES - QK_ROPE))).astype(BF16)
    wuq = jnp.pad(w_uq[l].reshape(QL, H, QK_NOPE + QK_ROPE),
                  ((0, 0), (0, 0), (0, HEAD_PAD - QK_NOPE - QK_ROPE))).reshape(QL, H * HEAD_PAD).astype(BF16)
    wkv = w_ukv[l].reshape(KVL, H, QK_NOPE + V_DIM)
    wuk = wkv[:, :, :QK_NOPE].reshape(KVL, H * QK_NOPE).astype(BF16)
    wuvt = wkv[:, :, QK_NOPE:].reshape(KVL, H * V_DIM).T.astype(BF16)
    ws = w_spatial[l].astype(BF16)
    bsb = jnp.broadcast_to(b_spatial[l][:, :, None], (G_GROUPS, CHUNK, G_DIM))
    wrt = jnp.concatenate([w_group[l], w_router[l], jnp.zeros((D, LANES - N_GRP - N_EXP), F32)],
                          axis=1).astype(BF16)
    brt = jnp.concatenate([b_group[l], b_router[l], jnp.zeros((LANES - N_GRP - N_EXP,), F32)])[None, :]
    cos_t, sp_t, sm_t = _rope_tables()
    row = lambda v: v[l][None, :]

    q, k, vt, gm, ckv_n, kr = _proj(xc, xl, mods, win_a, win_g, win_r, row(q_norm_g), wuq, row(kv_norm_g), wuk, wuvt,
                                   row(gmlp_ln_g), row(gmlp_ln_b), ws, bsb, cos_t, sp_t, sm_t)
    kc, vct = _kvexp(cache_ckv[:, l].reshape(N_LAT_B * PAST, KVL),
                     cache_krope[:, l].reshape(N_LAT_B * PAST, QK_ROPE), wuk, wuvt)
    attn_c = _attn_ctx(q, k, vt)
    attn_l = _attn_lat(q, k, vt, kc, vct)

    x1, h2p, ridx, rwt = _post(attn_c, attn_l, gm, xc, xl, mods, w_out[l].astype(BF16), row(ln1_g), row(ln1_b),
                               wrt, brt)
    eid_k, rank, hist = _rank(ridx)
    dest, blk_expert, blk_valid, zero_blk, zero_valid, switch, wslot, next_expert = _block_plan(
        eid_k, rank, hist[:, 0, :N_EXP].astype(I32))
    xs = _dispatch(zero_blk, zero_valid, dest, h2p)
    ys = _experts(blk_expert, blk_valid, switch, wslot, next_expert, xs, w1[l], w3[l], w2[l])
    y_c, y_l = _final(_regroup_steps(dest, TM), ys, x1, rwt, mods, row(ln2_g), row(ln2_b))

    return (y_c.reshape(N_CTX_B, N_CTX, D), y_l.reshape(N_LAT_B, N_LAT, D),
            ckv_n.reshape(N_CTX_B, 1, N_CTX, KVL), kr.reshape(N_CTX_B, 1, N_CTX, QK_ROPE))
```

```python
import functools

import jax
import jax.numpy as jnp
import numpy as np
from jax import lax
from jax.experimental import pallas as pl
from jax.experimental.pallas import tpu as pltpu

F32 = jnp.float32
BF16 = jnp.bfloat16
U32 = jnp.uint32
I32 = jnp.int32

D = 2048
N_CTX_B, N_CTX = 32, 256
N_LAT_B, N_LAT = 4, 4096
PAST = 512
GRID_W = 64
H = 8
QK_NOPE, QK_ROPE, V_DIM = 128, 64, 128
QL, KVL = 512, 512
GW = 1024
G_GROUPS, G_DIM, CHUNK = 8, 128, 128
N_GRP, E_PER_GRP, N_EXP, TOP_K = 4, 8, 32, 2
E_HID = 512
ROPE_THETA = 10000.0
EPS = 1e-6
ALPHA = 2.0 ** 0.25
SM_SCALE = (QK_NOPE + QK_ROPE) ** -0.5
Q_SCALE = SM_SCALE * float(np.log2(np.e))

T_CTX = N_CTX_B * N_CTX
T_LAT = N_LAT_B * N_LAT
T = T_CTX + T_LAT
N_ASSIGN = T * TOP_K

LANES = 128
HEAD_PAD = 256
V_AUG = V_DIM + 16
TM = 256
TP = 512
POST_SUBS = (256, 256)
SUB = 256
TD = 1024
TF = 512
CTX_PER_STEP = 2
TQ = 2048
TK = 512
BM = 256
N_BLOCKS = N_ASSIGN // BM + N_EXP
N_ZERO = 2 * N_EXP
VMEM_LIMIT = 56 * 1024 * 1024

NP_TILES = T_CTX // TM
LAT_TILES_PER_B = N_LAT // TM


def _cp(sem):
    return pltpu.CompilerParams(dimension_semantics=sem, vmem_limit_bytes=VMEM_LIMIT)


def _const_spec(shape):
    nd = len(shape)
    return pl.BlockSpec(shape, lambda *a: (0,) * nd, pipeline_mode=pl.Buffered(1))


def _dot(a, b):
    return jnp.dot(a, b, preferred_element_type=F32)


def _dot_nt(a, b):
    return lax.dot_general(a, b, (((1,), (1,)), ((), ())), preferred_element_type=F32)


def _layer_norm(x, g, b):
    mu = jnp.mean(x, -1, keepdims=True)
    xc = x - mu
    var = jnp.mean(xc * xc, -1, keepdims=True)
    return xc * lax.rsqrt(var + EPS) * g + b


def _rms_norm(x, g):
    return x * lax.rsqrt(jnp.mean(x * x, -1, keepdims=True) + EPS) * g


def _gelu(x):
    return 0.5 * x * (1.0 + lax.erf(x * np.float32(np.sqrt(0.5))))


def _pack_pair(lo, hi):
    lo_b = lax.bitcast_convert_type(lo.astype(BF16).astype(F32), U32)
    hi_b = lax.bitcast_convert_type(hi.astype(BF16).astype(F32), U32)
    return hi_b | (lo_b >> 16)


def _unpack_pair(w):
    lo = lax.bitcast_convert_type(w << 16, F32)
    hi = lax.bitcast_convert_type(w & jnp.uint32(0xFFFF0000), F32)
    return lo, hi


RT = 8


def _store_rows_as_tiles(ref, row0, val):
    n = val.shape[0]
    for j in range(RT):
        ref[pl.ds(row0 * RT + j, n, stride=RT), :] = val[:, j * LANES:(j + 1) * LANES]


def _load_tiles_as_rows(ref, row0, n):
    return jnp.concatenate([ref[pl.ds(row0 * RT + j, n, stride=RT), :] for j in range(RT)], axis=1)


def _adaln_kernel(cond_ref, w_ref, b_ref, o_ref):
    s = jax.nn.silu(cond_ref[...])
    s_hi = s.astype(BF16)
    s_lo = (s - s_hi.astype(F32)).astype(BF16)
    w = w_ref[...]
    w_hi = w.astype(BF16)
    w_lo = (w - w_hi.astype(F32)).astype(BF16)
    o_ref[...] = _dot(s_hi, w_hi) + _dot(s_lo, w_hi) + _dot(s_hi, w_lo) + b_ref[...]


def _adaln(cond16, w_ada, b_ada):
    tn = 1024
    n = w_ada.shape[1]
    return pl.pallas_call(
        _adaln_kernel,
        grid=(n // tn,),
        in_specs=[pl.BlockSpec((16, D), lambda j: (0, 0)),
                  pl.BlockSpec((D, tn), lambda j: (0, j)),
                  pl.BlockSpec((1, tn), lambda j: (0, j))],
        out_specs=pl.BlockSpec((16, tn), lambda j: (0, j)),
        out_shape=jax.ShapeDtypeStruct((16, n), F32),
        compiler_params=_cp(("arbitrary",)),
        name="adaln",
    )(cond16, w_ada, b_ada)


def _expand_kv(ckv_n, kr128, wuk_ref, wuvt_ref, k_ref, vt_ref):
    cb = ckv_n.astype(BF16)
    kn = _dot(cb, wuk_ref[...])
    vt = _dot_nt(wuvt_ref[...], cb).astype(BF16)
    ones = jnp.ones((V_AUG - V_DIM, vt.shape[1]), BF16)
    for h in range(H):
        vt_ref[h * V_AUG:h * V_AUG + V_DIM, :] = vt[h * V_DIM:(h + 1) * V_DIM]
        vt_ref[h * V_AUG + V_DIM:(h + 1) * V_AUG, :] = ones
    krb = kr128.astype(BF16)
    for h in range(H):
        k_ref[:, h * HEAD_PAD:h * HEAD_PAD + QK_NOPE] = kn[:, h * QK_NOPE:(h + 1) * QK_NOPE].astype(BF16)
        k_ref[:, h * HEAD_PAD + QK_NOPE:(h + 1) * HEAD_PAD] = krb


def _rope128(x, cos, s_plus, s_minus):
    return (x * cos + pltpu.roll(x, 16, 1) * s_plus + pltpu.roll(x, LANES - 16, 1) * s_minus)


def _proj_kernel(xc_ref, xl_ref, mods_ref, win_a_ref, win_g_ref, win_r_ref, qg_ref, wuq_ref, kvg_ref,
                 wuk_ref, wuv_ref,
                 lng_ref, lnb_ref, ws_ref, bsb_ref, cos_ref, sp_ref, sm_ref,
                 q_ref, k_ref, v_ref, gm_ref, ckv_ref, kr_ref):
    i = pl.program_id(0)
    x = jnp.where(i < NP_TILES, xc_ref[...], xl_ref[...])
    m = mods_ref[0]
    shift1, scale1 = m[0:1], m[1:2]
    hmod = (x * (1.0 + scale1) + shift1).astype(BF16)
    pg = _dot(hmod, win_g_ref[...])
    u, v = pg[:, :GW], pg[:, GW:]
    gu = _gelu(u)
    vn = _layer_norm(_gelu(v), lng_ref[...], lnb_ref[...]).astype(BF16)
    pa = _dot(hmod, win_a_ref[...])
    kr = _dot(hmod, win_r_ref[...])
    cq, ckv = pa[:, :QL], pa[:, QL:]

    cos, s_plus, s_minus = cos_ref[...], sp_ref[...], sm_ref[...]

    cq_n = _rms_norm(cq, qg_ref[...]).astype(BF16)
    q = _dot(cq_n, wuq_ref[...]) * Q_SCALE
    for h in range(H):
        q_ref[:, h * HEAD_PAD:h * HEAD_PAD + QK_NOPE] = q[:, h * HEAD_PAD:h * HEAD_PAD + QK_NOPE].astype(BF16)
        q_ref[:, h * HEAD_PAD + QK_NOPE:(h + 1) * HEAD_PAD] = _rope128(
            q[:, h * HEAD_PAD + QK_NOPE:(h + 1) * HEAD_PAD], cos, s_plus, s_minus).astype(BF16)

    ckv_n = _rms_norm(ckv, kvg_ref[...])
    _expand_kv(ckv_n, _rope128(kr, cos, s_plus, s_minus), wuk_ref, wuv_ref, k_ref, v_ref)

    for c in range(TM // CHUNK):
        rows = slice(c * CHUNK, (c + 1) * CHUNK)
        for g in range(G_GROUPS):
            cols = slice(g * G_DIM, (g + 1) * G_DIM)
            s = _dot(ws_ref[g], vn[rows, cols]) + bsb_ref[g]
            gm_ref[rows, cols] = (gu[rows, cols] * s).astype(BF16)

    @pl.when(i < NP_TILES)
    def _():
        ckv_ref[...] = ckv_n
        kr_ref[...] = kr[:, :QK_ROPE]


def _mods_index(i, tile=TM):
    return jnp.where(i < T_CTX // tile, N_LAT_B, (i - T_CTX // tile) // (N_LAT // tile))


def _proj(xc, xl, mods, win_a, win_g, win_r, qg, wuq, kvg, wuk, wuv, lng, lnb, ws, bsb, cos_t, sp_t, sm_t):
    def tok(width):
        return pl.BlockSpec((TM, width), lambda i: (i, 0))

    def pos_block(i):
        return jnp.where(i < NP_TILES, LAT_TILES_PER_B, (i - NP_TILES) % LAT_TILES_PER_B)

    rope_spec = pl.BlockSpec((TM, LANES), lambda i: (pos_block(i), 0))
    return pl.pallas_call(
        _proj_kernel,
        grid=(T // TM,),
        in_specs=[
            pl.BlockSpec((TM, D), lambda i: (jnp.minimum(i, NP_TILES - 1), 0)),
            pl.BlockSpec((TM, D), lambda i: (jnp.maximum(i - NP_TILES, 0), 0)),
            pl.BlockSpec((1, 6, D), lambda i: (_mods_index(i), 0, 0)),
            _const_spec((D, QL + KVL)), _const_spec((D, 2 * GW)), _const_spec((D, LANES)),
            _const_spec((1, QL)), _const_spec((QL, H * HEAD_PAD)),
            _const_spec((1, KVL)), _const_spec((KVL, H * QK_NOPE)), _const_spec((H * V_DIM, KVL)),
            _const_spec((1, GW)), _const_spec((1, GW)),
            _const_spec((G_GROUPS, CHUNK, CHUNK)), _const_spec((G_GROUPS, CHUNK, G_DIM)),
            rope_spec, rope_spec, rope_spec,
        ],
        out_specs=[tok(H * HEAD_PAD), tok(H * HEAD_PAD),
                   pl.BlockSpec((H * V_AUG, TM), lambda i: (0, i)), tok(GW),
                   pl.BlockSpec((TM, KVL), lambda i: (jnp.minimum(i, NP_TILES - 1), 0)),
                   pl.BlockSpec((TM, QK_ROPE), lambda i: (jnp.minimum(i, NP_TILES - 1), 0))],
        out_shape=[
            jax.ShapeDtypeStruct((T, H * HEAD_PAD), BF16),
            jax.ShapeDtypeStruct((T, H * HEAD_PAD), BF16),
            jax.ShapeDtypeStruct((H * V_AUG, T), BF16),
            jax.ShapeDtypeStruct((T, GW), BF16),
            jax.ShapeDtypeStruct((T_CTX, KVL), F32),
            jax.ShapeDtypeStruct((T_CTX, QK_ROPE), F32),
        ],
        compiler_params=_cp(("arbitrary",)),
        name="proj",
    )(xc, xl, mods, win_a, win_g, win_r, qg, wuq, kvg, wuk, wuv, lng, lnb, ws, bsb, cos_t, sp_t, sm_t)


def _kvexp_kernel(ckv_ref, kr_ref, wuk_ref, wuv_ref, k_ref, v_ref):
    kr = kr_ref[...]
    kr128 = jnp.concatenate([kr, jnp.zeros_like(kr)], axis=1)
    _expand_kv(ckv_ref[...], kr128, wuk_ref, wuv_ref, k_ref, v_ref)


def _kvexp(ckv, kr, wuk, wuv):
    rows = ckv.shape[0]
    return pl.pallas_call(
        _kvexp_kernel,
        grid=(rows // TM,),
        in_specs=[pl.BlockSpec((TM, KVL), lambda i: (i, 0)), pl.BlockSpec((TM, QK_ROPE), lambda i: (i, 0)),
                  _const_spec((KVL, H * QK_NOPE)), _const_spec((H * V_DIM, KVL))],
        out_specs=[pl.BlockSpec((TM, H * HEAD_PAD), lambda i: (i, 0)),
                   pl.BlockSpec((H * V_AUG, TM), lambda i: (0, i))],
        out_shape=[jax.ShapeDtypeStruct((rows, H * HEAD_PAD), BF16),
                   jax.ShapeDtypeStruct((H * V_AUG, rows), BF16)],
        compiler_params=_cp(("arbitrary",)),
        name="kvexp",
    )(ckv, kr, wuk, wuv)


def _probs(s, m):
    m_new = jnp.maximum(m, jnp.max(s, 0, keepdims=True))
    return m_new, jnp.exp2(m - m_new), jnp.exp2(s - m_new).astype(BF16)


def _attn_finish(acc):
    return (acc[:V_DIM] / acc[V_DIM:V_DIM + 1]).T.astype(BF16)


def _attn_ctx_kernel(q_ref, k_ref, vt_ref, o_ref):
    m0 = jnp.full((1, N_CTX), -jnp.inf, F32)
    for b in range(CTX_PER_STEP):
        rows = slice(b * N_CTX, (b + 1) * N_CTX)
        for h in range(H):
            qh = q_ref[rows, h * HEAD_PAD:(h + 1) * HEAD_PAD]
            kh = k_ref[rows, h * HEAD_PAD:(h + 1) * HEAD_PAD]
            _, _, p = _probs(_dot_nt(kh, qh), m0)
            acc = _dot(vt_ref[h * V_AUG:(h + 1) * V_AUG, rows], p)
            o_ref[rows, h * V_DIM:(h + 1) * V_DIM] = _attn_finish(acc)


def _attn_ctx(q, k, vt):
    n = CTX_PER_STEP * N_CTX
    return pl.pallas_call(
        _attn_ctx_kernel,
        grid=(N_CTX_B // CTX_PER_STEP,),
        in_specs=[pl.BlockSpec((n, H * HEAD_PAD), lambda b: (b, 0)),
                  pl.BlockSpec((n, H * HEAD_PAD), lambda b: (b, 0)),
                  pl.BlockSpec((H * V_AUG, n), lambda b: (0, b))],
        out_specs=pl.BlockSpec((n, H * V_DIM), lambda b: (b, 0)),
        out_shape=jax.ShapeDtypeStruct((T_CTX, H * V_DIM), BF16),
        compiler_params=_cp(("arbitrary",)),
        name="attn_ctx",
    )(q, k, vt)


def _attn_lat_kernel(q_ref, k_ref, vt_ref, kc_ref, vct_ref, o_ref):
    q = q_ref[...]
    n_lat = N_LAT // TK

    def keys(c):
        return k_ref[c * TK:(c + 1) * TK, :] if c < n_lat else kc_ref[...]

    def values_t(c):
        return vt_ref[:, c * TK:(c + 1) * TK] if c < n_lat else vct_ref[...]

    m = jnp.full((1, TQ), -jnp.inf, F32)
    acc = jnp.zeros((V_AUG, TQ), F32)
    s_next = _dot_nt(keys(0), q)
    m_next = jnp.maximum(m, jnp.max(s_next, 0, keepdims=True))
    p_prev = alpha_prev = None
    for c in range(n_lat + 1):
        s, m_new = s_next, m_next
        if c < n_lat:
            s_next = _dot_nt(keys(c + 1), q)
            m_next = jnp.maximum(m_new, jnp.max(s_next, 0, keepdims=True))
        if c > 0:
            acc = alpha_prev * acc + _dot(values_t(c - 1), p_prev)
        alpha_prev = jnp.exp2(m - m_new)
        p_prev = jnp.exp2(s - m_new).astype(BF16)
        m = m_new
    acc = alpha_prev * acc + _dot(values_t(n_lat), p_prev)
    o_ref[...] = _attn_finish(acc)


def _attn_lat(q, k, vt, kc, vct):
    qb0 = T_CTX // TQ
    qpb = N_LAT // TQ
    kb0 = T_CTX // N_LAT
    return pl.pallas_call(
        _attn_lat_kernel,
        grid=(N_LAT_B, H, qpb),
        in_specs=[pl.BlockSpec((TQ, HEAD_PAD), lambda b, h, qi: (qb0 + b * qpb + qi, h)),
                  pl.BlockSpec((N_LAT, HEAD_PAD), lambda b, h, qi: (kb0 + b, h)),
                  pl.BlockSpec((V_AUG, N_LAT), lambda b, h, qi: (h, kb0 + b)),
                  pl.BlockSpec((PAST, HEAD_PAD), lambda b, h, qi: (b, h)),
                  pl.BlockSpec((V_AUG, PAST), lambda b, h, qi: (h, b))],
        out_specs=pl.BlockSpec((TQ, V_DIM), lambda b, h, qi: (b * qpb + qi, h)),
        out_shape=jax.ShapeDtypeStruct((T_LAT, H * V_DIM), BF16),
        compiler_params=_cp(("arbitrary", "arbitrary", "arbitrary")),
        name="attn_lat",
    )(q, k, vt, kc, vct)


def _route(logits):
    lane_i = lax.broadcasted_iota(I32, logits.shape, 1)
    lane = lane_i.astype(F32)
    neg = jnp.float32(-jnp.inf)
    far = jnp.float32(LANES)
    gl = jnp.where(lane_i < N_GRP, logits, neg)
    gmax = jnp.max(gl, -1, keepdims=True)
    gidx = jnp.min(jnp.where(gl == gmax, lane, far), -1, keepdims=True)
    p_top = 1.0 / jnp.sum(jnp.exp(gl - gmax), -1, keepdims=True)
    grp_of_lane = ((lane_i - N_GRP) >> 3).astype(F32)
    in_grp = (lane_i >= N_GRP) & (lane_i < N_GRP + N_EXP) & (grp_of_lane == gidx)
    el = jnp.where(in_grp, logits, neg)
    m1 = jnp.max(el, -1, keepdims=True)
    i1 = jnp.min(jnp.where(el == m1, lane, far), -1, keepdims=True)
    el2 = jnp.where(lane == i1, neg, el)
    m2 = jnp.max(el2, -1, keepdims=True)
    i2 = jnp.min(jnp.where(el2 == m2, lane, far), -1, keepdims=True)
    e2 = jnp.exp(m2 - m1)
    w1 = p_top / (1.0 + e2)
    w2 = p_top * e2 / (1.0 + e2)
    return i1 - N_GRP, i2 - N_GRP, w1, w2


def _pair(a, b, dtype):
    two = lax.broadcasted_iota(I32, (a.shape[0], TOP_K), 1)
    return jnp.where(two == 0, a, b).astype(dtype)


def _post_kernel(ac_ref, al_ref, gm_ref, xc_ref, xl_ref, mods_ref, wout_ref, g1_ref, b1_ref, wrt_ref,
                 brt_ref, x1_ref, h2p_ref, ridx_ref, rwt_ref):
    i = pl.program_id(0)
    is_ctx = i < T_CTX // TP
    m = mods_ref[0]
    gate1, shift2, scale2 = m[2:3], m[3:4], m[4:5]

    def mix_of(rows):
        attn = jnp.where(is_ctx, ac_ref[rows, :], al_ref[rows, :])
        return _dot(jnp.concatenate([attn, gm_ref[rows, :]], axis=1), wout_ref[...])

    bounds = np.cumsum((0,) + POST_SUBS)
    subs = [slice(int(a), int(b)) for a, b in zip(bounds[:-1], bounds[1:])]
    mix_next = mix_of(subs[0])
    for s, rows in enumerate(subs):
        mix = mix_next
        if s + 1 < len(subs):
            mix_next = mix_of(subs[s + 1])
        x = jnp.where(is_ctx, xc_ref[rows, :], xl_ref[rows, :])
        x1 = _layer_norm(ALPHA * x + gate1 * mix, g1_ref[...], b1_ref[...])
        x1_ref[rows, :] = x1
        h2 = x1 * (1.0 + scale2) + shift2
        _store_rows_as_tiles(h2p_ref, rows.start, _pack_pair(h2[:, :D // 2], h2[:, D // 2:]))
        e1, e2, w1, w2 = _route(_dot(h2.astype(BF16), wrt_ref[...]) + brt_ref[...])
        ridx_ref[rows, :] = _pair(e1, e2, I32)
        rwt_ref[rows, :] = _pair(w1, w2, F32)


def _post(attn_c, attn_l, gm, xc, xl, mods, wout, g1, b1, wrt, brt):
    npt = T_CTX // TP

    def tok(width):
        return pl.BlockSpec((TP, width), lambda i: (i, 0))

    def ctx(width):
        return pl.BlockSpec((TP, width), lambda i: (jnp.minimum(i, npt - 1), 0))

    def lat(width):
        return pl.BlockSpec((TP, width), lambda i: (jnp.maximum(i - npt, 0), 0))

    return pl.pallas_call(
        _post_kernel,
        grid=(T // TP,),
        in_specs=[ctx(H * V_DIM), lat(H * V_DIM), tok(GW), ctx(D), lat(D),
                  pl.BlockSpec((1, 6, D), lambda i: (_mods_index(i, TP), 0, 0)),
                  _const_spec((D, D)), _const_spec((1, D)), _const_spec((1, D)),
                  _const_spec((D, LANES)), _const_spec((1, LANES))],
        out_specs=[tok(D), pl.BlockSpec((TP * RT, LANES), lambda i: (i, 0)), tok(TOP_K), tok(TOP_K)],
        out_shape=[jax.ShapeDtypeStruct((T, D), F32),
                   jax.ShapeDtypeStruct((T * RT, LANES), U32),
                   jax.ShapeDtypeStruct((T, TOP_K), I32),
                   jax.ShapeDtypeStruct((T, TOP_K), F32)],
        compiler_params=_cp(("arbitrary",)),
        name="post",
    )(attn_c, attn_l, gm, xc, xl, mods, wout, g1, b1, wrt, brt)


def _as_lane_row(col):
    return jnp.broadcast_to(col, (col.shape[0], LANES)).T[0:1, :]


def _rank_kernel(ridx_ref, tri_ref, e_ref, rank_ref, hist_ref):
    lane = lax.broadcasted_iota(I32, (SUB, LANES), 1).astype(F32)
    seen = jnp.zeros((1, LANES), F32)
    for s in range(TP // SUB):
        rows = slice(s * SUB, (s + 1) * SUB)
        e = ridx_ref[rows, :].astype(F32)
        ohs = [(lane == e[:, k:k + 1]).astype(F32) for k in range(TOP_K)]
        both = ohs[0] + ohs[1]
        before = _dot(tri_ref[...], both.astype(BF16)) + seen
        for k in range(TOP_K):
            rank_k = jnp.sum(before * ohs[k], -1, keepdims=True)
            e_ref[0, k:k + 1, rows] = _as_lane_row(e[:, k:k + 1]).astype(I32)
            rank_ref[0, k:k + 1, rows] = _as_lane_row(rank_k).astype(I32)
        seen = seen + jnp.sum(both, 0, keepdims=True)
    hist_ref[0] = jnp.broadcast_to(seen, (8, LANES))


def _rank(ridx):
    dense = pl.BlockSpec((1, TOP_K, TP), lambda i: (i, 0, 0))
    return pl.pallas_call(
        _rank_kernel,
        grid=(T // TP,),
        in_specs=[pl.BlockSpec((TP, TOP_K), lambda i: (i, 0)), _const_spec((SUB, SUB))],
        out_specs=[dense, dense, pl.BlockSpec((1, 8, LANES), lambda i: (i, 0, 0))],
        out_shape=[jax.ShapeDtypeStruct((T // TP, TOP_K, TP), I32),
                   jax.ShapeDtypeStruct((T // TP, TOP_K, TP), I32),
                   jax.ShapeDtypeStruct((T // TP, 8, LANES), F32)],
        compiler_params=_cp(("arbitrary",)),
        name="rank",
    )(ridx, jnp.asarray(np.tril(np.ones((SUB, SUB), np.float32), -1), BF16))


def _row_copy(src, src_row, dst, dst_row, sem):
    def tile(ref, row):
        return ref.at[pl.ds(pl.multiple_of(row * RT, RT), RT)]

    return pltpu.make_async_copy(tile(src, src_row), tile(dst, dst_row), sem)


def _dispatch_kernel(zb_ref, zv_ref, d0_ref, d1_ref, h2p_ref, xs_ref, zbuf, sem):
    i = pl.program_id(0)
    dest_refs = (d0_ref, d1_ref)

    @pl.when(i == 0)
    def _():
        zbuf[...] = jnp.zeros_like(zbuf)

        def zero_block(j):
            start = pl.multiple_of(zb_ref[j] * (BM * RT), BM * RT)
            return pltpu.make_async_copy(zbuf, xs_ref.at[pl.ds(start, BM * RT)], sem)

        def start(j, carry):
            @pl.when(zv_ref[j] != 0)
            def _():
                zero_block(j).start()
            return carry

        def wait(j, carry):
            @pl.when(zv_ref[j] != 0)
            def _():
                zero_block(j).wait()
            return carry

        lax.fori_loop(0, N_ZERO, start, 0)
        lax.fori_loop(0, N_ZERO, wait, 0)

    def issue(r, carry):
        for k in range(TOP_K):
            _row_copy(h2p_ref, r, xs_ref, dest_refs[k][0, 0, r], sem).start(priority=k)
        return carry

    lax.fori_loop(0, TD, issue, 0, unroll=8)
    for _ in range(TOP_K):
        pltpu.make_async_copy(h2p_ref, xs_ref.at[pl.ds(0, TD * RT)], sem).wait()


def _regroup_steps(dest, rows):
    return [dest[:, k, :].reshape(T // rows, 1, rows) for k in range(TOP_K)]


def _dispatch(zero_blk, zero_valid, dest, h2p):
    grid_spec = pltpu.PrefetchScalarGridSpec(
        num_scalar_prefetch=2,
        grid=(T // TD,),
        in_specs=[pl.BlockSpec((1, 1, TD), lambda i, zb, zv: (i, 0, 0), memory_space=pltpu.SMEM),
                  pl.BlockSpec((1, 1, TD), lambda i, zb, zv: (i, 0, 0), memory_space=pltpu.SMEM),
                  pl.BlockSpec((TD * RT, LANES), lambda i, zb, zv: (i, 0))],
        out_specs=pl.BlockSpec(memory_space=pl.ANY),
        scratch_shapes=[pltpu.VMEM((BM * RT, LANES), U32), pltpu.SemaphoreType.DMA(())],
    )
    return pl.pallas_call(
        _dispatch_kernel,
        grid_spec=grid_spec,
        out_shape=jax.ShapeDtypeStruct((N_BLOCKS * BM * RT, LANES), U32),
        compiler_params=_cp(("arbitrary",)),
        name="dispatch",
    )(zero_blk, zero_valid, *_regroup_steps(dest, TD), h2p)


def _experts_kernel(be_ref, bv_ref, sw_ref, es_ref, nx_ref, x_ref, w1_hbm, w3_hbm, w2_hbm, y_ref,
                    wf1, wf3, wf2, w1b, w3b, w2b, sems):
    i = pl.program_id(0)
    e = be_ref[i]
    nv = bv_ref[i]

    def weight_copies(expert, s):
        return (pltpu.make_async_copy(w1_hbm.at[expert], wf1.at[s], sems.at[s]),
                pltpu.make_async_copy(w3_hbm.at[expert], wf3.at[s], sems.at[s]),
                pltpu.make_async_copy(w2_hbm.at[expert], wf2.at[s], sems.at[s]))

    @pl.when(sw_ref[i] != 0)
    def _():
        s = es_ref[i]

        @pl.when(i == 0)
        def _():
            for c in weight_copies(e, s):
                c.start()

        for c in weight_copies(e, s):
            c.wait()
        nxt = nx_ref[i]

        @pl.when(nxt >= 0)
        def _():
            for c in weight_copies(nxt, 1 - s):
                c.start(priority=1)

        w1b[...] = wf1[s].astype(BF16)
        w3b[...] = wf3[s].astype(BF16)
        w2b[...] = wf2[s].astype(BF16)

    @pl.when(nv > 0)
    def _():
        lo, hi = _unpack_pair(_load_tiles_as_rows(x_ref, 0, BM))
        xb = jnp.concatenate([lo.astype(BF16), hi.astype(BF16)], axis=1)
        a = _dot(xb, w1b[...])
        b = _dot(xb, w3b[...])
        hid = (jax.nn.silu(a) * b).astype(BF16)
        y = _dot(hid, w2b[...])
        _store_rows_as_tiles(y_ref, 0, _pack_pair(y[:, :D // 2], y[:, D // 2:]))

    @pl.when(nv == 0)
    def _():
        y_ref[...] = jnp.zeros_like(y_ref)


def _experts(blk_expert, blk_valid, switch, wslot, next_expert, xs, w1, w3, w2):
    hbm = pl.BlockSpec(memory_space=pl.ANY)
    grid_spec = pltpu.PrefetchScalarGridSpec(
        num_scalar_prefetch=5,
        grid=(N_BLOCKS,),
        in_specs=[pl.BlockSpec((BM * RT, LANES), lambda i, *_: (i, 0)), hbm, hbm, hbm],
        out_specs=pl.BlockSpec((BM * RT, LANES), lambda i, *_: (i, 0)),
        scratch_shapes=[pltpu.VMEM((2, D, E_HID), F32), pltpu.VMEM((2, D, E_HID), F32),
                        pltpu.VMEM((2, E_HID, D), F32),
                        pltpu.VMEM((D, E_HID), BF16), pltpu.VMEM((D, E_HID), BF16),
                        pltpu.VMEM((E_HID, D), BF16), pltpu.SemaphoreType.DMA((2,))],
    )
    return pl.pallas_call(
        _experts_kernel,
        grid_spec=grid_spec,
        out_shape=jax.ShapeDtypeStruct((N_BLOCKS * BM * RT, LANES), U32),
        compiler_params=_cp(("arbitrary",)),
        name="experts",
    )(blk_expert, blk_valid, switch, wslot, next_expert, xs, w1, w3, w2)


def _final_kernel(dcur0_ref, dcur1_ref, dnext0_ref, dnext1_ref, ys_ref, x1_ref, rwt_ref, mods_ref,
                  g2_ref, b2_ref, oc_ref, ol_ref, buf, sems):
    i = pl.program_id(0)
    dcur_ref, dnext_ref = (dcur0_ref, dcur1_ref), (dnext0_ref, dnext1_ref)
    n = pl.num_programs(0)
    slot = i % 2

    def first_row(s, k):
        return (s * TOP_K + k) * TF

    def issue(d_ref, s):
        def body(r, carry):
            for k in range(TOP_K):
                _row_copy(ys_ref, d_ref[k][0, 0, r], buf, first_row(s, k) + r, sems.at[s]).start()
            return carry

        lax.fori_loop(0, TF, body, 0, unroll=8)

    @pl.when(i == 0)
    def _():
        issue(dcur_ref, 0)

    @pl.when(i + 1 < n)
    def _():
        issue(dnext_ref, 1 - slot)

    for k in range(TOP_K):
        pltpu.make_async_copy(ys_ref.at[pl.ds(0, TF * RT)], buf.at[pl.ds(0, TF * RT)], sems.at[slot]).wait()

    w = rwt_ref[...]
    lo0, hi0 = _unpack_pair(_load_tiles_as_rows(buf, first_row(slot, 0), TF))
    lo1, hi1 = _unpack_pair(_load_tiles_as_rows(buf, first_row(slot, 1), TF))
    w0, w1 = w[:, 0:1], w[:, 1:2]
    moe = jnp.concatenate([lo0 * w0 + lo1 * w1, hi0 * w0 + hi1 * w1], axis=1)
    gate2 = mods_ref[0][5:6]
    out = _layer_norm(ALPHA * x1_ref[...] + gate2 * moe, g2_ref[...], b2_ref[...])

    @pl.when(i < T_CTX // TF)
    def _():
        oc_ref[...] = out

    @pl.when(i >= T_CTX // TF)
    def _():
        ol_ref[...] = out


def _final(dest_k, ys, x1, rwt, mods, g2, b2):
    n = T // TF
    npt = T_CTX // TF
    cur = pl.BlockSpec((1, 1, TF), lambda i: (i, 0, 0), memory_space=pltpu.SMEM)
    nxt = pl.BlockSpec((1, 1, TF), lambda i: (jnp.minimum(i + 1, n - 1), 0, 0), memory_space=pltpu.SMEM)
    return pl.pallas_call(
        _final_kernel,
        grid=(n,),
        in_specs=[cur, cur, nxt, nxt,
                  pl.BlockSpec(memory_space=pl.ANY),
                  pl.BlockSpec((TF, D), lambda i: (i, 0)),
                  pl.BlockSpec((TF, TOP_K), lambda i: (i, 0)),
                  pl.BlockSpec((1, 6, D), lambda i: (_mods_index(i, TF), 0, 0)),
                  _const_spec((1, D)), _const_spec((1, D))],
        out_specs=[pl.BlockSpec((TF, D), lambda i: (jnp.minimum(i, npt - 1), 0)),
                   pl.BlockSpec((TF, D), lambda i: (jnp.maximum(i - npt, 0), 0))],
        out_shape=[jax.ShapeDtypeStruct((T_CTX, D), F32), jax.ShapeDtypeStruct((T_LAT, D), F32)],
        scratch_shapes=[pltpu.VMEM((2 * TOP_K * TF * RT, LANES), U32), pltpu.SemaphoreType.DMA((2,))],
        compiler_params=_cp(("arbitrary",)),
        name="final",
    )(*dest_k, *dest_k, ys, x1, rwt, mods, g2, b2)


def _rope_tables():
    n = N_LAT
    rows = n // GRID_W
    f32 = np.float32
    row = np.repeat(np.arange(rows, dtype=f32), GRID_W)
    col = np.tile(np.arange(GRID_W, dtype=f32), rows)
    half = QK_ROPE // 2
    inv = (f32(ROPE_THETA) ** (-np.arange(0, half, 2, dtype=f32) / f32(half))).astype(f32)
    ang_r, ang_c = (row[:, None] * inv).astype(f32), (col[:, None] * inv).astype(f32)
    cr, sr = np.cos(ang_r).astype(f32), np.sin(ang_r).astype(f32)
    cc, sc = np.cos(ang_c).astype(f32), np.sin(ang_c).astype(f32)
    z16 = np.zeros_like(cr)
    pad = lambda v: np.full((n, LANES - QK_ROPE), v, f32)
    cos = np.concatenate([cr, cr, cc, cc, pad(1.0)], axis=1)
    s_plus = np.concatenate([z16, sr, z16, sc, pad(0.0)], axis=1)
    s_minus = np.concatenate([-sr, z16, -sc, z16, pad(0.0)], axis=1)
    ident = lambda v: np.full((TM, LANES), v, f32)
    return (jnp.asarray(np.concatenate([cos, ident(1.0)], 0)),
            jnp.asarray(np.concatenate([s_plus, ident(0.0)], 0)),
            jnp.asarray(np.concatenate([s_minus, ident(0.0)], 0)))


def _block_plan(eid_k, rank, hist):
    counts = jnp.sum(hist, axis=0)
    nblk = (counts + BM - 1) // BM
    bends = jnp.cumsum(nblk)
    bstarts = bends - nblk
    base = bstarts[None, :] * BM + jnp.cumsum(hist, axis=0) - hist
    eid = jnp.arange(N_EXP, dtype=I32)
    hit = eid_k[:, :, None, :] == eid[None, None, :, None]
    first = jnp.sum(jnp.where(hit, base[:, None, :, None], 0), axis=2)
    dest = (first + rank).astype(I32)
    blk = jnp.arange(N_BLOCKS, dtype=I32)
    blk_expert = jnp.minimum(jnp.sum((blk[:, None] >= bends[None, :]).astype(I32), axis=1), N_EXP - 1)
    of_blk = blk_expert[:, None] == eid[None, :]

    def per_block(v):
        return jnp.sum(jnp.where(of_blk, v[None, :], 0), axis=1)

    left = per_block(counts) - (blk - per_block(bstarts)) * BM
    blk_valid = jnp.where(blk < bends[-1], jnp.clip(left, 0, BM), 0).astype(I32)
    unused = bends[-1] + jnp.arange(N_EXP, dtype=I32)
    zero_blk = jnp.concatenate([jnp.maximum(bends - 1, 0), jnp.minimum(unused, N_BLOCKS - 1)])
    zero_valid = jnp.concatenate([counts % BM != 0, unused < N_BLOCKS])
    prev_expert = jnp.concatenate([jnp.full((1,), -1, I32), blk_expert[:-1]])
    switch = (blk < bends[-1]) & (blk_expert != prev_expert)
    wslot = (jnp.cumsum(switch.astype(I32)) - 1) % 2
    later = (eid[None, :] > eid[:, None]) & (nblk[None, :] > 0)
    next_of = jnp.min(jnp.where(later, eid[None, :], N_EXP), axis=1)
    next_expert = per_block(jnp.where(next_of < N_EXP, next_of, -1))
    return (dest, blk_expert.astype(I32), blk_valid, zero_blk.astype(I32),
            zero_valid.astype(I32), switch.astype(I32), wslot.astype(I32), next_expert.astype(I32))


def kernel(x_prompt, x_sample, cache_ckv, cache_krope, c, c_ctx, w_ada, b_ada, w_in, q_norm_g, w_uq,
           kv_norm_g, w_ukv, gmlp_ln_g, gmlp_ln_b, w_spatial, b_spatial, w_out, ln1_g, ln1_b, w_group,
           b_group, w_router, b_router, w1, w3, w2, ln2_g, ln2_b):
    l = 0
    xc = x_prompt.reshape(T_CTX, D)
    xl = x_sample.reshape(T_LAT, D)

    cond16 = jnp.concatenate([c, c_ctx[None, :], jnp.zeros((16 - N_LAT_B - 1, D), F32)], axis=0)
    mods = _adaln(cond16, w_ada[l], b_ada[l][None, :]).reshape(16, 6, D)

    wi = w_in[l]
    o_kr = QL + KVL
    o_u = o_kr + QK_ROPE
    win_a = wi[:, :o_kr].astype(BF16)
    win_g = wi[:, o_u:].astype(BF16)
    win_r = jnp.pad(wi[:, o_kr:o_u], ((0, 0), (0, LANES - QK_ROPE))).astype(BF16)
    wuq = jnp.pad(w_uq[l].reshape(QL, H, QK_NOPE + QK_ROPE),
                  ((0, 0), (0, 0), (0, HEAD_PAD - QK_NOPE - QK_ROPE))).reshape(QL, H * HEAD_PAD).astype(BF16)
    wkv = w_ukv[l].reshape(KVL, H, QK_NOPE + V_DIM)
    wuk = wkv[:, :, :QK_NOPE].reshape(KVL, H * QK_NOPE).astype(BF16)
    wuvt = wkv[:, :, QK_NOPE:].reshape(KVL, H * V_DIM).T.astype(BF16)
    ws = w_spatial[l].astype(BF16)
    bsb = jnp.broadcast_to(b_spatial[l][:, :, None], (G_GROUPS, CHUNK, G_DIM))
    wrt = jnp.concatenate([w_group[l], w_router[l], jnp.zeros((D, LANES - N_GRP - N_EXP), F32)],
                          axis=1).astype(BF16)
    brt = jnp.concatenate([b_group[l], b_router[l], jnp.zeros((LANES - N_GRP - N_EXP,), F32)])[None, :]
    cos_t, sp_t, sm_t = _rope_tables()
    row = lambda v: v[l][None, :]

    q, k, vt, gm, ckv_n, kr = _proj(xc, xl, mods, win_a, win_g, win_r, row(q_norm_g), wuq, row(kv_norm_g), wuk, wuvt,
                                   row(gmlp_ln_g), row(gmlp_ln_b), ws, bsb, cos_t, sp_t, sm_t)
    kc, vct = _kvexp(cache_ckv[:, l].reshape(N_LAT_B * PAST, KVL),
                     cache_krope[:, l].reshape(N_LAT_B * PAST, QK_ROPE), wuk, wuvt)
    attn_c = _attn_ctx(q, k, vt)
    attn_l = _attn_lat(q, k, vt, kc, vct)

    x1, h2p, ridx, rwt = _post(attn_c, attn_l, gm, xc, xl, mods, w_out[l].astype(BF16), row(ln1_g), row(ln1_b),
                               wrt, brt)
    eid_k, rank, hist = _rank(ridx)
    dest, blk_expert, blk_valid, zero_blk, zero_valid, switch, wslot, next_expert = _block_plan(
        eid_k, rank, hist[:, 0, :N_EXP].astype(I32))
    xs = _dispatch(zero_blk, zero_valid, dest, h2p)
    ys = _experts(blk_expert, blk_valid, switch, wslot, next_expert, xs, w1[l], w3[l], w2[l])
    y_c, y_l = _final(_regroup_steps(dest, TF), ys, x1, rwt, mods, row(ln2_g), row(ln2_b))

    return (y_c.reshape(N_CTX_B, N_CTX, D), y_l.reshape(N_LAT_B, N_LAT, D),
            ckv_n.reshape(N_CTX_B, 1, N_CTX, KVL), kr.reshape(N_CTX_B, 1, N_CTX, QK_ROPE))
```

```python
import functools

import jax
import jax.numpy as jnp
import numpy as np
from jax import lax
from jax.experimental import pallas as pl
from jax.experimental.pallas import tpu as pltpu

F32 = jnp.float32
BF16 = jnp.bfloat16
U32 = jnp.uint32
I32 = jnp.int32

D = 2048
N_CTX_B, N_CTX = 32, 256
N_LAT_B, N_LAT = 4, 4096
PAST = 512
GRID_W = 64
H = 8
QK_NOPE, QK_ROPE, V_DIM = 128, 64, 128
QL, KVL = 512, 512
GW = 1024
G_GROUPS, G_DIM, CHUNK = 8, 128, 128
N_GRP, E_PER_GRP, N_EXP, TOP_K = 4, 8, 32, 2
E_HID = 512
ROPE_THETA = 10000.0
EPS = 1e-6
ALPHA = 2.0 ** 0.25
SM_SCALE = (QK_NOPE + QK_ROPE) ** -0.5
Q_SCALE = SM_SCALE * float(np.log2(np.e))

T_CTX = N_CTX_B * N_CTX
T_LAT = N_LAT_B * N_LAT
T = T_CTX + T_LAT
N_ASSIGN = T * TOP_K

LANES = 128
HEAD_PAD = 256
V_AUG = V_DIM + 16
TM = 256
TP = 512
POST_SUBS = (256, 256)
SUB = 256
TD = 1024
TF = 256
CTX_PER_STEP = 2
TQ = 2048
TK = 512
BM = 256
N_BLOCKS = N_ASSIGN // BM + N_EXP
N_ZERO = 2 * N_EXP
VMEM_LIMIT = 56 * 1024 * 1024

NP_TILES = T_CTX // TM
LAT_TILES_PER_B = N_LAT // TM


def _cp(sem):
    return pltpu.CompilerParams(dimension_semantics=sem, vmem_limit_bytes=VMEM_LIMIT)


def _const_spec(shape):
    nd = len(shape)
    return pl.BlockSpec(shape, lambda *a: (0,) * nd, pipeline_mode=pl.Buffered(1))


def _dot(a, b):
    return jnp.dot(a, b, preferred_element_type=F32)


def _dot_nt(a, b):
    return lax.dot_general(a, b, (((1,), (1,)), ((), ())), preferred_element_type=F32)


def _layer_norm(x, g, b):
    mu = jnp.mean(x, -1, keepdims=True)
    xc = x - mu
    var = jnp.mean(xc * xc, -1, keepdims=True)
    return xc * lax.rsqrt(var + EPS) * g + b


def _rms_norm(x, g):
    return x * lax.rsqrt(jnp.mean(x * x, -1, keepdims=True) + EPS) * g


def _gelu(x):
    return 0.5 * x * (1.0 + lax.erf(x * np.float32(np.sqrt(0.5))))


def _pack_pair(lo, hi):
    lo_b = lax.bitcast_convert_type(lo.astype(BF16).astype(F32), U32)
    hi_b = lax.bitcast_convert_type(hi.astype(BF16).astype(F32), U32)
    return hi_b | (lo_b >> 16)


def _unpack_pair(w):
    lo = lax.bitcast_convert_type(w << 16, F32)
    hi = lax.bitcast_convert_type(w & jnp.uint32(0xFFFF0000), F32)
    return lo, hi


RT = 8


def _store_rows_as_tiles(ref, row0, val):
    n = val.shape[0]
    for j in range(RT):
        ref[pl.ds(row0 * RT + j, n, stride=RT), :] = val[:, j * LANES:(j + 1) * LANES]


def _load_tiles_as_rows(ref, row0, n):
    return jnp.concatenate([ref[pl.ds(row0 * RT + j, n, stride=RT), :] for j in range(RT)], axis=1)


def _adaln_kernel(cond_ref, w_ref, b_ref, o_ref):
    s = jax.nn.silu(cond_ref[...])
    s_hi = s.astype(BF16)
    s_lo = (s - s_hi.astype(F32)).astype(BF16)
    w = w_ref[...]
    w_hi = w.astype(BF16)
    w_lo = (w - w_hi.astype(F32)).astype(BF16)
    o_ref[...] = _dot(s_hi, w_hi) + _dot(s_lo, w_hi) + _dot(s_hi, w_lo) + b_ref[...]


def _adaln(cond16, w_ada, b_ada):
    tn = 1024
    n = w_ada.shape[1]
    return pl.pallas_call(
        _adaln_kernel,
        grid=(n // tn,),
        in_specs=[pl.BlockSpec((16, D), lambda j: (0, 0)),
                  pl.BlockSpec((D, tn), lambda j: (0, j)),
                  pl.BlockSpec((1, tn), lambda j: (0, j))],
        out_specs=pl.BlockSpec((16, tn), lambda j: (0, j)),
        out_shape=jax.ShapeDtypeStruct((16, n), F32),
        compiler_params=_cp(("arbitrary",)),
        name="adaln",
    )(cond16, w_ada, b_ada)


def _expand_kv(ckv_n, kr128, wuk_ref, wuvt_ref, k_ref, vt_ref):
    cb = ckv_n.astype(BF16)
    kn = _dot(cb, wuk_ref[...])
    vt = _dot_nt(wuvt_ref[...], cb).astype(BF16)
    ones = jnp.ones((V_AUG - V_DIM, vt.shape[1]), BF16)
    for h in range(H):
        vt_ref[h * V_AUG:h * V_AUG + V_DIM, :] = vt[h * V_DIM:(h + 1) * V_DIM]
        vt_ref[h * V_AUG + V_DIM:(h + 1) * V_AUG, :] = ones
    krb = kr128.astype(BF16)
    for h in range(H):
        k_ref[:, h * HEAD_PAD:h * HEAD_PAD + QK_NOPE] = kn[:, h * QK_NOPE:(h + 1) * QK_NOPE].astype(BF16)
        k_ref[:, h * HEAD_PAD + QK_NOPE:(h + 1) * HEAD_PAD] = krb


def _rope128(x, cos, s_plus, s_minus):
    return (x * cos + pltpu.roll(x, 16, 1) * s_plus + pltpu.roll(x, LANES - 16, 1) * s_minus)


def _proj_kernel(xc_ref, xl_ref, mods_ref, win_a_ref, win_g_ref, win_r_ref, qg_ref, wuq_ref, kvg_ref,
                 wuk_ref, wuv_ref,
                 lng_ref, lnb_ref, ws_ref, bsb_ref, cos_ref, sp_ref, sm_ref,
                 q_ref, k_ref, v_ref, gm_ref, ckv_ref, kr_ref):
    i = pl.program_id(0)
    x = jnp.where(i < NP_TILES, xc_ref[...], xl_ref[...])
    m = mods_ref[0]
    shift1, scale1 = m[0:1], m[1:2]
    hmod = (x * (1.0 + scale1) + shift1).astype(BF16)
    pg = _dot(hmod, win_g_ref[...])
    u, v = pg[:, :GW], pg[:, GW:]
    gu = _gelu(u)
    vn = _layer_norm(_gelu(v), lng_ref[...], lnb_ref[...]).astype(BF16)
    pa = _dot(hmod, win_a_ref[...])
    kr = _dot(hmod, win_r_ref[...])
    cq, ckv = pa[:, :QL], pa[:, QL:]

    cos, s_plus, s_minus = cos_ref[...], sp_ref[...], sm_ref[...]

    cq_n = _rms_norm(cq, qg_ref[...]).astype(BF16)
    q = _dot(cq_n, wuq_ref[...]) * Q_SCALE
    for h in range(H):
        q_ref[:, h * HEAD_PAD:h * HEAD_PAD + QK_NOPE] = q[:, h * HEAD_PAD:h * HEAD_PAD + QK_NOPE].astype(BF16)
        q_ref[:, h * HEAD_PAD + QK_NOPE:(h + 1) * HEAD_PAD] = _rope128(
            q[:, h * HEAD_PAD + QK_NOPE:(h + 1) * HEAD_PAD], cos, s_plus, s_minus).astype(BF16)

    ckv_n = _rms_norm(ckv, kvg_ref[...])
    _expand_kv(ckv_n, _rope128(kr, cos, s_plus, s_minus), wuk_ref, wuv_ref, k_ref, v_ref)

    for c in range(TM // CHUNK):
        rows = slice(c * CHUNK, (c + 1) * CHUNK)
        for g in range(G_GROUPS):
            cols = slice(g * G_DIM, (g + 1) * G_DIM)
            s = _dot(ws_ref[g], vn[rows, cols]) + bsb_ref[g]
            gm_ref[rows, cols] = (gu[rows, cols] * s).astype(BF16)

    @pl.when(i < NP_TILES)
    def _():
        ckv_ref[...] = ckv_n
        kr_ref[...] = kr[:, :QK_ROPE]


def _mods_index(i, tile=TM):
    return jnp.where(i < T_CTX // tile, N_LAT_B, (i - T_CTX // tile) // (N_LAT // tile))


def _proj(xc, xl, mods, win_a, win_g, win_r, qg, wuq, kvg, wuk, wuv, lng, lnb, ws, bsb, cos_t, sp_t, sm_t):
    def tok(width):
        return pl.BlockSpec((TM, width), lambda i: (i, 0))

    def pos_block(i):
        return jnp.where(i < NP_TILES, LAT_TILES_PER_B, (i - NP_TILES) % LAT_TILES_PER_B)

    rope_spec = pl.BlockSpec((TM, LANES), lambda i: (pos_block(i), 0))
    return pl.pallas_call(
        _proj_kernel,
        grid=(T // TM,),
        in_specs=[
            pl.BlockSpec((TM, D), lambda i: (jnp.minimum(i, NP_TILES - 1), 0)),
            pl.BlockSpec((TM, D), lambda i: (jnp.maximum(i - NP_TILES, 0), 0)),
            pl.BlockSpec((1, 6, D), lambda i: (_mods_index(i), 0, 0)),
            _const_spec((D, QL + KVL)), _const_spec((D, 2 * GW)), _const_spec((D, LANES)),
            _const_spec((1, QL)), _const_spec((QL, H * HEAD_PAD)),
            _const_spec((1, KVL)), _const_spec((KVL, H * QK_NOPE)), _const_spec((H * V_DIM, KVL)),
            _const_spec((1, GW)), _const_spec((1, GW)),
            _const_spec((G_GROUPS, CHUNK, CHUNK)), _const_spec((G_GROUPS, CHUNK, G_DIM)),
            rope_spec, rope_spec, rope_spec,
        ],
        out_specs=[tok(H * HEAD_PAD), tok(H * HEAD_PAD),
                   pl.BlockSpec((H * V_AUG, TM), lambda i: (0, i)), tok(GW),
                   pl.BlockSpec((TM, KVL), lambda i: (jnp.minimum(i, NP_TILES - 1), 0)),
                   pl.BlockSpec((TM, QK_ROPE), lambda i: (jnp.minimum(i, NP_TILES - 1), 0))],
        out_shape=[
            jax.ShapeDtypeStruct((T, H * HEAD_PAD), BF16),
            jax.ShapeDtypeStruct((T, H * HEAD_PAD), BF16),
            jax.ShapeDtypeStruct((H * V_AUG, T), BF16),
            jax.ShapeDtypeStruct((T, GW), BF16),
            jax.ShapeDtypeStruct((T_CTX, KVL), F32),
            jax.ShapeDtypeStruct((T_CTX, QK_ROPE), F32),
        ],
        compiler_params=_cp(("arbitrary",)),
        name="proj",
    )(xc, xl, mods, win_a, win_g, win_r, qg, wuq, kvg, wuk, wuv, lng, lnb, ws, bsb, cos_t, sp_t, sm_t)


def _kvexp_kernel(ckv_ref, kr_ref, wuk_ref, wuv_ref, k_ref, v_ref):
    kr = kr_ref[...]
    kr128 = jnp.concatenate([kr, jnp.zeros_like(kr)], axis=1)
    _expand_kv(ckv_ref[...], kr128, wuk_ref, wuv_ref, k_ref, v_ref)


def _kvexp(ckv, kr, wuk, wuv):
    rows = ckv.shape[0]
    return pl.pallas_call(
        _kvexp_kernel,
        grid=(rows // TM,),
        in_specs=[pl.BlockSpec((TM, KVL), lambda i: (i, 0)), pl.BlockSpec((TM, QK_ROPE), lambda i: (i, 0)),
                  _const_spec((KVL, H * QK_NOPE)), _const_spec((H * V_DIM, KVL))],
        out_specs=[pl.BlockSpec((TM, H * HEAD_PAD), lambda i: (i, 0)),
                   pl.BlockSpec((H * V_AUG, TM), lambda i: (0, i))],
        out_shape=[jax.ShapeDtypeStruct((rows, H * HEAD_PAD), BF16),
                   jax.ShapeDtypeStruct((H * V_AUG, rows), BF16)],
        compiler_params=_cp(("arbitrary",)),
        name="kvexp",
    )(ckv, kr, wuk, wuv)


def _probs(s, m):
    m_new = jnp.maximum(m, jnp.max(s, 0, keepdims=True))
    return m_new, jnp.exp2(m - m_new), jnp.exp2(s - m_new).astype(BF16)


def _attn_finish(acc):
    return (acc[:V_DIM] / acc[V_DIM:V_DIM + 1]).T.astype(BF16)


def _attn_ctx_kernel(q_ref, k_ref, vt_ref, o_ref):
    m0 = jnp.full((1, N_CTX), -jnp.inf, F32)
    for b in range(CTX_PER_STEP):
        rows = slice(b * N_CTX, (b + 1) * N_CTX)
        for h in range(H):
            qh = q_ref[rows, h * HEAD_PAD:(h + 1) * HEAD_PAD]
            kh = k_ref[rows, h * HEAD_PAD:(h + 1) * HEAD_PAD]
            _, _, p = _probs(_dot_nt(kh, qh), m0)
            acc = _dot(vt_ref[h * V_AUG:(h + 1) * V_AUG, rows], p)
            o_ref[rows, h * V_DIM:(h + 1) * V_DIM] = _attn_finish(acc)


def _attn_ctx(q, k, vt):
    n = CTX_PER_STEP * N_CTX
    return pl.pallas_call(
        _attn_ctx_kernel,
        grid=(N_CTX_B // CTX_PER_STEP,),
        in_specs=[pl.BlockSpec((n, H * HEAD_PAD), lambda b: (b, 0)),
                  pl.BlockSpec((n, H * HEAD_PAD), lambda b: (b, 0)),
                  pl.BlockSpec((H * V_AUG, n), lambda b: (0, b))],
        out_specs=pl.BlockSpec((n, H * V_DIM), lambda b: (b, 0)),
        out_shape=jax.ShapeDtypeStruct((T_CTX, H * V_DIM), BF16),
        compiler_params=_cp(("arbitrary",)),
        name="attn_ctx",
    )(q, k, vt)


def _attn_lat_kernel(q_ref, k_ref, vt_ref, kc_ref, vct_ref, o_ref):
    q = q_ref[...]
    n_lat = N_LAT // TK

    def keys(c):
        return k_ref[c * TK:(c + 1) * TK, :] if c < n_lat else kc_ref[...]

    def values_t(c):
        return vt_ref[:, c * TK:(c + 1) * TK] if c < n_lat else vct_ref[...]

    m = jnp.full((1, TQ), -jnp.inf, F32)
    acc = jnp.zeros((V_AUG, TQ), F32)
    s_next = _dot_nt(keys(0), q)
    m_next = jnp.maximum(m, jnp.max(s_next, 0, keepdims=True))
    p_prev = alpha_prev = None
    for c in range(n_lat + 1):
        s, m_new = s_next, m_next
        if c < n_lat:
            s_next = _dot_nt(keys(c + 1), q)
            m_next = jnp.maximum(m_new, jnp.max(s_next, 0, keepdims=True))
        if c > 0:
            acc = alpha_prev * acc + _dot(values_t(c - 1), p_prev)
        alpha_prev = jnp.exp2(m - m_new)
        p_prev = jnp.exp2(s - m_new).astype(BF16)
        m = m_new
    acc = alpha_prev * acc + _dot(values_t(n_lat), p_prev)
    o_ref[...] = _attn_finish(acc)


def _attn_lat(q, k, vt, kc, vct):
    qb0 = T_CTX // TQ
    qpb = N_LAT // TQ
    kb0 = T_CTX // N_LAT
    return pl.pallas_call(
        _attn_lat_kernel,
        grid=(N_LAT_B, H, qpb),
        in_specs=[pl.BlockSpec((TQ, HEAD_PAD), lambda b, h, qi: (qb0 + b * qpb + qi, h)),
                  pl.BlockSpec((N_LAT, HEAD_PAD), lambda b, h, qi: (kb0 + b, h)),
                  pl.BlockSpec((V_AUG, N_LAT), lambda b, h, qi: (h, kb0 + b)),
                  pl.BlockSpec((PAST, HEAD_PAD), lambda b, h, qi: (b, h)),
                  pl.BlockSpec((V_AUG, PAST), lambda b, h, qi: (h, b))],
        out_specs=pl.BlockSpec((TQ, V_DIM), lambda b, h, qi: (b * qpb + qi, h)),
        out_shape=jax.ShapeDtypeStruct((T_LAT, H * V_DIM), BF16),
        compiler_params=_cp(("arbitrary", "arbitrary", "arbitrary")),
        name="attn_lat",
    )(q, k, vt, kc, vct)


def _route(logits):
    lane_i = lax.broadcasted_iota(I32, logits.shape, 1)
    lane = lane_i.astype(F32)
    neg = jnp.float32(-jnp.inf)
    far = jnp.float32(LANES)
    gl = jnp.where(lane_i < N_GRP, logits, neg)
    gmax = jnp.max(gl, -1, keepdims=True)
    gidx = jnp.min(jnp.where(gl == gmax, lane, far), -1, keepdims=True)
    p_top = 1.0 / jnp.sum(jnp.exp(gl - gmax), -1, keepdims=True)
    grp_of_lane = ((lane_i - N_GRP) >> 3).astype(F32)
    in_grp = (lane_i >= N_GRP) & (lane_i < N_GRP + N_EXP) & (grp_of_lane == gidx)
    el = jnp.where(in_grp, logits, neg)
    m1 = jnp.max(el, -1, keepdims=True)
    i1 = jnp.min(jnp.where(el == m1, lane, far), -1, keepdims=True)
    el2 = jnp.where(lane == i1, neg, el)
    m2 = jnp.max(el2, -1, keepdims=True)
    i2 = jnp.min(jnp.where(el2 == m2, lane, far), -1, keepdims=True)
    e2 = jnp.exp(m2 - m1)
    w1 = p_top / (1.0 + e2)
    w2 = p_top * e2 / (1.0 + e2)
    return i1 - N_GRP, i2 - N_GRP, w1, w2


def _pair(a, b, dtype):
    two = lax.broadcasted_iota(I32, (a.shape[0], TOP_K), 1)
    return jnp.where(two == 0, a, b).astype(dtype)


def _post_kernel(ac_ref, al_ref, gm_ref, xc_ref, xl_ref, mods_ref, wout_ref, g1_ref, b1_ref, wrt_ref,
                 brt_ref, x1_ref, h2p_ref, ridx_ref, rwt_ref):
    i = pl.program_id(0)
    is_ctx = i < T_CTX // TP
    m = mods_ref[0]
    gate1, shift2, scale2 = m[2:3], m[3:4], m[4:5]

    def mix_of(rows):
        attn = jnp.where(is_ctx, ac_ref[rows, :], al_ref[rows, :])
        return _dot(jnp.concatenate([attn, gm_ref[rows, :]], axis=1), wout_ref[...])

    bounds = np.cumsum((0,) + POST_SUBS)
    subs = [slice(int(a), int(b)) for a, b in zip(bounds[:-1], bounds[1:])]
    mix_next = mix_of(subs[0])
    for s, rows in enumerate(subs):
        mix = mix_next
        if s + 1 < len(subs):
            mix_next = mix_of(subs[s + 1])
        x = jnp.where(is_ctx, xc_ref[rows, :], xl_ref[rows, :])
        x1 = _layer_norm(ALPHA * x + gate1 * mix, g1_ref[...], b1_ref[...])
        x1_ref[rows, :] = x1
        h2 = x1 * (1.0 + scale2) + shift2
        _store_rows_as_tiles(h2p_ref, rows.start, _pack_pair(h2[:, :D // 2], h2[:, D // 2:]))
        e1, e2, w1, w2 = _route(_dot(h2.astype(BF16), wrt_ref[...]) + brt_ref[...])
        ridx_ref[rows, :] = _pair(e1, e2, I32)
        rwt_ref[rows, :] = _pair(w1, w2, F32)


def _post(attn_c, attn_l, gm, xc, xl, mods, wout, g1, b1, wrt, brt):
    npt = T_CTX // TP

    def tok(width):
        return pl.BlockSpec((TP, width), lambda i: (i, 0))

    def ctx(width):
        return pl.BlockSpec((TP, width), lambda i: (jnp.minimum(i, npt - 1), 0))

    def lat(width):
        return pl.BlockSpec((TP, width), lambda i: (jnp.maximum(i - npt, 0), 0))

    return pl.pallas_call(
        _post_kernel,
        grid=(T // TP,),
        in_specs=[ctx(H * V_DIM), lat(H * V_DIM), tok(GW), ctx(D), lat(D),
                  pl.BlockSpec((1, 6, D), lambda i: (_mods_index(i, TP), 0, 0)),
                  _const_spec((D, D)), _const_spec((1, D)), _const_spec((1, D)),
                  _const_spec((D, LANES)), _const_spec((1, LANES))],
        out_specs=[tok(D), pl.BlockSpec((TP * RT, LANES), lambda i: (i, 0)), tok(TOP_K), tok(TOP_K)],
        out_shape=[jax.ShapeDtypeStruct((T, D), F32),
                   jax.ShapeDtypeStruct((T * RT, LANES), U32),
                   jax.ShapeDtypeStruct((T, TOP_K), I32),
                   jax.ShapeDtypeStruct((T, TOP_K), F32)],
        compiler_params=_cp(("arbitrary",)),
        name="post",
    )(attn_c, attn_l, gm, xc, xl, mods, wout, g1, b1, wrt, brt)


def _as_lane_row(col):
    return jnp.broadcast_to(col, (col.shape[0], LANES)).T[0:1, :]


def _rank_kernel(ridx_ref, tri_ref, e_ref, rank_ref, hist_ref):
    lane = lax.broadcasted_iota(I32, (SUB, LANES), 1).astype(F32)
    seen = jnp.zeros((1, LANES), F32)
    for s in range(TP // SUB):
        rows = slice(s * SUB, (s + 1) * SUB)
        e = ridx_ref[rows, :].astype(F32)
        ohs = [(lane == e[:, k:k + 1]).astype(F32) for k in range(TOP_K)]
        both = ohs[0] + ohs[1]
        before = _dot(tri_ref[...], both.astype(BF16)) + seen
        for k in range(TOP_K):
            rank_k = jnp.sum(before * ohs[k], -1, keepdims=True)
            e_ref[0, k:k + 1, rows] = _as_lane_row(e[:, k:k + 1]).astype(I32)
            rank_ref[0, k:k + 1, rows] = _as_lane_row(rank_k).astype(I32)
        seen = seen + jnp.sum(both, 0, keepdims=True)
    hist_ref[0] = jnp.broadcast_to(seen, (8, LANES))


def _rank(ridx):
    dense = pl.BlockSpec((1, TOP_K, TP), lambda i: (i, 0, 0))
    return pl.pallas_call(
        _rank_kernel,
        grid=(T // TP,),
        in_specs=[pl.BlockSpec((TP, TOP_K), lambda i: (i, 0)), _const_spec((SUB, SUB))],
        out_specs=[dense, dense, pl.BlockSpec((1, 8, LANES), lambda i: (i, 0, 0))],
        out_shape=[jax.ShapeDtypeStruct((T // TP, TOP_K, TP), I32),
                   jax.ShapeDtypeStruct((T // TP, TOP_K, TP), I32),
                   jax.ShapeDtypeStruct((T // TP, 8, LANES), F32)],
        compiler_params=_cp(("arbitrary",)),
        name="rank",
    )(ridx, jnp.asarray(np.tril(np.ones((SUB, SUB), np.float32), -1), BF16))


def _row_copy(src, src_row, dst, dst_row, sem):
    def tile(ref, row):
        return ref.at[pl.ds(pl.multiple_of(row * RT, RT), RT)]

    return pltpu.make_async_copy(tile(src, src_row), tile(dst, dst_row), sem)


def _dispatch_kernel(zb_ref, zv_ref, d0_ref, d1_ref, h2p_ref, xs_ref, zbuf, sem):
    i = pl.program_id(0)
    dest_refs = (d0_ref, d1_ref)

    @pl.when(i == 0)
    def _():
        zbuf[...] = jnp.zeros_like(zbuf)

        def zero_block(j):
            start = pl.multiple_of(zb_ref[j] * (BM * RT), BM * RT)
            return pltpu.make_async_copy(zbuf, xs_ref.at[pl.ds(start, BM * RT)], sem)

        def start(j, carry):
            @pl.when(zv_ref[j] != 0)
            def _():
                zero_block(j).start()
            return carry

        def wait(j, carry):
            @pl.when(zv_ref[j] != 0)
            def _():
                zero_block(j).wait()
            return carry

        lax.fori_loop(0, N_ZERO, start, 0)
        lax.fori_loop(0, N_ZERO, wait, 0)

    def issue(r, carry):
        for k in range(TOP_K):
            _row_copy(h2p_ref, r, xs_ref, dest_refs[k][0, 0, r], sem).start(priority=k)
        return carry

    lax.fori_loop(0, TD, issue, 0, unroll=8)
    for _ in range(TOP_K):
        pltpu.make_async_copy(h2p_ref, xs_ref.at[pl.ds(0, TD * RT)], sem).wait()


def _regroup_steps(dest, rows):
    return [dest[:, k, :].reshape(T // rows, 1, rows) for k in range(TOP_K)]


def _dispatch(zero_blk, zero_valid, dest, h2p):
    grid_spec = pltpu.PrefetchScalarGridSpec(
        num_scalar_prefetch=2,
        grid=(T // TD,),
        in_specs=[pl.BlockSpec((1, 1, TD), lambda i, zb, zv: (i, 0, 0), memory_space=pltpu.SMEM),
                  pl.BlockSpec((1, 1, TD), lambda i, zb, zv: (i, 0, 0), memory_space=pltpu.SMEM),
                  pl.BlockSpec((TD * RT, LANES), lambda i, zb, zv: (i, 0))],
        out_specs=pl.BlockSpec(memory_space=pl.ANY),
        scratch_shapes=[pltpu.VMEM((BM * RT, LANES), U32), pltpu.SemaphoreType.DMA(())],
    )
    return pl.pallas_call(
        _dispatch_kernel,
        grid_spec=grid_spec,
        out_shape=jax.ShapeDtypeStruct((N_BLOCKS * BM * RT, LANES), U32),
        compiler_params=_cp(("arbitrary",)),
        name="dispatch",
    )(zero_blk, zero_valid, *_regroup_steps(dest, TD), h2p)


def _experts_kernel(be_ref, bv_ref, sw_ref, es_ref, nx_ref, x_ref, w1_hbm, w3_hbm, w2_hbm, y_ref,
                    wf1, wf3, wf2, w1b, w3b, w2b, sems):
    i = pl.program_id(0)
    e = be_ref[i]
    nv = bv_ref[i]

    def weight_copies(expert, s):
        return (pltpu.make_async_copy(w1_hbm.at[expert], wf1.at[s], sems.at[s]),
                pltpu.make_async_copy(w3_hbm.at[expert], wf3.at[s], sems.at[s]),
                pltpu.make_async_copy(w2_hbm.at[expert], wf2.at[s], sems.at[s]))

    @pl.when(sw_ref[i] != 0)
    def _():
        s = es_ref[i]

        @pl.when(i == 0)
        def _():
            for c in weight_copies(e, s):
                c.start()

        for c in weight_copies(e, s):
            c.wait()
        nxt = nx_ref[i]

        @pl.when(nxt >= 0)
        def _():
            for c in weight_copies(nxt, 1 - s):
                c.start(priority=1)

        w1b[...] = wf1[s].astype(BF16)
        w3b[...] = wf3[s].astype(BF16)
        w2b[...] = wf2[s].astype(BF16)

    @pl.when(nv > 0)
    def _():
        lo, hi = _unpack_pair(_load_tiles_as_rows(x_ref, 0, BM))
        xb = jnp.concatenate([lo.astype(BF16), hi.astype(BF16)], axis=1)
        a = _dot(xb, w1b[...])
        b = _dot(xb, w3b[...])
        hid = (jax.nn.silu(a) * b).astype(BF16)
        y = _dot(hid, w2b[...])
        _store_rows_as_tiles(y_ref, 0, _pack_pair(y[:, :D // 2], y[:, D // 2:]))

    @pl.when(nv == 0)
    def _():
        y_ref[...] = jnp.zeros_like(y_ref)


def _experts(blk_expert, blk_valid, switch, wslot, next_expert, xs, w1, w3, w2):
    hbm = pl.BlockSpec(memory_space=pl.ANY)
    grid_spec = pltpu.PrefetchScalarGridSpec(
        num_scalar_prefetch=5,
        grid=(N_BLOCKS,),
        in_specs=[pl.BlockSpec((BM * RT, LANES), lambda i, *_: (i, 0)), hbm, hbm, hbm],
        out_specs=pl.BlockSpec((BM * RT, LANES), lambda i, *_: (i, 0)),
        scratch_shapes=[pltpu.VMEM((2, D, E_HID), F32), pltpu.VMEM((2, D, E_HID), F32),
                        pltpu.VMEM((2, E_HID, D), F32),
                        pltpu.VMEM((D, E_HID), BF16), pltpu.VMEM((D, E_HID), BF16),
                        pltpu.VMEM((E_HID, D), BF16), pltpu.SemaphoreType.DMA((2,))],
    )
    return pl.pallas_call(
        _experts_kernel,
        grid_spec=grid_spec,
        out_shape=jax.ShapeDtypeStruct((N_BLOCKS * BM * RT, LANES), U32),
        compiler_params=_cp(("arbitrary",)),
        name="experts",
    )(blk_expert, blk_valid, switch, wslot, next_expert, xs, w1, w3, w2)


def _final_kernel(dcur0_ref, dcur1_ref, dnext0_ref, dnext1_ref, ys_ref, x1_ref, rwt_ref, mods_ref,
                  g2_ref, b2_ref, oc_ref, ol_ref, buf, sems):
    i = pl.program_id(0)
    dcur_ref, dnext_ref = (dcur0_ref, dcur1_ref), (dnext0_ref, dnext1_ref)
    n = pl.num_programs(0)
    slot = i % 2

    def first_row(s, k):
        return (s * TOP_K + k) * TF

    def issue(d_ref, s):
        def body(r, carry):
            for k in range(TOP_K):
                _row_copy(ys_ref, d_ref[k][0, 0, r], buf, first_row(s, k) + r, sems.at[s]).start()
            return carry

        lax.fori_loop(0, TF, body, 0, unroll=8)

    @pl.when(i == 0)
    def _():
        issue(dcur_ref, 0)

    @pl.when(i + 1 < n)
    def _():
        issue(dnext_ref, 1 - slot)

    for k in range(TOP_K):
        pltpu.make_async_copy(ys_ref.at[pl.ds(0, TF * RT)], buf.at[pl.ds(0, TF * RT)], sems.at[slot]).wait()

    w = rwt_ref[...]
    lo0, hi0 = _unpack_pair(_load_tiles_as_rows(buf, first_row(slot, 0), TF))
    lo1, hi1 = _unpack_pair(_load_tiles_as_rows(buf, first_row(slot, 1), TF))
    w0, w1 = w[:, 0:1], w[:, 1:2]
    moe = jnp.concatenate([lo0 * w0 + lo1 * w1, hi0 * w0 + hi1 * w1], axis=1)
    gate2 = mods_ref[0][5:6]
    out = _layer_norm(ALPHA * x1_ref[...] + gate2 * moe, g2_ref[...], b2_ref[...])

    @pl.when(i < T_CTX // TF)
    def _():
        oc_ref[...] = out

    @pl.when(i >= T_CTX // TF)
    def _():
        ol_ref[...] = out


def _final(dest_k, ys, x1, rwt, mods, g2, b2):
    n = T // TF
    npt = T_CTX // TF
    cur = pl.BlockSpec((1, 1, TF), lambda i: (i, 0, 0), memory_space=pltpu.SMEM)
    nxt = pl.BlockSpec((1, 1, TF), lambda i: (jnp.minimum(i + 1, n - 1), 0, 0), memory_space=pltpu.SMEM)
    return pl.pallas_call(
        _final_kernel,
        grid=(n,),
        in_specs=[cur, cur, nxt, nxt,
                  pl.BlockSpec(memory_space=pl.ANY),
                  pl.BlockSpec((TF, D), lambda i: (i, 0)),
                  pl.BlockSpec((TF, TOP_K), lambda i: (i, 0)),
                  pl.BlockSpec((1, 6, D), lambda i: (_mods_index(i, TF), 0, 0)),
                  _const_spec((1, D)), _const_spec((1, D))],
        out_specs=[pl.BlockSpec((TF, D), lambda i: (jnp.minimum(i, npt - 1), 0)),
                   pl.BlockSpec((TF, D), lambda i: (jnp.maximum(i - npt, 0), 0))],
        out_shape=[jax.ShapeDtypeStruct((T_CTX, D), F32), jax.ShapeDtypeStruct((T_LAT, D), F32)],
        scratch_shapes=[pltpu.VMEM((2 * TOP_K * TF * RT, LANES), U32), pltpu.SemaphoreType.DMA((2,))],
        compiler_params=_cp(("arbitrary",)),
        name="final",
    )(*dest_k, *dest_k, ys, x1, rwt, mods, g2, b2)


def _rope_tables():
    n = N_LAT
    rows = n // GRID_W
    f32 = np.float32
    row = np.repeat(np.arange(rows, dtype=f32), GRID_W)
    col = np.tile(np.arange(GRID_W, dtype=f32), rows)
    half = QK_ROPE // 2
    inv = (f32(ROPE_THETA) ** (-np.arange(0, half, 2, dtype=f32) / f32(half))).astype(f32)
    ang_r, ang_c = (row[:, None] * inv).astype(f32), (col[:, None] * inv).astype(f32)
    cr, sr = np.cos(ang_r).astype(f32), np.sin(ang_r).astype(f32)
    cc, sc = np.cos(ang_c).astype(f32), np.sin(ang_c).astype(f32)
    z16 = np.zeros_like(cr)
    pad = lambda v: np.full((n, LANES - QK_ROPE), v, f32)
    cos = np.concatenate([cr, cr, cc, cc, pad(1.0)], axis=1)
    s_plus = np.concatenate([z16, sr, z16, sc, pad(0.0)], axis=1)
    s_minus = np.concatenate([-sr, z16, -sc, z16, pad(0.0)], axis=1)
    ident = lambda v: np.full((TM, LANES), v, f32)
    return (jnp.asarray(np.concatenate([cos, ident(1.0)], 0)),
            jnp.asarray(np.concatenate([s_plus, ident(0.0)], 0)),
            jnp.asarray(np.concatenate([s_minus, ident(0.0)], 0)))


def _block_plan(eid_k, rank, hist):
    counts = jnp.sum(hist, axis=0)
    nblk = (counts + BM - 1) // BM
    bends = jnp.cumsum(nblk)
    bstarts = bends - nblk
    base = bstarts[None, :] * BM + jnp.cumsum(hist, axis=0) - hist
    eid = jnp.arange(N_EXP, dtype=I32)
    hit = eid_k[:, :, None, :] == eid[None, None, :, None]
    first = jnp.sum(jnp.where(hit, base[:, None, :, None], 0), axis=2)
    dest = (first + rank).astype(I32)
    blk = jnp.arange(N_BLOCKS, dtype=I32)
    blk_expert = jnp.minimum(jnp.sum((blk[:, None] >= bends[None, :]).astype(I32), axis=1), N_EXP - 1)
    of_blk = blk_expert[:, None] == eid[None, :]

    def per_block(v):
        return jnp.sum(jnp.where(of_blk, v[None, :], 0), axis=1)

    left = per_block(counts) - (blk - per_block(bstarts)) * BM
    blk_valid = jnp.where(blk < bends[-1], jnp.clip(left, 0, BM), 0).astype(I32)
    unused = bends[-1] + jnp.arange(N_EXP, dtype=I32)
    zero_blk = jnp.concatenate([jnp.maximum(bends - 1, 0), jnp.minimum(unused, N_BLOCKS - 1)])
    zero_valid = jnp.concatenate([counts % BM != 0, unused < N_BLOCKS])
    prev_expert = jnp.concatenate([jnp.full((1,), -1, I32), blk_expert[:-1]])
    switch = (blk < bends[-1]) & (blk_expert != prev_expert)
    wslot = (jnp.cumsum(switch.astype(I32)) - 1) % 2
    later = (eid[None, :] > eid[:, None]) & (nblk[None, :] > 0)
    next_of = jnp.min(jnp.where(later, eid[None, :], N_EXP), axis=1)
    next_expert = per_block(jnp.where(next_of < N_EXP, next_of, -1))
    return (dest, blk_expert.astype(I32), blk_valid, zero_blk.astype(I32),
            zero_valid.astype(I32), switch.astype(I32), wslot.astype(I32), next_expert.astype(I32))


def kernel(x_prompt, x_sample, cache_ckv, cache_krope, c, c_ctx, w_ada, b_ada, w_in, q_norm_g, w_uq,
           kv_norm_g, w_ukv, gmlp_ln_g, gmlp_ln_b, w_spatial, b_spatial, w_out, ln1_g, ln1_b, w_group,
           b_group, w_router, b_router, w1, w3, w2, ln2_g, ln2_b):
    l = 0
    xc = x_prompt.reshape(T_CTX, D)
    xl = x_sample.reshape(T_LAT, D)

    cond16 = jnp.concatenate([c, c_ctx[None, :], jnp.zeros((16 - N_LAT_B - 1, D), F32)], axis=0)
    mods = _adaln(cond16, w_ada[l], b_ada[l][None, :]).reshape(16, 6, D)

    wi = w_in[l]
    o_kr = QL + KVL
    o_u = o_kr + QK_ROPE
    win_a = wi[:, :o_kr].astype(BF16)
    win_g = wi[:, o_u:].astype(BF16)
    win_r = jnp.pad(wi[:, o_kr:o_u], ((0, 0), (0, LANES - QK_ROPE))).astype(BF16)
    wuq = jnp.pad(w_uq[l].reshape(QL, H, QK_NOPE + QK_ROPE),
                  ((0, 0), (0, 0), (0, HEAD_PAD - QK_NOPE - QK_ROPE))).reshape(QL, H * HEAD_PAD).astype(BF16)
    wkv = w_ukv[l].reshape(KVL, H, QK_NOPE + V_DIM)
    wuk = wkv[:, :, :QK_NOPE].reshape(KVL, H * QK_NOPE).astype(BF16)
    wuvt = wkv[:, :, QK_NOPE:].reshape(KVL, H * V_DIM).T.astype(BF16)
    ws = w_spatial[l].astype(BF16)
    bsb = jnp.broadcast_to(b_spatial[l][:, :, None], (G_GROUPS, CHUNK, G_DIM))
    wrt = jnp.concatenate([w_group[l], w_router[l], jnp.zeros((D, LANES - N_GRP - N_EXP), F32)],
                          axis=1).astype(BF16)
    brt = jnp.concatenate([b_group[l], b_router[l], jnp.zeros((LANES - N_GRP - N_EXP,), F32)])[None, :]
    cos_t, sp_t, sm_t = _rope_tables()
    row = lambda v: v[l][None, :]

    q, k, vt, gm, ckv_n, kr = _proj(xc, xl, mods, win_a, win_g, win_r, row(q_norm_g), wuq, row(kv_norm_g), wuk, wuvt,
                                   row(gmlp_ln_g), row(gmlp_ln_b), ws, bsb, cos_t, sp_t, sm_t)
    kc, vct = _kvexp(cache_ckv[:, l].reshape(N_LAT_B * PAST, KVL),
                     cache_krope[:, l].reshape(N_LAT_B * PAST, QK_ROPE), wuk, wuvt)
    attn_c = _attn_ctx(q, k, vt)
    attn_l = _attn_lat(q, k, vt, kc, vct)

    x1, h2p, ridx, rwt = _post(attn_c, attn_l, gm, xc, xl, mods, w_out[l].astype(BF16), row(ln1_g), row(ln1_b),
                               wrt, brt)
    eid_k, rank, hist = _rank(ridx)
    dest, blk_expert, blk_valid, zero_blk, zero_valid, switch, wslot, next_expert = _block_plan(
        eid_k, rank, hist[:, 0, :N_EXP].astype(I32))
    xs = _dispatch(zero_blk, zero_valid, dest, h2p)
    ys = _experts(blk_expert, blk_valid, switch, wslot, next_expert, xs, w1[l], w3[l], w2[l])
    y_c, y_l = _final(_regroup_steps(dest, TF), ys, x1, rwt, mods, row(ln2_g), row(ln2_b))

    return (y_c.reshape(N_CTX_B, N_CTX, D), y_l.reshape(N_LAT_B, N_LAT, D),
            ckv_n.reshape(N_CTX_B, 1, N_CTX, KVL), kr.reshape(N_CTX_B, 1, N_CTX, QK_ROPE))
```

```python
import functools

import jax
import jax.numpy as jnp
import numpy as np
from jax import lax
from jax.experimental import pallas as pl
from jax.experimental.pallas import tpu as pltpu

F32 = jnp.float32
BF16 = jnp.bfloat16
U32 = jnp.uint32
I32 = jnp.int32

D = 2048
N_CTX_B, N_CTX = 32, 256
N_LAT_B, N_LAT = 4, 4096
PAST = 512
GRID_W = 64
H = 8
QK_NOPE, QK_ROPE, V_DIM = 128, 64, 128
QL, KVL = 512, 512
GW = 1024
G_GROUPS, G_DIM, CHUNK = 8, 128, 128
N_GRP, E_PER_GRP, N_EXP, TOP_K = 4, 8, 32, 2
E_HID = 512
ROPE_THETA = 10000.0
EPS = 1e-6
ALPHA = 2.0 ** 0.25
SM_SCALE = (QK_NOPE + QK_ROPE) ** -0.5
Q_SCALE = SM_SCALE * float(np.log2(np.e))

T_CTX = N_CTX_B * N_CTX
T_LAT = N_LAT_B * N_LAT
T = T_CTX + T_LAT
N_ASSIGN = T * TOP_K

LANES = 128
HEAD_PAD = 256
V_AUG = V_DIM + 16
TM = 256
TP = 512
POST_SUBS = (256, 256)
SUB = 256
TD = 2048
TF = 256
CTX_PER_STEP = 2
TQ = 4096
TK = 512
BM = 256
N_BLOCKS = N_ASSIGN // BM + N_EXP
N_ZERO = 2 * N_EXP
VMEM_LIMIT = 56 * 1024 * 1024

NP_TILES = T_CTX // TM
LAT_TILES_PER_B = N_LAT // TM


def _cp(sem):
    return pltpu.CompilerParams(dimension_semantics=sem, vmem_limit_bytes=VMEM_LIMIT)


def _const_spec(shape):
    nd = len(shape)
    return pl.BlockSpec(shape, lambda *a: (0,) * nd, pipeline_mode=pl.Buffered(1))


def _dot(a, b):
    return jnp.dot(a, b, preferred_element_type=F32)


def _dot_nt(a, b):
    return lax.dot_general(a, b, (((1,), (1,)), ((), ())), preferred_element_type=F32)


def _layer_norm(x, g, b):
    mu = jnp.mean(x, -1, keepdims=True)
    xc = x - mu
    var = jnp.mean(xc * xc, -1, keepdims=True)
    return xc * lax.rsqrt(var + EPS) * g + b


def _rms_norm(x, g):
    return x * lax.rsqrt(jnp.mean(x * x, -1, keepdims=True) + EPS) * g


def _gelu(x):
    return 0.5 * x * (1.0 + lax.erf(x * np.float32(np.sqrt(0.5))))


def _pack_pair(lo, hi):
    lo_b = lax.bitcast_convert_type(lo.astype(BF16).astype(F32), U32)
    hi_b = lax.bitcast_convert_type(hi.astype(BF16).astype(F32), U32)
    return hi_b | (lo_b >> 16)


def _unpack_pair(w):
    lo = lax.bitcast_convert_type(w << 16, F32)
    hi = lax.bitcast_convert_type(w & jnp.uint32(0xFFFF0000), F32)
    return lo, hi


RT = 8


def _store_rows_as_tiles(ref, row0, val):
    n = val.shape[0]
    for j in range(RT):
        ref[pl.ds(row0 * RT + j, n, stride=RT), :] = val[:, j * LANES:(j + 1) * LANES]


def _load_tiles_as_rows(ref, row0, n):
    return jnp.concatenate([ref[pl.ds(row0 * RT + j, n, stride=RT), :] for j in range(RT)], axis=1)


def _adaln_kernel(cond_ref, w_ref, b_ref, o_ref):
    s = jax.nn.silu(cond_ref[...])
    s_hi = s.astype(BF16)
    s_lo = (s - s_hi.astype(F32)).astype(BF16)
    w = w_ref[...]
    w_hi = w.astype(BF16)
    w_lo = (w - w_hi.astype(F32)).astype(BF16)
    o_ref[...] = _dot(s_hi, w_hi) + _dot(s_lo, w_hi) + _dot(s_hi, w_lo) + b_ref[...]


def _adaln(cond16, w_ada, b_ada):
    tn = 1024
    n = w_ada.shape[1]
    return pl.pallas_call(
        _adaln_kernel,
        grid=(n // tn,),
        in_specs=[pl.BlockSpec((16, D), lambda j: (0, 0)),
                  pl.BlockSpec((D, tn), lambda j: (0, j)),
                  pl.BlockSpec((1, tn), lambda j: (0, j))],
        out_specs=pl.BlockSpec((16, tn), lambda j: (0, j)),
        out_shape=jax.ShapeDtypeStruct((16, n), F32),
        compiler_params=_cp(("arbitrary",)),
        name="adaln",
    )(cond16, w_ada, b_ada)


def _expand_kv(ckv_n, kr128, wuk_ref, wuvt_ref, k_ref, vt_ref):
    cb = ckv_n.astype(BF16)
    kn = _dot(cb, wuk_ref[...])
    vt = _dot_nt(wuvt_ref[...], cb).astype(BF16)
    ones = jnp.ones((V_AUG - V_DIM, vt.shape[1]), BF16)
    for h in range(H):
        vt_ref[h * V_AUG:h * V_AUG + V_DIM, :] = vt[h * V_DIM:(h + 1) * V_DIM]
        vt_ref[h * V_AUG + V_DIM:(h + 1) * V_AUG, :] = ones
    krb = kr128.astype(BF16)
    for h in range(H):
        k_ref[:, h * HEAD_PAD:h * HEAD_PAD + QK_NOPE] = kn[:, h * QK_NOPE:(h + 1) * QK_NOPE].astype(BF16)
        k_ref[:, h * HEAD_PAD + QK_NOPE:(h + 1) * HEAD_PAD] = krb


def _rope128(x, cos, s_plus, s_minus):
    return (x * cos + pltpu.roll(x, 16, 1) * s_plus + pltpu.roll(x, LANES - 16, 1) * s_minus)


def _proj_kernel(xc_ref, xl_ref, mods_ref, win_a_ref, win_g_ref, win_r_ref, qg_ref, wuq_ref, kvg_ref,
                 wuk_ref, wuv_ref,
                 lng_ref, lnb_ref, ws_ref, bsb_ref, cos_ref, sp_ref, sm_ref,
                 q_ref, k_ref, v_ref, gm_ref, ckv_ref, kr_ref):
    i = pl.program_id(0)
    x = jnp.where(i < NP_TILES, xc_ref[...], xl_ref[...])
    m = mods_ref[0]
    shift1, scale1 = m[0:1], m[1:2]
    hmod = (x * (1.0 + scale1) + shift1).astype(BF16)
    pg = _dot(hmod, win_g_ref[...])
    u, v = pg[:, :GW], pg[:, GW:]
    gu = _gelu(u)
    vn = _layer_norm(_gelu(v), lng_ref[...], lnb_ref[...]).astype(BF16)
    pa = _dot(hmod, win_a_ref[...])
    kr = _dot(hmod, win_r_ref[...])
    cq, ckv = pa[:, :QL], pa[:, QL:]

    cos, s_plus, s_minus = cos_ref[...], sp_ref[...], sm_ref[...]

    cq_n = _rms_norm(cq, qg_ref[...]).astype(BF16)
    q = _dot(cq_n, wuq_ref[...]) * Q_SCALE
    for h in range(H):
        q_ref[:, h * HEAD_PAD:h * HEAD_PAD + QK_NOPE] = q[:, h * HEAD_PAD:h * HEAD_PAD + QK_NOPE].astype(BF16)
        q_ref[:, h * HEAD_PAD + QK_NOPE:(h + 1) * HEAD_PAD] = _rope128(
            q[:, h * HEAD_PAD + QK_NOPE:(h + 1) * HEAD_PAD], cos, s_plus, s_minus).astype(BF16)

    ckv_n = _rms_norm(ckv, kvg_ref[...])
    _expand_kv(ckv_n, _rope128(kr, cos, s_plus, s_minus), wuk_ref, wuv_ref, k_ref, v_ref)

    for c in range(TM // CHUNK):
        rows = slice(c * CHUNK, (c + 1) * CHUNK)
        for g in range(G_GROUPS):
            cols = slice(g * G_DIM, (g + 1) * G_DIM)
            s = _dot(ws_ref[g], vn[rows, cols]) + bsb_ref[g]
            gm_ref[rows, cols] = (gu[rows, cols] * s).astype(BF16)

    @pl.when(i < NP_TILES)
    def _():
        ckv_ref[...] = ckv_n
        kr_ref[...] = kr[:, :QK_ROPE]


def _mods_index(i, tile=TM):
    return jnp.where(i < T_CTX // tile, N_LAT_B, (i - T_CTX // tile) // (N_LAT // tile))


def _proj(xc, xl, mods, win_a, win_g, win_r, qg, wuq, kvg, wuk, wuv, lng, lnb, ws, bsb, cos_t, sp_t, sm_t):
    def tok(width):
        return pl.BlockSpec((TM, width), lambda i: (i, 0))

    def pos_block(i):
        return jnp.where(i < NP_TILES, LAT_TILES_PER_B, (i - NP_TILES) % LAT_TILES_PER_B)

    rope_spec = pl.BlockSpec((TM, LANES), lambda i: (pos_block(i), 0))
    return pl.pallas_call(
        _proj_kernel,
        grid=(T // TM,),
        in_specs=[
            pl.BlockSpec((TM, D), lambda i: (jnp.minimum(i, NP_TILES - 1), 0)),
            pl.BlockSpec((TM, D), lambda i: (jnp.maximum(i - NP_TILES, 0), 0)),
            pl.BlockSpec((1, 6, D), lambda i: (_mods_index(i), 0, 0)),
            _const_spec((D, QL + KVL)), _const_spec((D, 2 * GW)), _const_spec((D, LANES)),
            _const_spec((1, QL)), _const_spec((QL, H * HEAD_PAD)),
            _const_spec((1, KVL)), _const_spec((KVL, H * QK_NOPE)), _const_spec((H * V_DIM, KVL)),
            _const_spec((1, GW)), _const_spec((1, GW)),
            _const_spec((G_GROUPS, CHUNK, CHUNK)), _const_spec((G_GROUPS, CHUNK, G_DIM)),
            rope_spec, rope_spec, rope_spec,
        ],
        out_specs=[tok(H * HEAD_PAD), tok(H * HEAD_PAD),
                   pl.BlockSpec((H * V_AUG, TM), lambda i: (0, i)), tok(GW),
                   pl.BlockSpec((TM, KVL), lambda i: (jnp.minimum(i, NP_TILES - 1), 0)),
                   pl.BlockSpec((TM, QK_ROPE), lambda i: (jnp.minimum(i, NP_TILES - 1), 0))],
        out_shape=[
            jax.ShapeDtypeStruct((T, H * HEAD_PAD), BF16),
            jax.ShapeDtypeStruct((T, H * HEAD_PAD), BF16),
            jax.ShapeDtypeStruct((H * V_AUG, T), BF16),
            jax.ShapeDtypeStruct((T, GW), BF16),
            jax.ShapeDtypeStruct((T_CTX, KVL), F32),
            jax.ShapeDtypeStruct((T_CTX, QK_ROPE), F32),
        ],
        compiler_params=_cp(("arbitrary",)),
        name="proj",
    )(xc, xl, mods, win_a, win_g, win_r, qg, wuq, kvg, wuk, wuv, lng, lnb, ws, bsb, cos_t, sp_t, sm_t)


def _kvexp_kernel(ckv_ref, kr_ref, wuk_ref, wuv_ref, k_ref, v_ref):
    kr = kr_ref[...]
    kr128 = jnp.concatenate([kr, jnp.zeros_like(kr)], axis=1)
    _expand_kv(ckv_ref[...], kr128, wuk_ref, wuv_ref, k_ref, v_ref)


def _kvexp(ckv, kr, wuk, wuv):
    rows = ckv.shape[0]
    return pl.pallas_call(
        _kvexp_kernel,
        grid=(rows // TM,),
        in_specs=[pl.BlockSpec((TM, KVL), lambda i: (i, 0)), pl.BlockSpec((TM, QK_ROPE), lambda i: (i, 0)),
                  _const_spec((KVL, H * QK_NOPE)), _const_spec((H * V_DIM, KVL))],
        out_specs=[pl.BlockSpec((TM, H * HEAD_PAD), lambda i: (i, 0)),
                   pl.BlockSpec((H * V_AUG, TM), lambda i: (0, i))],
        out_shape=[jax.ShapeDtypeStruct((rows, H * HEAD_PAD), BF16),
                   jax.ShapeDtypeStruct((H * V_AUG, rows), BF16)],
        compiler_params=_cp(("arbitrary",)),
        name="kvexp",
    )(ckv, kr, wuk, wuv)


def _probs(s, m):
    m_new = jnp.maximum(m, jnp.max(s, 0, keepdims=True))
    return m_new, jnp.exp2(m - m_new), jnp.exp2(s - m_new).astype(BF16)


def _attn_finish(acc):
    return (acc[:V_DIM] / acc[V_DIM:V_DIM + 1]).T.astype(BF16)


def _attn_ctx_kernel(q_ref, k_ref, vt_ref, o_ref):
    m0 = jnp.full((1, N_CTX), -jnp.inf, F32)
    for b in range(CTX_PER_STEP):
        rows = slice(b * N_CTX, (b + 1) * N_CTX)
        for h in range(H):
            qh = q_ref[rows, h * HEAD_PAD:(h + 1) * HEAD_PAD]
            kh = k_ref[rows, h * HEAD_PAD:(h + 1) * HEAD_PAD]
            _, _, p = _probs(_dot_nt(kh, qh), m0)
            acc = _dot(vt_ref[h * V_AUG:(h + 1) * V_AUG, rows], p)
            o_ref[rows, h * V_DIM:(h + 1) * V_DIM] = _attn_finish(acc)


def _attn_ctx(q, k, vt):
    n = CTX_PER_STEP * N_CTX
    return pl.pallas_call(
        _attn_ctx_kernel,
        grid=(N_CTX_B // CTX_PER_STEP,),
        in_specs=[pl.BlockSpec((n, H * HEAD_PAD), lambda b: (b, 0)),
                  pl.BlockSpec((n, H * HEAD_PAD), lambda b: (b, 0)),
                  pl.BlockSpec((H * V_AUG, n), lambda b: (0, b))],
        out_specs=pl.BlockSpec((n, H * V_DIM), lambda b: (b, 0)),
        out_shape=jax.ShapeDtypeStruct((T_CTX, H * V_DIM), BF16),
        compiler_params=_cp(("arbitrary",)),
        name="attn_ctx",
    )(q, k, vt)


def _attn_lat_kernel(q_ref, k_ref, vt_ref, kc_ref, vct_ref, o_ref):
    q = q_ref[...]
    n_lat = N_LAT // TK

    def keys(c):
        return k_ref[c * TK:(c + 1) * TK, :] if c < n_lat else kc_ref[...]

    def values_t(c):
        return vt_ref[:, c * TK:(c + 1) * TK] if c < n_lat else vct_ref[...]

    m = jnp.full((1, TQ), -jnp.inf, F32)
    acc = jnp.zeros((V_AUG, TQ), F32)
    s_next = _dot_nt(keys(0), q)
    m_next = jnp.maximum(m, jnp.max(s_next, 0, keepdims=True))
    p_prev = alpha_prev = None
    for c in range(n_lat + 1):
        s, m_new = s_next, m_next
        if c < n_lat:
            s_next = _dot_nt(keys(c + 1), q)
            m_next = jnp.maximum(m_new, jnp.max(s_next, 0, keepdims=True))
        if c > 0:
            acc = alpha_prev * acc + _dot(values_t(c - 1), p_prev)
        alpha_prev = jnp.exp2(m - m_new)
        p_prev = jnp.exp2(s - m_new).astype(BF16)
        m = m_new
    acc = alpha_prev * acc + _dot(values_t(n_lat), p_prev)
    o_ref[...] = _attn_finish(acc)


def _attn_lat(q, k, vt, kc, vct):
    qb0 = T_CTX // TQ
    qpb = N_LAT // TQ
    kb0 = T_CTX // N_LAT
    return pl.pallas_call(
        _attn_lat_kernel,
        grid=(N_LAT_B, H, qpb),
        in_specs=[pl.BlockSpec((TQ, HEAD_PAD), lambda b, h, qi: (qb0 + b * qpb + qi, h)),
                  pl.BlockSpec((N_LAT, HEAD_PAD), lambda b, h, qi: (kb0 + b, h)),
                  pl.BlockSpec((V_AUG, N_LAT), lambda b, h, qi: (h, kb0 + b)),
                  pl.BlockSpec((PAST, HEAD_PAD), lambda b, h, qi: (b, h)),
                  pl.BlockSpec((V_AUG, PAST), lambda b, h, qi: (h, b))],
        out_specs=pl.BlockSpec((TQ, V_DIM), lambda b, h, qi: (b * qpb + qi, h)),
        out_shape=jax.ShapeDtypeStruct((T_LAT, H * V_DIM), BF16),
        compiler_params=_cp(("arbitrary", "arbitrary", "arbitrary")),
        name="attn_lat",
    )(q, k, vt, kc, vct)


def _route(logits):
    lane_i = lax.broadcasted_iota(I32, logits.shape, 1)
    lane = lane_i.astype(F32)
    neg = jnp.float32(-jnp.inf)
    far = jnp.float32(LANES)
    gl = jnp.where(lane_i < N_GRP, logits, neg)
    gmax = jnp.max(gl, -1, keepdims=True)
    gidx = jnp.min(jnp.where(gl == gmax, lane, far), -1, keepdims=True)
    p_top = 1.0 / jnp.sum(jnp.exp(gl - gmax), -1, keepdims=True)
    grp_of_lane = ((lane_i - N_GRP) >> 3).astype(F32)
    in_grp = (lane_i >= N_GRP) & (lane_i < N_GRP + N_EXP) & (grp_of_lane == gidx)
    el = jnp.where(in_grp, logits, neg)
    m1 = jnp.max(el, -1, keepdims=True)
    i1 = jnp.min(jnp.where(el == m1, lane, far), -1, keepdims=True)
    el2 = jnp.where(lane == i1, neg, el)
    m2 = jnp.max(el2, -1, keepdims=True)
    i2 = jnp.min(jnp.where(el2 == m2, lane, far), -1, keepdims=True)
    e2 = jnp.exp(m2 - m1)
    w1 = p_top / (1.0 + e2)
    w2 = p_top * e2 / (1.0 + e2)
    return i1 - N_GRP, i2 - N_GRP, w1, w2


def _pair(a, b, dtype):
    two = lax.broadcasted_iota(I32, (a.shape[0], TOP_K), 1)
    return jnp.where(two == 0, a, b).astype(dtype)


def _post_kernel(ac_ref, al_ref, gm_ref, xc_ref, xl_ref, mods_ref, wout_ref, g1_ref, b1_ref, wrt_ref,
                 brt_ref, x1_ref, h2p_ref, ridx_ref, rwt_ref):
    i = pl.program_id(0)
    is_ctx = i < T_CTX // TP
    m = mods_ref[0]
    gate1, shift2, scale2 = m[2:3], m[3:4], m[4:5]

    def mix_of(rows):
        attn = jnp.where(is_ctx, ac_ref[rows, :], al_ref[rows, :])
        return _dot(jnp.concatenate([attn, gm_ref[rows, :]], axis=1), wout_ref[...])

    bounds = np.cumsum((0,) + POST_SUBS)
    subs = [slice(int(a), int(b)) for a, b in zip(bounds[:-1], bounds[1:])]
    mix_next = mix_of(subs[0])
    for s, rows in enumerate(subs):
        mix = mix_next
        if s + 1 < len(subs):
            mix_next = mix_of(subs[s + 1])
        x = jnp.where(is_ctx, xc_ref[rows, :], xl_ref[rows, :])
        x1 = _layer_norm(ALPHA * x + gate1 * mix, g1_ref[...], b1_ref[...])
        x1_ref[rows, :] = x1
        h2 = x1 * (1.0 + scale2) + shift2
        _store_rows_as_tiles(h2p_ref, rows.start, _pack_pair(h2[:, :D // 2], h2[:, D // 2:]))
        e1, e2, w1, w2 = _route(_dot(h2.astype(BF16), wrt_ref[...]) + brt_ref[...])
        ridx_ref[rows, :] = _pair(e1, e2, I32)
        rwt_ref[rows, :] = _pair(w1, w2, F32)


def _post(attn_c, attn_l, gm, xc, xl, mods, wout, g1, b1, wrt, brt):
    npt = T_CTX // TP

    def tok(width):
        return pl.BlockSpec((TP, width), lambda i: (i, 0))

    def ctx(width):
        return pl.BlockSpec((TP, width), lambda i: (jnp.minimum(i, npt - 1), 0))

    def lat(width):
        return pl.BlockSpec((TP, width), lambda i: (jnp.maximum(i - npt, 0), 0))

    return pl.pallas_call(
        _post_kernel,
        grid=(T // TP,),
        in_specs=[ctx(H * V_DIM), lat(H * V_DIM), tok(GW), ctx(D), lat(D),
                  pl.BlockSpec((1, 6, D), lambda i: (_mods_index(i, TP), 0, 0)),
                  _const_spec((D, D)), _const_spec((1, D)), _const_spec((1, D)),
                  _const_spec((D, LANES)), _const_spec((1, LANES))],
        out_specs=[tok(D), pl.BlockSpec((TP * RT, LANES), lambda i: (i, 0)), tok(TOP_K), tok(TOP_K)],
        out_shape=[jax.ShapeDtypeStruct((T, D), F32),
                   jax.ShapeDtypeStruct((T * RT, LANES), U32),
                   jax.ShapeDtypeStruct((T, TOP_K), I32),
                   jax.ShapeDtypeStruct((T, TOP_K), F32)],
        compiler_params=_cp(("arbitrary",)),
        name="post",
    )(attn_c, attn_l, gm, xc, xl, mods, wout, g1, b1, wrt, brt)


def _as_lane_row(col):
    return jnp.broadcast_to(col, (col.shape[0], LANES)).T[0:1, :]


def _rank_kernel(ridx_ref, tri_ref, e_ref, rank_ref, hist_ref):
    lane = lax.broadcasted_iota(I32, (SUB, LANES), 1).astype(F32)
    seen = jnp.zeros((1, LANES), F32)
    for s in range(TP // SUB):
        rows = slice(s * SUB, (s + 1) * SUB)
        e = ridx_ref[rows, :].astype(F32)
        ohs = [(lane == e[:, k:k + 1]).astype(F32) for k in range(TOP_K)]
        both = ohs[0] + ohs[1]
        before = _dot(tri_ref[...], both.astype(BF16)) + seen
        for k in range(TOP_K):
            rank_k = jnp.sum(before * ohs[k], -1, keepdims=True)
            e_ref[0, k:k + 1, rows] = _as_lane_row(e[:, k:k + 1]).astype(I32)
            rank_ref[0, k:k + 1, rows] = _as_lane_row(rank_k).astype(I32)
        seen = seen + jnp.sum(both, 0, keepdims=True)
    hist_ref[0] = jnp.broadcast_to(seen, (8, LANES))


def _rank(ridx):
    dense = pl.BlockSpec((1, TOP_K, TP), lambda i: (i, 0, 0))
    return pl.pallas_call(
        _rank_kernel,
        grid=(T // TP,),
        in_specs=[pl.BlockSpec((TP, TOP_K), lambda i: (i, 0)), _const_spec((SUB, SUB))],
        out_specs=[dense, dense, pl.BlockSpec((1, 8, LANES), lambda i: (i, 0, 0))],
        out_shape=[jax.ShapeDtypeStruct((T // TP, TOP_K, TP), I32),
                   jax.ShapeDtypeStruct((T // TP, TOP_K, TP), I32),
                   jax.ShapeDtypeStruct((T // TP, 8, LANES), F32)],
        compiler_params=_cp(("arbitrary",)),
        name="rank",
    )(ridx, jnp.asarray(np.tril(np.ones((SUB, SUB), np.float32), -1), BF16))


def _row_copy(src, src_row, dst, dst_row, sem):
    def tile(ref, row):
        return ref.at[pl.ds(pl.multiple_of(row * RT, RT), RT)]

    return pltpu.make_async_copy(tile(src, src_row), tile(dst, dst_row), sem)


def _dispatch_kernel(zb_ref, zv_ref, d0_ref, d1_ref, h2p_ref, xs_ref, zbuf, sem):
    i = pl.program_id(0)
    dest_refs = (d0_ref, d1_ref)

    @pl.when(i == 0)
    def _():
        zbuf[...] = jnp.zeros_like(zbuf)

        def zero_block(j):
            start = pl.multiple_of(zb_ref[j] * (BM * RT), BM * RT)
            return pltpu.make_async_copy(zbuf, xs_ref.at[pl.ds(start, BM * RT)], sem)

        def start(j, carry):
            @pl.when(zv_ref[j] != 0)
            def _():
                zero_block(j).start()
            return carry

        def wait(j, carry):
            @pl.when(zv_ref[j] != 0)
            def _():
                zero_block(j).wait()
            return carry

        lax.fori_loop(0, N_ZERO, start, 0)
        lax.fori_loop(0, N_ZERO, wait, 0)

    def issue(r, carry):
        for k in range(TOP_K):
            _row_copy(h2p_ref, r, xs_ref, dest_refs[k][0, 0, r], sem).start(priority=k)
        return carry

    lax.fori_loop(0, TD, issue, 0, unroll=8)
    for _ in range(TOP_K):
        pltpu.make_async_copy(h2p_ref, xs_ref.at[pl.ds(0, TD * RT)], sem).wait()


def _regroup_steps(dest, rows):
    return [dest[:, k, :].reshape(T // rows, 1, rows) for k in range(TOP_K)]


def _dispatch(zero_blk, zero_valid, dest, h2p):
    grid_spec = pltpu.PrefetchScalarGridSpec(
        num_scalar_prefetch=2,
        grid=(T // TD,),
        in_specs=[pl.BlockSpec((1, 1, TD), lambda i, zb, zv: (i, 0, 0), memory_space=pltpu.SMEM),
                  pl.BlockSpec((1, 1, TD), lambda i, zb, zv: (i, 0, 0), memory_space=pltpu.SMEM),
                  pl.BlockSpec((TD * RT, LANES), lambda i, zb, zv: (i, 0))],
        out_specs=pl.BlockSpec(memory_space=pl.ANY),
        scratch_shapes=[pltpu.VMEM((BM * RT, LANES), U32), pltpu.SemaphoreType.DMA(())],
    )
    return pl.pallas_call(
        _dispatch_kernel,
        grid_spec=grid_spec,
        out_shape=jax.ShapeDtypeStruct((N_BLOCKS * BM * RT, LANES), U32),
        compiler_params=_cp(("arbitrary",)),
        name="dispatch",
    )(zero_blk, zero_valid, *_regroup_steps(dest, TD), h2p)


def _experts_kernel(be_ref, bv_ref, sw_ref, es_ref, nx_ref, x_ref, w1_hbm, w3_hbm, w2_hbm, y_ref,
                    wf1, wf3, wf2, w1b, w3b, w2b, sems):
    i = pl.program_id(0)
    e = be_ref[i]
    nv = bv_ref[i]

    def weight_copies(expert, s):
        return (pltpu.make_async_copy(w1_hbm.at[expert], wf1.at[s], sems.at[s]),
                pltpu.make_async_copy(w3_hbm.at[expert], wf3.at[s], sems.at[s]),
                pltpu.make_async_copy(w2_hbm.at[expert], wf2.at[s], sems.at[s]))

    @pl.when(sw_ref[i] != 0)
    def _():
        s = es_ref[i]

        @pl.when(i == 0)
        def _():
            for c in weight_copies(e, s):
                c.start()

        for c in weight_copies(e, s):
            c.wait()
        nxt = nx_ref[i]

        @pl.when(nxt >= 0)
        def _():
            for c in weight_copies(nxt, 1 - s):
                c.start(priority=1)

        w1b[...] = wf1[s].astype(BF16)
        w3b[...] = wf3[s].astype(BF16)
        w2b[...] = wf2[s].astype(BF16)

    @pl.when(nv > 0)
    def _():
        lo, hi = _unpack_pair(_load_tiles_as_rows(x_ref, 0, BM))
        xb = jnp.concatenate([lo.astype(BF16), hi.astype(BF16)], axis=1)
        a = _dot(xb, w1b[...])
        b = _dot(xb, w3b[...])
        hid = (jax.nn.silu(a) * b).astype(BF16)
        y = _dot(hid, w2b[...])
        _store_rows_as_tiles(y_ref, 0, _pack_pair(y[:, :D // 2], y[:, D // 2:]))

    @pl.when(nv == 0)
    def _():
        y_ref[...] = jnp.zeros_like(y_ref)


def _experts(blk_expert, blk_valid, switch, wslot, next_expert, xs, w1, w3, w2):
    hbm = pl.BlockSpec(memory_space=pl.ANY)
    grid_spec = pltpu.PrefetchScalarGridSpec(
        num_scalar_prefetch=5,
        grid=(N_BLOCKS,),
        in_specs=[pl.BlockSpec((BM * RT, LANES), lambda i, *_: (i, 0)), hbm, hbm, hbm],
        out_specs=pl.BlockSpec((BM * RT, LANES), lambda i, *_: (i, 0)),
        scratch_shapes=[pltpu.VMEM((2, D, E_HID), F32), pltpu.VMEM((2, D, E_HID), F32),
                        pltpu.VMEM((2, E_HID, D), F32),
                        pltpu.VMEM((D, E_HID), BF16), pltpu.VMEM((D, E_HID), BF16),
                        pltpu.VMEM((E_HID, D), BF16), pltpu.SemaphoreType.DMA((2,))],
    )
    return pl.pallas_call(
        _experts_kernel,
        grid_spec=grid_spec,
        out_shape=jax.ShapeDtypeStruct((N_BLOCKS * BM * RT, LANES), U32),
        compiler_params=_cp(("arbitrary",)),
        name="experts",
    )(blk_expert, blk_valid, switch, wslot, next_expert, xs, w1, w3, w2)


def _final_kernel(dcur0_ref, dcur1_ref, dnext0_ref, dnext1_ref, ys_ref, x1_ref, rwt_ref, mods_ref,
                  g2_ref, b2_ref, oc_ref, ol_ref, buf, sems):
    i = pl.program_id(0)
    dcur_ref, dnext_ref = (dcur0_ref, dcur1_ref), (dnext0_ref, dnext1_ref)
    n = pl.num_programs(0)
    slot = i % 2

    def first_row(s, k):
        return (s * TOP_K + k) * TF

    def issue(d_ref, s):
        def body(r, carry):
            for k in range(TOP_K):
                _row_copy(ys_ref, d_ref[k][0, 0, r], buf, first_row(s, k) + r, sems.at[s]).start()
            return carry

        lax.fori_loop(0, TF, body, 0, unroll=8)

    @pl.when(i == 0)
    def _():
        issue(dcur_ref, 0)

    @pl.when(i + 1 < n)
    def _():
        issue(dnext_ref, 1 - slot)

    for k in range(TOP_K):
        pltpu.make_async_copy(ys_ref.at[pl.ds(0, TF * RT)], buf.at[pl.ds(0, TF * RT)], sems.at[slot]).wait()

    w = rwt_ref[...]
    lo0, hi0 = _unpack_pair(_load_tiles_as_rows(buf, first_row(slot, 0), TF))
    lo1, hi1 = _unpack_pair(_load_tiles_as_rows(buf, first_row(slot, 1), TF))
    w0, w1 = w[:, 0:1], w[:, 1:2]
    moe = jnp.concatenate([lo0 * w0 + lo1 * w1, hi0 * w0 + hi1 * w1], axis=1)
    gate2 = mods_ref[0][5:6]
    out = _layer_norm(ALPHA * x1_ref[...] + gate2 * moe, g2_ref[...], b2_ref[...])

    @pl.when(i < T_CTX // TF)
    def _():
        oc_ref[...] = out

    @pl.when(i >= T_CTX // TF)
    def _():
        ol_ref[...] = out


def _final(dest_k, ys, x1, rwt, mods, g2, b2):
    n = T // TF
    npt = T_CTX // TF
    cur = pl.BlockSpec((1, 1, TF), lambda i: (i, 0, 0), memory_space=pltpu.SMEM)
    nxt = pl.BlockSpec((1, 1, TF), lambda i: (jnp.minimum(i + 1, n - 1), 0, 0), memory_space=pltpu.SMEM)
    return pl.pallas_call(
        _final_kernel,
        grid=(n,),
        in_specs=[cur, cur, nxt, nxt,
                  pl.BlockSpec(memory_space=pl.ANY),
                  pl.BlockSpec((TF, D), lambda i: (i, 0)),
                  pl.BlockSpec((TF, TOP_K), lambda i: (i, 0)),
                  pl.BlockSpec((1, 6, D), lambda i: (_mods_index(i, TF), 0, 0)),
                  _const_spec((1, D)), _const_spec((1, D))],
        out_specs=[pl.BlockSpec((TF, D), lambda i: (jnp.minimum(i, npt - 1), 0)),
                   pl.BlockSpec((TF, D), lambda i: (jnp.maximum(i - npt, 0), 0))],
        out_shape=[jax.ShapeDtypeStruct((T_CTX, D), F32), jax.ShapeDtypeStruct((T_LAT, D), F32)],
        scratch_shapes=[pltpu.VMEM((2 * TOP_K * TF * RT, LANES), U32), pltpu.SemaphoreType.DMA((2,))],
        compiler_params=_cp(("arbitrary",)),
        name="final",
    )(*dest_k, *dest_k, ys, x1, rwt, mods, g2, b2)


def _rope_tables():
    n = N_LAT
    rows = n // GRID_W
    f32 = np.float32
    row = np.repeat(np.arange(rows, dtype=f32), GRID_W)
    col = np.tile(np.arange(GRID_W, dtype=f32), rows)
    half = QK_ROPE // 2
    inv = (f32(ROPE_THETA) ** (-np.arange(0, half, 2, dtype=f32) / f32(half))).astype(f32)
    ang_r, ang_c = (row[:, None] * inv).astype(f32), (col[:, None] * inv).astype(f32)
    cr, sr = np.cos(ang_r).astype(f32), np.sin(ang_r).astype(f32)
    cc, sc = np.cos(ang_c).astype(f32), np.sin(ang_c).astype(f32)
    z16 = np.zeros_like(cr)
    pad = lambda v: np.full((n, LANES - QK_ROPE), v, f32)
    cos = np.concatenate([cr, cr, cc, cc, pad(1.0)], axis=1)
    s_plus = np.concatenate([z16, sr, z16, sc, pad(0.0)], axis=1)
    s_minus = np.concatenate([-sr, z16, -sc, z16, pad(0.0)], axis=1)
    ident = lambda v: np.full((TM, LANES), v, f32)
    return (jnp.asarray(np.concatenate([cos, ident(1.0)], 0)),
            jnp.asarray(np.concatenate([s_plus, ident(0.0)], 0)),
            jnp.asarray(np.concatenate([s_minus, ident(0.0)], 0)))


def _block_plan(eid_k, rank, hist):
    counts = jnp.sum(hist, axis=0)
    nblk = (counts + BM - 1) // BM
    bends = jnp.cumsum(nblk)
    bstarts = bends - nblk
    base = bstarts[None, :] * BM + jnp.cumsum(hist, axis=0) - hist
    eid = jnp.arange(N_EXP, dtype=I32)
    hit = eid_k[:, :, None, :] == eid[None, None, :, None]
    first = jnp.sum(jnp.where(hit, base[:, None, :, None], 0), axis=2)
    dest = (first + rank).astype(I32)
    blk = jnp.arange(N_BLOCKS, dtype=I32)
    blk_expert = jnp.minimum(jnp.sum((blk[:, None] >= bends[None, :]).astype(I32), axis=1), N_EXP - 1)
    of_blk = blk_expert[:, None] == eid[None, :]

    def per_block(v):
        return jnp.sum(jnp.where(of_blk, v[None, :], 0), axis=1)

    left = per_block(counts) - (blk - per_block(bstarts)) * BM
    blk_valid = jnp.where(blk < bends[-1], jnp.clip(left, 0, BM), 0).astype(I32)
    unused = bends[-1] + jnp.arange(N_EXP, dtype=I32)
    zero_blk = jnp.concatenate([jnp.maximum(bends - 1, 0), jnp.minimum(unused, N_BLOCKS - 1)])
    zero_valid = jnp.concatenate([counts % BM != 0, unused < N_BLOCKS])
    prev_expert = jnp.concatenate([jnp.full((1,), -1, I32), blk_expert[:-1]])
    switch = (blk < bends[-1]) & (blk_expert != prev_expert)
    wslot = (jnp.cumsum(switch.astype(I32)) - 1) % 2
    later = (eid[None, :] > eid[:, None]) & (nblk[None, :] > 0)
    next_of = jnp.min(jnp.where(later, eid[None, :], N_EXP), axis=1)
    next_expert = per_block(jnp.where(next_of < N_EXP, next_of, -1))
    return (dest, blk_expert.astype(I32), blk_valid, zero_blk.astype(I32),
            zero_valid.astype(I32), switch.astype(I32), wslot.astype(I32), next_expert.astype(I32))


def kernel(x_prompt, x_sample, cache_ckv, cache_krope, c, c_ctx, w_ada, b_ada, w_in, q_norm_g, w_uq,
           kv_norm_g, w_ukv, gmlp_ln_g, gmlp_ln_b, w_spatial, b_spatial, w_out, ln1_g, ln1_b, w_group,
           b_group, w_router, b_router, w1, w3, w2, ln2_g, ln2_b):
    l = 0
    xc = x_prompt.reshape(T_CTX, D)
    xl = x_sample.reshape(T_LAT, D)

    cond16 = jnp.concatenate([c, c_ctx[None, :], jnp.zeros((16 - N_LAT_B - 1, D), F32)], axis=0)
    mods = _adaln(cond16, w_ada[l], b_ada[l][None, :]).reshape(16, 6, D)

    wi = w_in[l]
    o_kr = QL + KVL
    o_u = o_kr + QK_ROPE
    win_a = wi[:, :o_kr].astype(BF16)
    win_g = wi[:, o_u:].astype(BF16)
    win_r = jnp.pad(wi[:, o_kr:o_u], ((0, 0), (0, LANES - QK_ROPE))).astype(BF16)
    wuq = jnp.pad(w_uq[l].reshape(QL, H, QK_NOPE + QK_ROPE),
                  ((0, 0), (0, 0), (0, HEAD_PAD - QK_NOPE - QK_ROPE))).reshape(QL, H * HEAD_PAD).astype(BF16)
    wkv = w_ukv[l].reshape(KVL, H, QK_NOPE + V_DIM)
    wuk = wkv[:, :, :QK_NOPE].reshape(KVL, H * QK_NOPE).astype(BF16)
    wuvt = wkv[:, :, QK_NOPE:].reshape(KVL, H * V_DIM).T.astype(BF16)
    ws = w_spatial[l].astype(BF16)
    bsb = jnp.broadcast_to(b_spatial[l][:, :, None], (G_GROUPS, CHUNK, G_DIM))
    wrt = jnp.concatenate([w_group[l], w_router[l], jnp.zeros((D, LANES - N_GRP - N_EXP), F32)],
                          axis=1).astype(BF16)
    brt = jnp.concatenate([b_group[l], b_router[l], jnp.zeros((LANES - N_GRP - N_EXP,), F32)])[None, :]
    cos_t, sp_t, sm_t = _rope_tables()
    row = lambda v: v[l][None, :]

    q, k, vt, gm, ckv_n, kr = _proj(xc, xl, mods, win_a, win_g, win_r, row(q_norm_g), wuq, row(kv_norm_g), wuk, wuvt,
                                   row(gmlp_ln_g), row(gmlp_ln_b), ws, bsb, cos_t, sp_t, sm_t)
    kc, vct = _kvexp(cache_ckv[:, l].reshape(N_LAT_B * PAST, KVL),
                     cache_krope[:, l].reshape(N_LAT_B * PAST, QK_ROPE), wuk, wuvt)
    attn_c = _attn_ctx(q, k, vt)
    attn_l = _attn_lat(q, k, vt, kc, vct)

    x1, h2p, ridx, rwt = _post(attn_c, attn_l, gm, xc, xl, mods, w_out[l].astype(BF16), row(ln1_g), row(ln1_b),
                               wrt, brt)
    eid_k, rank, hist = _rank(ridx)
    dest, blk_expert, blk_valid, zero_blk, zero_valid, switch, wslot, next_expert = _block_plan(
        eid_k, rank, hist[:, 0, :N_EXP].astype(I32))
    xs = _dispatch(zero_blk, zero_valid, dest, h2p)
    ys = _experts(blk_expert, blk_valid, switch, wslot, next_expert, xs, w1[l], w3[l], w2[l])
    y_c, y_l = _final(_regroup_steps(dest, TF), ys, x1, rwt, mods, row(ln2_g), row(ln2_b))

    return (y_c.reshape(N_CTX_B, N_CTX, D), y_l.reshape(N_LAT_B, N_LAT, D),
            ckv_n.reshape(N_CTX_B, 1, N_CTX, KVL), kr.reshape(N_CTX_B, 1, N_CTX, QK_ROPE))
```

```python
import functools

import jax
import jax.numpy as jnp
import numpy as np
from jax import lax
from jax.experimental import pallas as pl
from jax.experimental.pallas import tpu as pltpu

F32 = jnp.float32
BF16 = jnp.bfloat16
U32 = jnp.uint32
I32 = jnp.int32

D = 2048
N_CTX_B, N_CTX = 32, 256
N_LAT_B, N_LAT = 4, 4096
PAST = 512
GRID_W = 64
H = 8
QK_NOPE, QK_ROPE, V_DIM = 128, 64, 128
QL, KVL = 512, 512
GW = 1024
G_GROUPS, G_DIM, CHUNK = 8, 128, 128
N_GRP, E_PER_GRP, N_EXP, TOP_K = 4, 8, 32, 2
E_HID = 512
ROPE_THETA = 10000.0
EPS = 1e-6
ALPHA = 2.0 ** 0.25
SM_SCALE = (QK_NOPE + QK_ROPE) ** -0.5
Q_SCALE = SM_SCALE * float(np.log2(np.e))

T_CTX = N_CTX_B * N_CTX
T_LAT = N_LAT_B * N_LAT
T = T_CTX + T_LAT
N_ASSIGN = T * TOP_K

LANES = 128
HEAD_PAD = 256
V_AUG = V_DIM + 16
TM = 256
TP = 512
POST_SUBS = (256, 256)
SUB = 256
TD = 2048
TF = 256
CTX_PER_STEP = 4
TQ = 4096
TK = 512
BM = 256
N_BLOCKS = N_ASSIGN // BM + N_EXP
N_ZERO = 2 * N_EXP
VMEM_LIMIT = 56 * 1024 * 1024

NP_TILES = T_CTX // TM
LAT_TILES_PER_B = N_LAT // TM


def _cp(sem):
    return pltpu.CompilerParams(dimension_semantics=sem, vmem_limit_bytes=VMEM_LIMIT)


def _const_spec(shape):
    nd = len(shape)
    return pl.BlockSpec(shape, lambda *a: (0,) * nd, pipeline_mode=pl.Buffered(1))


def _dot(a, b):
    return jnp.dot(a, b, preferred_element_type=F32)


def _dot_nt(a, b):
    return lax.dot_general(a, b, (((1,), (1,)), ((), ())), preferred_element_type=F32)


def _layer_norm(x, g, b):
    mu = jnp.mean(x, -1, keepdims=True)
    xc = x - mu
    var = jnp.mean(xc * xc, -1, keepdims=True)
    return xc * lax.rsqrt(var + EPS) * g + b


def _rms_norm(x, g):
    return x * lax.rsqrt(jnp.mean(x * x, -1, keepdims=True) + EPS) * g


def _gelu(x):
    return 0.5 * x * (1.0 + lax.erf(x * np.float32(np.sqrt(0.5))))


def _pack_pair(lo, hi):
    lo_b = lax.bitcast_convert_type(lo.astype(BF16).astype(F32), U32)
    hi_b = lax.bitcast_convert_type(hi.astype(BF16).astype(F32), U32)
    return hi_b | (lo_b >> 16)


def _unpack_pair(w):
    lo = lax.bitcast_convert_type(w << 16, F32)
    hi = lax.bitcast_convert_type(w & jnp.uint32(0xFFFF0000), F32)
    return lo, hi


RT = 8


def _store_rows_as_tiles(ref, row0, val):
    n = val.shape[0]
    for j in range(RT):
        ref[pl.ds(row0 * RT + j, n, stride=RT), :] = val[:, j * LANES:(j + 1) * LANES]


def _load_tiles_as_rows(ref, row0, n):
    return jnp.concatenate([ref[pl.ds(row0 * RT + j, n, stride=RT), :] for j in range(RT)], axis=1)


def _adaln_kernel(cond_ref, w_ref, b_ref, o_ref):
    s = jax.nn.silu(cond_ref[...])
    s_hi = s.astype(BF16)
    s_lo = (s - s_hi.astype(F32)).astype(BF16)
    w = w_ref[...]
    w_hi = w.astype(BF16)
    w_lo = (w - w_hi.astype(F32)).astype(BF16)
    o_ref[...] = _dot(s_hi, w_hi) + _dot(s_lo, w_hi) + _dot(s_hi, w_lo) + b_ref[...]


def _adaln(cond16, w_ada, b_ada):
    tn = 1024
    n = w_ada.shape[1]
    return pl.pallas_call(
        _adaln_kernel,
        grid=(n // tn,),
        in_specs=[pl.BlockSpec((16, D), lambda j: (0, 0)),
                  pl.BlockSpec((D, tn), lambda j: (0, j)),
                  pl.BlockSpec((1, tn), lambda j: (0, j))],
        out_specs=pl.BlockSpec((16, tn), lambda j: (0, j)),
        out_shape=jax.ShapeDtypeStruct((16, n), F32),
        compiler_params=_cp(("arbitrary",)),
        name="adaln",
    )(cond16, w_ada, b_ada)


def _expand_kv(ckv_n, kr128, wuk_ref, wuvt_ref, k_ref, vt_ref):
    cb = ckv_n.astype(BF16)
    kn = _dot(cb, wuk_ref[...])
    vt = _dot_nt(wuvt_ref[...], cb).astype(BF16)
    ones = jnp.ones((V_AUG - V_DIM, vt.shape[1]), BF16)
    for h in range(H):
        vt_ref[h * V_AUG:h * V_AUG + V_DIM, :] = vt[h * V_DIM:(h + 1) * V_DIM]
        vt_ref[h * V_AUG + V_DIM:(h + 1) * V_AUG, :] = ones
    krb = kr128.astype(BF16)
    for h in range(H):
        k_ref[:, h * HEAD_PAD:h * HEAD_PAD + QK_NOPE] = kn[:, h * QK_NOPE:(h + 1) * QK_NOPE].astype(BF16)
        k_ref[:, h * HEAD_PAD + QK_NOPE:(h + 1) * HEAD_PAD] = krb


def _rope128(x, cos, s_plus, s_minus):
    return (x * cos + pltpu.roll(x, 16, 1) * s_plus + pltpu.roll(x, LANES - 16, 1) * s_minus)


def _proj_kernel(xc_ref, xl_ref, mods_ref, win_a_ref, win_g_ref, win_r_ref, qg_ref, wuq_ref, kvg_ref,
                 wuk_ref, wuv_ref,
                 lng_ref, lnb_ref, ws_ref, bsb_ref, cos_ref, sp_ref, sm_ref,
                 q_ref, k_ref, v_ref, gm_ref, ckv_ref, kr_ref):
    i = pl.program_id(0)
    x = jnp.where(i < NP_TILES, xc_ref[...], xl_ref[...])
    m = mods_ref[0]
    shift1, scale1 = m[0:1], m[1:2]
    hmod = (x * (1.0 + scale1) + shift1).astype(BF16)
    pg = _dot(hmod, win_g_ref[...])
    u, v = pg[:, :GW], pg[:, GW:]
    gu = _gelu(u)
    vn = _layer_norm(_gelu(v), lng_ref[...], lnb_ref[...]).astype(BF16)
    pa = _dot(hmod, win_a_ref[...])
    kr = _dot(hmod, win_r_ref[...])
    cq, ckv = pa[:, :QL], pa[:, QL:]

    cos, s_plus, s_minus = cos_ref[...], sp_ref[...], sm_ref[...]

    cq_n = _rms_norm(cq, qg_ref[...]).astype(BF16)
    q = _dot(cq_n, wuq_ref[...]) * Q_SCALE
    for h in range(H):
        q_ref[:, h * HEAD_PAD:h * HEAD_PAD + QK_NOPE] = q[:, h * HEAD_PAD:h * HEAD_PAD + QK_NOPE].astype(BF16)
        q_ref[:, h * HEAD_PAD + QK_NOPE:(h + 1) * HEAD_PAD] = _rope128(
            q[:, h * HEAD_PAD + QK_NOPE:(h + 1) * HEAD_PAD], cos, s_plus, s_minus).astype(BF16)

    ckv_n = _rms_norm(ckv, kvg_ref[...])
    _expand_kv(ckv_n, _rope128(kr, cos, s_plus, s_minus), wuk_ref, wuv_ref, k_ref, v_ref)

    for c in range(TM // CHUNK):
        rows = slice(c * CHUNK, (c + 1) * CHUNK)
        for g in range(G_GROUPS):
            cols = slice(g * G_DIM, (g + 1) * G_DIM)
            s = _dot(ws_ref[g], vn[rows, cols]) + bsb_ref[g]
            gm_ref[rows, cols] = (gu[rows, cols] * s).astype(BF16)

    @pl.when(i < NP_TILES)
    def _():
        ckv_ref[...] = ckv_n
        kr_ref[...] = kr[:, :QK_ROPE]


def _mods_index(i, tile=TM):
    return jnp.where(i < T_CTX // tile, N_LAT_B, (i - T_CTX // tile) // (N_LAT // tile))


def _proj(xc, xl, mods, win_a, win_g, win_r, qg, wuq, kvg, wuk, wuv, lng, lnb, ws, bsb, cos_t, sp_t, sm_t):
    def tok(width):
        return pl.BlockSpec((TM, width), lambda i: (i, 0))

    def pos_block(i):
        return jnp.where(i < NP_TILES, LAT_TILES_PER_B, (i - NP_TILES) % LAT_TILES_PER_B)

    rope_spec = pl.BlockSpec((TM, LANES), lambda i: (pos_block(i), 0))
    return pl.pallas_call(
        _proj_kernel,
        grid=(T // TM,),
        in_specs=[
            pl.BlockSpec((TM, D), lambda i: (jnp.minimum(i, NP_TILES - 1), 0)),
            pl.BlockSpec((TM, D), lambda i: (jnp.maximum(i - NP_TILES, 0), 0)),
            pl.BlockSpec((1, 6, D), lambda i: (_mods_index(i), 0, 0)),
            _const_spec((D, QL + KVL)), _const_spec((D, 2 * GW)), _const_spec((D, LANES)),
            _const_spec((1, QL)), _const_spec((QL, H * HEAD_PAD)),
            _const_spec((1, KVL)), _const_spec((KVL, H * QK_NOPE)), _const_spec((H * V_DIM, KVL)),
            _const_spec((1, GW)), _const_spec((1, GW)),
            _const_spec((G_GROUPS, CHUNK, CHUNK)), _const_spec((G_GROUPS, CHUNK, G_DIM)),
            rope_spec, rope_spec, rope_spec,
        ],
        out_specs=[tok(H * HEAD_PAD), tok(H * HEAD_PAD),
                   pl.BlockSpec((H * V_AUG, TM), lambda i: (0, i)), tok(GW),
                   pl.BlockSpec((TM, KVL), lambda i: (jnp.minimum(i, NP_TILES - 1), 0)),
                   pl.BlockSpec((TM, QK_ROPE), lambda i: (jnp.minimum(i, NP_TILES - 1), 0))],
        out_shape=[
            jax.ShapeDtypeStruct((T, H * HEAD_PAD), BF16),
            jax.ShapeDtypeStruct((T, H * HEAD_PAD), BF16),
            jax.ShapeDtypeStruct((H * V_AUG, T), BF16),
            jax.ShapeDtypeStruct((T, GW), BF16),
            jax.ShapeDtypeStruct((T_CTX, KVL), F32),
            jax.ShapeDtypeStruct((T_CTX, QK_ROPE), F32),
        ],
        compiler_params=_cp(("arbitrary",)),
        name="proj",
    )(xc, xl, mods, win_a, win_g, win_r, qg, wuq, kvg, wuk, wuv, lng, lnb, ws, bsb, cos_t, sp_t, sm_t)


def _kvexp_kernel(ckv_ref, kr_ref, wuk_ref, wuv_ref, k_ref, v_ref):
    kr = kr_ref[...]
    kr128 = jnp.concatenate([kr, jnp.zeros_like(kr)], axis=1)
    _expand_kv(ckv_ref[...], kr128, wuk_ref, wuv_ref, k_ref, v_ref)


def _kvexp(ckv, kr, wuk, wuv):
    rows = ckv.shape[0]
    return pl.pallas_call(
        _kvexp_kernel,
        grid=(rows // TM,),
        in_specs=[pl.BlockSpec((TM, KVL), lambda i: (i, 0)), pl.BlockSpec((TM, QK_ROPE), lambda i: (i, 0)),
                  _const_spec((KVL, H * QK_NOPE)), _const_spec((H * V_DIM, KVL))],
        out_specs=[pl.BlockSpec((TM, H * HEAD_PAD), lambda i: (i, 0)),
                   pl.BlockSpec((H * V_AUG, TM), lambda i: (0, i))],
        out_shape=[jax.ShapeDtypeStruct((rows, H * HEAD_PAD), BF16),
                   jax.ShapeDtypeStruct((H * V_AUG, rows), BF16)],
        compiler_params=_cp(("arbitrary",)),
        name="kvexp",
    )(ckv, kr, wuk, wuv)


def _probs(s, m):
    m_new = jnp.maximum(m, jnp.max(s, 0, keepdims=True))
    return m_new, jnp.exp2(m - m_new), jnp.exp2(s - m_new).astype(BF16)


def _attn_finish(acc):
    return (acc[:V_DIM] / acc[V_DIM:V_DIM + 1]).T.astype(BF16)


def _attn_ctx_kernel(q_ref, k_ref, vt_ref, o_ref):
    m0 = jnp.full((1, N_CTX), -jnp.inf, F32)
    for b in range(CTX_PER_STEP):
        rows = slice(b * N_CTX, (b + 1) * N_CTX)
        for h in range(H):
            qh = q_ref[rows, h * HEAD_PAD:(h + 1) * HEAD_PAD]
            kh = k_ref[rows, h * HEAD_PAD:(h + 1) * HEAD_PAD]
            _, _, p = _probs(_dot_nt(kh, qh), m0)
            acc = _dot(vt_ref[h * V_AUG:(h + 1) * V_AUG, rows], p)
            o_ref[rows, h * V_DIM:(h + 1) * V_DIM] = _attn_finish(acc)


def _attn_ctx(q, k, vt):
    n = CTX_PER_STEP * N_CTX
    return pl.pallas_call(
        _attn_ctx_kernel,
        grid=(N_CTX_B // CTX_PER_STEP,),
        in_specs=[pl.BlockSpec((n, H * HEAD_PAD), lambda b: (b, 0)),
                  pl.BlockSpec((n, H * HEAD_PAD), lambda b: (b, 0)),
                  pl.BlockSpec((H * V_AUG, n), lambda b: (0, b))],
        out_specs=pl.BlockSpec((n, H * V_DIM), lambda b: (b, 0)),
        out_shape=jax.ShapeDtypeStruct((T_CTX, H * V_DIM), BF16),
        compiler_params=_cp(("arbitrary",)),
        name="attn_ctx",
    )(q, k, vt)


def _attn_lat_kernel(q_ref, k_ref, vt_ref, kc_ref, vct_ref, o_ref):
    q = q_ref[...]
    n_lat = N_LAT // TK

    def keys(c):
        return k_ref[c * TK:(c + 1) * TK, :] if c < n_lat else kc_ref[...]

    def values_t(c):
        return vt_ref[:, c * TK:(c + 1) * TK] if c < n_lat else vct_ref[...]

    m = jnp.full((1, TQ), -jnp.inf, F32)
    acc = jnp.zeros((V_AUG, TQ), F32)
    s_next = _dot_nt(keys(0), q)
    m_next = jnp.maximum(m, jnp.max(s_next, 0, keepdims=True))
    p_prev = alpha_prev = None
    for c in range(n_lat + 1):
        s, m_new = s_next, m_next
        if c < n_lat:
            s_next = _dot_nt(keys(c + 1), q)
            m_next = jnp.maximum(m_new, jnp.max(s_next, 0, keepdims=True))
        if c > 0:
            acc = alpha_prev * acc + _dot(values_t(c - 1), p_prev)
        alpha_prev = jnp.exp2(m - m_new)
        p_prev = jnp.exp2(s - m_new).astype(BF16)
        m = m_new
    acc = alpha_prev * acc + _dot(values_t(n_lat), p_prev)
    o_ref[...] = _attn_finish(acc)


def _attn_lat(q, k, vt, kc, vct):
    qb0 = T_CTX // TQ
    qpb = N_LAT // TQ
    kb0 = T_CTX // N_LAT
    return pl.pallas_call(
        _attn_lat_kernel,
        grid=(N_LAT_B, H, qpb),
        in_specs=[pl.BlockSpec((TQ, HEAD_PAD), lambda b, h, qi: (qb0 + b * qpb + qi, h)),
                  pl.BlockSpec((N_LAT, HEAD_PAD), lambda b, h, qi: (kb0 + b, h)),
                  pl.BlockSpec((V_AUG, N_LAT), lambda b, h, qi: (h, kb0 + b)),
                  pl.BlockSpec((PAST, HEAD_PAD), lambda b, h, qi: (b, h)),
                  pl.BlockSpec((V_AUG, PAST), lambda b, h, qi: (h, b))],
        out_specs=pl.BlockSpec((TQ, V_DIM), lambda b, h, qi: (b * qpb + qi, h)),
        out_shape=jax.ShapeDtypeStruct((T_LAT, H * V_DIM), BF16),
        compiler_params=_cp(("arbitrary", "arbitrary", "arbitrary")),
        name="attn_lat",
    )(q, k, vt, kc, vct)


def _route(logits):
    lane_i = lax.broadcasted_iota(I32, logits.shape, 1)
    lane = lane_i.astype(F32)
    neg = jnp.float32(-jnp.inf)
    far = jnp.float32(LANES)
    gl = jnp.where(lane_i < N_GRP, logits, neg)
    gmax = jnp.max(gl, -1, keepdims=True)
    gidx = jnp.min(jnp.where(gl == gmax, lane, far), -1, keepdims=True)
    p_top = 1.0 / jnp.sum(jnp.exp(gl - gmax), -1, keepdims=True)
    grp_of_lane = ((lane_i - N_GRP) >> 3).astype(F32)
    in_grp = (lane_i >= N_GRP) & (lane_i < N_GRP + N_EXP) & (grp_of_lane == gidx)
    el = jnp.where(in_grp, logits, neg)
    m1 = jnp.max(el, -1, keepdims=True)
    i1 = jnp.min(jnp.where(el == m1, lane, far), -1, keepdims=True)
    el2 = jnp.where(lane == i1, neg, el)
    m2 = jnp.max(el2, -1, keepdims=True)
    i2 = jnp.min(jnp.where(el2 == m2, lane, far), -1, keepdims=True)
    e2 = jnp.exp(m2 - m1)
    w1 = p_top / (1.0 + e2)
    w2 = p_top * e2 / (1.0 + e2)
    return i1 - N_GRP, i2 - N_GRP, w1, w2


def _pair(a, b, dtype):
    two = lax.broadcasted_iota(I32, (a.shape[0], TOP_K), 1)
    return jnp.where(two == 0, a, b).astype(dtype)


def _post_kernel(ac_ref, al_ref, gm_ref, xc_ref, xl_ref, mods_ref, wout_ref, g1_ref, b1_ref, wrt_ref,
                 brt_ref, x1_ref, h2p_ref, ridx_ref, rwt_ref):
    i = pl.program_id(0)
    is_ctx = i < T_CTX // TP
    m = mods_ref[0]
    gate1, shift2, scale2 = m[2:3], m[3:4], m[4:5]

    def mix_of(rows):
        attn = jnp.where(is_ctx, ac_ref[rows, :], al_ref[rows, :])
        return _dot(jnp.concatenate([attn, gm_ref[rows, :]], axis=1), wout_ref[...])

    bounds = np.cumsum((0,) + POST_SUBS)
    subs = [slice(int(a), int(b)) for a, b in zip(bounds[:-1], bounds[1:])]
    mix_next = mix_of(subs[0])
    for s, rows in enumerate(subs):
        mix = mix_next
        if s + 1 < len(subs):
            mix_next = mix_of(subs[s + 1])
        x = jnp.where(is_ctx, xc_ref[rows, :], xl_ref[rows, :])
        x1 = _layer_norm(ALPHA * x + gate1 * mix, g1_ref[...], b1_ref[...])
        x1_ref[rows, :] = x1
        h2 = x1 * (1.0 + scale2) + shift2
        _store_rows_as_tiles(h2p_ref, rows.start, _pack_pair(h2[:, :D // 2], h2[:, D // 2:]))
        e1, e2, w1, w2 = _route(_dot(h2.astype(BF16), wrt_ref[...]) + brt_ref[...])
        ridx_ref[rows, :] = _pair(e1, e2, I32)
        rwt_ref[rows, :] = _pair(w1, w2, F32)


def _post(attn_c, attn_l, gm, xc, xl, mods, wout, g1, b1, wrt, brt):
    npt = T_CTX // TP

    def tok(width):
        return pl.BlockSpec((TP, width), lambda i: (i, 0))

    def ctx(width):
        return pl.BlockSpec((TP, width), lambda i: (jnp.minimum(i, npt - 1), 0))

    def lat(width):
        return pl.BlockSpec((TP, width), lambda i: (jnp.maximum(i - npt, 0), 0))

    return pl.pallas_call(
        _post_kernel,
        grid=(T // TP,),
        in_specs=[ctx(H * V_DIM), lat(H * V_DIM), tok(GW), ctx(D), lat(D),
                  pl.BlockSpec((1, 6, D), lambda i: (_mods_index(i, TP), 0, 0)),
                  _const_spec((D, D)), _const_spec((1, D)), _const_spec((1, D)),
                  _const_spec((D, LANES)), _const_spec((1, LANES))],
        out_specs=[tok(D), pl.BlockSpec((TP * RT, LANES), lambda i: (i, 0)), tok(TOP_K), tok(TOP_K)],
        out_shape=[jax.ShapeDtypeStruct((T, D), F32),
                   jax.ShapeDtypeStruct((T * RT, LANES), U32),
                   jax.ShapeDtypeStruct((T, TOP_K), I32),
                   jax.ShapeDtypeStruct((T, TOP_K), F32)],
        compiler_params=_cp(("arbitrary",)),
        name="post",
    )(attn_c, attn_l, gm, xc, xl, mods, wout, g1, b1, wrt, brt)


def _as_lane_row(col):
    return jnp.broadcast_to(col, (col.shape[0], LANES)).T[0:1, :]


def _rank_kernel(ridx_ref, tri_ref, e_ref, rank_ref, hist_ref):
    lane = lax.broadcasted_iota(I32, (SUB, LANES), 1).astype(F32)
    seen = jnp.zeros((1, LANES), F32)
    for s in range(TP // SUB):
        rows = slice(s * SUB, (s + 1) * SUB)
        e = ridx_ref[rows, :].astype(F32)
        ohs = [(lane == e[:, k:k + 1]).astype(F32) for k in range(TOP_K)]
        both = ohs[0] + ohs[1]
        before = _dot(tri_ref[...], both.astype(BF16)) + seen
        for k in range(TOP_K):
            rank_k = jnp.sum(before * ohs[k], -1, keepdims=True)
            e_ref[0, k:k + 1, rows] = _as_lane_row(e[:, k:k + 1]).astype(I32)
            rank_ref[0, k:k + 1, rows] = _as_lane_row(rank_k).astype(I32)
        seen = seen + jnp.sum(both, 0, keepdims=True)
    hist_ref[0] = jnp.broadcast_to(seen, (8, LANES))


def _rank(ridx):
    dense = pl.BlockSpec((1, TOP_K, TP), lambda i: (i, 0, 0))
    return pl.pallas_call(
        _rank_kernel,
        grid=(T // TP,),
        in_specs=[pl.BlockSpec((TP, TOP_K), lambda i: (i, 0)), _const_spec((SUB, SUB))],
        out_specs=[dense, dense, pl.BlockSpec((1, 8, LANES), lambda i: (i, 0, 0))],
        out_shape=[jax.ShapeDtypeStruct((T // TP, TOP_K, TP), I32),
                   jax.ShapeDtypeStruct((T // TP, TOP_K, TP), I32),
                   jax.ShapeDtypeStruct((T // TP, 8, LANES), F32)],
        compiler_params=_cp(("arbitrary",)),
        name="rank",
    )(ridx, jnp.asarray(np.tril(np.ones((SUB, SUB), np.float32), -1), BF16))


def _row_copy(src, src_row, dst, dst_row, sem):
    def tile(ref, row):
        return ref.at[pl.ds(pl.multiple_of(row * RT, RT), RT)]

    return pltpu.make_async_copy(tile(src, src_row), tile(dst, dst_row), sem)


def _dispatch_kernel(zb_ref, zv_ref, d0_ref, d1_ref, h2p_ref, xs_ref, zbuf, sem):
    i = pl.program_id(0)
    dest_refs = (d0_ref, d1_ref)

    @pl.when(i == 0)
    def _():
        zbuf[...] = jnp.zeros_like(zbuf)

        def zero_block(j):
            start = pl.multiple_of(zb_ref[j] * (BM * RT), BM * RT)
            return pltpu.make_async_copy(zbuf, xs_ref.at[pl.ds(start, BM * RT)], sem)

        def start(j, carry):
            @pl.when(zv_ref[j] != 0)
            def _():
                zero_block(j).start()
            return carry

        def wait(j, carry):
            @pl.when(zv_ref[j] != 0)
            def _():
                zero_block(j).wait()
            return carry

        lax.fori_loop(0, N_ZERO, start, 0)
        lax.fori_loop(0, N_ZERO, wait, 0)

    def issue(r, carry):
        for k in range(TOP_K):
            _row_copy(h2p_ref, r, xs_ref, dest_refs[k][0, 0, r], sem).start(priority=k)
        return carry

    lax.fori_loop(0, TD, issue, 0, unroll=16)
    for _ in range(TOP_K):
        pltpu.make_async_copy(h2p_ref, xs_ref.at[pl.ds(0, TD * RT)], sem).wait()


def _regroup_steps(dest, rows):
    return [dest[:, k, :].reshape(T // rows, 1, rows) for k in range(TOP_K)]


def _dispatch(zero_blk, zero_valid, dest, h2p):
    grid_spec = pltpu.PrefetchScalarGridSpec(
        num_scalar_prefetch=2,
        grid=(T // TD,),
        in_specs=[pl.BlockSpec((1, 1, TD), lambda i, zb, zv: (i, 0, 0), memory_space=pltpu.SMEM),
                  pl.BlockSpec((1, 1, TD), lambda i, zb, zv: (i, 0, 0), memory_space=pltpu.SMEM),
                  pl.BlockSpec((TD * RT, LANES), lambda i, zb, zv: (i, 0))],
        out_specs=pl.BlockSpec(memory_space=pl.ANY),
        scratch_shapes=[pltpu.VMEM((BM * RT, LANES), U32), pltpu.SemaphoreType.DMA(())],
    )
    return pl.pallas_call(
        _dispatch_kernel,
        grid_spec=grid_spec,
        out_shape=jax.ShapeDtypeStruct((N_BLOCKS * BM * RT, LANES), U32),
        compiler_params=_cp(("arbitrary",)),
        name="dispatch",
    )(zero_blk, zero_valid, *_regroup_steps(dest, TD), h2p)


def _experts_kernel(be_ref, bv_ref, sw_ref, es_ref, nx_ref, x_ref, w1_hbm, w3_hbm, w2_hbm, y_ref,
                    wf1, wf3, wf2, w1b, w3b, w2b, sems):
    i = pl.program_id(0)
    e = be_ref[i]
    nv = bv_ref[i]

    def weight_copies(expert, s):
        return (pltpu.make_async_copy(w1_hbm.at[expert], wf1.at[s], sems.at[s]),
                pltpu.make_async_copy(w3_hbm.at[expert], wf3.at[s], sems.at[s]),
                pltpu.make_async_copy(w2_hbm.at[expert], wf2.at[s], sems.at[s]))

    @pl.when(sw_ref[i] != 0)
    def _():
        s = es_ref[i]

        @pl.when(i == 0)
        def _():
            for c in weight_copies(e, s):
                c.start()

        for c in weight_copies(e, s):
            c.wait()
        nxt = nx_ref[i]

        @pl.when(nxt >= 0)
        def _():
            for c in weight_copies(nxt, 1 - s):
                c.start(priority=1)

        w1b[...] = wf1[s].astype(BF16)
        w3b[...] = wf3[s].astype(BF16)
        w2b[...] = wf2[s].astype(BF16)

    @pl.when(nv > 0)
    def _():
        lo, hi = _unpack_pair(_load_tiles_as_rows(x_ref, 0, BM))
        xb = jnp.concatenate([lo.astype(BF16), hi.astype(BF16)], axis=1)
        a = _dot(xb, w1b[...])
        b = _dot(xb, w3b[...])
        hid = (jax.nn.silu(a) * b).astype(BF16)
        y = _dot(hid, w2b[...])
        _store_rows_as_tiles(y_ref, 0, _pack_pair(y[:, :D // 2], y[:, D // 2:]))

    @pl.when(nv == 0)
    def _():
        y_ref[...] = jnp.zeros_like(y_ref)


def _experts(blk_expert, blk_valid, switch, wslot, next_expert, xs, w1, w3, w2):
    hbm = pl.BlockSpec(memory_space=pl.ANY)
    grid_spec = pltpu.PrefetchScalarGridSpec(
        num_scalar_prefetch=5,
        grid=(N_BLOCKS,),
        in_specs=[pl.BlockSpec((BM * RT, LANES), lambda i, *_: (i, 0)), hbm, hbm, hbm],
        out_specs=pl.BlockSpec((BM * RT, LANES), lambda i, *_: (i, 0)),
        scratch_shapes=[pltpu.VMEM((2, D, E_HID), F32), pltpu.VMEM((2, D, E_HID), F32),
                        pltpu.VMEM((2, E_HID, D), F32),
                        pltpu.VMEM((D, E_HID), BF16), pltpu.VMEM((D, E_HID), BF16),
                        pltpu.VMEM((E_HID, D), BF16), pltpu.SemaphoreType.DMA((2,))],
    )
    return pl.pallas_call(
        _experts_kernel,
        grid_spec=grid_spec,
        out_shape=jax.ShapeDtypeStruct((N_BLOCKS * BM * RT, LANES), U32),
        compiler_params=_cp(("arbitrary",)),
        name="experts",
    )(blk_expert, blk_valid, switch, wslot, next_expert, xs, w1, w3, w2)


def _final_kernel(dcur0_ref, dcur1_ref, dnext0_ref, dnext1_ref, ys_ref, x1_ref, rwt_ref, mods_ref,
                  g2_ref, b2_ref, oc_ref, ol_ref, buf, sems):
    i = pl.program_id(0)
    dcur_ref, dnext_ref = (dcur0_ref, dcur1_ref), (dnext0_ref, dnext1_ref)
    n = pl.num_programs(0)
    slot = i % 2

    def first_row(s, k):
        return (s * TOP_K + k) * TF

    def issue(d_ref, s):
        def body(r, carry):
            for k in range(TOP_K):
                _row_copy(ys_ref, d_ref[k][0, 0, r], buf, first_row(s, k) + r, sems.at[s]).start()
            return carry

        lax.fori_loop(0, TF, body, 0, unroll=16)

    @pl.when(i == 0)
    def _():
        issue(dcur_ref, 0)

    @pl.when(i + 1 < n)
    def _():
        issue(dnext_ref, 1 - slot)

    for k in range(TOP_K):
        pltpu.make_async_copy(ys_ref.at[pl.ds(0, TF * RT)], buf.at[pl.ds(0, TF * RT)], sems.at[slot]).wait()

    w = rwt_ref[...]
    lo0, hi0 = _unpack_pair(_load_tiles_as_rows(buf, first_row(slot, 0), TF))
    lo1, hi1 = _unpack_pair(_load_tiles_as_rows(buf, first_row(slot, 1), TF))
    w0, w1 = w[:, 0:1], w[:, 1:2]
    moe = jnp.concatenate([lo0 * w0 + lo1 * w1, hi0 * w0 + hi1 * w1], axis=1)
    gate2 = mods_ref[0][5:6]
    out = _layer_norm(ALPHA * x1_ref[...] + gate2 * moe, g2_ref[...], b2_ref[...])

    @pl.when(i < T_CTX // TF)
    def _():
        oc_ref[...] = out

    @pl.when(i >= T_CTX // TF)
    def _():
        ol_ref[...] = out


def _final(dest_k, ys, x1, rwt, mods, g2, b2):
    n = T // TF
    npt = T_CTX // TF
    cur = pl.BlockSpec((1, 1, TF), lambda i: (i, 0, 0), memory_space=pltpu.SMEM)
    nxt = pl.BlockSpec((1, 1, TF), lambda i: (jnp.minimum(i + 1, n - 1), 0, 0), memory_space=pltpu.SMEM)
    return pl.pallas_call(
        _final_kernel,
        grid=(n,),
        in_specs=[cur, cur, nxt, nxt,
                  pl.BlockSpec(memory_space=pl.ANY),
                  pl.BlockSpec((TF, D), lambda i: (i, 0)),
                  pl.BlockSpec((TF, TOP_K), lambda i: (i, 0)),
                  pl.BlockSpec((1, 6, D), lambda i: (_mods_index(i, TF), 0, 0)),
                  _const_spec((1, D)), _const_spec((1, D))],
        out_specs=[pl.BlockSpec((TF, D), lambda i: (jnp.minimum(i, npt - 1), 0)),
                   pl.BlockSpec((TF, D), lambda i: (jnp.maximum(i - npt, 0), 0))],
        out_shape=[jax.ShapeDtypeStruct((T_CTX, D), F32), jax.ShapeDtypeStruct((T_LAT, D), F32)],
        scratch_shapes=[pltpu.VMEM((2 * TOP_K * TF * RT, LANES), U32), pltpu.SemaphoreType.DMA((2,))],
        compiler_params=_cp(("arbitrary",)),
        name="final",
    )(*dest_k, *dest_k, ys, x1, rwt, mods, g2, b2)


def _rope_tables():
    n = N_LAT
    rows = n // GRID_W
    f32 = np.float32
    row = np.repeat(np.arange(rows, dtype=f32), GRID_W)
    col = np.tile(np.arange(GRID_W, dtype=f32), rows)
    half = QK_ROPE // 2
    inv = (f32(ROPE_THETA) ** (-np.arange(0, half, 2, dtype=f32) / f32(half))).astype(f32)
    ang_r, ang_c = (row[:, None] * inv).astype(f32), (col[:, None] * inv).astype(f32)
    cr, sr = np.cos(ang_r).astype(f32), np.sin(ang_r).astype(f32)
    cc, sc = np.cos(ang_c).astype(f32), np.sin(ang_c).astype(f32)
    z16 = np.zeros_like(cr)
    pad = lambda v: np.full((n, LANES - QK_ROPE), v, f32)
    cos = np.concatenate([cr, cr, cc, cc, pad(1.0)], axis=1)
    s_plus = np.concatenate([z16, sr, z16, sc, pad(0.0)], axis=1)
    s_minus = np.concatenate([-sr, z16, -sc, z16, pad(0.0)], axis=1)
    ident = lambda v: np.full((TM, LANES), v, f32)
    return (jnp.asarray(np.concatenate([cos, ident(1.0)], 0)),
            jnp.asarray(np.concatenate([s_plus, ident(0.0)], 0)),
            jnp.asarray(np.concatenate([s_minus, ident(0.0)], 0)))


def _block_plan(eid_k, rank, hist):
    counts = jnp.sum(hist, axis=0)
    nblk = (counts + BM - 1) // BM
    bends = jnp.cumsum(nblk)
    bstarts = bends - nblk
    base = bstarts[None, :] * BM + jnp.cumsum(hist, axis=0) - hist
    eid = jnp.arange(N_EXP, dtype=I32)
    hit = eid_k[:, :, None, :] == eid[None, None, :, None]
    first = jnp.sum(jnp.where(hit, base[:, None, :, None], 0), axis=2)
    dest = (first + rank).astype(I32)
    blk = jnp.arange(N_BLOCKS, dtype=I32)
    blk_expert = jnp.minimum(jnp.sum((blk[:, None] >= bends[None, :]).astype(I32), axis=1), N_EXP - 1)
    of_blk = blk_expert[:, None] == eid[None, :]

    def per_block(v):
        return jnp.sum(jnp.where(of_blk, v[None, :], 0), axis=1)

    left = per_block(counts) - (blk - per_block(bstarts)) * BM
    blk_valid = jnp.where(blk < bends[-1], jnp.clip(left, 0, BM), 0).astype(I32)
    unused = bends[-1] + jnp.arange(N_EXP, dtype=I32)
    zero_blk = jnp.concatenate([jnp.maximum(bends - 1, 0), jnp.minimum(unused, N_BLOCKS - 1)])
    zero_valid = jnp.concatenate([counts % BM != 0, unused < N_BLOCKS])
    prev_expert = jnp.concatenate([jnp.full((1,), -1, I32), blk_expert[:-1]])
    switch = (blk < bends[-1]) & (blk_expert != prev_expert)
    wslot = (jnp.cumsum(switch.astype(I32)) - 1) % 2
    later = (eid[None, :] > eid[:, None]) & (nblk[None, :] > 0)
    next_of = jnp.min(jnp.where(later, eid[None, :], N_EXP), axis=1)
    next_expert = per_block(jnp.where(next_of < N_EXP, next_of, -1))
    return (dest, blk_expert.astype(I32), blk_valid, zero_blk.astype(I32),
            zero_valid.astype(I32), switch.astype(I32), wslot.astype(I32), next_expert.astype(I32))


def kernel(x_prompt, x_sample, cache_ckv, cache_krope, c, c_ctx, w_ada, b_ada, w_in, q_norm_g, w_uq,
           kv_norm_g, w_ukv, gmlp_ln_g, gmlp_ln_b, w_spatial, b_spatial, w_out, ln1_g, ln1_b, w_group,
           b_group, w_router, b_router, w1, w3, w2, ln2_g, ln2_b):
    l = 0
    xc = x_prompt.reshape(T_CTX, D)
    xl = x_sample.reshape(T_LAT, D)

    cond16 = jnp.concatenate([c, c_ctx[None, :], jnp.zeros((16 - N_LAT_B - 1, D), F32)], axis=0)
    mods = _adaln(cond16, w_ada[l], b_ada[l][None, :]).reshape(16, 6, D)

    wi = w_in[l]
    o_kr = QL + KVL
    o_u = o_kr + QK_ROPE
    win_a = wi[:, :o_kr].astype(BF16)
    win_g = wi[:, o_u:].astype(BF16)
    win_r = jnp.pad(wi[:, o_kr:o_u], ((0, 0), (0, LANES - QK_ROPE))).astype(BF16)
    wuq = jnp.pad(w_uq[l].reshape(QL, H, QK_NOPE + QK_ROPE),
                  ((0, 0), (0, 0), (0, HEAD_PAD - QK_NOPE - QK_ROPE))).reshape(QL, H * HEAD_PAD).astype(BF16)
    wkv = w_ukv[l].reshape(KVL, H, QK_NOPE + V_DIM)
    wuk = wkv[:, :, :QK_NOPE].reshape(KVL, H * QK_NOPE).astype(BF16)
    wuvt = wkv[:, :, QK_NOPE:].reshape(KVL, H * V_DIM).T.astype(BF16)
    ws = w_spatial[l].astype(BF16)
    bsb = jnp.broadcast_to(b_spatial[l][:, :, None], (G_GROUPS, CHUNK, G_DIM))
    wrt = jnp.concatenate([w_group[l], w_router[l], jnp.zeros((D, LANES - N_GRP - N_EXP), F32)],
                          axis=1).astype(BF16)
    brt = jnp.concatenate([b_group[l], b_router[l], jnp.zeros((LANES - N_GRP - N_EXP,), F32)])[None, :]
    cos_t, sp_t, sm_t = _rope_tables()
    row = lambda v: v[l][None, :]

    q, k, vt, gm, ckv_n, kr = _proj(xc, xl, mods, win_a, win_g, win_r, row(q_norm_g), wuq, row(kv_norm_g), wuk, wuvt,
                                   row(gmlp_ln_g), row(gmlp_ln_b), ws, bsb, cos_t, sp_t, sm_t)
    kc, vct = _kvexp(cache_ckv[:, l].reshape(N_LAT_B * PAST, KVL),
                     cache_krope[:, l].reshape(N_LAT_B * PAST, QK_ROPE), wuk, wuvt)
    attn_c = _attn_ctx(q, k, vt)
    attn_l = _attn_lat(q, k, vt, kc, vct)

    x1, h2p, ridx, rwt = _post(attn_c, attn_l, gm, xc, xl, mods, w_out[l].astype(BF16), row(ln1_g), row(ln1_b),
                               wrt, brt)
    eid_k, rank, hist = _rank(ridx)
    dest, blk_expert, blk_valid, zero_blk, zero_valid, switch, wslot, next_expert = _block_plan(
        eid_k, rank, hist[:, 0, :N_EXP].astype(I32))
    xs = _dispatch(zero_blk, zero_valid, dest, h2p)
    ys = _experts(blk_expert, blk_valid, switch, wslot, next_expert, xs, w1[l], w3[l], w2[l])
    y_c, y_l = _final(_regroup_steps(dest, TF), ys, x1, rwt, mods, row(ln2_g), row(ln2_b))

    return (y_c.reshape(N_CTX_B, N_CTX, D), y_l.reshape(N_LAT_B, N_LAT, D),
            ckv_n.reshape(N_CTX_B, 1, N_CTX, KVL), kr.reshape(N_CTX_B, 1, N_CTX, QK_ROPE))
```

```python
import functools

import jax
import jax.numpy as jnp
import numpy as np
from jax import lax
from jax.experimental import pallas as pl
from jax.experimental.pallas import tpu as pltpu

F32 = jnp.float32
BF16 = jnp.bfloat16
U32 = jnp.uint32
I32 = jnp.int32

D = 2048
N_CTX_B, N_CTX = 32, 256
N_LAT_B, N_LAT = 4, 4096
PAST = 512
GRID_W = 64
H = 8
QK_NOPE, QK_ROPE, V_DIM = 128, 64, 128
QL, KVL = 512, 512
GW = 1024
G_GROUPS, G_DIM, CHUNK = 8, 128, 128
N_GRP, E_PER_GRP, N_EXP, TOP_K = 4, 8, 32, 2
E_HID = 512
ROPE_THETA = 10000.0
EPS = 1e-6
ALPHA = 2.0 ** 0.25
SM_SCALE = (QK_NOPE + QK_ROPE) ** -0.5
Q_SCALE = SM_SCALE * float(np.log2(np.e))

T_CTX = N_CTX_B * N_CTX
T_LAT = N_LAT_B * N_LAT
T = T_CTX + T_LAT
N_ASSIGN = T * TOP_K

LANES = 128
HEAD_PAD = 256
V_AUG = V_DIM + 16
TM = 256
TP = 512
POST_SUBS = (256, 256)
SUB = 256
TD = 2048
TF = 256
CTX_PER_STEP = 2
TQ = 4096
TK = 512
BM = 256
N_BLOCKS = N_ASSIGN // BM + N_EXP
N_ZERO = 2 * N_EXP
VMEM_LIMIT = 56 * 1024 * 1024

NP_TILES = T_CTX // TM
LAT_TILES_PER_B = N_LAT // TM


def _cp(sem):
    return pltpu.CompilerParams(dimension_semantics=sem, vmem_limit_bytes=VMEM_LIMIT)


def _const_spec(shape):
    nd = len(shape)
    return pl.BlockSpec(shape, lambda *a: (0,) * nd, pipeline_mode=pl.Buffered(1))


def _dot(a, b):
    return jnp.dot(a, b, preferred_element_type=F32)


def _dot_nt(a, b):
    return lax.dot_general(a, b, (((1,), (1,)), ((), ())), preferred_element_type=F32)


def _layer_norm(x, g, b):
    mu = jnp.mean(x, -1, keepdims=True)
    xc = x - mu
    var = jnp.mean(xc * xc, -1, keepdims=True)
    return xc * lax.rsqrt(var + EPS) * g + b


def _rms_norm(x, g):
    return x * lax.rsqrt(jnp.mean(x * x, -1, keepdims=True) + EPS) * g


def _gelu(x):
    return 0.5 * x * (1.0 + lax.erf(x * np.float32(np.sqrt(0.5))))


def _pack_pair(lo, hi):
    lo_b = lax.bitcast_convert_type(lo.astype(BF16).astype(F32), U32)
    hi_b = lax.bitcast_convert_type(hi.astype(BF16).astype(F32), U32)
    return hi_b | (lo_b >> 16)


def _unpack_pair(w):
    lo = lax.bitcast_convert_type(w << 16, F32)
    hi = lax.bitcast_convert_type(w & jnp.uint32(0xFFFF0000), F32)
    return lo, hi


RT = 8


def _store_rows_as_tiles(ref, row0, val):
    n = val.shape[0]
    for j in range(RT):
        ref[pl.ds(row0 * RT + j, n, stride=RT), :] = val[:, j * LANES:(j + 1) * LANES]


def _load_tiles_as_rows(ref, row0, n):
    return jnp.concatenate([ref[pl.ds(row0 * RT + j, n, stride=RT), :] for j in range(RT)], axis=1)


def _adaln_kernel(cond_ref, w_ref, b_ref, o_ref):
    s = jax.nn.silu(cond_ref[...])
    s_hi = s.astype(BF16)
    s_lo = (s - s_hi.astype(F32)).astype(BF16)
    w = w_ref[...]
    w_hi = w.astype(BF16)
    w_lo = (w - w_hi.astype(F32)).astype(BF16)
    o_ref[...] = _dot(s_hi, w_hi) + _dot(s_lo, w_hi) + _dot(s_hi, w_lo) + b_ref[...]


def _adaln(cond16, w_ada, b_ada):
    tn = 1024
    n = w_ada.shape[1]
    return pl.pallas_call(
        _adaln_kernel,
        grid=(n // tn,),
        in_specs=[pl.BlockSpec((16, D), lambda j: (0, 0)),
                  pl.BlockSpec((D, tn), lambda j: (0, j)),
                  pl.BlockSpec((1, tn), lambda j: (0, j))],
        out_specs=pl.BlockSpec((16, tn), lambda j: (0, j)),
        out_shape=jax.ShapeDtypeStruct((16, n), F32),
        compiler_params=_cp(("arbitrary",)),
        name="adaln",
    )(cond16, w_ada, b_ada)


def _expand_kv(ckv_n, kr128, wuk_ref, wuvt_ref, k_ref, vt_ref):
    cb = ckv_n.astype(BF16)
    kn = _dot(cb, wuk_ref[...])
    vt = _dot_nt(wuvt_ref[...], cb).astype(BF16)
    ones = jnp.ones((V_AUG - V_DIM, vt.shape[1]), BF16)
    for h in range(H):
        vt_ref[h * V_AUG:h * V_AUG + V_DIM, :] = vt[h * V_DIM:(h + 1) * V_DIM]
        vt_ref[h * V_AUG + V_DIM:(h + 1) * V_AUG, :] = ones
    krb = kr128.astype(BF16)
    for h in range(H):
        k_ref[:, h * HEAD_PAD:h * HEAD_PAD + QK_NOPE] = kn[:, h * QK_NOPE:(h + 1) * QK_NOPE].astype(BF16)
        k_ref[:, h * HEAD_PAD + QK_NOPE:(h + 1) * HEAD_PAD] = krb


def _rope128(x, cos, s_plus, s_minus):
    return (x * cos + pltpu.roll(x, 16, 1) * s_plus + pltpu.roll(x, LANES - 16, 1) * s_minus)


def _proj_kernel(xc_ref, xl_ref, mods_ref, win_a_ref, win_g_ref, win_r_ref, qg_ref, wuq_ref, kvg_ref,
                 wuk_ref, wuv_ref,
                 lng_ref, lnb_ref, ws_ref, bsb_ref, cos_ref, sp_ref, sm_ref,
                 q_ref, k_ref, v_ref, gm_ref, ckv_ref, kr_ref):
    i = pl.program_id(0)
    x = jnp.where(i < NP_TILES, xc_ref[...], xl_ref[...])
    m = mods_ref[0]
    shift1, scale1 = m[0:1], m[1:2]
    hmod = (x * (1.0 + scale1) + shift1).astype(BF16)
    pg = _dot(hmod, win_g_ref[...])
    u, v = pg[:, :GW], pg[:, GW:]
    gu = _gelu(u)
    vn = _layer_norm(_gelu(v), lng_ref[...], lnb_ref[...]).astype(BF16)
    pa = _dot(hmod, win_a_ref[...])
    kr = _dot(hmod, win_r_ref[...])
    cq, ckv = pa[:, :QL], pa[:, QL:]

    cos, s_plus, s_minus = cos_ref[...], sp_ref[...], sm_ref[...]

    cq_n = _rms_norm(cq, qg_ref[...]).astype(BF16)
    q = _dot(cq_n, wuq_ref[...]) * Q_SCALE
    for h in range(H):
        q_ref[:, h * HEAD_PAD:h * HEAD_PAD + QK_NOPE] = q[:, h * HEAD_PAD:h * HEAD_PAD + QK_NOPE].astype(BF16)
        q_ref[:, h * HEAD_PAD + QK_NOPE:(h + 1) * HEAD_PAD] = _rope128(
            q[:, h * HEAD_PAD + QK_NOPE:(h + 1) * HEAD_PAD], cos, s_plus, s_minus).astype(BF16)

    ckv_n = _rms_norm(ckv, kvg_ref[...])
    _expand_kv(ckv_n, _rope128(kr, cos, s_plus, s_minus), wuk_ref, wuv_ref, k_ref, v_ref)

    for c in range(TM // CHUNK):
        rows = slice(c * CHUNK, (c + 1) * CHUNK)
        for g in range(G_GROUPS):
            cols = slice(g * G_DIM, (g + 1) * G_DIM)
            s = _dot(ws_ref[g], vn[rows, cols]) + bsb_ref[g]
            gm_ref[rows, cols] = (gu[rows, cols] * s).astype(BF16)

    @pl.when(i < NP_TILES)
    def _():
        ckv_ref[...] = ckv_n
        kr_ref[...] = kr[:, :QK_ROPE]


def _mods_index(i, tile=TM):
    return jnp.where(i < T_CTX // tile, N_LAT_B, (i - T_CTX // tile) // (N_LAT // tile))


def _proj(xc, xl, mods, win_a, win_g, win_r, qg, wuq, kvg, wuk, wuv, lng, lnb, ws, bsb, cos_t, sp_t, sm_t):
    def tok(width):
        return pl.BlockSpec((TM, width), lambda i: (i, 0))

    def pos_block(i):
        return jnp.where(i < NP_TILES, LAT_TILES_PER_B, (i - NP_TILES) % LAT_TILES_PER_B)

    rope_spec = pl.BlockSpec((TM, LANES), lambda i: (pos_block(i), 0))
    return pl.pallas_call(
        _proj_kernel,
        grid=(T // TM,),
        in_specs=[
            pl.BlockSpec((TM, D), lambda i: (jnp.minimum(i, NP_TILES - 1), 0)),
            pl.BlockSpec((TM, D), lambda i: (jnp.maximum(i - NP_TILES, 0), 0)),
            pl.BlockSpec((1, 6, D), lambda i: (_mods_index(i), 0, 0)),
            _const_spec((D, QL + KVL)), _const_spec((D, 2 * GW)), _const_spec((D, LANES)),
            _const_spec((1, QL)), _const_spec((QL, H * HEAD_PAD)),
            _const_spec((1, KVL)), _const_spec((KVL, H * QK_NOPE)), _const_spec((H * V_DIM, KVL)),
            _const_spec((1, GW)), _const_spec((1, GW)),
            _const_spec((G_GROUPS, CHUNK, CHUNK)), _const_spec((G_GROUPS, CHUNK, G_DIM)),
            rope_spec, rope_spec, rope_spec,
        ],
        out_specs=[tok(H * HEAD_PAD), tok(H * HEAD_PAD),
                   pl.BlockSpec((H * V_AUG, TM), lambda i: (0, i)), tok(GW),
                   pl.BlockSpec((TM, KVL), lambda i: (jnp.minimum(i, NP_TILES - 1), 0)),
                   pl.BlockSpec((TM, QK_ROPE), lambda i: (jnp.minimum(i, NP_TILES - 1), 0))],
        out_shape=[
            jax.ShapeDtypeStruct((T, H * HEAD_PAD), BF16),
            jax.ShapeDtypeStruct((T, H * HEAD_PAD), BF16),
            jax.ShapeDtypeStruct((H * V_AUG, T), BF16),
            jax.ShapeDtypeStruct((T, GW), BF16),
            jax.ShapeDtypeStruct((T_CTX, KVL), F32),
            jax.ShapeDtypeStruct((T_CTX, QK_ROPE), F32),
        ],
        compiler_params=_cp(("arbitrary",)),
        name="proj",
    )(xc, xl, mods, win_a, win_g, win_r, qg, wuq, kvg, wuk, wuv, lng, lnb, ws, bsb, cos_t, sp_t, sm_t)


def _kvexp_kernel(ckv_ref, kr_ref, wuk_ref, wuv_ref, k_ref, v_ref):
    kr = kr_ref[...]
    kr128 = jnp.concatenate([kr, jnp.zeros_like(kr)], axis=1)
    _expand_kv(ckv_ref[...], kr128, wuk_ref, wuv_ref, k_ref, v_ref)


def _kvexp(ckv, kr, wuk, wuv):
    rows = ckv.shape[0]
    return pl.pallas_call(
        _kvexp_kernel,
        grid=(rows // TM,),
        in_specs=[pl.BlockSpec((TM, KVL), lambda i: (i, 0)), pl.BlockSpec((TM, QK_ROPE), lambda i: (i, 0)),
                  _const_spec((KVL, H * QK_NOPE)), _const_spec((H * V_DIM, KVL))],
        out_specs=[pl.BlockSpec((TM, H * HEAD_PAD), lambda i: (i, 0)),
                   pl.BlockSpec((H * V_AUG, TM), lambda i: (0, i))],
        out_shape=[jax.ShapeDtypeStruct((rows, H * HEAD_PAD), BF16),
                   jax.ShapeDtypeStruct((H * V_AUG, rows), BF16)],
        compiler_params=_cp(("arbitrary",)),
        name="kvexp",
    )(ckv, kr, wuk, wuv)


def _probs(s, m):
    m_new = jnp.maximum(m, jnp.max(s, 0, keepdims=True))
    return m_new, jnp.exp2(m - m_new), jnp.exp2(s - m_new).astype(BF16)


def _attn_finish(acc):
    return (acc[:V_DIM] / acc[V_DIM:V_DIM + 1]).T.astype(BF16)


def _attn_ctx_kernel(q_ref, k_ref, vt_ref, o_ref):
    m0 = jnp.full((1, N_CTX), -jnp.inf, F32)
    for b in range(CTX_PER_STEP):
        rows = slice(b * N_CTX, (b + 1) * N_CTX)
        for h in range(H):
            qh = q_ref[rows, h * HEAD_PAD:(h + 1) * HEAD_PAD]
            kh = k_ref[rows, h * HEAD_PAD:(h + 1) * HEAD_PAD]
            _, _, p = _probs(_dot_nt(kh, qh), m0)
            acc = _dot(vt_ref[h * V_AUG:(h + 1) * V_AUG, rows], p)
            o_ref[rows, h * V_DIM:(h + 1) * V_DIM] = _attn_finish(acc)


def _attn_ctx(q, k, vt):
    n = CTX_PER_STEP * N_CTX
    return pl.pallas_call(
        _attn_ctx_kernel,
        grid=(N_CTX_B // CTX_PER_STEP,),
        in_specs=[pl.BlockSpec((n, H * HEAD_PAD), lambda b: (b, 0)),
                  pl.BlockSpec((n, H * HEAD_PAD), lambda b: (b, 0)),
                  pl.BlockSpec((H * V_AUG, n), lambda b: (0, b))],
        out_specs=pl.BlockSpec((n, H * V_DIM), lambda b: (b, 0)),
        out_shape=jax.ShapeDtypeStruct((T_CTX, H * V_DIM), BF16),
        compiler_params=_cp(("arbitrary",)),
        name="attn_ctx",
    )(q, k, vt)


def _attn_lat_kernel(q_ref, k_ref, vt_ref, kc_ref, vct_ref, o_ref):
    q = q_ref[...]
    n_lat = N_LAT // TK

    def keys(c):
        return k_ref[c * TK:(c + 1) * TK, :] if c < n_lat else kc_ref[...]

    def values_t(c):
        return vt_ref[:, c * TK:(c + 1) * TK] if c < n_lat else vct_ref[...]

    m = jnp.full((1, TQ), -jnp.inf, F32)
    acc = jnp.zeros((V_AUG, TQ), F32)
    s_next = _dot_nt(keys(0), q)
    m_next = jnp.maximum(m, jnp.max(s_next, 0, keepdims=True))
    p_prev = alpha_prev = None
    for c in range(n_lat + 1):
        s, m_new = s_next, m_next
        if c < n_lat:
            s_next = _dot_nt(keys(c + 1), q)
            m_next = jnp.maximum(m_new, jnp.max(s_next, 0, keepdims=True))
        if c > 0:
            acc = alpha_prev * acc + _dot(values_t(c - 1), p_prev)
        alpha_prev = jnp.exp2(m - m_new)
        p_prev = jnp.exp2(s - m_new).astype(BF16)
        m = m_new
    acc = alpha_prev * acc + _dot(values_t(n_lat), p_prev)
    o_ref[...] = _attn_finish(acc)


def _attn_lat(q, k, vt, kc, vct):
    qb0 = T_CTX // TQ
    qpb = N_LAT // TQ
    kb0 = T_CTX // N_LAT
    return pl.pallas_call(
        _attn_lat_kernel,
        grid=(N_LAT_B, H, qpb),
        in_specs=[pl.BlockSpec((TQ, HEAD_PAD), lambda b, h, qi: (qb0 + b * qpb + qi, h)),
                  pl.BlockSpec((N_LAT, HEAD_PAD), lambda b, h, qi: (kb0 + b, h)),
                  pl.BlockSpec((V_AUG, N_LAT), lambda b, h, qi: (h, kb0 + b)),
                  pl.BlockSpec((PAST, HEAD_PAD), lambda b, h, qi: (b, h)),
                  pl.BlockSpec((V_AUG, PAST), lambda b, h, qi: (h, b))],
        out_specs=pl.BlockSpec((TQ, V_DIM), lambda b, h, qi: (b * qpb + qi, h)),
        out_shape=jax.ShapeDtypeStruct((T_LAT, H * V_DIM), BF16),
        compiler_params=_cp(("arbitrary", "arbitrary", "arbitrary")),
        name="attn_lat",
    )(q, k, vt, kc, vct)


def _route(logits):
    lane_i = lax.broadcasted_iota(I32, logits.shape, 1)
    lane = lane_i.astype(F32)
    neg = jnp.float32(-jnp.inf)
    far = jnp.float32(LANES)
    gl = jnp.where(lane_i < N_GRP, logits, neg)
    gmax = jnp.max(gl, -1, keepdims=True)
    gidx = jnp.min(jnp.where(gl == gmax, lane, far), -1, keepdims=True)
    p_top = 1.0 / jnp.sum(jnp.exp(gl - gmax), -1, keepdims=True)
    grp_of_lane = ((lane_i - N_GRP) >> 3).astype(F32)
    in_grp = (lane_i >= N_GRP) & (lane_i < N_GRP + N_EXP) & (grp_of_lane == gidx)
    el = jnp.where(in_grp, logits, neg)
    m1 = jnp.max(el, -1, keepdims=True)
    i1 = jnp.min(jnp.where(el == m1, lane, far), -1, keepdims=True)
    el2 = jnp.where(lane == i1, neg, el)
    m2 = jnp.max(el2, -1, keepdims=True)
    i2 = jnp.min(jnp.where(el2 == m2, lane, far), -1, keepdims=True)
    e2 = jnp.exp(m2 - m1)
    w1 = p_top / (1.0 + e2)
    w2 = p_top * e2 / (1.0 + e2)
    return i1 - N_GRP, i2 - N_GRP, w1, w2


def _pair(a, b, dtype):
    two = lax.broadcasted_iota(I32, (a.shape[0], TOP_K), 1)
    return jnp.where(two == 0, a, b).astype(dtype)


def _post_kernel(ac_ref, al_ref, gm_ref, xc_ref, xl_ref, mods_ref, wout_ref, g1_ref, b1_ref, wrt_ref,
                 brt_ref, x1_ref, h2p_ref, ridx_ref, rwt_ref):
    i = pl.program_id(0)
    is_ctx = i < T_CTX // TP
    m = mods_ref[0]
    gate1, shift2, scale2 = m[2:3], m[3:4], m[4:5]

    def mix_of(rows):
        attn = jnp.where(is_ctx, ac_ref[rows, :], al_ref[rows, :])
        return _dot(jnp.concatenate([attn, gm_ref[rows, :]], axis=1), wout_ref[...])

    bounds = np.cumsum((0,) + POST_SUBS)
    subs = [slice(int(a), int(b)) for a, b in zip(bounds[:-1], bounds[1:])]
    mix_next = mix_of(subs[0])
    for s, rows in enumerate(subs):
        mix = mix_next
        if s + 1 < len(subs):
            mix_next = mix_of(subs[s + 1])
        x = jnp.where(is_ctx, xc_ref[rows, :], xl_ref[rows, :])
        x1 = _layer_norm(ALPHA * x + gate1 * mix, g1_ref[...], b1_ref[...])
        x1_ref[rows, :] = x1
        h2 = x1 * (1.0 + scale2) + shift2
        _store_rows_as_tiles(h2p_ref, rows.start, _pack_pair(h2[:, :D // 2], h2[:, D // 2:]))
        e1, e2, w1, w2 = _route(_dot(h2.astype(BF16), wrt_ref[...]) + brt_ref[...])
        ridx_ref[rows, :] = _pair(e1, e2, I32)
        rwt_ref[rows, :] = _pair(w1, w2, F32)


def _post(attn_c, attn_l, gm, xc, xl, mods, wout, g1, b1, wrt, brt):
    npt = T_CTX // TP

    def tok(width):
        return pl.BlockSpec((TP, width), lambda i: (i, 0))

    def ctx(width):
        return pl.BlockSpec((TP, width), lambda i: (jnp.minimum(i, npt - 1), 0))

    def lat(width):
        return pl.BlockSpec((TP, width), lambda i: (jnp.maximum(i - npt, 0), 0))

    return pl.pallas_call(
        _post_kernel,
        grid=(T // TP,),
        in_specs=[ctx(H * V_DIM), lat(H * V_DIM), tok(GW), ctx(D), lat(D),
                  pl.BlockSpec((1, 6, D), lambda i: (_mods_index(i, TP), 0, 0)),
                  _const_spec((D, D)), _const_spec((1, D)), _const_spec((1, D)),
                  _const_spec((D, LANES)), _const_spec((1, LANES))],
        out_specs=[tok(D), pl.BlockSpec((TP * RT, LANES), lambda i: (i, 0)), tok(TOP_K), tok(TOP_K)],
        out_shape=[jax.ShapeDtypeStruct((T, D), F32),
                   jax.ShapeDtypeStruct((T * RT, LANES), U32),
                   jax.ShapeDtypeStruct((T, TOP_K), I32),
                   jax.ShapeDtypeStruct((T, TOP_K), F32)],
        compiler_params=_cp(("arbitrary",)),
        name="post",
    )(attn_c, attn_l, gm, xc, xl, mods, wout, g1, b1, wrt, brt)


def _as_lane_row(col):
    return jnp.broadcast_to(col, (col.shape[0], LANES)).T[0:1, :]


def _rank_kernel(ridx_ref, tri_ref, e_ref, rank_ref, hist_ref):
    lane = lax.broadcasted_iota(I32, (SUB, LANES), 1).astype(F32)
    seen = jnp.zeros((1, LANES), F32)
    for s in range(TP // SUB):
        rows = slice(s * SUB, (s + 1) * SUB)
        e = ridx_ref[rows, :].astype(F32)
        ohs = [(lane == e[:, k:k + 1]).astype(F32) for k in range(TOP_K)]
        both = ohs[0] + ohs[1]
        before = _dot(tri_ref[...], both.astype(BF16)) + seen
        for k in range(TOP_K):
            rank_k = jnp.sum(before * ohs[k], -1, keepdims=True)
            e_ref[0, k:k + 1, rows] = _as_lane_row(e[:, k:k + 1]).astype(I32)
            rank_ref[0, k:k + 1, rows] = _as_lane_row(rank_k).astype(I32)
        seen = seen + jnp.sum(both, 0, keepdims=True)
    hist_ref[0] = jnp.broadcast_to(seen, (8, LANES))


def _rank(ridx):
    dense = pl.BlockSpec((1, TOP_K, TP), lambda i: (i, 0, 0))
    return pl.pallas_call(
        _rank_kernel,
        grid=(T // TP,),
        in_specs=[pl.BlockSpec((TP, TOP_K), lambda i: (i, 0)), _const_spec((SUB, SUB))],
        out_specs=[dense, dense, pl.BlockSpec((1, 8, LANES), lambda i: (i, 0, 0))],
        out_shape=[jax.ShapeDtypeStruct((T // TP, TOP_K, TP), I32),
                   jax.ShapeDtypeStruct((T // TP, TOP_K, TP), I32),
                   jax.ShapeDtypeStruct((T // TP, 8, LANES), F32)],
        compiler_params=_cp(("arbitrary",)),
        name="rank",
    )(ridx, jnp.asarray(np.tril(np.ones((SUB, SUB), np.float32), -1), BF16))


def _row_copy(src, src_row, dst, dst_row, sem):
    def tile(ref, row):
        return ref.at[pl.ds(pl.multiple_of(row * RT, RT), RT)]

    return pltpu.make_async_copy(tile(src, src_row), tile(dst, dst_row), sem)


def _dispatch_kernel(zb_ref, zv_ref, d0_ref, d1_ref, h2p_ref, xs_ref, zbuf, sem):
    i = pl.program_id(0)
    dest_refs = (d0_ref, d1_ref)

    @pl.when(i == 0)
    def _():
        zbuf[...] = jnp.zeros_like(zbuf)

        def zero_block(j):
            start = pl.multiple_of(zb_ref[j] * (BM * RT), BM * RT)
            return pltpu.make_async_copy(zbuf, xs_ref.at[pl.ds(start, BM * RT)], sem)

        def start(j, carry):
            @pl.when(zv_ref[j] != 0)
            def _():
                zero_block(j).start()
            return carry

        def wait(j, carry):
            @pl.when(zv_ref[j] != 0)
            def _():
                zero_block(j).wait()
            return carry

        lax.fori_loop(0, N_ZERO, start, 0)
        lax.fori_loop(0, N_ZERO, wait, 0)

    def issue(r, carry):
        for k in range(TOP_K):
            _row_copy(h2p_ref, r, xs_ref, dest_refs[k][0, 0, r], sem).start(priority=k)
        return carry

    lax.fori_loop(0, TD, issue, 0, unroll=16)
    for _ in range(TOP_K):
        pltpu.make_async_copy(h2p_ref, xs_ref.at[pl.ds(0, TD * RT)], sem).wait()


def _regroup_steps(dest, rows):
    return [dest[:, k, :].reshape(T // rows, 1, rows) for k in range(TOP_K)]


def _dispatch(zero_blk, zero_valid, dest, h2p):
    grid_spec = pltpu.PrefetchScalarGridSpec(
        num_scalar_prefetch=2,
        grid=(T // TD,),
        in_specs=[pl.BlockSpec((1, 1, TD), lambda i, zb, zv: (i, 0, 0), memory_space=pltpu.SMEM),
                  pl.BlockSpec((1, 1, TD), lambda i, zb, zv: (i, 0, 0), memory_space=pltpu.SMEM),
                  pl.BlockSpec((TD * RT, LANES), lambda i, zb, zv: (i, 0))],
        out_specs=pl.BlockSpec(memory_space=pl.ANY),
        scratch_shapes=[pltpu.VMEM((BM * RT, LANES), U32), pltpu.SemaphoreType.DMA(())],
    )
    return pl.pallas_call(
        _dispatch_kernel,
        grid_spec=grid_spec,
        out_shape=jax.ShapeDtypeStruct((N_BLOCKS * BM * RT, LANES), U32),
        compiler_params=_cp(("arbitrary",)),
        name="dispatch",
    )(zero_blk, zero_valid, *_regroup_steps(dest, TD), h2p)


def _experts_kernel(be_ref, bv_ref, sw_ref, es_ref, nx_ref, x_ref, w1_hbm, w3_hbm, w2_hbm, y_ref,
                    wf1, wf3, wf2, w1b, w3b, w2b, sems):
    i = pl.program_id(0)
    e = be_ref[i]
    nv = bv_ref[i]

    def weight_copies(expert, s):
        return (pltpu.make_async_copy(w1_hbm.at[expert], wf1.at[s], sems.at[s]),
                pltpu.make_async_copy(w3_hbm.at[expert], wf3.at[s], sems.at[s]),
                pltpu.make_async_copy(w2_hbm.at[expert], wf2.at[s], sems.at[s]))

    @pl.when(sw_ref[i] != 0)
    def _():
        s = es_ref[i]

        @pl.when(i == 0)
        def _():
            for c in weight_copies(e, s):
                c.start()

        for c in weight_copies(e, s):
            c.wait()
        nxt = nx_ref[i]

        @pl.when(nxt >= 0)
        def _():
            for c in weight_copies(nxt, 1 - s):
                c.start(priority=1)

        w1b[...] = wf1[s].astype(BF16)
        w3b[...] = wf3[s].astype(BF16)
        w2b[...] = wf2[s].astype(BF16)

    @pl.when(nv > 0)
    def _():
        lo, hi = _unpack_pair(_load_tiles_as_rows(x_ref, 0, BM))
        xb = jnp.concatenate([lo.astype(BF16), hi.astype(BF16)], axis=1)
        a = _dot(xb, w1b[...])
        b = _dot(xb, w3b[...])
        hid = (jax.nn.silu(a) * b).astype(BF16)
        y = _dot(hid, w2b[...])
        _store_rows_as_tiles(y_ref, 0, _pack_pair(y[:, :D // 2], y[:, D // 2:]))

    @pl.when(nv == 0)
    def _():
        y_ref[...] = jnp.zeros_like(y_ref)


def _experts(blk_expert, blk_valid, switch, wslot, next_expert, xs, w1, w3, w2):
    hbm = pl.BlockSpec(memory_space=pl.ANY)
    grid_spec = pltpu.PrefetchScalarGridSpec(
        num_scalar_prefetch=5,
        grid=(N_BLOCKS,),
        in_specs=[pl.BlockSpec((BM * RT, LANES), lambda i, *_: (i, 0)), hbm, hbm, hbm],
        out_specs=pl.BlockSpec((BM * RT, LANES), lambda i, *_: (i, 0)),
        scratch_shapes=[pltpu.VMEM((2, D, E_HID), F32), pltpu.VMEM((2, D, E_HID), F32),
                        pltpu.VMEM((2, E_HID, D), F32),
                        pltpu.VMEM((D, E_HID), BF16), pltpu.VMEM((D, E_HID), BF16),
                        pltpu.VMEM((E_HID, D), BF16), pltpu.SemaphoreType.DMA((2,))],
    )
    return pl.pallas_call(
        _experts_kernel,
        grid_spec=grid_spec,
        out_shape=jax.ShapeDtypeStruct((N_BLOCKS * BM * RT, LANES), U32),
        compiler_params=_cp(("arbitrary",)),
        name="experts",
    )(blk_expert, blk_valid, switch, wslot, next_expert, xs, w1, w3, w2)


def _final_kernel(dcur0_ref, dcur1_ref, dnext0_ref, dnext1_ref, ys_ref, x1_ref, rwt_ref, mods_ref,
                  g2_ref, b2_ref, oc_ref, ol_ref, buf, sems):
    i = pl.program_id(0)
    dcur_ref, dnext_ref = (dcur0_ref, dcur1_ref), (dnext0_ref, dnext1_ref)
    n = pl.num_programs(0)
    slot = i % 2

    def first_row(s, k):
        return (s * TOP_K + k) * TF

    def issue(d_ref, s):
        def body(r, carry):
            for k in range(TOP_K):
                _row_copy(ys_ref, d_ref[k][0, 0, r], buf, first_row(s, k) + r, sems.at[s]).start()
            return carry

        lax.fori_loop(0, TF, body, 0, unroll=32)

    @pl.when(i == 0)
    def _():
        issue(dcur_ref, 0)

    @pl.when(i + 1 < n)
    def _():
        issue(dnext_ref, 1 - slot)

    for k in range(TOP_K):
        pltpu.make_async_copy(ys_ref.at[pl.ds(0, TF * RT)], buf.at[pl.ds(0, TF * RT)], sems.at[slot]).wait()

    w = rwt_ref[...]
    lo0, hi0 = _unpack_pair(_load_tiles_as_rows(buf, first_row(slot, 0), TF))
    lo1, hi1 = _unpack_pair(_load_tiles_as_rows(buf, first_row(slot, 1), TF))
    w0, w1 = w[:, 0:1], w[:, 1:2]
    moe = jnp.concatenate([lo0 * w0 + lo1 * w1, hi0 * w0 + hi1 * w1], axis=1)
    gate2 = mods_ref[0][5:6]
    out = _layer_norm(ALPHA * x1_ref[...] + gate2 * moe, g2_ref[...], b2_ref[...])

    @pl.when(i < T_CTX // TF)
    def _():
        oc_ref[...] = out

    @pl.when(i >= T_CTX // TF)
    def _():
        ol_ref[...] = out


def _final(dest_k, ys, x1, rwt, mods, g2, b2):
    n = T // TF
    npt = T_CTX // TF
    cur = pl.BlockSpec((1, 1, TF), lambda i: (i, 0, 0), memory_space=pltpu.SMEM)
    nxt = pl.BlockSpec((1, 1, TF), lambda i: (jnp.minimum(i + 1, n - 1), 0, 0), memory_space=pltpu.SMEM)
    return pl.pallas_call(
        _final_kernel,
        grid=(n,),
        in_specs=[cur, cur, nxt, nxt,
                  pl.BlockSpec(memory_space=pl.ANY),
                  pl.BlockSpec((TF, D), lambda i: (i, 0)),
                  pl.BlockSpec((TF, TOP_K), lambda i: (i, 0)),
                  pl.BlockSpec((1, 6, D), lambda i: (_mods_index(i, TF), 0, 0)),
                  _const_spec((1, D)), _const_spec((1, D))],
        out_specs=[pl.BlockSpec((TF, D), lambda i: (jnp.minimum(i, npt - 1), 0)),
                   pl.BlockSpec((TF, D), lambda i: (jnp.maximum(i - npt, 0), 0))],
        out_shape=[jax.ShapeDtypeStruct((T_CTX, D), F32), jax.ShapeDtypeStruct((T_LAT, D), F32)],
        scratch_shapes=[pltpu.VMEM((2 * TOP_K * TF * RT, LANES), U32), pltpu.SemaphoreType.DMA((2,))],
        compiler_params=_cp(("arbitrary",)),
        name="final",
    )(*dest_k, *dest_k, ys, x1, rwt, mods, g2, b2)


def _rope_tables():
    n = N_LAT
    rows = n // GRID_W
    f32 = np.float32
    row = np.repeat(np.arange(rows, dtype=f32), GRID_W)
    col = np.tile(np.arange(GRID_W, dtype=f32), rows)
    half = QK_ROPE // 2
    inv = (f32(ROPE_THETA) ** (-np.arange(0, half, 2, dtype=f32) / f32(half))).astype(f32)
    ang_r, ang_c = (row[:, None] * inv).astype(f32), (col[:, None] * inv).astype(f32)
    cr, sr = np.cos(ang_r).astype(f32), np.sin(ang_r).astype(f32)
    cc, sc = np.cos(ang_c).astype(f32), np.sin(ang_c).astype(f32)
    z16 = np.zeros_like(cr)
    pad = lambda v: np.full((n, LANES - QK_ROPE), v, f32)
    cos = np.concatenate([cr, cr, cc, cc, pad(1.0)], axis=1)
    s_plus = np.concatenate([z16, sr, z16, sc, pad(0.0)], axis=1)
    s_minus = np.concatenate([-sr, z16, -sc, z16, pad(0.0)], axis=1)
    ident = lambda v: np.full((TM, LANES), v, f32)
    return (jnp.asarray(np.concatenate([cos, ident(1.0)], 0)),
            jnp.asarray(np.concatenate([s_plus, ident(0.0)], 0)),
            jnp.asarray(np.concatenate([s_minus, ident(0.0)], 0)))


def _block_plan(eid_k, rank, hist):
    counts = jnp.sum(hist, axis=0)
    nblk = (counts + BM - 1) // BM
    bends = jnp.cumsum(nblk)
    bstarts = bends - nblk
    base = bstarts[None, :] * BM + jnp.cumsum(hist, axis=0) - hist
    eid = jnp.arange(N_EXP, dtype=I32)
    hit = eid_k[:, :, None, :] == eid[None, None, :, None]
    first = jnp.sum(jnp.where(hit, base[:, None, :, None], 0), axis=2)
    dest = (first + rank).astype(I32)
    blk = jnp.arange(N_BLOCKS, dtype=I32)
    blk_expert = jnp.minimum(jnp.sum((blk[:, None] >= bends[None, :]).astype(I32), axis=1), N_EXP - 1)
    of_blk = blk_expert[:, None] == eid[None, :]

    def per_block(v):
        return jnp.sum(jnp.where(of_blk, v[None, :], 0), axis=1)

    left = per_block(counts) - (blk - per_block(bstarts)) * BM
    blk_valid = jnp.where(blk < bends[-1], jnp.clip(left, 0, BM), 0).astype(I32)
    unused = bends[-1] + jnp.arange(N_EXP, dtype=I32)
    zero_blk = jnp.concatenate([jnp.maximum(bends - 1, 0), jnp.minimum(unused, N_BLOCKS - 1)])
    zero_valid = jnp.concatenate([counts % BM != 0, unused < N_BLOCKS])
    prev_expert = jnp.concatenate([jnp.full((1,), -1, I32), blk_expert[:-1]])
    switch = (blk < bends[-1]) & (blk_expert != prev_expert)
    wslot = (jnp.cumsum(switch.astype(I32)) - 1) % 2
    later = (eid[None, :] > eid[:, None]) & (nblk[None, :] > 0)
    next_of = jnp.min(jnp.where(later, eid[None, :], N_EXP), axis=1)
    next_expert = per_block(jnp.where(next_of < N_EXP, next_of, -1))
    return (dest, blk_expert.astype(I32), blk_valid, zero_blk.astype(I32),
            zero_valid.astype(I32), switch.astype(I32), wslot.astype(I32), next_expert.astype(I32))


def kernel(x_prompt, x_sample, cache_ckv, cache_krope, c, c_ctx, w_ada, b_ada, w_in, q_norm_g, w_uq,
           kv_norm_g, w_ukv, gmlp_ln_g, gmlp_ln_b, w_spatial, b_spatial, w_out, ln1_g, ln1_b, w_group,
           b_group, w_router, b_router, w1, w3, w2, ln2_g, ln2_b):
    l = 0
    xc = x_prompt.reshape(T_CTX, D)
    xl = x_sample.reshape(T_LAT, D)

    cond16 = jnp.concatenate([c, c_ctx[None, :], jnp.zeros((16 - N_LAT_B - 1, D), F32)], axis=0)
    mods = _adaln(cond16, w_ada[l], b_ada[l][None, :]).reshape(16, 6, D)

    wi = w_in[l]
    o_kr = QL + KVL
    o_u = o_kr + QK_ROPE
    win_a = wi[:, :o_kr].astype(BF16)
    win_g = wi[:, o_u:].astype(BF16)
    win_r = jnp.pad(wi[:, o_kr:o_u], ((0, 0), (0, LANES - QK_ROPE))).astype(BF16)
    wuq = jnp.pad(w_uq[l].reshape(QL, H, QK_NOPE + QK_ROPE),
                  ((0, 0), (0, 0), (0, HEAD_PAD - QK_NOPE - QK_ROPE))).reshape(QL, H * HEAD_PAD).astype(BF16)
    wkv = w_ukv[l].reshape(KVL, H, QK_NOPE + V_DIM)
    wuk = wkv[:, :, :QK_NOPE].reshape(KVL, H * QK_NOPE).astype(BF16)
    wuvt = wkv[:, :, QK_NOPE:].reshape(KVL, H * V_DIM).T.astype(BF16)
    ws = w_spatial[l].astype(BF16)
    bsb = jnp.broadcast_to(b_spatial[l][:, :, None], (G_GROUPS, CHUNK, G_DIM))
    wrt = jnp.concatenate([w_group[l], w_router[l], jnp.zeros((D, LANES - N_GRP - N_EXP), F32)],
                          axis=1).astype(BF16)
    brt = jnp.concatenate([b_group[l], b_router[l], jnp.zeros((LANES - N_GRP - N_EXP,), F32)])[None, :]
    cos_t, sp_t, sm_t = _rope_tables()
    row = lambda v: v[l][None, :]

    q, k, vt, gm, ckv_n, kr = _proj(xc, xl, mods, win_a, win_g, win_r, row(q_norm_g), wuq, row(kv_norm_g), wuk, wuvt,
                                   row(gmlp_ln_g), row(gmlp_ln_b), ws, bsb, cos_t, sp_t, sm_t)
    kc, vct = _kvexp(cache_ckv[:, l].reshape(N_LAT_B * PAST, KVL),
                     cache_krope[:, l].reshape(N_LAT_B * PAST, QK_ROPE), wuk, wuvt)
    attn_c = _attn_ctx(q, k, vt)
    attn_l = _attn_lat(q, k, vt, kc, vct)

    x1, h2p, ridx, rwt = _post(attn_c, attn_l, gm, xc, xl, mods, w_out[l].astype(BF16), row(ln1_g), row(ln1_b),
                               wrt, brt)
    eid_k, rank, hist = _rank(ridx)
    dest, blk_expert, blk_valid, zero_blk, zero_valid, switch, wslot, next_expert = _block_plan(
        eid_k, rank, hist[:, 0, :N_EXP].astype(I32))
    xs = _dispatch(zero_blk, zero_valid, dest, h2p)
    ys = _experts(blk_expert, blk_valid, switch, wslot, next_expert, xs, w1[l], w3[l], w2[l])
    y_c, y_l = _final(_regroup_steps(dest, TF), ys, x1, rwt, mods, row(ln2_g), row(ln2_b))

    return (y_c.reshape(N_CTX_B, N_CTX, D), y_l.reshape(N_LAT_B, N_LAT, D),
            ckv_n.reshape(N_CTX_B, 1, N_CTX, KVL), kr.reshape(N_CTX_B, 1, N_CTX, QK_ROPE))
```

```python
import functools

import jax
import jax.numpy as jnp
import numpy as np
from jax import lax
from jax.experimental import pallas as pl
from jax.experimental.pallas import tpu as pltpu

F32 = jnp.float32
BF16 = jnp.bfloat16
U32 = jnp.uint32
I32 = jnp.int32

D = 2048
N_CTX_B, N_CTX = 32, 256
N_LAT_B, N_LAT = 4, 4096
PAST = 512
GRID_W = 64
H = 8
QK_NOPE, QK_ROPE, V_DIM = 128, 64, 128
QL, KVL = 512, 512
GW = 1024
G_GROUPS, G_DIM, CHUNK = 8, 128, 128
N_GRP, E_PER_GRP, N_EXP, TOP_K = 4, 8, 32, 2
E_HID = 512
ROPE_THETA = 10000.0
EPS = 1e-6
ALPHA = 2.0 ** 0.25
SM_SCALE = (QK_NOPE + QK_ROPE) ** -0.5
Q_SCALE = SM_SCALE * float(np.log2(np.e))

T_CTX = N_CTX_B * N_CTX
T_LAT = N_LAT_B * N_LAT
T = T_CTX + T_LAT
N_ASSIGN = T * TOP_K

LANES = 128
HEAD_PAD = 256
V_AUG = V_DIM + 16
TM = 256
TP = 512
POST_SUBS = (256, 256)
SUB = 256
TD = 2048
TF = 256
CTX_PER_STEP = 2
TQ = 4096
TK = 512
BM = 256
N_BLOCKS = N_ASSIGN // BM + N_EXP
N_ZERO = 2 * N_EXP
VMEM_LIMIT = 56 * 1024 * 1024

NP_TILES = T_CTX // TM
LAT_TILES_PER_B = N_LAT // TM


def _cp(sem):
    return pltpu.CompilerParams(dimension_semantics=sem, vmem_limit_bytes=VMEM_LIMIT)


def _const_spec(shape):
    nd = len(shape)
    return pl.BlockSpec(shape, lambda *a: (0,) * nd, pipeline_mode=pl.Buffered(1))


def _dot(a, b):
    return jnp.dot(a, b, preferred_element_type=F32)


def _dot_nt(a, b):
    return lax.dot_general(a, b, (((1,), (1,)), ((), ())), preferred_element_type=F32)


def _layer_norm(x, g, b):
    mu = jnp.mean(x, -1, keepdims=True)
    xc = x - mu
    var = jnp.mean(xc * xc, -1, keepdims=True)
    return xc * lax.rsqrt(var + EPS) * g + b


def _rms_norm(x, g):
    return x * lax.rsqrt(jnp.mean(x * x, -1, keepdims=True) + EPS) * g


def _gelu(x):
    return 0.5 * x * (1.0 + lax.erf(x * np.float32(np.sqrt(0.5))))


def _pack_pair(lo, hi):
    lo_b = lax.bitcast_convert_type(lo.astype(BF16).astype(F32), U32)
    hi_b = lax.bitcast_convert_type(hi.astype(BF16).astype(F32), U32)
    return hi_b | (lo_b >> 16)


def _unpack_pair(w):
    lo = lax.bitcast_convert_type(w << 16, F32)
    hi = lax.bitcast_convert_type(w & jnp.uint32(0xFFFF0000), F32)
    return lo, hi


RT = 8


def _store_rows_as_tiles(ref, row0, val):
    n = val.shape[0]
    for j in range(RT):
        ref[pl.ds(row0 * RT + j, n, stride=RT), :] = val[:, j * LANES:(j + 1) * LANES]


def _load_tiles_as_rows(ref, row0, n):
    return jnp.concatenate([ref[pl.ds(row0 * RT + j, n, stride=RT), :] for j in range(RT)], axis=1)


def _adaln_kernel(cond_ref, w_ref, b_ref, o_ref):
    s = jax.nn.silu(cond_ref[...])
    s_hi = s.astype(BF16)
    s_lo = (s - s_hi.astype(F32)).astype(BF16)
    w = w_ref[...]
    w_hi = w.astype(BF16)
    w_lo = (w - w_hi.astype(F32)).astype(BF16)
    o_ref[...] = _dot(s_hi, w_hi) + _dot(s_lo, w_hi) + _dot(s_hi, w_lo) + b_ref[...]


def _adaln(cond16, w_ada, b_ada):
    tn = 1024
    n = w_ada.shape[1]
    return pl.pallas_call(
        _adaln_kernel,
        grid=(n // tn,),
        in_specs=[pl.BlockSpec((16, D), lambda j: (0, 0)),
                  pl.BlockSpec((D, tn), lambda j: (0, j)),
                  pl.BlockSpec((1, tn), lambda j: (0, j))],
        out_specs=pl.BlockSpec((16, tn), lambda j: (0, j)),
        out_shape=jax.ShapeDtypeStruct((16, n), F32),
        compiler_params=_cp(("arbitrary",)),
        name="adaln",
    )(cond16, w_ada, b_ada)


def _expand_kv(ckv_n, kr128, wuk_ref, wuvt_ref, k_ref, vt_ref):
    cb = ckv_n.astype(BF16)
    kn = _dot(cb, wuk_ref[...])
    vt = _dot_nt(wuvt_ref[...], cb).astype(BF16)
    ones = jnp.ones((V_AUG - V_DIM, vt.shape[1]), BF16)
    for h in range(H):
        vt_ref[h * V_AUG:h * V_AUG + V_DIM, :] = vt[h * V_DIM:(h + 1) * V_DIM]
        vt_ref[h * V_AUG + V_DIM:(h + 1) * V_AUG, :] = ones
    krb = kr128.astype(BF16)
    for h in range(H):
        k_ref[:, h * HEAD_PAD:h * HEAD_PAD + QK_NOPE] = kn[:, h * QK_NOPE:(h + 1) * QK_NOPE].astype(BF16)
        k_ref[:, h * HEAD_PAD + QK_NOPE:(h + 1) * HEAD_PAD] = krb


def _rope128(x, cos, s_plus, s_minus):
    return (x * cos + pltpu.roll(x, 16, 1) * s_plus + pltpu.roll(x, LANES - 16, 1) * s_minus)


def _proj_kernel(xc_ref, xl_ref, mods_ref, win_a_ref, win_g_ref, win_r_ref, qg_ref, wuq_ref, kvg_ref,
                 wuk_ref, wuv_ref,
                 lng_ref, lnb_ref, ws_ref, bsb_ref, cos_ref, sp_ref, sm_ref,
                 q_ref, k_ref, v_ref, gm_ref, ckv_ref, kr_ref):
    i = pl.program_id(0)
    x = jnp.where(i < NP_TILES, xc_ref[...], xl_ref[...])
    m = mods_ref[0]
    shift1, scale1 = m[0:1], m[1:2]
    hmod = (x * (1.0 + scale1) + shift1).astype(BF16)
    pg = _dot(hmod, win_g_ref[...])
    u, v = pg[:, :GW], pg[:, GW:]
    gu = _gelu(u)
    vn = _layer_norm(_gelu(v), lng_ref[...], lnb_ref[...]).astype(BF16)
    pa = _dot(hmod, win_a_ref[...])
    kr = _dot(hmod, win_r_ref[...])
    cq, ckv = pa[:, :QL], pa[:, QL:]

    cos, s_plus, s_minus = cos_ref[...], sp_ref[...], sm_ref[...]

    cq_n = _rms_norm(cq, qg_ref[...]).astype(BF16)
    q = _dot(cq_n, wuq_ref[...]) * Q_SCALE
    for h in range(H):
        q_ref[:, h * HEAD_PAD:h * HEAD_PAD + QK_NOPE] = q[:, h * HEAD_PAD:h * HEAD_PAD + QK_NOPE].astype(BF16)
        q_ref[:, h * HEAD_PAD + QK_NOPE:(h + 1) * HEAD_PAD] = _rope128(
            q[:, h * HEAD_PAD + QK_NOPE:(h + 1) * HEAD_PAD], cos, s_plus, s_minus).astype(BF16)

    ckv_n = _rms_norm(ckv, kvg_ref[...])
    _expand_kv(ckv_n, _rope128(kr, cos, s_plus, s_minus), wuk_ref, wuv_ref, k_ref, v_ref)

    for c in range(TM // CHUNK):
        rows = slice(c * CHUNK, (c + 1) * CHUNK)
        for g in range(G_GROUPS):
            cols = slice(g * G_DIM, (g + 1) * G_DIM)
            s = _dot(ws_ref[g], vn[rows, cols]) + bsb_ref[g]
            gm_ref[rows, cols] = (gu[rows, cols] * s).astype(BF16)

    @pl.when(i < NP_TILES)
    def _():
        ckv_ref[...] = ckv_n
        kr_ref[...] = kr[:, :QK_ROPE]


def _mods_index(i, tile=TM):
    return jnp.where(i < T_CTX // tile, N_LAT_B, (i - T_CTX // tile) // (N_LAT // tile))


def _proj(xc, xl, mods, win_a, win_g, win_r, qg, wuq, kvg, wuk, wuv, lng, lnb, ws, bsb, cos_t, sp_t, sm_t):
    def tok(width):
        return pl.BlockSpec((TM, width), lambda i: (i, 0))

    def pos_block(i):
        return jnp.where(i < NP_TILES, LAT_TILES_PER_B, (i - NP_TILES) % LAT_TILES_PER_B)

    rope_spec = pl.BlockSpec((TM, LANES), lambda i: (pos_block(i), 0))
    return pl.pallas_call(
        _proj_kernel,
        grid=(T // TM,),
        in_specs=[
            pl.BlockSpec((TM, D), lambda i: (jnp.minimum(i, NP_TILES - 1), 0)),
            pl.BlockSpec((TM, D), lambda i: (jnp.maximum(i - NP_TILES, 0), 0)),
            pl.BlockSpec((1, 6, D), lambda i: (_mods_index(i), 0, 0)),
            _const_spec((D, QL + KVL)), _const_spec((D, 2 * GW)), _const_spec((D, LANES)),
            _const_spec((1, QL)), _const_spec((QL, H * HEAD_PAD)),
            _const_spec((1, KVL)), _const_spec((KVL, H * QK_NOPE)), _const_spec((H * V_DIM, KVL)),
            _const_spec((1, GW)), _const_spec((1, GW)),
            _const_spec((G_GROUPS, CHUNK, CHUNK)), _const_spec((G_GROUPS, CHUNK, G_DIM)),
            rope_spec, rope_spec, rope_spec,
        ],
        out_specs=[tok(H * HEAD_PAD), tok(H * HEAD_PAD),
                   pl.BlockSpec((H * V_AUG, TM), lambda i: (0, i)), tok(GW),
                   pl.BlockSpec((TM, KVL), lambda i: (jnp.minimum(i, NP_TILES - 1), 0)),
                   pl.BlockSpec((TM, QK_ROPE), lambda i: (jnp.minimum(i, NP_TILES - 1), 0))],
        out_shape=[
            jax.ShapeDtypeStruct((T, H * HEAD_PAD), BF16),
            jax.ShapeDtypeStruct((T, H * HEAD_PAD), BF16),
            jax.ShapeDtypeStruct((H * V_AUG, T), BF16),
            jax.ShapeDtypeStruct((T, GW), BF16),
            jax.ShapeDtypeStruct((T_CTX, KVL), F32),
            jax.ShapeDtypeStruct((T_CTX, QK_ROPE), F32),
        ],
        compiler_params=_cp(("arbitrary",)),
        name="proj",
    )(xc, xl, mods, win_a, win_g, win_r, qg, wuq, kvg, wuk, wuv, lng, lnb, ws, bsb, cos_t, sp_t, sm_t)


def _kvexp_kernel(ckv_ref, kr_ref, wuk_ref, wuv_ref, k_ref, v_ref):
    kr = kr_ref[...]
    kr128 = jnp.concatenate([kr, jnp.zeros_like(kr)], axis=1)
    _expand_kv(ckv_ref[...], kr128, wuk_ref, wuv_ref, k_ref, v_ref)


def _kvexp(ckv, kr, wuk, wuv):
    rows = ckv.shape[0]
    return pl.pallas_call(
        _kvexp_kernel,
        grid=(rows // TM,),
        in_specs=[pl.BlockSpec((TM, KVL), lambda i: (i, 0)), pl.BlockSpec((TM, QK_ROPE), lambda i: (i, 0)),
                  _const_spec((KVL, H * QK_NOPE)), _const_spec((H * V_DIM, KVL))],
        out_specs=[pl.BlockSpec((TM, H * HEAD_PAD), lambda i: (i, 0)),
                   pl.BlockSpec((H * V_AUG, TM), lambda i: (0, i))],
        out_shape=[jax.ShapeDtypeStruct((rows, H * HEAD_PAD), BF16),
                   jax.ShapeDtypeStruct((H * V_AUG, rows), BF16)],
        compiler_params=_cp(("arbitrary",)),
        name="kvexp",
    )(ckv, kr, wuk, wuv)


def _probs(s, m):
    m_new = jnp.maximum(m, jnp.max(s, 0, keepdims=True))
    return m_new, jnp.exp2(m - m_new), jnp.exp2(s - m_new).astype(BF16)


def _attn_finish(acc):
    return (acc[:V_DIM] / acc[V_DIM:V_DIM + 1]).T.astype(BF16)


def _attn_ctx_kernel(q_ref, k_ref, vt_ref, o_ref):
    m0 = jnp.full((1, N_CTX), -jnp.inf, F32)
    for b in range(CTX_PER_STEP):
        rows = slice(b * N_CTX, (b + 1) * N_CTX)
        for h in range(H):
            qh = q_ref[rows, h * HEAD_PAD:(h + 1) * HEAD_PAD]
            kh = k_ref[rows, h * HEAD_PAD:(h + 1) * HEAD_PAD]
            _, _, p = _probs(_dot_nt(kh, qh), m0)
            acc = _dot(vt_ref[h * V_AUG:(h + 1) * V_AUG, rows], p)
            o_ref[rows, h * V_DIM:(h + 1) * V_DIM] = _attn_finish(acc)


def _attn_ctx(q, k, vt):
    n = CTX_PER_STEP * N_CTX
    return pl.pallas_call(
        _attn_ctx_kernel,
        grid=(N_CTX_B // CTX_PER_STEP,),
        in_specs=[pl.BlockSpec((n, H * HEAD_PAD), lambda b: (b, 0)),
                  pl.BlockSpec((n, H * HEAD_PAD), lambda b: (b, 0)),
                  pl.BlockSpec((H * V_AUG, n), lambda b: (0, b))],
        out_specs=pl.BlockSpec((n, H * V_DIM), lambda b: (b, 0)),
        out_shape=jax.ShapeDtypeStruct((T_CTX, H * V_DIM), BF16),
        compiler_params=_cp(("arbitrary",)),
        name="attn_ctx",
    )(q, k, vt)


def _attn_lat_kernel(q_ref, k_ref, vt_ref, kc_ref, vct_ref, o_ref):
    q = q_ref[...]
    n_lat = N_LAT // TK

    def keys(c):
        return k_ref[c * TK:(c + 1) * TK, :] if c < n_lat else kc_ref[...]

    def values_t(c):
        return vt_ref[:, c * TK:(c + 1) * TK] if c < n_lat else vct_ref[...]

    m = jnp.full((1, TQ), -jnp.inf, F32)
    acc = jnp.zeros((V_AUG, TQ), F32)
    s_next = _dot_nt(keys(0), q)
    m_next = jnp.maximum(m, jnp.max(s_next, 0, keepdims=True))
    p_prev = alpha_prev = None
    for c in range(n_lat + 1):
        s, m_new = s_next, m_next
        if c < n_lat:
            s_next = _dot_nt(keys(c + 1), q)
            m_next = jnp.maximum(m_new, jnp.max(s_next, 0, keepdims=True))
        if c > 0:
            acc = alpha_prev * acc + _dot(values_t(c - 1), p_prev)
        alpha_prev = jnp.exp2(m - m_new)
        p_prev = jnp.exp2(s - m_new).astype(BF16)
        m = m_new
    acc = alpha_prev * acc + _dot(values_t(n_lat), p_prev)
    o_ref[...] = _attn_finish(acc)


def _attn_lat(q, k, vt, kc, vct):
    qb0 = T_CTX // TQ
    qpb = N_LAT // TQ
    kb0 = T_CTX // N_LAT
    return pl.pallas_call(
        _attn_lat_kernel,
        grid=(N_LAT_B, H, qpb),
        in_specs=[pl.BlockSpec((TQ, HEAD_PAD), lambda b, h, qi: (qb0 + b * qpb + qi, h)),
                  pl.BlockSpec((N_LAT, HEAD_PAD), lambda b, h, qi: (kb0 + b, h)),
                  pl.BlockSpec((V_AUG, N_LAT), lambda b, h, qi: (h, kb0 + b)),
                  pl.BlockSpec((PAST, HEAD_PAD), lambda b, h, qi: (b, h)),
                  pl.BlockSpec((V_AUG, PAST), lambda b, h, qi: (h, b))],
        out_specs=pl.BlockSpec((TQ, V_DIM), lambda b, h, qi: (b * qpb + qi, h)),
        out_shape=jax.ShapeDtypeStruct((T_LAT, H * V_DIM), BF16),
        compiler_params=_cp(("arbitrary", "arbitrary", "arbitrary")),
        name="attn_lat",
    )(q, k, vt, kc, vct)


def _route(logits):
    lane_i = lax.broadcasted_iota(I32, logits.shape, 1)
    lane = lane_i.astype(F32)
    neg = jnp.float32(-jnp.inf)
    far = jnp.float32(LANES)
    gl = jnp.where(lane_i < N_GRP, logits, neg)
    gmax = jnp.max(gl, -1, keepdims=True)
    gidx = jnp.min(jnp.where(gl == gmax, lane, far), -1, keepdims=True)
    p_top = 1.0 / jnp.sum(jnp.exp(gl - gmax), -1, keepdims=True)
    grp_of_lane = ((lane_i - N_GRP) >> 3).astype(F32)
    in_grp = (lane_i >= N_GRP) & (lane_i < N_GRP + N_EXP) & (grp_of_lane == gidx)
    el = jnp.where(in_grp, logits, neg)
    m1 = jnp.max(el, -1, keepdims=True)
    i1 = jnp.min(jnp.where(el == m1, lane, far), -1, keepdims=True)
    el2 = jnp.where(lane == i1, neg, el)
    m2 = jnp.max(el2, -1, keepdims=True)
    i2 = jnp.min(jnp.where(el2 == m2, lane, far), -1, keepdims=True)
    e2 = jnp.exp(m2 - m1)
    w1 = p_top / (1.0 + e2)
    w2 = p_top * e2 / (1.0 + e2)
    return i1 - N_GRP, i2 - N_GRP, w1, w2


def _pair(a, b, dtype):
    two = lax.broadcasted_iota(I32, (a.shape[0], TOP_K), 1)
    return jnp.where(two == 0, a, b).astype(dtype)


def _post_kernel(ac_ref, al_ref, gm_ref, xc_ref, xl_ref, mods_ref, wout_ref, g1_ref, b1_ref, wrt_ref,
                 brt_ref, x1_ref, h2p_ref, ridx_ref, rwt_ref):
    i = pl.program_id(0)
    is_ctx = i < T_CTX // TP
    m = mods_ref[0]
    gate1, shift2, scale2 = m[2:3], m[3:4], m[4:5]

    def mix_of(rows):
        attn = jnp.where(is_ctx, ac_ref[rows, :], al_ref[rows, :])
        return _dot(jnp.concatenate([attn, gm_ref[rows, :]], axis=1), wout_ref[...])

    bounds = np.cumsum((0,) + POST_SUBS)
    subs = [slice(int(a), int(b)) for a, b in zip(bounds[:-1], bounds[1:])]
    mix_next = mix_of(subs[0])
    for s, rows in enumerate(subs):
        mix = mix_next
        if s + 1 < len(subs):
            mix_next = mix_of(subs[s + 1])
        x = jnp.where(is_ctx, xc_ref[rows, :], xl_ref[rows, :])
        x1 = _layer_norm(ALPHA * x + gate1 * mix, g1_ref[...], b1_ref[...])
        x1_ref[rows, :] = x1
        h2 = x1 * (1.0 + scale2) + shift2
        _store_rows_as_tiles(h2p_ref, rows.start, _pack_pair(h2[:, :D // 2], h2[:, D // 2:]))
        e1, e2, w1, w2 = _route(_dot(h2.astype(BF16), wrt_ref[...]) + brt_ref[...])
        ridx_ref[rows, :] = _pair(e1, e2, I32)
        rwt_ref[rows, :] = _pair(w1, w2, F32)


def _post(attn_c, attn_l, gm, xc, xl, mods, wout, g1, b1, wrt, brt):
    npt = T_CTX // TP

    def tok(width):
        return pl.BlockSpec((TP, width), lambda i: (i, 0))

    def ctx(width):
        return pl.BlockSpec((TP, width), lambda i: (jnp.minimum(i, npt - 1), 0))

    def lat(width):
        return pl.BlockSpec((TP, width), lambda i: (jnp.maximum(i - npt, 0), 0))

    return pl.pallas_call(
        _post_kernel,
        grid=(T // TP,),
        in_specs=[ctx(H * V_DIM), lat(H * V_DIM), tok(GW), ctx(D), lat(D),
                  pl.BlockSpec((1, 6, D), lambda i: (_mods_index(i, TP), 0, 0)),
                  _const_spec((D, D)), _const_spec((1, D)), _const_spec((1, D)),
                  _const_spec((D, LANES)), _const_spec((1, LANES))],
        out_specs=[tok(D), pl.BlockSpec((TP * RT, LANES), lambda i: (i, 0)), tok(TOP_K), tok(TOP_K)],
        out_shape=[jax.ShapeDtypeStruct((T, D), F32),
                   jax.ShapeDtypeStruct((T * RT, LANES), U32),
                   jax.ShapeDtypeStruct((T, TOP_K), I32),
                   jax.ShapeDtypeStruct((T, TOP_K), F32)],
        compiler_params=_cp(("arbitrary",)),
        name="post",
    )(attn_c, attn_l, gm, xc, xl, mods, wout, g1, b1, wrt, brt)


def _rank_kernel(ridx_ref, tri_ref, e_ref, rank_ref, hist_ref):
    lane = lax.broadcasted_iota(I32, (SUB, LANES), 1).astype(F32)
    seen = jnp.zeros((1, LANES), F32)
    for s in range(TP // SUB):
        rows = slice(s * SUB, (s + 1) * SUB)
        e = ridx_ref[rows, :].astype(F32)
        ohs = [(lane == e[:, k:k + 1]).astype(F32) for k in range(TOP_K)]
        both = ohs[0] + ohs[1]
        before = _dot(tri_ref[...], both.astype(BF16)) + seen
        cols = jnp.zeros((SUB, LANES), F32)
        for k in range(TOP_K):
            rank_k = jnp.sum(before * ohs[k], -1, keepdims=True)
            cols = jnp.where(lane == k, e[:, k:k + 1], cols)
            cols = jnp.where(lane == TOP_K + k, rank_k, cols)
        as_rows = cols.T.astype(I32)
        e_ref[0, :, rows] = as_rows[0:TOP_K]
        rank_ref[0, :, rows] = as_rows[TOP_K:2 * TOP_K]
        seen = seen + jnp.sum(both, 0, keepdims=True)
    hist_ref[0] = jnp.broadcast_to(seen, (8, LANES))


def _rank(ridx):
    dense = pl.BlockSpec((1, TOP_K, TP), lambda i: (i, 0, 0))
    return pl.pallas_call(
        _rank_kernel,
        grid=(T // TP,),
        in_specs=[pl.BlockSpec((TP, TOP_K), lambda i: (i, 0)), _const_spec((SUB, SUB))],
        out_specs=[dense, dense, pl.BlockSpec((1, 8, LANES), lambda i: (i, 0, 0))],
        out_shape=[jax.ShapeDtypeStruct((T // TP, TOP_K, TP), I32),
                   jax.ShapeDtypeStruct((T // TP, TOP_K, TP), I32),
                   jax.ShapeDtypeStruct((T // TP, 8, LANES), F32)],
        compiler_params=_cp(("arbitrary",)),
        name="rank",
    )(ridx, jnp.asarray(np.tril(np.ones((SUB, SUB), np.float32), -1), BF16))


def _row_copy(src, src_row, dst, dst_row, sem):
    def tile(ref, row):
        return ref.at[pl.ds(pl.multiple_of(row * RT, RT), RT)]

    return pltpu.make_async_copy(tile(src, src_row), tile(dst, dst_row), sem)


def _dispatch_kernel(zb_ref, zv_ref, d0_ref, d1_ref, h2p_ref, xs_ref, zbuf, sem):
    i = pl.program_id(0)
    dest_refs = (d0_ref, d1_ref)

    @pl.when(i == 0)
    def _():
        zbuf[...] = jnp.zeros_like(zbuf)

        def zero_block(j):
            start = pl.multiple_of(zb_ref[j] * (BM * RT), BM * RT)
            return pltpu.make_async_copy(zbuf, xs_ref.at[pl.ds(start, BM * RT)], sem)

        def start(j, carry):
            @pl.when(zv_ref[j] != 0)
            def _():
                zero_block(j).start()
            return carry

        def wait(j, carry):
            @pl.when(zv_ref[j] != 0)
            def _():
                zero_block(j).wait()
            return carry

        lax.fori_loop(0, N_ZERO, start, 0)
        lax.fori_loop(0, N_ZERO, wait, 0)

    def issue(r, carry):
        for k in range(TOP_K):
            _row_copy(h2p_ref, r, xs_ref, dest_refs[k][0, 0, r], sem).start(priority=k)
        return carry

    lax.fori_loop(0, TD, issue, 0, unroll=16)
    for _ in range(TOP_K):
        pltpu.make_async_copy(h2p_ref, xs_ref.at[pl.ds(0, TD * RT)], sem).wait()


def _regroup_steps(dest, rows):
    return [dest[:, k, :].reshape(T // rows, 1, rows) for k in range(TOP_K)]


def _dispatch(zero_blk, zero_valid, dest, h2p):
    grid_spec = pltpu.PrefetchScalarGridSpec(
        num_scalar_prefetch=2,
        grid=(T // TD,),
        in_specs=[pl.BlockSpec((1, 1, TD), lambda i, zb, zv: (i, 0, 0), memory_space=pltpu.SMEM),
                  pl.BlockSpec((1, 1, TD), lambda i, zb, zv: (i, 0, 0), memory_space=pltpu.SMEM),
                  pl.BlockSpec((TD * RT, LANES), lambda i, zb, zv: (i, 0))],
        out_specs=pl.BlockSpec(memory_space=pl.ANY),
        scratch_shapes=[pltpu.VMEM((BM * RT, LANES), U32), pltpu.SemaphoreType.DMA(())],
    )
    return pl.pallas_call(
        _dispatch_kernel,
        grid_spec=grid_spec,
        out_shape=jax.ShapeDtypeStruct((N_BLOCKS * BM * RT, LANES), U32),
        compiler_params=_cp(("arbitrary",)),
        name="dispatch",
    )(zero_blk, zero_valid, *_regroup_steps(dest, TD), h2p)


def _experts_kernel(be_ref, bv_ref, sw_ref, es_ref, nx_ref, x_ref, w1_hbm, w3_hbm, w2_hbm, y_ref,
                    wf1, wf3, wf2, w1b, w3b, w2b, sems):
    i = pl.program_id(0)
    e = be_ref[i]
    nv = bv_ref[i]

    def weight_copies(expert, s):
        return (pltpu.make_async_copy(w1_hbm.at[expert], wf1.at[s], sems.at[s]),
                pltpu.make_async_copy(w3_hbm.at[expert], wf3.at[s], sems.at[s]),
                pltpu.make_async_copy(w2_hbm.at[expert], wf2.at[s], sems.at[s]))

    @pl.when(sw_ref[i] != 0)
    def _():
        s = es_ref[i]

        @pl.when(i == 0)
        def _():
            for c in weight_copies(e, s):
                c.start()

        for c in weight_copies(e, s):
            c.wait()
        nxt = nx_ref[i]

        @pl.when(nxt >= 0)
        def _():
            for c in weight_copies(nxt, 1 - s):
                c.start(priority=1)

        w1b[...] = wf1[s].astype(BF16)
        w3b[...] = wf3[s].astype(BF16)
        w2b[...] = wf2[s].astype(BF16)

    @pl.when(nv > 0)
    def _():
        lo, hi = _unpack_pair(_load_tiles_as_rows(x_ref, 0, BM))
        xb = jnp.concatenate([lo.astype(BF16), hi.astype(BF16)], axis=1)
        a = _dot(xb, w1b[...])
        b = _dot(xb, w3b[...])
        hid = (jax.nn.silu(a) * b).astype(BF16)
        y = _dot(hid, w2b[...])
        _store_rows_as_tiles(y_ref, 0, _pack_pair(y[:, :D // 2], y[:, D // 2:]))

    @pl.when(nv == 0)
    def _():
        y_ref[...] = jnp.zeros_like(y_ref)


def _experts(blk_expert, blk_valid, switch, wslot, next_expert, xs, w1, w3, w2):
    hbm = pl.BlockSpec(memory_space=pl.ANY)
    grid_spec = pltpu.PrefetchScalarGridSpec(
        num_scalar_prefetch=5,
        grid=(N_BLOCKS,),
        in_specs=[pl.BlockSpec((BM * RT, LANES), lambda i, *_: (i, 0)), hbm, hbm, hbm],
        out_specs=pl.BlockSpec((BM * RT, LANES), lambda i, *_: (i, 0)),
        scratch_shapes=[pltpu.VMEM((2, D, E_HID), F32), pltpu.VMEM((2, D, E_HID), F32),
                        pltpu.VMEM((2, E_HID, D), F32),
                        pltpu.VMEM((D, E_HID), BF16), pltpu.VMEM((D, E_HID), BF16),
                        pltpu.VMEM((E_HID, D), BF16), pltpu.SemaphoreType.DMA((2,))],
    )
    return pl.pallas_call(
        _experts_kernel,
        grid_spec=grid_spec,
        out_shape=jax.ShapeDtypeStruct((N_BLOCKS * BM * RT, LANES), U32),
        compiler_params=_cp(("arbitrary",)),
        name="experts",
    )(blk_expert, blk_valid, switch, wslot, next_expert, xs, w1, w3, w2)


def _final_kernel(dcur0_ref, dcur1_ref, dnext0_ref, dnext1_ref, ys_ref, x1_ref, rwt_ref, mods_ref,
                  g2_ref, b2_ref, oc_ref, ol_ref, buf, sems):
    i = pl.program_id(0)
    dcur_ref, dnext_ref = (dcur0_ref, dcur1_ref), (dnext0_ref, dnext1_ref)
    n = pl.num_programs(0)
    slot = i % 2

    def first_row(s, k):
        return (s * TOP_K + k) * TF

    def issue(d_ref, s):
        def body(r, carry):
            for k in range(TOP_K):
                _row_copy(ys_ref, d_ref[k][0, 0, r], buf, first_row(s, k) + r, sems.at[s]).start()
            return carry

        lax.fori_loop(0, TF, body, 0, unroll=32)

    @pl.when(i == 0)
    def _():
        issue(dcur_ref, 0)

    @pl.when(i + 1 < n)
    def _():
        issue(dnext_ref, 1 - slot)

    for k in range(TOP_K):
        pltpu.make_async_copy(ys_ref.at[pl.ds(0, TF * RT)], buf.at[pl.ds(0, TF * RT)], sems.at[slot]).wait()

    w = rwt_ref[...]
    lo0, hi0 = _unpack_pair(_load_tiles_as_rows(buf, first_row(slot, 0), TF))
    lo1, hi1 = _unpack_pair(_load_tiles_as_rows(buf, first_row(slot, 1), TF))
    w0, w1 = w[:, 0:1], w[:, 1:2]
    moe = jnp.concatenate([lo0 * w0 + lo1 * w1, hi0 * w0 + hi1 * w1], axis=1)
    gate2 = mods_ref[0][5:6]
    out = _layer_norm(ALPHA * x1_ref[...] + gate2 * moe, g2_ref[...], b2_ref[...])

    @pl.when(i < T_CTX // TF)
    def _():
        oc_ref[...] = out

    @pl.when(i >= T_CTX // TF)
    def _():
        ol_ref[...] = out


def _final(dest_k, ys, x1, rwt, mods, g2, b2):
    n = T // TF
    npt = T_CTX // TF
    cur = pl.BlockSpec((1, 1, TF), lambda i: (i, 0, 0), memory_space=pltpu.SMEM)
    nxt = pl.BlockSpec((1, 1, TF), lambda i: (jnp.minimum(i + 1, n - 1), 0, 0), memory_space=pltpu.SMEM)
    return pl.pallas_call(
        _final_kernel,
        grid=(n,),
        in_specs=[cur, cur, nxt, nxt,
                  pl.BlockSpec(memory_space=pl.ANY),
                  pl.BlockSpec((TF, D), lambda i: (i, 0)),
                  pl.BlockSpec((TF, TOP_K), lambda i: (i, 0)),
                  pl.BlockSpec((1, 6, D), lambda i: (_mods_index(i, TF), 0, 0)),
                  _const_spec((1, D)), _const_spec((1, D))],
        out_specs=[pl.BlockSpec((TF, D), lambda i: (jnp.minimum(i, npt - 1), 0)),
                   pl.BlockSpec((TF, D), lambda i: (jnp.maximum(i - npt, 0), 0))],
        out_shape=[jax.ShapeDtypeStruct((T_CTX, D), F32), jax.ShapeDtypeStruct((T_LAT, D), F32)],
        scratch_shapes=[pltpu.VMEM((2 * TOP_K * TF * RT, LANES), U32), pltpu.SemaphoreType.DMA((2,))],
        compiler_params=_cp(("arbitrary",)),
        name="final",
    )(*dest_k, *dest_k, ys, x1, rwt, mods, g2, b2)


def _rope_tables():
    n = N_LAT
    rows = n // GRID_W
    f32 = np.float32
    row = np.repeat(np.arange(rows, dtype=f32), GRID_W)
    col = np.tile(np.arange(GRID_W, dtype=f32), rows)
    half = QK_ROPE // 2
    inv = (f32(ROPE_THETA) ** (-np.arange(0, half, 2, dtype=f32) / f32(half))).astype(f32)
    ang_r, ang_c = (row[:, None] * inv).astype(f32), (col[:, None] * inv).astype(f32)
    cr, sr = np.cos(ang_r).astype(f32), np.sin(ang_r).astype(f32)
    cc, sc = np.cos(ang_c).astype(f32), np.sin(ang_c).astype(f32)
    z16 = np.zeros_like(cr)
    pad = lambda v: np.full((n, LANES - QK_ROPE), v, f32)
    cos = np.concatenate([cr, cr, cc, cc, pad(1.0)], axis=1)
    s_plus = np.concatenate([z16, sr, z16, sc, pad(0.0)], axis=1)
    s_minus = np.concatenate([-sr, z16, -sc, z16, pad(0.0)], axis=1)
    ident = lambda v: np.full((TM, LANES), v, f32)
    return (jnp.asarray(np.concatenate([cos, ident(1.0)], 0)),
            jnp.asarray(np.concatenate([s_plus, ident(0.0)], 0)),
            jnp.asarray(np.concatenate([s_minus, ident(0.0)], 0)))


def _block_plan(eid_k, rank, hist):
    counts = jnp.sum(hist, axis=0)
    nblk = (counts + BM - 1) // BM
    bends = jnp.cumsum(nblk)
    bstarts = bends - nblk
    base = bstarts[None, :] * BM + jnp.cumsum(hist, axis=0) - hist
    eid = jnp.arange(N_EXP, dtype=I32)
    hit = eid_k[:, :, None, :] == eid[None, None, :, None]
    first = jnp.sum(jnp.where(hit, base[:, None, :, None], 0), axis=2)
    dest = (first + rank).astype(I32)
    blk = jnp.arange(N_BLOCKS, dtype=I32)
    blk_expert = jnp.minimum(jnp.sum((blk[:, None] >= bends[None, :]).astype(I32), axis=1), N_EXP - 1)
    of_blk = blk_expert[:, None] == eid[None, :]

    def per_block(v):
        return jnp.sum(jnp.where(of_blk, v[None, :], 0), axis=1)

    left = per_block(counts) - (blk - per_block(bstarts)) * BM
    blk_valid = jnp.where(blk < bends[-1], jnp.clip(left, 0, BM), 0).astype(I32)
    unused = bends[-1] + jnp.arange(N_EXP, dtype=I32)
    zero_blk = jnp.concatenate([jnp.maximum(bends - 1, 0), jnp.minimum(unused, N_BLOCKS - 1)])
    zero_valid = jnp.concatenate([counts % BM != 0, unused < N_BLOCKS])
    prev_expert = jnp.concatenate([jnp.full((1,), -1, I32), blk_expert[:-1]])
    switch = (blk < bends[-1]) & (blk_expert != prev_expert)
    wslot = (jnp.cumsum(switch.astype(I32)) - 1) % 2
    later = (eid[None, :] > eid[:, None]) & (nblk[None, :] > 0)
    next_of = jnp.min(jnp.where(later, eid[None, :], N_EXP), axis=1)
    next_expert = per_block(jnp.where(next_of < N_EXP, next_of, -1))
    return (dest, blk_expert.astype(I32), blk_valid, zero_blk.astype(I32),
            zero_valid.astype(I32), switch.astype(I32), wslot.astype(I32), next_expert.astype(I32))


def kernel(x_prompt, x_sample, cache_ckv, cache_krope, c, c_ctx, w_ada, b_ada, w_in, q_norm_g, w_uq,
           kv_norm_g, w_ukv, gmlp_ln_g, gmlp_ln_b, w_spatial, b_spatial, w_out, ln1_g, ln1_b, w_group,
           b_group, w_router, b_router, w1, w3, w2, ln2_g, ln2_b):
    l = 0
    xc = x_prompt.reshape(T_CTX, D)
    xl = x_sample.reshape(T_LAT, D)

    cond16 = jnp.concatenate([c, c_ctx[None, :], jnp.zeros((16 - N_LAT_B - 1, D), F32)], axis=0)
    mods = _adaln(cond16, w_ada[l], b_ada[l][None, :]).reshape(16, 6, D)

    wi = w_in[l]
    o_kr = QL + KVL
    o_u = o_kr + QK_ROPE
    win_a = wi[:, :o_kr].astype(BF16)
    win_g = wi[:, o_u:].astype(BF16)
    win_r = jnp.pad(wi[:, o_kr:o_u], ((0, 0), (0, LANES - QK_ROPE))).astype(BF16)
    wuq = jnp.pad(w_uq[l].reshape(QL, H, QK_NOPE + QK_ROPE),
                  ((0, 0), (0, 0), (0, HEAD_PAD - QK_NOPE - QK_ROPE))).reshape(QL, H * HEAD_PAD).astype(BF16)
    wkv = w_ukv[l].reshape(KVL, H, QK_NOPE + V_DIM)
    wuk = wkv[:, :, :QK_NOPE].reshape(KVL, H * QK_NOPE).astype(BF16)
    wuvt = wkv[:, :, QK_NOPE:].reshape(KVL, H * V_DIM).T.astype(BF16)
    ws = w_spatial[l].astype(BF16)
    bsb = jnp.broadcast_to(b_spatial[l][:, :, None], (G_GROUPS, CHUNK, G_DIM))
    wrt = jnp.concatenate([w_group[l], w_router[l], jnp.zeros((D, LANES - N_GRP - N_EXP), F32)],
                          axis=1).astype(BF16)
    brt = jnp.concatenate([b_group[l], b_router[l], jnp.zeros((LANES - N_GRP - N_EXP,), F32)])[None, :]
    cos_t, sp_t, sm_t = _rope_tables()
    row = lambda v: v[l][None, :]

    q, k, vt, gm, ckv_n, kr = _proj(xc, xl, mods, win_a, win_g, win_r, row(q_norm_g), wuq, row(kv_norm_g), wuk, wuvt,
                                   row(gmlp_ln_g), row(gmlp_ln_b), ws, bsb, cos_t, sp_t, sm_t)
    kc, vct = _kvexp(cache_ckv[:, l].reshape(N_LAT_B * PAST, KVL),
                     cache_krope[:, l].reshape(N_LAT_B * PAST, QK_ROPE), wuk, wuvt)
    attn_c = _attn_ctx(q, k, vt)
    attn_l = _attn_lat(q, k, vt, kc, vct)

    x1, h2p, ridx, rwt = _post(attn_c, attn_l, gm, xc, xl, mods, w_out[l].astype(BF16), row(ln1_g), row(ln1_b),
                               wrt, brt)
    eid_k, rank, hist = _rank(ridx)
    dest, blk_expert, blk_valid, zero_blk, zero_valid, switch, wslot, next_expert = _block_plan(
        eid_k, rank, hist[:, 0, :N_EXP].astype(I32))
    xs = _dispatch(zero_blk, zero_valid, dest, h2p)
    ys = _experts(blk_expert, blk_valid, switch, wslot, next_expert, xs, w1[l], w3[l], w2[l])
    y_c, y_l = _final(_regroup_steps(dest, TF), ys, x1, rwt, mods, row(ln2_g), row(ln2_b))

    return (y_c.reshape(N_CTX_B, N_CTX, D), y_l.reshape(N_LAT_B, N_LAT, D),
            ckv_n.reshape(N_CTX_B, 1, N_CTX, KVL), kr.reshape(N_CTX_B, 1, N_CTX, QK_ROPE))
```

```python
import functools

import jax
import jax.numpy as jnp
import numpy as np
from jax import lax
from jax.experimental import pallas as pl
from jax.experimental.pallas import tpu as pltpu

F32 = jnp.float32
BF16 = jnp.bfloat16
U32 = jnp.uint32
I32 = jnp.int32

D = 2048
N_CTX_B, N_CTX = 32, 256
N_LAT_B, N_LAT = 4, 4096
PAST = 512
GRID_W = 64
H = 8
QK_NOPE, QK_ROPE, V_DIM = 128, 64, 128
QL, KVL = 512, 512
GW = 1024
G_GROUPS, G_DIM, CHUNK = 8, 128, 128
N_GRP, E_PER_GRP, N_EXP, TOP_K = 4, 8, 32, 2
E_HID = 512
ROPE_THETA = 10000.0
EPS = 1e-6
ALPHA = 2.0 ** 0.25
SM_SCALE = (QK_NOPE + QK_ROPE) ** -0.5
Q_SCALE = SM_SCALE * float(np.log2(np.e))

T_CTX = N_CTX_B * N_CTX
T_LAT = N_LAT_B * N_LAT
T = T_CTX + T_LAT
N_ASSIGN = T * TOP_K

LANES = 128
HEAD_PAD = 256
V_AUG = V_DIM + 16
TM = 256
TP = 512
POST_SUBS = (256, 256)
SUB = 256
TD = 2048
TF = 256
CTX_PER_STEP = 2
TQ = 4096
TK = 512
BM = 256
N_BLOCKS = N_ASSIGN // BM + N_EXP
N_ZERO = 2 * N_EXP
VMEM_LIMIT = 56 * 1024 * 1024

NP_TILES = T_CTX // TM
LAT_TILES_PER_B = N_LAT // TM


def _cp(sem):
    return pltpu.CompilerParams(dimension_semantics=sem, vmem_limit_bytes=VMEM_LIMIT)


def _const_spec(shape):
    nd = len(shape)
    return pl.BlockSpec(shape, lambda *a: (0,) * nd, pipeline_mode=pl.Buffered(1))


def _dot(a, b):
    return jnp.dot(a, b, preferred_element_type=F32)


def _dot_nt(a, b):
    return lax.dot_general(a, b, (((1,), (1,)), ((), ())), preferred_element_type=F32)


def _layer_norm(x, g, b):
    mu = jnp.mean(x, -1, keepdims=True)
    xc = x - mu
    var = jnp.mean(xc * xc, -1, keepdims=True)
    return xc * lax.rsqrt(var + EPS) * g + b


def _rms_norm(x, g):
    return x * lax.rsqrt(jnp.mean(x * x, -1, keepdims=True) + EPS) * g


def _gelu(x):
    return 0.5 * x * (1.0 + lax.erf(x * np.float32(np.sqrt(0.5))))


def _pack_pair(lo, hi):
    lo_b = lax.bitcast_convert_type(lo.astype(BF16).astype(F32), U32)
    hi_b = lax.bitcast_convert_type(hi.astype(BF16).astype(F32), U32)
    return hi_b | (lo_b >> 16)


def _unpack_pair(w):
    lo = lax.bitcast_convert_type(w << 16, F32)
    hi = lax.bitcast_convert_type(w & jnp.uint32(0xFFFF0000), F32)
    return lo, hi


RT = 8


def _store_rows_as_tiles(ref, row0, val):
    n = val.shape[0]
    for j in range(RT):
        ref[pl.ds(row0 * RT + j, n, stride=RT), :] = val[:, j * LANES:(j + 1) * LANES]


def _load_tiles_as_rows(ref, row0, n):
    return jnp.concatenate([ref[pl.ds(row0 * RT + j, n, stride=RT), :] for j in range(RT)], axis=1)


def _adaln_kernel(cond_ref, w_ref, b_ref, o_ref):
    s = jax.nn.silu(cond_ref[...])
    s_hi = s.astype(BF16)
    s_lo = (s - s_hi.astype(F32)).astype(BF16)
    w = w_ref[...]
    w_hi = w.astype(BF16)
    w_lo = (w - w_hi.astype(F32)).astype(BF16)
    o_ref[...] = _dot(s_hi, w_hi) + _dot(s_lo, w_hi) + _dot(s_hi, w_lo) + b_ref[...]


def _adaln(cond16, w_ada, b_ada):
    tn = 1024
    n = w_ada.shape[1]
    return pl.pallas_call(
        _adaln_kernel,
        grid=(n // tn,),
        in_specs=[pl.BlockSpec((16, D), lambda j: (0, 0)),
                  pl.BlockSpec((D, tn), lambda j: (0, j)),
                  pl.BlockSpec((1, tn), lambda j: (0, j))],
        out_specs=pl.BlockSpec((16, tn), lambda j: (0, j)),
        out_shape=jax.ShapeDtypeStruct((16, n), F32),
        compiler_params=_cp(("arbitrary",)),
        name="adaln",
    )(cond16, w_ada, b_ada)


def _expand_kv(ckv_n, kr128, wuk_ref, wuvt_ref, k_ref, vt_ref):
    cb = ckv_n.astype(BF16)
    kn = _dot(cb, wuk_ref[...])
    vt = _dot_nt(wuvt_ref[...], cb).astype(BF16)
    ones = jnp.ones((V_AUG - V_DIM, vt.shape[1]), BF16)
    for h in range(H):
        vt_ref[h * V_AUG:h * V_AUG + V_DIM, :] = vt[h * V_DIM:(h + 1) * V_DIM]
        vt_ref[h * V_AUG + V_DIM:(h + 1) * V_AUG, :] = ones
    krb = kr128.astype(BF16)
    for h in range(H):
        k_ref[:, h * HEAD_PAD:h * HEAD_PAD + QK_NOPE] = kn[:, h * QK_NOPE:(h + 1) * QK_NOPE].astype(BF16)
        k_ref[:, h * HEAD_PAD + QK_NOPE:(h + 1) * HEAD_PAD] = krb


def _rope128(x, cos, s_plus, s_minus):
    return (x * cos + pltpu.roll(x, 16, 1) * s_plus + pltpu.roll(x, LANES - 16, 1) * s_minus)


def _proj_kernel(xc_ref, xl_ref, mods_ref, win_a_ref, win_g_ref, win_r_ref, qg_ref, wuq_ref, kvg_ref,
                 wuk_ref, wuv_ref,
                 lng_ref, lnb_ref, ws_ref, bsb_ref, cos_ref, sp_ref, sm_ref,
                 q_ref, k_ref, v_ref, gm_ref, ckv_ref, kr_ref):
    i = pl.program_id(0)
    x = jnp.where(i < NP_TILES, xc_ref[...], xl_ref[...])
    m = mods_ref[0]
    shift1, scale1 = m[0:1], m[1:2]
    hmod = (x * (1.0 + scale1) + shift1).astype(BF16)
    pg = _dot(hmod, win_g_ref[...])
    u, v = pg[:, :GW], pg[:, GW:]
    gu = _gelu(u)
    vn = _layer_norm(_gelu(v), lng_ref[...], lnb_ref[...]).astype(BF16)
    pa = _dot(hmod, win_a_ref[...])
    kr = _dot(hmod, win_r_ref[...])
    cq, ckv = pa[:, :QL], pa[:, QL:]

    cos, s_plus, s_minus = cos_ref[...], sp_ref[...], sm_ref[...]

    cq_n = _rms_norm(cq, qg_ref[...]).astype(BF16)
    q = _dot(cq_n, wuq_ref[...]) * Q_SCALE
    for h in range(H):
        q_ref[:, h * HEAD_PAD:h * HEAD_PAD + QK_NOPE] = q[:, h * HEAD_PAD:h * HEAD_PAD + QK_NOPE].astype(BF16)
        q_ref[:, h * HEAD_PAD + QK_NOPE:(h + 1) * HEAD_PAD] = _rope128(
            q[:, h * HEAD_PAD + QK_NOPE:(h + 1) * HEAD_PAD], cos, s_plus, s_minus).astype(BF16)

    ckv_n = _rms_norm(ckv, kvg_ref[...])
    _expand_kv(ckv_n, _rope128(kr, cos, s_plus, s_minus), wuk_ref, wuv_ref, k_ref, v_ref)

    for c in range(TM // CHUNK):
        rows = slice(c * CHUNK, (c + 1) * CHUNK)
        for g in range(G_GROUPS):
            cols = slice(g * G_DIM, (g + 1) * G_DIM)
            s = _dot(ws_ref[g], vn[rows, cols]) + bsb_ref[g]
            gm_ref[rows, cols] = (gu[rows, cols] * s).astype(BF16)

    @pl.when(i < NP_TILES)
    def _():
        ckv_ref[...] = ckv_n
        kr_ref[...] = kr[:, :QK_ROPE]


def _mods_index(i, tile=TM):
    return jnp.where(i < T_CTX // tile, N_LAT_B, (i - T_CTX // tile) // (N_LAT // tile))


def _proj(xc, xl, mods, win_a, win_g, win_r, qg, wuq, kvg, wuk, wuv, lng, lnb, ws, bsb, cos_t, sp_t, sm_t):
    def tok(width):
        return pl.BlockSpec((TM, width), lambda i: (i, 0))

    def pos_block(i):
        return jnp.where(i < NP_TILES, LAT_TILES_PER_B, (i - NP_TILES) % LAT_TILES_PER_B)

    rope_spec = pl.BlockSpec((TM, LANES), lambda i: (pos_block(i), 0))
    return pl.pallas_call(
        _proj_kernel,
        grid=(T // TM,),
        in_specs=[
            pl.BlockSpec((TM, D), lambda i: (jnp.minimum(i, NP_TILES - 1), 0)),
            pl.BlockSpec((TM, D), lambda i: (jnp.maximum(i - NP_TILES, 0), 0)),
            pl.BlockSpec((1, 6, D), lambda i: (_mods_index(i), 0, 0)),
            _const_spec((D, QL + KVL)), _const_spec((D, 2 * GW)), _const_spec((D, LANES)),
            _const_spec((1, QL)), _const_spec((QL, H * HEAD_PAD)),
            _const_spec((1, KVL)), _const_spec((KVL, H * QK_NOPE)), _const_spec((H * V_DIM, KVL)),
            _const_spec((1, GW)), _const_spec((1, GW)),
            _const_spec((G_GROUPS, CHUNK, CHUNK)), _const_spec((G_GROUPS, CHUNK, G_DIM)),
            rope_spec, rope_spec, rope_spec,
        ],
        out_specs=[tok(H * HEAD_PAD), tok(H * HEAD_PAD),
                   pl.BlockSpec((H * V_AUG, TM), lambda i: (0, i)), tok(GW),
                   pl.BlockSpec((TM, KVL), lambda i: (jnp.minimum(i, NP_TILES - 1), 0)),
                   pl.BlockSpec((TM, QK_ROPE), lambda i: (jnp.minimum(i, NP_TILES - 1), 0))],
        out_shape=[
            jax.ShapeDtypeStruct((T, H * HEAD_PAD), BF16),
            jax.ShapeDtypeStruct((T, H * HEAD_PAD), BF16),
            jax.ShapeDtypeStruct((H * V_AUG, T), BF16),
            jax.ShapeDtypeStruct((T, GW), BF16),
            jax.ShapeDtypeStruct((T_CTX, KVL), F32),
            jax.ShapeDtypeStruct((T_CTX, QK_ROPE), F32),
        ],
        compiler_params=_cp(("arbitrary",)),
        name="proj",
    )(xc, xl, mods, win_a, win_g, win_r, qg, wuq, kvg, wuk, wuv, lng, lnb, ws, bsb, cos_t, sp_t, sm_t)


def _kvexp_kernel(ckv_ref, kr_ref, wuk_ref, wuv_ref, k_ref, v_ref):
    kr = kr_ref[...]
    kr128 = jnp.concatenate([kr, jnp.zeros_like(kr)], axis=1)
    _expand_kv(ckv_ref[...], kr128, wuk_ref, wuv_ref, k_ref, v_ref)


def _kvexp(ckv, kr, wuk, wuv):
    rows = ckv.shape[0]
    return pl.pallas_call(
        _kvexp_kernel,
        grid=(rows // TM,),
        in_specs=[pl.BlockSpec((TM, KVL), lambda i: (i, 0)), pl.BlockSpec((TM, QK_ROPE), lambda i: (i, 0)),
                  _const_spec((KVL, H * QK_NOPE)), _const_spec((H * V_DIM, KVL))],
        out_specs=[pl.BlockSpec((TM, H * HEAD_PAD), lambda i: (i, 0)),
                   pl.BlockSpec((H * V_AUG, TM), lambda i: (0, i))],
        out_shape=[jax.ShapeDtypeStruct((rows, H * HEAD_PAD), BF16),
                   jax.ShapeDtypeStruct((H * V_AUG, rows), BF16)],
        compiler_params=_cp(("arbitrary",)),
        name="kvexp",
    )(ckv, kr, wuk, wuv)


def _probs(s, m):
    m_new = jnp.maximum(m, jnp.max(s, 0, keepdims=True))
    return m_new, jnp.exp2(m - m_new), jnp.exp2(s - m_new).astype(BF16)


def _attn_finish(acc):
    return (acc[:V_DIM] / acc[V_DIM:V_DIM + 1]).T.astype(BF16)


def _attn_ctx_kernel(q_ref, k_ref, vt_ref, o_ref):
    m0 = jnp.full((1, N_CTX), -jnp.inf, F32)
    for b in range(CTX_PER_STEP):
        rows = slice(b * N_CTX, (b + 1) * N_CTX)
        for h in range(H):
            qh = q_ref[rows, h * HEAD_PAD:(h + 1) * HEAD_PAD]
            kh = k_ref[rows, h * HEAD_PAD:(h + 1) * HEAD_PAD]
            _, _, p = _probs(_dot_nt(kh, qh), m0)
            acc = _dot(vt_ref[h * V_AUG:(h + 1) * V_AUG, rows], p)
            o_ref[rows, h * V_DIM:(h + 1) * V_DIM] = _attn_finish(acc)


def _attn_ctx(q, k, vt):
    n = CTX_PER_STEP * N_CTX
    return pl.pallas_call(
        _attn_ctx_kernel,
        grid=(N_CTX_B // CTX_PER_STEP,),
        in_specs=[pl.BlockSpec((n, H * HEAD_PAD), lambda b: (b, 0)),
                  pl.BlockSpec((n, H * HEAD_PAD), lambda b: (b, 0)),
                  pl.BlockSpec((H * V_AUG, n), lambda b: (0, b))],
        out_specs=pl.BlockSpec((n, H * V_DIM), lambda b: (b, 0)),
        out_shape=jax.ShapeDtypeStruct((T_CTX, H * V_DIM), BF16),
        compiler_params=_cp(("arbitrary",)),
        name="attn_ctx",
    )(q, k, vt)


def _attn_lat_kernel(q_ref, k_ref, vt_ref, kc_ref, vct_ref, o_ref):
    q = q_ref[...]
    n_lat = N_LAT // TK

    def keys(c):
        return k_ref[c * TK:(c + 1) * TK, :] if c < n_lat else kc_ref[...]

    def values_t(c):
        return vt_ref[:, c * TK:(c + 1) * TK] if c < n_lat else vct_ref[...]

    m = jnp.full((1, TQ), -jnp.inf, F32)
    acc = jnp.zeros((V_AUG, TQ), F32)
    s_next = _dot_nt(keys(0), q)
    m_next = jnp.maximum(m, jnp.max(s_next, 0, keepdims=True))
    p_prev = alpha_prev = None
    for c in range(n_lat + 1):
        s, m_new = s_next, m_next
        if c < n_lat:
            s_next = _dot_nt(keys(c + 1), q)
            m_next = jnp.maximum(m_new, jnp.max(s_next, 0, keepdims=True))
        if c > 0:
            acc = alpha_prev * acc + _dot(values_t(c - 1), p_prev)
        alpha_prev = jnp.exp2(m - m_new)
        p_prev = jnp.exp2(s - m_new).astype(BF16)
        m = m_new
    acc = alpha_prev * acc + _dot(values_t(n_lat), p_prev)
    o_ref[...] = _attn_finish(acc)


def _attn_lat(q, k, vt, kc, vct):
    qb0 = T_CTX // TQ
    qpb = N_LAT // TQ
    kb0 = T_CTX // N_LAT
    return pl.pallas_call(
        _attn_lat_kernel,
        grid=(N_LAT_B, H, qpb),
        in_specs=[pl.BlockSpec((TQ, HEAD_PAD), lambda b, h, qi: (qb0 + b * qpb + qi, h)),
                  pl.BlockSpec((N_LAT, HEAD_PAD), lambda b, h, qi: (kb0 + b, h)),
                  pl.BlockSpec((V_AUG, N_LAT), lambda b, h, qi: (h, kb0 + b)),
                  pl.BlockSpec((PAST, HEAD_PAD), lambda b, h, qi: (b, h)),
                  pl.BlockSpec((V_AUG, PAST), lambda b, h, qi: (h, b))],
        out_specs=pl.BlockSpec((TQ, V_DIM), lambda b, h, qi: (b * qpb + qi, h)),
        out_shape=jax.ShapeDtypeStruct((T_LAT, H * V_DIM), BF16),
        compiler_params=_cp(("arbitrary", "arbitrary", "arbitrary")),
        name="attn_lat",
    )(q, k, vt, kc, vct)


def _route(logits):
    lane_i = lax.broadcasted_iota(I32, logits.shape, 1)
    lane = lane_i.astype(F32)
    neg = jnp.float32(-jnp.inf)
    far = jnp.float32(LANES)
    gl = jnp.where(lane_i < N_GRP, logits, neg)
    gmax = jnp.max(gl, -1, keepdims=True)
    gidx = jnp.min(jnp.where(gl == gmax, lane, far), -1, keepdims=True)
    p_top = 1.0 / jnp.sum(jnp.exp(gl - gmax), -1, keepdims=True)
    grp_of_lane = ((lane_i - N_GRP) >> 3).astype(F32)
    in_grp = (lane_i >= N_GRP) & (lane_i < N_GRP + N_EXP) & (grp_of_lane == gidx)
    el = jnp.where(in_grp, logits, neg)
    m1 = jnp.max(el, -1, keepdims=True)
    i1 = jnp.min(jnp.where(el == m1, lane, far), -1, keepdims=True)
    el2 = jnp.where(lane == i1, neg, el)
    m2 = jnp.max(el2, -1, keepdims=True)
    i2 = jnp.min(jnp.where(el2 == m2, lane, far), -1, keepdims=True)
    e2 = jnp.exp(m2 - m1)
    w1 = p_top / (1.0 + e2)
    w2 = p_top * e2 / (1.0 + e2)
    return i1 - N_GRP, i2 - N_GRP, w1, w2


def _pair(a, b, dtype):
    two = lax.broadcasted_iota(I32, (a.shape[0], TOP_K), 1)
    return jnp.where(two == 0, a, b).astype(dtype)


def _post_kernel(ac_ref, al_ref, gm_ref, xc_ref, xl_ref, mods_ref, wout_ref, g1_ref, b1_ref, wrt_ref,
                 brt_ref, x1_ref, h2p_ref, ridx_ref, rwt_ref):
    i = pl.program_id(0)
    is_ctx = i < T_CTX // TP
    m = mods_ref[0]
    gate1, shift2, scale2 = m[2:3], m[3:4], m[4:5]

    def mix_of(rows):
        attn = jnp.where(is_ctx, ac_ref[rows, :], al_ref[rows, :])
        return _dot(jnp.concatenate([attn, gm_ref[rows, :]], axis=1), wout_ref[...])

    bounds = np.cumsum((0,) + POST_SUBS)
    subs = [slice(int(a), int(b)) for a, b in zip(bounds[:-1], bounds[1:])]
    mix_next = mix_of(subs[0])
    for s, rows in enumerate(subs):
        mix = mix_next
        if s + 1 < len(subs):
            mix_next = mix_of(subs[s + 1])
        x = jnp.where(is_ctx, xc_ref[rows, :], xl_ref[rows, :])
        x1 = _layer_norm(ALPHA * x + gate1 * mix, g1_ref[...], b1_ref[...])
        x1_ref[rows, :] = x1
        h2 = x1 * (1.0 + scale2) + shift2
        _store_rows_as_tiles(h2p_ref, rows.start, _pack_pair(h2[:, :D // 2], h2[:, D // 2:]))
        e1, e2, w1, w2 = _route(_dot(h2.astype(BF16), wrt_ref[...]) + brt_ref[...])
        ridx_ref[rows, :] = _pair(e1, e2, I32)
        rwt_ref[rows, :] = _pair(w1, w2, F32)


def _post(attn_c, attn_l, gm, xc, xl, mods, wout, g1, b1, wrt, brt):
    npt = T_CTX // TP

    def tok(width):
        return pl.BlockSpec((TP, width), lambda i: (i, 0))

    def ctx(width):
        return pl.BlockSpec((TP, width), lambda i: (jnp.minimum(i, npt - 1), 0))

    def lat(width):
        return pl.BlockSpec((TP, width), lambda i: (jnp.maximum(i - npt, 0), 0))

    return pl.pallas_call(
        _post_kernel,
        grid=(T // TP,),
        in_specs=[ctx(H * V_DIM), lat(H * V_DIM), tok(GW), ctx(D), lat(D),
                  pl.BlockSpec((1, 6, D), lambda i: (_mods_index(i, TP), 0, 0)),
                  _const_spec((D, D)), _const_spec((1, D)), _const_spec((1, D)),
                  _const_spec((D, LANES)), _const_spec((1, LANES))],
        out_specs=[tok(D), pl.BlockSpec((TP * RT, LANES), lambda i: (i, 0)), tok(TOP_K), tok(TOP_K)],
        out_shape=[jax.ShapeDtypeStruct((T, D), F32),
                   jax.ShapeDtypeStruct((T * RT, LANES), U32),
                   jax.ShapeDtypeStruct((T, TOP_K), I32),
                   jax.ShapeDtypeStruct((T, TOP_K), F32)],
        compiler_params=_cp(("arbitrary",)),
        name="post",
    )(attn_c, attn_l, gm, xc, xl, mods, wout, g1, b1, wrt, brt)


def _as_lane_row(col):
    return jnp.broadcast_to(col, (col.shape[0], LANES)).T[0:1, :]


def _rank_kernel(ridx_ref, tri_ref, e_ref, rank_ref, hist_ref):
    lane = lax.broadcasted_iota(I32, (SUB, LANES), 1).astype(F32)
    seen = jnp.zeros((1, LANES), F32)
    for s in range(TP // SUB):
        rows = slice(s * SUB, (s + 1) * SUB)
        e = ridx_ref[rows, :].astype(F32)
        ohs = [(lane == e[:, k:k + 1]).astype(F32) for k in range(TOP_K)]
        both = ohs[0] + ohs[1]
        before = _dot(tri_ref[...], both.astype(BF16)) + seen
        for k in range(TOP_K):
            rank_k = jnp.sum(before * ohs[k], -1, keepdims=True)
            e_ref[0, k:k + 1, rows] = _as_lane_row(e[:, k:k + 1]).astype(I32)
            rank_ref[0, k:k + 1, rows] = _as_lane_row(rank_k).astype(I32)
        seen = seen + jnp.sum(both, 0, keepdims=True)
    hist_ref[0] = jnp.broadcast_to(seen, (8, LANES))


def _rank(ridx):
    dense = pl.BlockSpec((1, TOP_K, TP), lambda i: (i, 0, 0))
    return pl.pallas_call(
        _rank_kernel,
        grid=(T // TP,),
        in_specs=[pl.BlockSpec((TP, TOP_K), lambda i: (i, 0)), _const_spec((SUB, SUB))],
        out_specs=[dense, dense, pl.BlockSpec((1, 8, LANES), lambda i: (i, 0, 0))],
        out_shape=[jax.ShapeDtypeStruct((T // TP, TOP_K, TP), I32),
                   jax.ShapeDtypeStruct((T // TP, TOP_K, TP), I32),
                   jax.ShapeDtypeStruct((T // TP, 8, LANES), F32)],
        compiler_params=_cp(("arbitrary",)),
        name="rank",
    )(ridx, jnp.asarray(np.tril(np.ones((SUB, SUB), np.float32), -1), BF16))


def _row_copy(src, src_row, dst, dst_row, sem):
    def tile(ref, row):
        return ref.at[pl.ds(pl.multiple_of(row * RT, RT), RT)]

    return pltpu.make_async_copy(tile(src, src_row), tile(dst, dst_row), sem)


def _dispatch_kernel(zb_ref, zv_ref, d0_ref, d1_ref, h2p_ref, xs_ref, zbuf, sem):
    i = pl.program_id(0)
    dest_refs = (d0_ref, d1_ref)

    @pl.when(i == 0)
    def _():
        zbuf[...] = jnp.zeros_like(zbuf)

        def zero_block(j):
            start = pl.multiple_of(zb_ref[j] * (BM * RT), BM * RT)
            return pltpu.make_async_copy(zbuf, xs_ref.at[pl.ds(start, BM * RT)], sem)

        def start(j, carry):
            @pl.when(zv_ref[j] != 0)
            def _():
                zero_block(j).start()
            return carry

        def wait(j, carry):
            @pl.when(zv_ref[j] != 0)
            def _():
                zero_block(j).wait()
            return carry

        lax.fori_loop(0, N_ZERO, start, 0)
        lax.fori_loop(0, N_ZERO, wait, 0)

    def issue(r, carry):
        for k in range(TOP_K):
            _row_copy(h2p_ref, r, xs_ref, dest_refs[k][0, 0, r], sem).start(priority=k)
        return carry

    lax.fori_loop(0, TD, issue, 0, unroll=16)
    for _ in range(TOP_K):
        pltpu.make_async_copy(h2p_ref, xs_ref.at[pl.ds(0, TD * RT)], sem).wait()


def _regroup_steps(dest, rows):
    return [dest[:, k, :].reshape(T // rows, 1, rows) for k in range(TOP_K)]


def _dispatch(zero_blk, zero_valid, dest, h2p):
    grid_spec = pltpu.PrefetchScalarGridSpec(
        num_scalar_prefetch=2,
        grid=(T // TD,),
        in_specs=[pl.BlockSpec((1, 1, TD), lambda i, zb, zv: (i, 0, 0), memory_space=pltpu.SMEM),
                  pl.BlockSpec((1, 1, TD), lambda i, zb, zv: (i, 0, 0), memory_space=pltpu.SMEM),
                  pl.BlockSpec((TD * RT, LANES), lambda i, zb, zv: (i, 0))],
        out_specs=pl.BlockSpec(memory_space=pl.ANY),
        scratch_shapes=[pltpu.VMEM((BM * RT, LANES), U32), pltpu.SemaphoreType.DMA(())],
    )
    return pl.pallas_call(
        _dispatch_kernel,
        grid_spec=grid_spec,
        out_shape=jax.ShapeDtypeStruct((N_BLOCKS * BM * RT, LANES), U32),
        compiler_params=_cp(("arbitrary",)),
        name="dispatch",
    )(zero_blk, zero_valid, *_regroup_steps(dest, TD), h2p)


def _experts_kernel(be_ref, bv_ref, sw_ref, es_ref, nx_ref, x_ref, w1_hbm, w3_hbm, w2_hbm, y_ref,
                    wf1, wf3, wf2, w1b, w3b, w2b, sems):
    i = pl.program_id(0)
    e = be_ref[i]
    nv = bv_ref[i]

    def weight_copies(expert, s):
        return (pltpu.make_async_copy(w1_hbm.at[expert], wf1.at[s], sems.at[s]),
                pltpu.make_async_copy(w3_hbm.at[expert], wf3.at[s], sems.at[s]),
                pltpu.make_async_copy(w2_hbm.at[expert], wf2.at[s], sems.at[s]))

    @pl.when(sw_ref[i] != 0)
    def _():
        s = es_ref[i]

        @pl.when(i == 0)
        def _():
            for c in weight_copies(e, s):
                c.start()

        for c in weight_copies(e, s):
            c.wait()
        nxt = nx_ref[i]

        @pl.when(nxt >= 0)
        def _():
            for c in weight_copies(nxt, 1 - s):
                c.start(priority=1)

        w1b[...] = wf1[s].astype(BF16)
        w3b[...] = wf3[s].astype(BF16)
        w2b[...] = wf2[s].astype(BF16)

    def gated_mlp(n):
        lo, hi = _unpack_pair(_load_tiles_as_rows(x_ref, 0, n))
        xb = jnp.concatenate([lo.astype(BF16), hi.astype(BF16)], axis=1)
        a = _dot(xb, w1b[...])
        b = _dot(xb, w3b[...])
        hid = (jax.nn.silu(a) * b).astype(BF16)
        y = _dot(hid, w2b[...])
        _store_rows_as_tiles(y_ref, 0, _pack_pair(y[:, :D // 2], y[:, D // 2:]))
        if n < BM:
            y_ref[pl.ds(n * RT, (BM - n) * RT), :] = jnp.zeros(((BM - n) * RT, LANES), U32)

    @pl.when(nv > BM // 2)
    def _():
        gated_mlp(BM)

    @pl.when((nv > 0) & (nv <= BM // 2))
    def _():
        gated_mlp(BM // 2)

    @pl.when(nv == 0)
    def _():
        y_ref[...] = jnp.zeros_like(y_ref)


def _experts(blk_expert, blk_valid, switch, wslot, next_expert, xs, w1, w3, w2):
    hbm = pl.BlockSpec(memory_space=pl.ANY)
    grid_spec = pltpu.PrefetchScalarGridSpec(
        num_scalar_prefetch=5,
        grid=(N_BLOCKS,),
        in_specs=[pl.BlockSpec((BM * RT, LANES), lambda i, *_: (i, 0)), hbm, hbm, hbm],
        out_specs=pl.BlockSpec((BM * RT, LANES), lambda i, *_: (i, 0)),
        scratch_shapes=[pltpu.VMEM((2, D, E_HID), F32), pltpu.VMEM((2, D, E_HID), F32),
                        pltpu.VMEM((2, E_HID, D), F32),
                        pltpu.VMEM((D, E_HID), BF16), pltpu.VMEM((D, E_HID), BF16),
                        pltpu.VMEM((E_HID, D), BF16), pltpu.SemaphoreType.DMA((2,))],
    )
    return pl.pallas_call(
        _experts_kernel,
        grid_spec=grid_spec,
        out_shape=jax.ShapeDtypeStruct((N_BLOCKS * BM * RT, LANES), U32),
        compiler_params=_cp(("arbitrary",)),
        name="experts",
    )(blk_expert, blk_valid, switch, wslot, next_expert, xs, w1, w3, w2)


def _final_kernel(dcur0_ref, dcur1_ref, dnext0_ref, dnext1_ref, ys_ref, x1_ref, rwt_ref, mods_ref,
                  g2_ref, b2_ref, oc_ref, ol_ref, buf, sems):
    i = pl.program_id(0)
    dcur_ref, dnext_ref = (dcur0_ref, dcur1_ref), (dnext0_ref, dnext1_ref)
    n = pl.num_programs(0)
    slot = i % 2

    def first_row(s, k):
        return (s * TOP_K + k) * TF

    def issue(d_ref, s):
        def body(r, carry):
            for k in range(TOP_K):
                _row_copy(ys_ref, d_ref[k][0, 0, r], buf, first_row(s, k) + r, sems.at[s]).start()
            return carry

        lax.fori_loop(0, TF, body, 0, unroll=32)

    @pl.when(i == 0)
    def _():
        issue(dcur_ref, 0)

    @pl.when(i + 1 < n)
    def _():
        issue(dnext_ref, 1 - slot)

    for k in range(TOP_K):
        pltpu.make_async_copy(ys_ref.at[pl.ds(0, TF * RT)], buf.at[pl.ds(0, TF * RT)], sems.at[slot]).wait()

    w = rwt_ref[...]
    lo0, hi0 = _unpack_pair(_load_tiles_as_rows(buf, first_row(slot, 0), TF))
    lo1, hi1 = _unpack_pair(_load_tiles_as_rows(buf, first_row(slot, 1), TF))
    w0, w1 = w[:, 0:1], w[:, 1:2]
    moe = jnp.concatenate([lo0 * w0 + lo1 * w1, hi0 * w0 + hi1 * w1], axis=1)
    gate2 = mods_ref[0][5:6]
    out = _layer_norm(ALPHA * x1_ref[...] + gate2 * moe, g2_ref[...], b2_ref[...])

    @pl.when(i < T_CTX // TF)
    def _():
        oc_ref[...] = out

    @pl.when(i >= T_CTX // TF)
    def _():
        ol_ref[...] = out


def _final(dest_k, ys, x1, rwt, mods, g2, b2):
    n = T // TF
    npt = T_CTX // TF
    cur = pl.BlockSpec((1, 1, TF), lambda i: (i, 0, 0), memory_space=pltpu.SMEM)
    nxt = pl.BlockSpec((1, 1, TF), lambda i: (jnp.minimum(i + 1, n - 1), 0, 0), memory_space=pltpu.SMEM)
    return pl.pallas_call(
        _final_kernel,
        grid=(n,),
        in_specs=[cur, cur, nxt, nxt,
                  pl.BlockSpec(memory_space=pl.ANY),
                  pl.BlockSpec((TF, D), lambda i: (i, 0)),
                  pl.BlockSpec((TF, TOP_K), lambda i: (i, 0)),
                  pl.BlockSpec((1, 6, D), lambda i: (_mods_index(i, TF), 0, 0)),
                  _const_spec((1, D)), _const_spec((1, D))],
        out_specs=[pl.BlockSpec((TF, D), lambda i: (jnp.minimum(i, npt - 1), 0)),
                   pl.BlockSpec((TF, D), lambda i: (jnp.maximum(i - npt, 0), 0))],
        out_shape=[jax.ShapeDtypeStruct((T_CTX, D), F32), jax.ShapeDtypeStruct((T_LAT, D), F32)],
        scratch_shapes=[pltpu.VMEM((2 * TOP_K * TF * RT, LANES), U32), pltpu.SemaphoreType.DMA((2,))],
        compiler_params=_cp(("arbitrary",)),
        name="final",
    )(*dest_k, *dest_k, ys, x1, rwt, mods, g2, b2)


def _rope_tables():
    n = N_LAT
    rows = n // GRID_W
    f32 = np.float32
    row = np.repeat(np.arange(rows, dtype=f32), GRID_W)
    col = np.tile(np.arange(GRID_W, dtype=f32), rows)
    half = QK_ROPE // 2
    inv = (f32(ROPE_THETA) ** (-np.arange(0, half, 2, dtype=f32) / f32(half))).astype(f32)
    ang_r, ang_c = (row[:, None] * inv).astype(f32), (col[:, None] * inv).astype(f32)
    cr, sr = np.cos(ang_r).astype(f32), np.sin(ang_r).astype(f32)
    cc, sc = np.cos(ang_c).astype(f32), np.sin(ang_c).astype(f32)
    z16 = np.zeros_like(cr)
    pad = lambda v: np.full((n, LANES - QK_ROPE), v, f32)
    cos = np.concatenate([cr, cr, cc, cc, pad(1.0)], axis=1)
    s_plus = np.concatenate([z16, sr, z16, sc, pad(0.0)], axis=1)
    s_minus = np.concatenate([-sr, z16, -sc, z16, pad(0.0)], axis=1)
    ident = lambda v: np.full((TM, LANES), v, f32)
    return (jnp.asarray(np.concatenate([cos, ident(1.0)], 0)),
            jnp.asarray(np.concatenate([s_plus, ident(0.0)], 0)),
            jnp.asarray(np.concatenate([s_minus, ident(0.0)], 0)))


def _block_plan(eid_k, rank, hist):
    counts = jnp.sum(hist, axis=0)
    nblk = (counts + BM - 1) // BM
    bends = jnp.cumsum(nblk)
    bstarts = bends - nblk
    base = bstarts[None, :] * BM + jnp.cumsum(hist, axis=0) - hist
    eid = jnp.arange(N_EXP, dtype=I32)
    hit = eid_k[:, :, None, :] == eid[None, None, :, None]
    first = jnp.sum(jnp.where(hit, base[:, None, :, None], 0), axis=2)
    dest = (first + rank).astype(I32)
    blk = jnp.arange(N_BLOCKS, dtype=I32)
    blk_expert = jnp.minimum(jnp.sum((blk[:, None] >= bends[None, :]).astype(I32), axis=1), N_EXP - 1)
    of_blk = blk_expert[:, None] == eid[None, :]

    def per_block(v):
        return jnp.sum(jnp.where(of_blk, v[None, :], 0), axis=1)

    left = per_block(counts) - (blk - per_block(bstarts)) * BM
    blk_valid = jnp.where(blk < bends[-1], jnp.clip(left, 0, BM), 0).astype(I32)
    unused = bends[-1] + jnp.arange(N_EXP, dtype=I32)
    zero_blk = jnp.concatenate([jnp.maximum(bends - 1, 0), jnp.minimum(unused, N_BLOCKS - 1)])
    zero_valid = jnp.concatenate([counts % BM != 0, unused < N_BLOCKS])
    prev_expert = jnp.concatenate([jnp.full((1,), -1, I32), blk_expert[:-1]])
    switch = (blk < bends[-1]) & (blk_expert != prev_expert)
    wslot = (jnp.cumsum(switch.astype(I32)) - 1) % 2
    later = (eid[None, :] > eid[:, None]) & (nblk[None, :] > 0)
    next_of = jnp.min(jnp.where(later, eid[None, :], N_EXP), axis=1)
    next_expert = per_block(jnp.where(next_of < N_EXP, next_of, -1))
    return (dest, blk_expert.astype(I32), blk_valid, zero_blk.astype(I32),
            zero_valid.astype(I32), switch.astype(I32), wslot.astype(I32), next_expert.astype(I32))


def kernel(x_prompt, x_sample, cache_ckv, cache_krope, c, c_ctx, w_ada, b_ada, w_in, q_norm_g, w_uq,
           kv_norm_g, w_ukv, gmlp_ln_g, gmlp_ln_b, w_spatial, b_spatial, w_out, ln1_g, ln1_b, w_group,
           b_group, w_router, b_router, w1, w3, w2, ln2_g, ln2_b):
    l = 0
    xc = x_prompt.reshape(T_CTX, D)
    xl = x_sample.reshape(T_LAT, D)

    cond16 = jnp.concatenate([c, c_ctx[None, :], jnp.zeros((16 - N_LAT_B - 1, D), F32)], axis=0)
    mods = _adaln(cond16, w_ada[l], b_ada[l][None, :]).reshape(16, 6, D)

    wi = w_in[l]
    o_kr = QL + KVL
    o_u = o_kr + QK_ROPE
    win_a = wi[:, :o_kr].astype(BF16)
    win_g = wi[:, o_u:].astype(BF16)
    win_r = jnp.pad(wi[:, o_kr:o_u], ((0, 0), (0, LANES - QK_ROPE))).astype(BF16)
    wuq = jnp.pad(w_uq[l].reshape(QL, H, QK_NOPE + QK_ROPE),
                  ((0, 0), (0, 0), (0, HEAD_PAD - QK_NOPE - QK_ROPE))).reshape(QL, H * HEAD_PAD).astype(BF16)
    wkv = w_ukv[l].reshape(KVL, H, QK_NOPE + V_DIM)
    wuk = wkv[:, :, :QK_NOPE].reshape(KVL, H * QK_NOPE).astype(BF16)
    wuvt = wkv[:, :, QK_NOPE:].reshape(KVL, H * V_DIM).T.astype(BF16)
    ws = w_spatial[l].astype(BF16)
    bsb = jnp.broadcast_to(b_spatial[l][:, :, None], (G_GROUPS, CHUNK, G_DIM))
    wrt = jnp.concatenate([w_group[l], w_router[l], jnp.zeros((D, LANES - N_GRP - N_EXP), F32)],
                          axis=1).astype(BF16)
    brt = jnp.concatenate([b_group[l], b_router[l], jnp.zeros((LANES - N_GRP - N_EXP,), F32)])[None, :]
    cos_t, sp_t, sm_t = _rope_tables()
    row = lambda v: v[l][None, :]

    q, k, vt, gm, ckv_n, kr = _proj(xc, xl, mods, win_a, win_g, win_r, row(q_norm_g), wuq, row(kv_norm_g), wuk, wuvt,
                                   row(gmlp_ln_g), row(gmlp_ln_b), ws, bsb, cos_t, sp_t, sm_t)
    kc, vct = _kvexp(cache_ckv[:, l].reshape(N_LAT_B * PAST, KVL),
                     cache_krope[:, l].reshape(N_LAT_B * PAST, QK_ROPE), wuk, wuvt)
    attn_c = _attn_ctx(q, k, vt)
    attn_l = _attn_lat(q, k, vt, kc, vct)

    x1, h2p, ridx, rwt = _post(attn_c, attn_l, gm, xc, xl, mods, w_out[l].astype(BF16), row(ln1_g), row(ln1_b),
                               wrt, brt)
    eid_k, rank, hist = _rank(ridx)
    dest, blk_expert, blk_valid, zero_blk, zero_valid, switch, wslot, next_expert = _block_plan(
        eid_k, rank, hist[:, 0, :N_EXP].astype(I32))
    xs = _dispatch(zero_blk, zero_valid, dest, h2p)
    ys = _experts(blk_expert, blk_valid, switch, wslot, next_expert, xs, w1[l], w3[l], w2[l])
    y_c, y_l = _final(_regroup_steps(dest, TF), ys, x1, rwt, mods, row(ln2_g), row(ln2_b))

    return (y_c.reshape(N_CTX_B, N_CTX, D), y_l.reshape(N_LAT_B, N_LAT, D),
            ckv_n.reshape(N_CTX_B, 1, N_CTX, KVL), kr.reshape(N_CTX_B, 1, N_CTX, QK_ROPE))
```
